```python
import math
import jax, jax.numpy as jnp
from jax import lax
import numpy as np

D_MODEL = 1024
BATCH = 8
SEQ = 2048
DEPTH = 1
DEC_BATCH = 128
DEC_SEQ = 8
PAST_LEN = 8192
PAGE_SIZE = 128

ATT_HEADS = 16
ATT_KV_HEADS = 4
ATT_GROUP = ATT_HEADS // ATT_KV_HEADS
HEAD_DIM = 64
ATT_WIDTH = ATT_HEADS * HEAD_DIM
KV_WIDTH = ATT_KV_HEADS * HEAD_DIM
WINDOW = 128
CACHE_WIN = min(WINDOW, PAST_LEN)
ROT_DIM = HEAD_DIM // 4
ROPE_THETA = 500000.0
SSM_HEADS = 16
SSM_HEAD_DIM = 64
SSM_WIDTH = SSM_HEADS * SSM_HEAD_DIM
SSM_GROUPS = 2
SSM_HPG = SSM_HEADS // SSM_GROUPS
SSM_GROUP_W = SSM_WIDTH // SSM_GROUPS
D_STATE = 128
CONV_W = 4
SSM_CHUNK = 128
CONV_DIM = SSM_WIDTH + 2 * SSM_GROUPS * D_STATE
MIX_WIDTH = ATT_WIDTH + SSM_WIDTH
IN_PROJ = ATT_WIDTH + 2 * KV_WIDTH + SSM_WIDTH + CONV_DIM + SSM_HEADS
SPLIT_IDX = [ATT_WIDTH, ATT_WIDTH + KV_WIDTH, ATT_WIDTH + 2 * KV_WIDTH,
             ATT_WIDTH + 2 * KV_WIDTH + SSM_WIDTH,
             ATT_WIDTH + 2 * KV_WIDTH + SSM_WIDTH + CONV_DIM]
FF_RAW = -(-8 * D_MODEL // 3)
D_FF = -(-FF_RAW // 256) * 256
EPS = 1e-6

kernel_name = "hymba_swa_sink_ssd_decode_step"

F32 = jnp.float32


def rmsnorm(x, g):
    xf = x.astype(F32)
    y = xf * lax.rsqrt(jnp.mean(xf * xf, axis=-1, keepdims=True) + EPS)
    return (y * g.astype(F32)).astype(x.dtype)


def rope_partial(x, pos):
    half = ROT_DIM // 2
    inv = ROPE_THETA ** (-jnp.arange(half, dtype=F32) * 2.0 / ROT_DIM)
    ang = pos.astype(F32)[:, None] * inv[None, :]
    cos = jnp.cos(ang)[None, :, None, :]
    sin = jnp.sin(ang)[None, :, None, :]
    xr = x[..., :ROT_DIM].astype(F32)
    x1, x2 = xr[..., :half], xr[..., half:]
    rot = jnp.concatenate([x1 * cos - x2 * sin, x2 * cos + x1 * sin], axis=-1)
    return jnp.concatenate([rot.astype(x.dtype), x[..., ROT_DIM:]], axis=-1)


def sink_attention(q, k, v, valid, sinks):
    s = jnp.einsum('bnqkgd,bnskd->bnkgqs', q.astype(F32), k.astype(F32)) * (HEAD_DIM ** -0.5)
    s = jnp.where(valid[None, :, None, None], s, -jnp.inf)
    sink = sinks.astype(F32).reshape(1, 1, ATT_KV_HEADS, ATT_GROUP, 1, 1)
    m = jnp.maximum(jnp.max(s, axis=-1, keepdims=True), sink)
    p = jnp.exp(s - m)
    den = jnp.sum(p, axis=-1, keepdims=True) + jnp.exp(sink - m)
    o = jnp.einsum('bnkgqs,bnskd->bnqkgd', p / den, v.astype(F32))
    return o.astype(q.dtype)


def swa_prompt(q, k, v, sinks):
    b, L = q.shape[:2]
    nb = L // WINDOW
    qb = q.reshape(b, nb, WINDOW, ATT_KV_HEADS, ATT_GROUP, HEAD_DIM)
    kb = k.reshape(b, nb, WINDOW, ATT_KV_HEADS, HEAD_DIM)
    vb = v.reshape(b, nb, WINDOW, ATT_KV_HEADS, HEAD_DIM)
    pad = ((0, 0), (1, 0), (0, 0), (0, 0), (0, 0))
    kk = jnp.concatenate([jnp.pad(kb[:, :-1], pad), kb], axis=2)
    vv = jnp.concatenate([jnp.pad(vb[:, :-1], pad), vb], axis=2)
    i = jnp.arange(WINDOW)[:, None]
    j = jnp.arange(2 * WINDOW)[None, :]
    d = WINDOW + i - j
    valid_rel = (d >= 0) & (d <= WINDOW)
    valid = valid_rel[None] & ((jnp.arange(nb)[:, None, None] > 0) | (j >= WINDOW)[None])
    o = sink_attention(qb, kk, vv, valid, sinks)
    return o.reshape(b, L, ATT_WIDTH)


def swa_sample(q, k, v, k_cache, v_cache, sinks):
    b, t = q.shape[:2]
    kk = jnp.concatenate([k_cache.astype(k.dtype), k], axis=1)
    vv = jnp.concatenate([v_cache.astype(v.dtype), v], axis=1)
    qpos = jnp.arange(t)[:, None]
    kpos = jnp.concatenate([jnp.arange(CACHE_WIN) - CACHE_WIN, jnp.arange(t)])[None, :]
    d = qpos - kpos
    valid = ((d >= 0) & (d <= WINDOW))[None]
    qb = q.reshape(b, 1, t, ATT_KV_HEADS, ATT_GROUP, HEAD_DIM)
    o = sink_attention(qb, kk[:, None], vv[:, None], valid, sinks)
    return o.reshape(b, t, ATT_WIDTH), kk[:, -CACHE_WIN:], vv[:, -CACHE_WIN:]


def causal_conv_silu(xpad, w, bias):
    L = xpad.shape[1] - (CONV_W - 1)
    acc = bias + xpad[:, 0:L] * w[0]
    for tap in range(1, CONV_W):
        acc = acc + xpad[:, tap:tap + L] * w[tap]
    return jax.nn.silu(acc)


def ssd_scan(x, dt, a, bm, cm, h0, chunk):
    b, L = x.shape[:2]
    nc = L // chunk

    def to_chunks(t):
        return t.reshape((b, nc, chunk) + t.shape[2:]).swapaxes(0, 1)

    xs = to_chunks(x.reshape(b, L, SSM_GROUPS, SSM_HPG, SSM_HEAD_DIM))
    dts = to_chunks(dt.reshape(b, L, SSM_GROUPS, SSM_HPG))
    bs = to_chunks(bm)
    cs = to_chunks(cm)
    ag = a.reshape(SSM_GROUPS, SSM_HPG)
    causal = jnp.tril(jnp.ones((chunk, chunk), dtype=bool))

    def step(h, inp):
        xc, dtc, bc, cc = inp
        acum = jnp.cumsum(dtc * ag, axis=1)
        seg = acum[:, :, None] - acum[:, None, :]
        decay = jnp.exp(jnp.where(causal[None, :, :, None, None], seg, -jnp.inf))
        cb = jnp.einsum('btgn,bsgn->btsg', cc, bc)
        wts = cb[..., None] * decay * dtc[:, None]
        y = jnp.einsum('btsge,bsgep->btgep', wts, xc)
        hg = h.reshape(b, SSM_GROUPS, SSM_HPG, SSM_HEAD_DIM, D_STATE)
        y = y + jnp.einsum('btgn,bgepn->btgep', cc, hg) * jnp.exp(acum)[..., None]
        tail = jnp.exp(acum[:, -1:] - acum) * dtc
        h_new = hg * jnp.exp(acum[:, -1])[..., None, None] + jnp.einsum('bsge,bsgep,bsgn->bgepn', tail, xc, bc)
        return h_new.reshape(b, SSM_HEADS, SSM_HEAD_DIM, D_STATE), y

    h, ys = lax.scan(step, h0, (xs, dts, bs, cs))
    y = ys.swapaxes(0, 1).reshape(b, L, SSM_HEADS, SSM_HEAD_DIM)
    return y, h


def decoder_layer(x, pos, k_cache, v_cache, conv_state, ssm_state, chunk,
                  g_pre_mix, w_in, sinks, conv_w, conv_b, dt_bias, a_log, d_skip, g_ssm_out,
                  w_out, g_post_mix, g_pre_ffn, w_gate, w_up, w_down, g_post_ffn):
    b, L, _ = x.shape
    h = rmsnorm(x, g_pre_mix)
    q, k, v, z, xbc, dt_raw = jnp.split(h @ w_in, SPLIT_IDX, axis=-1)
    q = rope_partial(q.reshape(b, L, ATT_HEADS, HEAD_DIM), pos)
    k = rope_partial(k.reshape(b, L, ATT_KV_HEADS, HEAD_DIM), pos)
    v = v.reshape(b, L, ATT_KV_HEADS, HEAD_DIM)
    if k_cache is None:
        o_att = swa_prompt(q, k, v, sinks)
        new_k, new_v = k[:, -CACHE_WIN:], v[:, -CACHE_WIN:]
    else:
        o_att, new_k, new_v = swa_sample(q, k, v, k_cache, v_cache, sinks)
    xpad = jnp.concatenate([conv_state.astype(xbc.dtype), xbc], axis=1)
    new_conv = xpad[:, -(CONV_W - 1):]
    xbc = causal_conv_silu(xpad, conv_w, conv_b)
    xs, bmat, cmat = jnp.split(xbc, [SSM_WIDTH, SSM_WIDTH + SSM_GROUPS * D_STATE], axis=-1)
    dt = jax.nn.softplus(dt_raw.astype(F32) + dt_bias.astype(F32))
    a = -jnp.exp(a_log.astype(F32))
    xh = xs.reshape(b, L, SSM_HEADS, SSM_HEAD_DIM).astype(F32)
    y, new_ssm = ssd_scan(xh, dt, a,
                          bmat.reshape(b, L, SSM_GROUPS, D_STATE).astype(F32),
                          cmat.reshape(b, L, SSM_GROUPS, D_STATE).astype(F32),
                          ssm_state.astype(F32), chunk)
    y = y + d_skip.astype(F32)[:, None] * xh
    gated = y.reshape(b, L, SSM_GROUPS, SSM_GROUP_W) * jax.nn.silu(z.astype(F32)).reshape(b, L, SSM_GROUPS, SSM_GROUP_W)
    gated = gated * lax.rsqrt(jnp.mean(gated * gated, axis=-1, keepdims=True) + EPS)
    o_ssm = (gated.reshape(b, L, SSM_WIDTH) * g_ssm_out.astype(F32)).astype(x.dtype)
    mix = jnp.concatenate([o_att.astype(x.dtype), o_ssm], axis=-1) @ w_out
    x = x + rmsnorm(mix, g_post_mix)
    f = rmsnorm(x, g_pre_ffn)
    f = (jax.nn.silu(f @ w_gate) * (f @ w_up)) @ w_down
    x = x + rmsnorm(f, g_post_ffn)
    return x, new_k, new_v, new_conv, new_ssm


def setup_inputs(seed: int = 0) -> dict:
    key = jax.random.key(seed)
    ks = jax.random.split(key, 24)

    def nrm(k, shape, scale):
        return jax.random.normal(k, shape, F32) * scale

    dt0 = jnp.exp(jax.random.uniform(ks[10], (DEPTH, SSM_HEADS), F32, math.log(1e-3), math.log(1e-1)))
    return {
        "x_prompt": nrm(ks[0], (BATCH, SEQ, D_MODEL), 1.0),
        "x_sample": nrm(ks[1], (DEC_BATCH, DEC_SEQ, D_MODEL), 1.0),
        "cache_k_win": nrm(ks[2], (DEPTH, DEC_BATCH, CACHE_WIN, ATT_KV_HEADS, HEAD_DIM), 1.0),
        "cache_v_win": nrm(ks[3], (DEPTH, DEC_BATCH, CACHE_WIN, ATT_KV_HEADS, HEAD_DIM), 1.0),
        "state_conv": nrm(ks[4], (DEPTH, DEC_BATCH, CONV_W - 1, CONV_DIM), 1.0),
        "state_ssm": nrm(ks[5], (DEPTH, DEC_BATCH, SSM_HEADS, SSM_HEAD_DIM, D_STATE), 0.1),
        "g_pre_mix": 1.0 + nrm(ks[6], (DEPTH, D_MODEL), 0.02),
        "w_in": nrm(ks[7], (DEPTH, D_MODEL, IN_PROJ), D_MODEL ** -0.5),
        "attn_sinks": nrm(ks[8], (DEPTH, ATT_HEADS), 1.0),
        "conv_w": nrm(ks[9], (DEPTH, CONV_W, CONV_DIM), CONV_W ** -0.5),
        "conv_b": nrm(ks[11], (DEPTH, CONV_DIM), 0.02),
        "dt_bias": dt0 + jnp.log(-jnp.expm1(-dt0)),
        "a_log": jnp.log(jax.random.uniform(ks[12], (DEPTH, SSM_HEADS), F32, 1.0, 16.0)),
        "d_skip": 1.0 + nrm(ks[13], (DEPTH, SSM_HEADS), 0.02),
        "g_ssm_out": 1.0 + nrm(ks[14], (DEPTH, SSM_WIDTH), 0.02),
        "w_out": nrm(ks[15], (DEPTH, MIX_WIDTH, D_MODEL), MIX_WIDTH ** -0.5),
        "g_post_mix": 1.0 + nrm(ks[16], (DEPTH, D_MODEL), 0.02),
        "g_pre_ffn": 1.0 + nrm(ks[17], (DEPTH, D_MODEL), 0.02),
        "w_gate": nrm(ks[18], (DEPTH, D_MODEL, D_FF), D_MODEL ** -0.5),
        "w_up": nrm(ks[19], (DEPTH, D_MODEL, D_FF), D_MODEL ** -0.5),
        "w_down": nrm(ks[20], (DEPTH, D_FF, D_MODEL), D_FF ** -0.5),
        "g_post_ffn": 1.0 + nrm(ks[21], (DEPTH, D_MODEL), 0.02),
    }


def reference(x_prompt, x_sample, cache_k_win, cache_v_win, state_conv, state_ssm,
              g_pre_mix, w_in, attn_sinks, conv_w, conv_b, dt_bias, a_log, d_skip, g_ssm_out,
              w_out, g_post_mix, g_pre_ffn, w_gate, w_up, w_down, g_post_ffn):
    bp, lp = x_prompt.shape[:2]
    ts = x_sample.shape[1]
    pos_p = jnp.arange(lp, dtype=jnp.int32)
    pos_s = PAST_LEN + jnp.arange(ts, dtype=jnp.int32)
    hp, hs = x_prompt, x_sample
    kp_l, vp_l, cp_l, sp_l = [], [], [], []
    ks_l, vs_l, cs_l, ss_l = [], [], [], []
    for l in range(DEPTH):
        w = (g_pre_mix[l], w_in[l], attn_sinks[l], conv_w[l], conv_b[l], dt_bias[l], a_log[l],
             d_skip[l], g_ssm_out[l], w_out[l], g_post_mix[l], g_pre_ffn[l], w_gate[l], w_up[l],
             w_down[l], g_post_ffn[l])
        conv0 = jnp.zeros((bp, CONV_W - 1, CONV_DIM), x_prompt.dtype)
        ssm0 = jnp.zeros((bp, SSM_HEADS, SSM_HEAD_DIM, D_STATE), F32)
        hp, kp, vp, cp, sp = decoder_layer(hp, pos_p, None, None, conv0, ssm0, SSM_CHUNK, *w)
        hs, ksm, vsm, csm, ssm = decoder_layer(hs, pos_s, cache_k_win[l], cache_v_win[l],
                                               state_conv[l], state_ssm[l], ts, *w)
        kp_l.append(kp); vp_l.append(vp); cp_l.append(cp); sp_l.append(sp)
        ks_l.append(ksm); vs_l.append(vsm); cs_l.append(csm); ss_l.append(ssm)
    return (hp, hs,
            jnp.stack(kp_l), jnp.stack(vp_l), jnp.stack(cp_l), jnp.stack(sp_l),
            jnp.stack(ks_l), jnp.stack(vs_l), jnp.stack(cs_l), jnp.stack(ss_l))
```

```python
import functools
import math

import numpy as np
import jax
import jax.numpy as jnp
from jax import lax
from jax.experimental import pallas as pl
from jax.experimental.pallas import tpu as pltpu

F32 = jnp.float32
BF16 = jnp.bfloat16

D_MODEL = 1024
ATT_HEADS = 16
ATT_KV_HEADS = 4
ATT_GROUP = ATT_HEADS // ATT_KV_HEADS
HEAD_DIM = 64
ATT_WIDTH = ATT_HEADS * HEAD_DIM
KV_WIDTH = ATT_KV_HEADS * HEAD_DIM
WINDOW = 128
ROT_DIM = HEAD_DIM // 4
ROPE_THETA = 500000.0
SSM_HEADS = 16
SSM_HEAD_DIM = 64
SSM_WIDTH = SSM_HEADS * SSM_HEAD_DIM
SSM_GROUPS = 2
SSM_HPG = SSM_HEADS // SSM_GROUPS
SSM_GROUP_W = SSM_WIDTH // SSM_GROUPS
D_STATE = 128
CONV_W = 4
CONV_DIM = SSM_WIDTH + 2 * SSM_GROUPS * D_STATE
MIX_WIDTH = ATT_WIDTH + SSM_WIDTH
EPS = 1e-6
PAST_LEN = 8192

LANES = 128
SUBLANES = 8
VMEM_LIMIT_BYTES = 60 * 1024 * 1024

CHUNK = 128
NEG = -1e30
SEQ_TILE = 512
SAMPLE_BT = 16
FFN_TILE = 512
FF_CHUNK = 256


def _nn(a, b):
    return jnp.dot(a, b, preferred_element_type=F32)


def _nt(a, b):
    return lax.dot_general(a, b, (((1,), (1,)), ((), ())), preferred_element_type=F32)


def _split_bf16(x, n):
    parts = []
    r = x
    for i in range(n):
        p = r.astype(BF16)
        parts.append(p)
        if i + 1 < n:
            r = r - p.astype(F32)
    return parts


def _nn_split_lhs(x, m01, n=3):
    acc = None
    for p in _split_bf16(x, n):
        t = _nn(p, m01)
        acc = t if acc is None else acc + t
    return acc


def _nn_split_rhs(m01, x, n=3):
    acc = None
    for p in _split_bf16(x, n):
        t = _nn(m01, p)
        acc = t if acc is None else acc + t
    return acc


def _nt_split_lhs(x, m01, n=3):
    acc = None
    for p in _split_bf16(x, n):
        t = _nt(p, m01)
        acc = t if acc is None else acc + t
    return acc


def _rms(x, g):
    ms = jnp.mean(x * x, axis=-1, keepdims=True)
    return x * lax.rsqrt(ms + EPS) * g


def _silu(x):
    return x / (1.0 + jnp.exp(-x))


def _softplus(x):
    return jnp.maximum(x, 0.0) + jnp.log1p(jnp.exp(-jnp.abs(x)))


def _rope(x, c, s1, s2):
    outs = []
    for j in range(x.shape[1] // LANES):
        xb = x[:, j * LANES:(j + 1) * LANES]
        outs.append(xb * c + pltpu.roll(xb, LANES - ROT_DIM // 2, 1) * s1 + pltpu.roll(xb, ROT_DIM // 2, 1) * s2)
    return outs[0] if len(outs) == 1 else jnp.concatenate(outs, axis=1)


def _iota(shape, dim):
    return lax.broadcasted_iota(jnp.int32, shape, dim)


def _project(hn, wq_ref, wkv_ref, ropec, ropes1, ropes2):
    q = _rope(_nn(hn, wq_ref[...]), ropec, ropes1, ropes2) * (HEAD_DIM ** -0.5)
    kv = _nn(hn, wkv_ref[...])
    k = _rope(kv[:, :KV_WIDTH], ropec, ropes1, ropes2)
    v = kv[:, KV_WIDTH:]
    return q, k, v


def _dt_both(hn, wdt_ref, wdtT_ref, dtb_row_ref, dtb_col_ref):
    dtc = _softplus(_nn(hn, wdt_ref[...]) + dtb_row_ref[...])
    dtr = _softplus(_nt(wdtT_ref[...], hn) + dtb_col_ref[...])
    return dtc, dtr


def _a_vectors(alog_row_ref, alog_col_ref):
    lane = _iota((1, LANES), 1)
    a_row = jnp.where(lane < SSM_HEADS, -jnp.exp(alog_row_ref[...]), 0.0)
    a_col = -jnp.exp(alog_col_ref[...])
    return a_row, a_col


def _ssd_intra(xdt, b_all, c_all, acol, arow, mask_bool):
    lane = _iota((1, LANES), 1)
    pieces = []
    for g in range(SSM_GROUPS):
        bb = b_all[:, g * D_STATE:(g + 1) * D_STATE].astype(BF16)
        cc = c_all[:, g * D_STATE:(g + 1) * D_STATE].astype(BF16)
        cb = _nt(cc, bb)
        for pr in range(SSM_HPG // 2):
            e0 = g * SSM_HPG + 2 * pr
            xpair = xdt[:, e0 * SSM_HEAD_DIM:(e0 + 2) * SSM_HEAD_DIM]
            acc = None
            for hh in range(2):
                e = e0 + hh
                seg = acol[:, e:e + 1] - arow[e:e + 1, :]
                w = (cb * jnp.exp(jnp.where(mask_bool, seg, NEG))).astype(BF16)
                xm = jnp.where((lane // SSM_HEAD_DIM) == hh, xpair, 0.0).astype(BF16)
                t = _nn(w, xm)
                acc = t if acc is None else acc + t
            pieces.append(acc)
    return jnp.concatenate(pieces, axis=1)


def _gate_and_out(y, xs, z, x_in, dskip_ref, gssm_ref, wout_ref, gpost_ref, mix_s):
    y = y + dskip_ref[...] * xs
    gated = y * _silu(z)
    for g in range(SSM_GROUPS):
        gg = gated[:, g * SSM_GROUP_W:(g + 1) * SSM_GROUP_W]
        ms = jnp.mean(gg * gg, axis=-1, keepdims=True)
        o = gg * lax.rsqrt(ms + EPS) * gssm_ref[:, g * SSM_GROUP_W:(g + 1) * SSM_GROUP_W]
        mix_s[:, ATT_WIDTH + g * SSM_GROUP_W:ATT_WIDTH + (g + 1) * SSM_GROUP_W] = o.astype(mix_s.dtype)
    mo = _nn(mix_s[...].astype(BF16), wout_ref[...])
    return x_in + _rms(mo, gpost_ref[...])


def _prompt_kernel(x_ref, gpre_ref, wq_ref, wkv_ref, wz_ref, wxbc_ref, wdt_ref, wdtT_ref,
                   ropec_ref, ropes1_ref, ropes2_ref, sink_ref,
                   convw_ref, convb_ref, dtb_row_ref, dtb_col_ref, alog_row_ref, alog_col_ref,
                   dskip_ref, gssm_ref, wout_ref, gpost_ref, expand_ref,
                   x1_ref, nk_ref, nv_ref, nconv_ref, nssm_ref,
                   hn_s, q_s, kbuf, vbuf, xbc_s, xc_s, dtc_s, dtr_s, y_s, mix_s, hT_s):
    tm = x_ref.shape[0]
    nchunk = tm // CHUNK
    s = pl.program_id(1)
    last = pl.num_programs(1) - 1

    @pl.when(s == 0)
    def _():
        kbuf[0:CHUNK, :] = jnp.zeros((CHUNK, KV_WIDTH), BF16)
        vbuf[0:CHUNK, :] = jnp.zeros((CHUNK, KV_WIDTH), BF16)
        xbc_s[0:SUBLANES, :] = jnp.zeros((SUBLANES, CONV_DIM), F32)
        hT_s[...] = jnp.zeros_like(hT_s)

    xt = x_ref[...]
    hn = _rms(xt, gpre_ref[...]).astype(BF16)
    hn_s[...] = hn

    q, k, v = _project(hn, wq_ref, wkv_ref, ropec_ref[...], ropes1_ref[...], ropes2_ref[...])
    q_s[...] = q.astype(BF16)
    kbuf[CHUNK:CHUNK + tm, :] = k.astype(BF16)
    vbuf[CHUNK:CHUNK + tm, :] = v.astype(BF16)

    xbc = _nn(hn, wxbc_ref[...])
    xbc_s[SUBLANES:SUBLANES + tm, :] = xbc

    @pl.when(s == last)
    def _():
        nk_ref[...] = k[tm - WINDOW:, :]
        nv_ref[...] = v[tm - WINDOW:, :]
        nconv_ref[...] = xbc[tm - (CONV_W - 1):, :]

    acc = convb_ref[...] + xbc_s[SUBLANES - 3:SUBLANES - 3 + tm, :] * convw_ref[0:1, :]
    acc = acc + xbc_s[SUBLANES - 2:SUBLANES - 2 + tm, :] * convw_ref[1:2, :]
    acc = acc + xbc_s[SUBLANES - 1:SUBLANES - 1 + tm, :] * convw_ref[2:3, :]
    acc = acc + xbc * convw_ref[3:4, :]
    xc_s[...] = _silu(acc)

    dtc, dtr = _dt_both(hn, wdt_ref, wdtT_ref, dtb_row_ref, dtb_col_ref)
    dtc_s[...] = dtc
    for j in range(nchunk):
        dtr_s[j] = dtr[:, j * CHUNK:(j + 1) * CHUNK]

    a_row, a_col = _a_vectors(alog_row_ref, alog_col_ref)

    r2 = _iota((CHUNK, CHUNK), 0)
    c2 = _iota((CHUNK, CHUNK), 1)
    tril = c2 <= r2
    tril_bf = tril.astype(BF16)
    qi = _iota((ATT_GROUP * CHUNK, 2 * CHUNK), 0) % CHUNK
    kj = _iota((ATT_GROUP * CHUNK, 2 * CHUNK), 1)
    band = jnp.logical_and(kj >= qi, kj <= qi + WINDOW)
    lane_kv = _iota((1, KV_WIDTH), 1) // HEAD_DIM
    expand = expand_ref[...]

    def chunk_body(c, carry):
        r0 = pl.multiple_of(c * CHUNK, CHUNK)
        kwin = kbuf[pl.ds(r0, 2 * CHUNK), :]
        vwin = vbuf[pl.ds(r0, 2 * CHUNK), :]
        first = jnp.logical_and(s == 0, c == 0)
        valid = jnp.logical_and(band, jnp.logical_or(kj >= CHUNK, jnp.logical_not(first)))
        ocomb = jnp.zeros((ATT_GROUP * CHUNK, KV_WIDTH), F32)
        for kvh in range(ATT_KV_HEADS):
            lm = lane_kv == kvh
            qm = jnp.concatenate(
                [jnp.where(lm, q_s[pl.ds(r0, CHUNK), g * KV_WIDTH:(g + 1) * KV_WIDTH], jnp.zeros((), BF16))
                 for g in range(ATT_GROUP)], axis=0)
            sc = jnp.where(valid, _nt(qm, kwin), NEG)
            sink = sink_ref[:, kvh:kvh + 1]
            m = jnp.maximum(jnp.max(sc, axis=1, keepdims=True), sink)
            p = jnp.exp(sc - m)
            den = jnp.sum(p, axis=1, keepdims=True) + jnp.exp(sink - m)
            o = _nn(p.astype(BF16), vwin) / den
            ocomb = ocomb + jnp.where(lm, o, 0.0)
        for g in range(ATT_GROUP):
            mix_s[pl.ds(r0, CHUNK), g * KV_WIDTH:(g + 1) * KV_WIDTH] = ocomb[g * CHUNK:(g + 1) * CHUNK].astype(BF16)

        dtc_c = dtc_s[pl.ds(r0, CHUNK), :]
        dtr_c = dtr_s[c]
        acol = _nn_split_rhs(tril_bf, dtc_c * a_row)
        arow = _nt_split_lhs(dtr_c * a_col, tril_bf)
        ea = jnp.exp(acol)
        tailc = jnp.exp(acol[CHUNK - 1:CHUNK, :] - acol) * dtc_c
        dtx = _nn_split_lhs(dtc_c, expand)
        eax = _nn_split_lhs(ea, expand)
        tlx = _nn_split_lhs(tailc, expand)
        xs = xc_s[pl.ds(r0, CHUNK), 0:SSM_WIDTH]
        b_all = xc_s[pl.ds(r0, CHUNK), SSM_WIDTH:SSM_WIDTH + SSM_GROUPS * D_STATE]
        c_all = xc_s[pl.ds(r0, CHUNK), SSM_WIDTH + SSM_GROUPS * D_STATE:CONV_DIM]
        ydiag = _ssd_intra(xs * dtx, b_all, c_all, acol, arow, tril)
        xtl = (xs * tlx).astype(BF16)
        for g in range(SSM_GROUPS):
            sl = slice(g * SSM_GROUP_W, (g + 1) * SSM_GROUP_W)
            hg = hT_s[:, sl]
            cc = c_all[:, g * D_STATE:(g + 1) * D_STATE].astype(BF16)
            yoff = _nn(cc, hg.astype(BF16)) * eax[:, sl]
            y_s[pl.ds(r0, CHUNK), sl] = ydiag[:, sl] + yoff
            bt = b_all[:, g * D_STATE:(g + 1) * D_STATE].T.astype(BF16)
            hT_s[:, sl] = hg * eax[CHUNK - 1:CHUNK, sl] + _nn(bt, xtl[:, sl])
        return carry

    lax.fori_loop(0, nchunk, chunk_body, 0)

    kbuf[0:CHUNK, :] = kbuf[tm:tm + CHUNK, :]
    vbuf[0:CHUNK, :] = vbuf[tm:tm + CHUNK, :]
    xbc_s[0:SUBLANES, :] = xbc_s[tm:tm + SUBLANES, :]

    @pl.when(s == last)
    def _():
        nssm_ref[...] = hT_s[...].T

    z = _nn(hn_s[...], wz_ref[...])
    x1_ref[...] = _gate_and_out(y_s[...], xc_s[:, 0:SSM_WIDTH], z, x_ref[...],
                                dskip_ref, gssm_ref, wout_ref, gpost_ref, mix_s)


def _sample_kernel(x_ref, ck_ref, cv_ref, sconv_ref, sssm_ref,
                   gpre_ref, wq_ref, wkv_ref, wz_ref, wxbc_ref, wdt_ref, wdtT_ref,
                   ropec_ref, ropes1_ref, ropes2_ref, sink_ref,
                   convw_ref, convb_ref, dtb_row_ref, dtb_col_ref, alog_row_ref, alog_col_ref,
                   dskip_ref, gssm_ref, wout_ref, gpost_ref, expand_ref,
                   x1_ref, nk_ref, nv_ref, nconv_ref, nssm_ref,
                   hn_s, q_s, kn_s, vn_s, xpad_s, xc_s, ea_s, eax_s, xT_s, y_s, mix_s):
    bt_n = sconv_ref.shape[0]
    m_rows = x_ref.shape[0]
    tdec = m_rows // bt_n
    b = pl.program_id(1)

    @pl.when(b == 0)
    def _():
        hn = _rms(x_ref[...], gpre_ref[...]).astype(BF16)
        hn_s[...] = hn
        q, k, v = _project(hn, wq_ref, wkv_ref, ropec_ref[...], ropes1_ref[...], ropes2_ref[...])
        q_s[...] = q
        kn_s[...] = k
        vn_s[...] = v

        xbc = _nn(hn, wxbc_ref[...])
        xpad_s[:, SUBLANES:2 * SUBLANES, :] = xbc.reshape(bt_n, tdec, CONV_DIM)
        xpad_s[:, SUBLANES - 3:SUBLANES, :] = sconv_ref[...]
        nconv_ref[...] = xpad_s[:, 2 * SUBLANES - 3:2 * SUBLANES, :]
        acc = convb_ref[...] + xpad_s[:, 5:13, :].reshape(m_rows, CONV_DIM) * convw_ref[0:1, :]
        acc = acc + xpad_s[:, 6:14, :].reshape(m_rows, CONV_DIM) * convw_ref[1:2, :]
        acc = acc + xpad_s[:, 7:15, :].reshape(m_rows, CONV_DIM) * convw_ref[2:3, :]
        acc = acc + xbc * convw_ref[3:4, :]
        xc = _silu(acc)
        xc_s[...] = xc
        xs = xc[:, 0:SSM_WIDTH]
        b_all = xc[:, SSM_WIDTH:SSM_WIDTH + SSM_GROUPS * D_STATE]
        c_all = xc[:, SSM_WIDTH + SSM_GROUPS * D_STATE:CONV_DIM]

        dtc, dtr = _dt_both(hn, wdt_ref, wdtT_ref, dtb_row_ref, dtb_col_ref)
        a_row, a_col = _a_vectors(alog_row_ref, alog_col_ref)

        r2 = _iota((m_rows, m_rows), 0)
        c2 = _iota((m_rows, m_rows), 1)
        same = (r2 // tdec) == (c2 // tdec)
        causal = jnp.logical_and(same, c2 <= r2)
        causal_bf = causal.astype(BF16)
        same_bf = same.astype(BF16)
        expand = expand_ref[...]

        dac = dtc * a_row
        acol = _nn_split_rhs(causal_bf, dac)
        alast = _nn_split_rhs(same_bf, dac)
        arow = _nt_split_lhs(dtr * a_col, causal_bf)
        tailc = jnp.exp(alast - acol) * dtc
        dtx = _nn_split_lhs(dtc, expand)
        eax = _nn_split_lhs(jnp.exp(acol), expand)
        tlx = _nn_split_lhs(tailc, expand)
        ea_s[...] = jnp.exp(alast)
        eax_s[...] = eax
        y_s[...] = _ssd_intra(xs * dtx, b_all, c_all, acol, arow, causal)
        xtl = xs * tlx
        for j in range(SSM_WIDTH // LANES):
            xT_s[j * LANES:(j + 1) * LANES, :] = xtl[:, j * LANES:(j + 1) * LANES].T.astype(BF16)

    r = pl.multiple_of(b * tdec, tdec)
    lane_kv = _iota((1, KV_WIDTH), 1) // HEAD_DIM
    row_t = _iota((m_rows, m_rows), 0) % tdec
    col_j = _iota((m_rows, m_rows), 1)
    valid_c = col_j >= row_t
    valid_n = jnp.logical_and((col_j // tdec) == b, (col_j % tdec) <= row_t)
    sink = sink_ref[...]

    q8 = q_s[pl.ds(r, tdec), :]
    qbd = jnp.concatenate(
        [jnp.where(lane_kv == kvh, q8[:, g * KV_WIDTH:(g + 1) * KV_WIDTH], 0.0)
         for g in range(ATT_GROUP) for kvh in range(ATT_KV_HEADS)], axis=0).astype(BF16)
    kc = ck_ref[...]
    vc = cv_ref[...]
    sc_c = jnp.where(valid_c, _nt(qbd, kc.astype(BF16)), NEG)
    sc_n = jnp.where(valid_n, _nt(qbd, kn_s[...].astype(BF16)), NEG)
    m = jnp.maximum(jnp.maximum(jnp.max(sc_c, axis=1, keepdims=True), jnp.max(sc_n, axis=1, keepdims=True)), sink)
    p_c = jnp.exp(sc_c - m)
    p_n = jnp.exp(sc_n - m)
    den = jnp.sum(p_c, axis=1, keepdims=True) + jnp.sum(p_n, axis=1, keepdims=True) + jnp.exp(sink - m)
    o = (_nn(p_c.astype(BF16), vc.astype(BF16)) + _nn(p_n.astype(BF16), vn_s[...].astype(BF16))) / den
    for g in range(ATT_GROUP):
        og = None
        for kvh in range(ATT_KV_HEADS):
            i0 = (g * ATT_KV_HEADS + kvh) * tdec
            t = jnp.where(lane_kv == kvh, o[i0:i0 + tdec, :], 0.0)
            og = t if og is None else og + t
        mix_s[pl.ds(r, tdec), g * KV_WIDTH:(g + 1) * KV_WIDTH] = og
    nk_ref[0:WINDOW - tdec, :] = kc[tdec:, :]
    nk_ref[WINDOW - tdec:WINDOW, :] = kn_s[pl.ds(r, tdec), :]
    nv_ref[0:WINDOW - tdec, :] = vc[tdec:, :]
    nv_ref[WINDOW - tdec:WINDOW, :] = vn_s[pl.ds(r, tdec), :]

    in_b = (_iota((m_rows, 1), 0) // tdec) == b
    ea_b = ea_s[pl.ds(r, 1), :]
    for g in range(SSM_GROUPS):
        sl = slice(g * SSM_GROUP_W, (g + 1) * SSM_GROUP_W)
        hg = sssm_ref[sl, :]
        b_g = xc_s[:, SSM_WIDTH + g * D_STATE:SSM_WIDTH + (g + 1) * D_STATE]
        bm = jnp.where(in_b, b_g, 0.0).astype(BF16)
        upd = _nn(xT_s[sl, :], bm)
        dec = jnp.concatenate(
            [jnp.broadcast_to(ea_b[:, g * SSM_HPG + e:g * SSM_HPG + e + 1], (SSM_HEAD_DIM, D_STATE))
             for e in range(SSM_HPG)], axis=0)
        nssm_ref[sl, :] = hg * dec + upd
        c0 = SSM_WIDTH + (SSM_GROUPS + g) * D_STATE
        cc = xc_s[pl.ds(r, tdec), c0:c0 + D_STATE].astype(BF16)
        yoff = _nt(cc, hg.astype(BF16)) * eax_s[pl.ds(r, tdec), sl]
        y_s[pl.ds(r, tdec), sl] = y_s[pl.ds(r, tdec), sl] + yoff

    @pl.when(b == bt_n - 1)
    def _():
        z = _nn(hn_s[...], wz_ref[...])
        x1_ref[...] = _gate_and_out(y_s[...], xc_s[:, 0:SSM_WIDTH], z, x_ref[...],
                                    dskip_ref, gssm_ref, wout_ref, gpost_ref, mix_s)


def _ffn_kernel(x_ref, gpre_ref, wg_ref, wu_ref, wd_ref, gpost_ref, o_ref):
    x = x_ref[...]
    f = _rms(x, gpre_ref[...]).astype(BF16)
    d_ff = wg_ref.shape[1]
    acc = jnp.zeros(x.shape, F32)
    for j in range(d_ff // FF_CHUNK):
        sl = slice(j * FF_CHUNK, (j + 1) * FF_CHUNK)
        gate = _nn(f, wg_ref[:, sl])
        up = _nn(f, wu_ref[:, sl])
        acc = acc + _nn((_silu(gate) * up).astype(BF16), wd_ref[sl, :])
    o_ref[...] = x + _rms(acc, gpost_ref[...])


def _const_spec(shape):
    nd = len(shape)
    return pl.BlockSpec(shape, lambda *_: (0,) * nd, pipeline_mode=pl.Buffered(1))


def _rope_tables(pos):
    half = ROT_DIM // 2
    inv = ROPE_THETA ** (-jnp.arange(half, dtype=F32) * 2.0 / ROT_DIM)
    ang = pos.astype(F32)[:, None] * inv[None, :]
    cos = jnp.cos(ang)
    sin = jnp.sin(ang)
    n = pos.shape[0]
    pad = HEAD_DIM - ROT_DIM
    c = jnp.concatenate([cos, cos, jnp.ones((n, pad), F32)], axis=1)
    s1 = jnp.concatenate([-sin, jnp.zeros((n, half + pad), F32)], axis=1)
    s2 = jnp.concatenate([jnp.zeros((n, half), F32), sin, jnp.zeros((n, pad), F32)], axis=1)
    rep = LANES // HEAD_DIM
    return jnp.tile(c, (1, rep)), jnp.tile(s1, (1, rep)), jnp.tile(s2, (1, rep))


def _layer_params(g_pre_mix, w_in, attn_sinks, conv_w, conv_b, dt_bias, a_log, d_skip, g_ssm_out, w_out, g_post_mix):
    qperm = np.arange(ATT_WIDTH).reshape(ATT_KV_HEADS, ATT_GROUP, HEAD_DIM).transpose(1, 0, 2).reshape(-1)
    o = 0
    wq = w_in[:, o:o + ATT_WIDTH][:, qperm].astype(BF16); o += ATT_WIDTH
    wkv = w_in[:, o:o + 2 * KV_WIDTH].astype(BF16); o += 2 * KV_WIDTH
    wz = w_in[:, o:o + SSM_WIDTH].astype(BF16); o += SSM_WIDTH
    wxbc = w_in[:, o:o + CONV_DIM].astype(BF16); o += CONV_DIM
    wdt_f = w_in[:, o:o + SSM_HEADS]
    wdt = jnp.pad(wdt_f, ((0, 0), (0, LANES - SSM_HEADS))).astype(BF16)
    wdtT = wdt_f.T.astype(BF16)
    wout = jnp.concatenate([w_out[:ATT_WIDTH][qperm], w_out[ATT_WIDTH:]], axis=0).astype(BF16)
    pad16 = ((0, 0), (0, LANES - SSM_HEADS))
    expand = (np.arange(LANES)[:, None] == (np.arange(SSM_WIDTH)[None, :] // SSM_HEAD_DIM)).astype(np.float32)
    return dict(
        gpre=g_pre_mix.reshape(1, D_MODEL), wq=wq, wkv=wkv, wz=wz, wxbc=wxbc, wdt=wdt, wdtT=wdtT,
        convw=conv_w, convb=conv_b.reshape(1, CONV_DIM),
        dtb_row=jnp.pad(dt_bias.reshape(1, SSM_HEADS), pad16), dtb_col=dt_bias.reshape(SSM_HEADS, 1),
        alog_row=jnp.pad(a_log.reshape(1, SSM_HEADS), pad16), alog_col=a_log.reshape(SSM_HEADS, 1),
        dskip=jnp.repeat(d_skip, SSM_HEAD_DIM).reshape(1, SSM_WIDTH), gssm=g_ssm_out.reshape(1, SSM_WIDTH),
        wout=wout, gpost=g_post_mix.reshape(1, D_MODEL), expand=jnp.asarray(expand, BF16),
        sinks_gk=attn_sinks.reshape(ATT_KV_HEADS, ATT_GROUP).T,
    )


_WEIGHT_ORDER = ("gpre", "wq", "wkv", "wz", "wxbc", "wdt", "wdtT")
_TAIL_ORDER = ("convw", "convb", "dtb_row", "dtb_col", "alog_row", "alog_col",
               "dskip", "gssm", "wout", "gpost", "expand")


def _prompt_mixer(x, p):
    bsz, seq, _ = x.shape
    tm = SEQ_TILE
    nchunk = tm // CHUNK
    c, s1, s2 = _rope_tables(jnp.arange(seq, dtype=jnp.int32))
    sink = jnp.repeat(p["sinks_gk"], CHUNK, axis=0)
    consts = [p[n] for n in _WEIGHT_ORDER]
    tail = [p[n] for n in _TAIL_ORDER]
    in_specs = ([pl.BlockSpec((None, tm, D_MODEL), lambda b, s: (b, s, 0))]
                + [_const_spec(a.shape) for a in consts]
                + [pl.BlockSpec((tm, LANES), lambda b, s: (s, 0))] * 3
                + [_const_spec(sink.shape)]
                + [_const_spec(a.shape) for a in tail])
    out_shape = (
        jax.ShapeDtypeStruct((bsz, seq, D_MODEL), F32),
        jax.ShapeDtypeStruct((bsz, WINDOW, KV_WIDTH), F32),
        jax.ShapeDtypeStruct((bsz, WINDOW, KV_WIDTH), F32),
        jax.ShapeDtypeStruct((bsz, CONV_W - 1, CONV_DIM), F32),
        jax.ShapeDtypeStruct((bsz, SSM_WIDTH, D_STATE), F32),
    )
    out_specs = (
        pl.BlockSpec((None, tm, D_MODEL), lambda b, s: (b, s, 0)),
        pl.BlockSpec((None, WINDOW, KV_WIDTH), lambda b, s: (b, 0, 0)),
        pl.BlockSpec((None, WINDOW, KV_WIDTH), lambda b, s: (b, 0, 0)),
        pl.BlockSpec((None, CONV_W - 1, CONV_DIM), lambda b, s: (b, 0, 0)),
        pl.BlockSpec((None, SSM_WIDTH, D_STATE), lambda b, s: (b, 0, 0)),
    )
    scratch = [
        pltpu.VMEM((tm, D_MODEL), BF16),
        pltpu.VMEM((tm, ATT_WIDTH), BF16),
        pltpu.VMEM((CHUNK + tm, KV_WIDTH), BF16),
        pltpu.VMEM((CHUNK + tm, KV_WIDTH), BF16),
        pltpu.VMEM((SUBLANES + tm, CONV_DIM), F32),
        pltpu.VMEM((tm, CONV_DIM), F32),
        pltpu.VMEM((tm, LANES), F32),
        pltpu.VMEM((nchunk, SSM_HEADS, CHUNK), F32),
        pltpu.VMEM((tm, SSM_WIDTH), F32),
        pltpu.VMEM((tm, MIX_WIDTH), BF16),
        pltpu.VMEM((D_STATE, SSM_WIDTH), F32),
    ]
    return pl.pallas_call(
        _prompt_kernel,
        grid=(bsz, seq // tm),
        in_specs=in_specs,
        out_specs=out_specs,
        out_shape=out_shape,
        scratch_shapes=scratch,
        compiler_params=pltpu.CompilerParams(
            dimension_semantics=("arbitrary", "arbitrary"), vmem_limit_bytes=VMEM_LIMIT_BYTES),
        name="prompt_mixer",
    )(x, *consts, c, s1, s2, sink, *tail)


def _sample_mixer(x, cache_k, cache_v, state_conv, state_ssm, p):
    nb, tdec, _ = x.shape
    bt = SAMPLE_BT
    rows = bt * tdec
    c, s1, s2 = _rope_tables(PAST_LEN + jnp.arange(tdec, dtype=jnp.int32))
    c, s1, s2 = (jnp.tile(t, (bt, 1)) for t in (c, s1, s2))
    sink = jnp.repeat(p["sinks_gk"].reshape(-1), tdec).reshape(rows, 1)
    consts = [p[n] for n in _WEIGHT_ORDER]
    tail = [p[n] for n in _TAIL_ORDER]
    x2 = x.reshape(nb * tdec, D_MODEL)
    ck = cache_k.reshape(nb, WINDOW, KV_WIDTH)
    cv = cache_v.reshape(nb, WINDOW, KV_WIDTH)
    ssm = state_ssm.reshape(nb, SSM_WIDTH, D_STATE)
    tmap = lambda i, j: (i, 0, 0)
    pmap = lambda i, j: (i * bt + j, 0, 0)
    in_specs = ([pl.BlockSpec((rows, D_MODEL), lambda i, j: (i, 0)),
                 pl.BlockSpec((None, WINDOW, KV_WIDTH), pmap),
                 pl.BlockSpec((None, WINDOW, KV_WIDTH), pmap),
                 pl.BlockSpec((bt, CONV_W - 1, CONV_DIM), tmap),
                 pl.BlockSpec((None, SSM_WIDTH, D_STATE), pmap)]
                + [_const_spec(a.shape) for a in consts]
                + [_const_spec(c.shape)] * 3
                + [_const_spec(sink.shape)]
                + [_const_spec(a.shape) for a in tail])
    out_shape = (
        jax.ShapeDtypeStruct((nb * tdec, D_MODEL), F32),
        jax.ShapeDtypeStruct((nb, WINDOW, KV_WIDTH), F32),
        jax.ShapeDtypeStruct((nb, WINDOW, KV_WIDTH), F32),
        jax.ShapeDtypeStruct((nb, CONV_W - 1, CONV_DIM), F32),
        jax.ShapeDtypeStruct((nb, SSM_WIDTH, D_STATE), F32),
    )
    out_specs = (
        pl.BlockSpec((rows, D_MODEL), lambda i, j: (i, 0)),
        pl.BlockSpec((None, WINDOW, KV_WIDTH), pmap),
        pl.BlockSpec((None, WINDOW, KV_WIDTH), pmap),
        pl.BlockSpec((bt, CONV_W - 1, CONV_DIM), tmap),
        pl.BlockSpec((None, SSM_WIDTH, D_STATE), pmap),
    )
    scratch = [
        pltpu.VMEM((rows, D_MODEL), BF16),
        pltpu.VMEM((rows, ATT_WIDTH), F32),
        pltpu.VMEM((rows, KV_WIDTH), F32),
        pltpu.VMEM((rows, KV_WIDTH), F32),
        pltpu.VMEM((bt, 2 * SUBLANES, CONV_DIM), F32),
        pltpu.VMEM((rows, CONV_DIM), F32),
        pltpu.VMEM((rows, LANES), F32),
        pltpu.VMEM((rows, SSM_WIDTH), F32),
        pltpu.VMEM((SSM_WIDTH, rows), BF16),
        pltpu.VMEM((rows, SSM_WIDTH), F32),
        pltpu.VMEM((rows, MIX_WIDTH), F32),
    ]
    return pl.pallas_call(
        _sample_kernel,
        grid=(nb // bt, bt),
        in_specs=in_specs,
        out_specs=out_specs,
        out_shape=out_shape,
        scratch_shapes=scratch,
        compiler_params=pltpu.CompilerParams(
            dimension_semantics=("arbitrary", "arbitrary"), vmem_limit_bytes=VMEM_LIMIT_BYTES),
        name="sample_mixer",
    )(x2, ck, cv, state_conv, ssm, *consts, c, s1, s2, sink, *tail)


def _ffn(x2, gpre, wg, wu, wd, gpost):
    n = x2.shape[0]
    tf = FFN_TILE
    consts = [gpre, wg, wu, wd, gpost]
    return pl.pallas_call(
        _ffn_kernel,
        grid=(n // tf,),
        in_specs=[pl.BlockSpec((tf, D_MODEL), lambda i: (i, 0))] + [_const_spec(a.shape) for a in consts],
        out_specs=pl.BlockSpec((tf, D_MODEL), lambda i: (i, 0)),
        out_shape=jax.ShapeDtypeStruct((n, D_MODEL), F32),
        compiler_params=pltpu.CompilerParams(
            dimension_semantics=("arbitrary",), vmem_limit_bytes=VMEM_LIMIT_BYTES),
        name="ffn",
    )(x2, *consts)


def kernel(x_prompt, x_sample, cache_k_win, cache_v_win, state_conv, state_ssm, g_pre_mix, w_in, attn_sinks, conv_w, conv_b, dt_bias, a_log, d_skip, g_ssm_out, w_out, g_post_mix, g_pre_ffn, w_gate, w_up, w_down, g_post_ffn):
    depth = w_in.shape[0]
    bp, lp, _ = x_prompt.shape
    nb, ts, _ = x_sample.shape
    hp, hs = x_prompt, x_sample
    outs = [[] for _ in range(8)]
    for l in range(depth):
        p = _layer_params(g_pre_mix[l], w_in[l], attn_sinks[l], conv_w[l], conv_b[l], dt_bias[l], a_log[l],
                          d_skip[l], g_ssm_out[l], w_out[l], g_post_mix[l])
        ffn_w = (g_pre_ffn[l].reshape(1, D_MODEL), w_gate[l].astype(BF16), w_up[l].astype(BF16),
                 w_down[l].astype(BF16), g_post_ffn[l].reshape(1, D_MODEL))
        x1p, kp, vp, cp, sp = _prompt_mixer(hp, p)
        x1s, ksm, vsm, csm, ssm = _sample_mixer(hs, cache_k_win[l], cache_v_win[l], state_conv[l], state_ssm[l], p)
        hp = _ffn(x1p.reshape(bp * lp, D_MODEL), *ffn_w).reshape(bp, lp, D_MODEL)
        hs = _ffn(x1s, *ffn_w).reshape(nb, ts, D_MODEL)
        kv_shape = (WINDOW, ATT_KV_HEADS, HEAD_DIM)
        ssm_shape = (SSM_HEADS, SSM_HEAD_DIM, D_STATE)
        for lst, val in zip(outs, (kp.reshape((bp,) + kv_shape), vp.reshape((bp,) + kv_shape), cp,
                                   sp.reshape((bp,) + ssm_shape),
                                   ksm.reshape((nb,) + kv_shape), vsm.reshape((nb,) + kv_shape), csm,
                                   ssm.reshape((nb,) + ssm_shape))):
            lst.append(val)
    return (hp, hs) + tuple(jnp.stack(o) for o in outs)
```

```python
import functools
import math

import numpy as np
import jax
import jax.numpy as jnp
from jax import lax
from jax.experimental import pallas as pl
from jax.experimental.pallas import tpu as pltpu

F32 = jnp.float32
BF16 = jnp.bfloat16

D_MODEL = 1024
ATT_HEADS = 16
ATT_KV_HEADS = 4
ATT_GROUP = ATT_HEADS // ATT_KV_HEADS
HEAD_DIM = 64
ATT_WIDTH = ATT_HEADS * HEAD_DIM
KV_WIDTH = ATT_KV_HEADS * HEAD_DIM
WINDOW = 128
ROT_DIM = HEAD_DIM // 4
ROPE_THETA = 500000.0
SSM_HEADS = 16
SSM_HEAD_DIM = 64
SSM_WIDTH = SSM_HEADS * SSM_HEAD_DIM
SSM_GROUPS = 2
SSM_HPG = SSM_HEADS // SSM_GROUPS
SSM_GROUP_W = SSM_WIDTH // SSM_GROUPS
D_STATE = 128
CONV_W = 4
CONV_DIM = SSM_WIDTH + 2 * SSM_GROUPS * D_STATE
MIX_WIDTH = ATT_WIDTH + SSM_WIDTH
EPS = 1e-6
PAST_LEN = 8192

LANES = 128
SUBLANES = 8
VMEM_LIMIT_BYTES = 60 * 1024 * 1024

CHUNK = 128
NEG = -1e30
SEQ_TILE = 512
SAMPLE_BT = 16
FFN_TILE = 512
FF_CHUNK = 256


def _nn(a, b):
    return jnp.dot(a, b, preferred_element_type=F32)


def _nt(a, b):
    return lax.dot_general(a, b, (((1,), (1,)), ((), ())), preferred_element_type=F32)


def _split_bf16(x, n):
    parts = []
    r = x
    for i in range(n):
        p = r.astype(BF16)
        parts.append(p)
        if i + 1 < n:
            r = r - p.astype(F32)
    return parts


def _expand_heads(x, expand2_ref):
    hi, mid = _split_bf16(x, 2)
    return _nn(jnp.concatenate([hi, mid], axis=1), expand2_ref[...])


def _cumsum_cols(m01, x):
    w = x.shape[1]
    r = _nn(m01, jnp.concatenate(_split_bf16(x, 3), axis=1))
    return r[:, :w] + r[:, w:2 * w] + r[:, 2 * w:]


def _cumsum_rows(x, m01):
    h = x.shape[0]
    r = _nt(jnp.concatenate(_split_bf16(x, 3), axis=0), m01)
    return r[:h] + r[h:2 * h] + r[2 * h:]


def _rms(x, g):
    ms = jnp.mean(x * x, axis=-1, keepdims=True)
    return x * lax.rsqrt(ms + EPS) * g


def _silu(x):
    return x / (1.0 + jnp.exp(-x))


def _softplus(x):
    return jnp.maximum(x, 0.0) + jnp.log1p(jnp.exp(-jnp.abs(x)))


def _rope(x, c, s1, s2):
    outs = []
    for j in range(x.shape[1] // LANES):
        xb = x[:, j * LANES:(j + 1) * LANES]
        outs.append(xb * c + pltpu.roll(xb, LANES - ROT_DIM // 2, 1) * s1 + pltpu.roll(xb, ROT_DIM // 2, 1) * s2)
    return outs[0] if len(outs) == 1 else jnp.concatenate(outs, axis=1)


def _iota(shape, dim):
    return lax.broadcasted_iota(jnp.int32, shape, dim)


def _project(hn, wq_ref, wkv_ref, ropec, ropes1, ropes2):
    q = _rope(_nn(hn, wq_ref[...]), ropec, ropes1, ropes2) * (HEAD_DIM ** -0.5)
    kv = _nn(hn, wkv_ref[...])
    k = _rope(kv[:, :KV_WIDTH], ropec, ropes1, ropes2)
    v = kv[:, KV_WIDTH:]
    return q, k, v


def _dt_both(hn, wdt_ref, wdtT_ref, dtb_row_ref, dtb_col_ref):
    dtc = _softplus(_nn(hn, wdt_ref[...]) + dtb_row_ref[...])
    dtr = _softplus(_nt(wdtT_ref[...], hn) + dtb_col_ref[...])
    return dtc, dtr


def _a_vectors(alog_row_ref, alog_col_ref):
    lane = _iota((1, LANES), 1)
    a_row = jnp.where(lane < SSM_HEADS, -jnp.exp(alog_row_ref[...]), 0.0)
    a_col = -jnp.exp(alog_col_ref[...])
    return a_row, a_col


def _ssd_block(xs_bf, b_all, c_all, acol, arow, dtr, mask_bool, hT_bf=None):
    quad = 4
    lane4 = _iota((1, quad * SSM_HEAD_DIM), 1) // SSM_HEAD_DIM
    zero = jnp.zeros((), BF16)
    ea = jnp.exp(acol) if hT_bf is not None else None
    pieces = []
    for g in range(SSM_GROUPS):
        cf = c_all[:, g * D_STATE:(g + 1) * D_STATE]
        cb = _nt(cf.astype(BF16), b_all[:, g * D_STATE:(g + 1) * D_STATE].astype(BF16))
        for qd in range(SSM_HPG // quad):
            e0 = g * SSM_HPG + quad * qd
            lanes = slice(e0 * SSM_HEAD_DIM, (e0 + quad) * SSM_HEAD_DIM)
            xq = xs_bf[:, lanes]
            lhs, rhs = [], []
            for i in range(quad):
                e = e0 + i
                seg = acol[:, e:e + 1] - arow[e:e + 1, :]
                w = cb * jnp.exp(jnp.where(mask_bool, seg, NEG)) * dtr[e:e + 1, :]
                lhs.append(w.astype(BF16))
                rhs.append(jnp.where(lane4 == i, xq, zero))
                if hT_bf is not None:
                    lhs.append((cf * ea[:, e:e + 1]).astype(BF16))
                    rhs.append(jnp.where(lane4 == i, hT_bf[:, lanes], zero))
            pieces.append(_nn(jnp.concatenate(lhs, axis=1), jnp.concatenate(rhs, axis=0)))
    return jnp.concatenate(pieces, axis=1)


def _gate_and_out(y, xs, z, x_in, dskip_ref, gssm_ref, wout_ref, gpost_ref, mix_s):
    y = y + dskip_ref[...] * xs
    gated = y * _silu(z)
    for g in range(SSM_GROUPS):
        gg = gated[:, g * SSM_GROUP_W:(g + 1) * SSM_GROUP_W]
        ms = jnp.mean(gg * gg, axis=-1, keepdims=True)
        o = gg * lax.rsqrt(ms + EPS) * gssm_ref[:, g * SSM_GROUP_W:(g + 1) * SSM_GROUP_W]
        mix_s[:, ATT_WIDTH + g * SSM_GROUP_W:ATT_WIDTH + (g + 1) * SSM_GROUP_W] = o.astype(mix_s.dtype)
    mo = _nn(mix_s[...].astype(BF16), wout_ref[...])
    return x_in + _rms(mo, gpost_ref[...])


def _prompt_kernel(x_ref, gpre_ref, wq_ref, wkv_ref, wz_ref, wxbc_ref, wdt_ref, wdtT_ref,
                   ropec_ref, ropes1_ref, ropes2_ref, sink_ref, wvT_ref, biasT_ref,
                   convw_ref, convb_ref, dtb_row_ref, dtb_col_ref, alog_row_ref, alog_col_ref,
                   dskip_ref, gssm_ref, wout_ref, gpost_ref, expand2_ref,
                   x1_ref, nk_ref, nv_ref, nconv_ref, nssm_ref,
                   hn_s, q_s, kbuf, vT_s, xbc_s, xc_s, dtc_s, dtr_s, y_s, mix_s, hT_s):
    tm = x_ref.shape[0]
    nchunk = tm // CHUNK
    s = pl.program_id(1)
    last = pl.num_programs(1) - 1

    @pl.when(s == 0)
    def _():
        kbuf[0:CHUNK, :] = jnp.zeros((CHUNK, KV_WIDTH), BF16)
        vT_s[0] = jnp.zeros((KV_WIDTH, CHUNK), BF16)
        xbc_s[0:SUBLANES, :] = jnp.zeros((SUBLANES, CONV_DIM), F32)
        hT_s[...] = jnp.zeros_like(hT_s)

    xt = x_ref[...]
    hn = _rms(xt, gpre_ref[...]).astype(BF16)
    hn_s[...] = hn

    ropec, ropes1, ropes2 = ropec_ref[...], ropes1_ref[...], ropes2_ref[...]
    q_s[...] = (_rope(_nn(hn, wq_ref[...]), ropec, ropes1, ropes2) * (HEAD_DIM ** -0.5)).astype(BF16)
    k = _rope(_nn(hn, wkv_ref[:, 0:KV_WIDTH]), ropec, ropes1, ropes2)
    kbuf[CHUNK:CHUNK + tm, :] = k.astype(BF16)
    vT = _nt(wvT_ref[...], hn).astype(BF16)
    for j in range(nchunk):
        vT_s[1 + j] = vT[:, j * CHUNK:(j + 1) * CHUNK]

    xbc = _nn(hn, wxbc_ref[...])
    xbc_s[SUBLANES:SUBLANES + tm, :] = xbc

    @pl.when(s == last)
    def _():
        nk_ref[...] = k[tm - WINDOW:, :]
        nv_ref[...] = _nn(hn[tm - WINDOW:, :], wkv_ref[:, KV_WIDTH:2 * KV_WIDTH])
        nconv_ref[...] = xbc[tm - (CONV_W - 1):, :]

    acc = convb_ref[...] + xbc_s[SUBLANES - 3:SUBLANES - 3 + tm, :] * convw_ref[0:1, :]
    acc = acc + xbc_s[SUBLANES - 2:SUBLANES - 2 + tm, :] * convw_ref[1:2, :]
    acc = acc + xbc_s[SUBLANES - 1:SUBLANES - 1 + tm, :] * convw_ref[2:3, :]
    acc = acc + xbc * convw_ref[3:4, :]
    xc_s[...] = _silu(acc)

    dtc, dtr = _dt_both(hn, wdt_ref, wdtT_ref, dtb_row_ref, dtb_col_ref)
    dtc_s[...] = dtc
    for j in range(nchunk):
        dtr_s[j] = dtr[:, j * CHUNK:(j + 1) * CHUNK]

    a_row, a_col = _a_vectors(alog_row_ref, alog_col_ref)

    r2 = _iota((CHUNK, CHUNK), 0)
    c2 = _iota((CHUNK, CHUNK), 1)
    tril = c2 <= r2
    tril_bf = tril.astype(BF16)
    lane_kv = _iota((1, KV_WIDTH), 1) // HEAD_DIM
    ones_rows = jnp.ones((2 * SUBLANES, 2 * CHUNK), BF16)

    def chunk_body(c, carry):
        r0 = pl.multiple_of(c * CHUNK, CHUNK)
        first = jnp.logical_and(s == 0, c == 0).astype(jnp.int32)
        bias = biasT_ref[first]
        qcat = jnp.concatenate([q_s[pl.ds(r0, CHUNK), g * KV_WIDTH:(g + 1) * KV_WIDTH]
                                for g in range(ATT_GROUP)], axis=0)
        kwin = kbuf[pl.ds(r0, 2 * CHUNK), :]
        kstack = jnp.concatenate([jnp.where(lane_kv == kvh, kwin, jnp.zeros((), BF16))
                                  for kvh in range(ATT_KV_HEADS)], axis=0)
        sT = _nt(kstack, qcat)
        vT_win = jnp.concatenate([vT_s[c], vT_s[c + 1]], axis=1)
        o_rows = []
        for kvh in range(ATT_KV_HEADS):
            blk = sT[kvh * 2 * CHUNK:(kvh + 1) * 2 * CHUNK] + bias
            sink = sink_ref[kvh]
            m = jnp.maximum(jnp.max(blk, axis=0, keepdims=True), sink)
            p = jnp.exp(blk - m).astype(BF16)
            lhs = jnp.concatenate([vT_win[kvh * HEAD_DIM:(kvh + 1) * HEAD_DIM], ones_rows], axis=0)
            oT = _nn(lhs, p)
            den = oT[HEAD_DIM:HEAD_DIM + 1] + jnp.exp(sink - m)
            o_rows.append(oT[:HEAD_DIM] * (1.0 / den))
        oT_all = jnp.concatenate(o_rows, axis=0)
        for g in range(ATT_GROUP):
            mix_s[pl.ds(r0, CHUNK), g * KV_WIDTH:(g + 1) * KV_WIDTH] = (
                oT_all[:, g * CHUNK:(g + 1) * CHUNK].T.astype(BF16))

        dtc_c = dtc_s[pl.ds(r0, CHUNK), :]
        dtr_c = dtr_s[c]
        acol = _cumsum_cols(tril_bf, dtc_c * a_row)
        arow = _cumsum_rows(dtr_c * a_col, tril_bf)
        a_end = acol[CHUNK - 1:CHUNK, :]
        tailc = jnp.exp(a_end - acol) * dtc_c
        ex = _expand_heads(
            jnp.concatenate([tailc, jnp.broadcast_to(jnp.exp(a_end), (SUBLANES, LANES))], axis=0), expand2_ref)
        tlx = ex[:CHUNK]
        dec_row = ex[CHUNK:CHUNK + 1]
        xs = xc_s[pl.ds(r0, CHUNK), 0:SSM_WIDTH]
        b_all = xc_s[pl.ds(r0, CHUNK), SSM_WIDTH:SSM_WIDTH + SSM_GROUPS * D_STATE]
        c_all = xc_s[pl.ds(r0, CHUNK), SSM_WIDTH + SSM_GROUPS * D_STATE:CONV_DIM]
        hT = hT_s[...]
        y_s[pl.ds(r0, CHUNK), :] = _ssd_block(xs.astype(BF16), b_all, c_all, acol, arow, dtr_c, tril,
                                              hT.astype(BF16))
        xtl = (xs * tlx).astype(BF16)
        for g in range(SSM_GROUPS):
            sl = slice(g * SSM_GROUP_W, (g + 1) * SSM_GROUP_W)
            bt = b_all[:, g * D_STATE:(g + 1) * D_STATE].T.astype(BF16)
            hT_s[:, sl] = hT[:, sl] * dec_row[:, sl] + _nn(bt, xtl[:, sl])
        return carry

    lax.fori_loop(0, nchunk, chunk_body, 0)

    kbuf[0:CHUNK, :] = kbuf[tm:tm + CHUNK, :]
    vT_s[0] = vT_s[nchunk]
    xbc_s[0:SUBLANES, :] = xbc_s[tm:tm + SUBLANES, :]

    @pl.when(s == last)
    def _():
        nssm_ref[...] = hT_s[...].T

    z = _nn(hn_s[...], wz_ref[...])
    x1_ref[...] = _gate_and_out(y_s[...], xc_s[:, 0:SSM_WIDTH], z, x_ref[...],
                                dskip_ref, gssm_ref, wout_ref, gpost_ref, mix_s)


def _sample_kernel(x_ref, ck_ref, cv_ref, sconv_ref, sssm_ref,
                   gpre_ref, wq_ref, wkv_ref, wz_ref, wxbc_ref, wdt_ref, wdtT_ref,
                   ropec_ref, ropes1_ref, ropes2_ref, sink_ref,
                   convw_ref, convb_ref, dtb_row_ref, dtb_col_ref, alog_row_ref, alog_col_ref,
                   dskip_ref, gssm_ref, wout_ref, gpost_ref, expand2_ref,
                   x1_ref, nk_ref, nv_ref, nconv_ref, nssm_ref,
                   hn_s, q_s, kn_s, vn_s, xpad_s, xc_s, ea_s, eax_s, xT_s, y_s, mix_s):
    bt_n = sconv_ref.shape[0]
    m_rows = x_ref.shape[0]
    tdec = m_rows // bt_n
    b = pl.program_id(1)

    @pl.when(b == 0)
    def _():
        hn = _rms(x_ref[...], gpre_ref[...]).astype(BF16)
        hn_s[...] = hn
        q, k, v = _project(hn, wq_ref, wkv_ref, ropec_ref[...], ropes1_ref[...], ropes2_ref[...])
        q_s[...] = q
        kn_s[...] = k
        vn_s[...] = v

        xbc = _nn(hn, wxbc_ref[...])
        xpad_s[:, SUBLANES:2 * SUBLANES, :] = xbc.reshape(bt_n, tdec, CONV_DIM)
        xpad_s[:, SUBLANES - 3:SUBLANES, :] = sconv_ref[...]
        nconv_ref[...] = xpad_s[:, 2 * SUBLANES - 3:2 * SUBLANES, :]
        acc = convb_ref[...] + xpad_s[:, 5:13, :].reshape(m_rows, CONV_DIM) * convw_ref[0:1, :]
        acc = acc + xpad_s[:, 6:14, :].reshape(m_rows, CONV_DIM) * convw_ref[1:2, :]
        acc = acc + xpad_s[:, 7:15, :].reshape(m_rows, CONV_DIM) * convw_ref[2:3, :]
        acc = acc + xbc * convw_ref[3:4, :]
        xc = _silu(acc)
        xc_s[...] = xc
        xs = xc[:, 0:SSM_WIDTH]
        b_all = xc[:, SSM_WIDTH:SSM_WIDTH + SSM_GROUPS * D_STATE]
        c_all = xc[:, SSM_WIDTH + SSM_GROUPS * D_STATE:CONV_DIM]

        dtc, dtr = _dt_both(hn, wdt_ref, wdtT_ref, dtb_row_ref, dtb_col_ref)
        a_row, a_col = _a_vectors(alog_row_ref, alog_col_ref)

        r2 = _iota((m_rows, m_rows), 0)
        c2 = _iota((m_rows, m_rows), 1)
        same = (r2 // tdec) == (c2 // tdec)
        causal = jnp.logical_and(same, c2 <= r2)
        causal_bf = causal.astype(BF16)
        same_bf = same.astype(BF16)

        dac = dtc * a_row
        acol = _cumsum_cols(causal_bf, dac)
        alast = _cumsum_cols(same_bf, dac)
        arow = _cumsum_rows(dtr * a_col, causal_bf)
        tailc = jnp.exp(alast - acol) * dtc
        ex = _expand_heads(jnp.concatenate([jnp.exp(acol), tailc], axis=0), expand2_ref)
        ea_s[...] = jnp.exp(alast)
        eax_s[...] = ex[:m_rows]
        y_s[...] = _ssd_block(xs.astype(BF16), b_all, c_all, acol, arow, dtr, causal)
        xtl = xs * ex[m_rows:]
        for j in range(SSM_WIDTH // LANES):
            xT_s[j * LANES:(j + 1) * LANES, :] = xtl[:, j * LANES:(j + 1) * LANES].T.astype(BF16)

    r = pl.multiple_of(b * tdec, tdec)
    lane_kv = _iota((1, KV_WIDTH), 1) // HEAD_DIM
    row_t = _iota((m_rows, m_rows), 0) % tdec
    col_j = _iota((m_rows, m_rows), 1)
    valid_c = col_j >= row_t
    valid_n = jnp.logical_and((col_j // tdec) == b, (col_j % tdec) <= row_t)
    sink = sink_ref[...]

    q8 = q_s[pl.ds(r, tdec), :]
    qbd = jnp.concatenate(
        [jnp.where(lane_kv == kvh, q8[:, g * KV_WIDTH:(g + 1) * KV_WIDTH], 0.0)
         for g in range(ATT_GROUP) for kvh in range(ATT_KV_HEADS)], axis=0).astype(BF16)
    kc = ck_ref[...]
    vc = cv_ref[...]
    sc_c = jnp.where(valid_c, _nt(qbd, kc.astype(BF16)), NEG)
    sc_n = jnp.where(valid_n, _nt(qbd, kn_s[...].astype(BF16)), NEG)
    m = jnp.maximum(jnp.maximum(jnp.max(sc_c, axis=1, keepdims=True), jnp.max(sc_n, axis=1, keepdims=True)), sink)
    p_c = jnp.exp(sc_c - m)
    p_n = jnp.exp(sc_n - m)
    den = jnp.sum(p_c, axis=1, keepdims=True) + jnp.sum(p_n, axis=1, keepdims=True) + jnp.exp(sink - m)
    o = (_nn(p_c.astype(BF16), vc.astype(BF16)) + _nn(p_n.astype(BF16), vn_s[...].astype(BF16))) / den
    for g in range(ATT_GROUP):
        og = None
        for kvh in range(ATT_KV_HEADS):
            i0 = (g * ATT_KV_HEADS + kvh) * tdec
            t = jnp.where(lane_kv == kvh, o[i0:i0 + tdec, :], 0.0)
            og = t if og is None else og + t
        mix_s[pl.ds(r, tdec), g * KV_WIDTH:(g + 1) * KV_WIDTH] = og
    nk_ref[0:WINDOW - tdec, :] = kc[tdec:, :]
    nk_ref[WINDOW - tdec:WINDOW, :] = kn_s[pl.ds(r, tdec), :]
    nv_ref[0:WINDOW - tdec, :] = vc[tdec:, :]
    nv_ref[WINDOW - tdec:WINDOW, :] = vn_s[pl.ds(r, tdec), :]

    in_b = (_iota((m_rows, 1), 0) // tdec) == b
    ea_b = ea_s[pl.ds(r, 1), :]
    for g in range(SSM_GROUPS):
        sl = slice(g * SSM_GROUP_W, (g + 1) * SSM_GROUP_W)
        hg = sssm_ref[sl, :]
        b_g = xc_s[:, SSM_WIDTH + g * D_STATE:SSM_WIDTH + (g + 1) * D_STATE]
        bm = jnp.where(in_b, b_g, 0.0).astype(BF16)
        upd = _nn(xT_s[sl, :], bm)
        dec = jnp.concatenate(
            [jnp.broadcast_to(ea_b[:, g * SSM_HPG + e:g * SSM_HPG + e + 1], (SSM_HEAD_DIM, D_STATE))
             for e in range(SSM_HPG)], axis=0)
        nssm_ref[sl, :] = hg * dec + upd
        c0 = SSM_WIDTH + (SSM_GROUPS + g) * D_STATE
        cc = xc_s[pl.ds(r, tdec), c0:c0 + D_STATE].astype(BF16)
        yoff = _nt(cc, hg.astype(BF16)) * eax_s[pl.ds(r, tdec), sl]
        y_s[pl.ds(r, tdec), sl] = y_s[pl.ds(r, tdec), sl] + yoff

    @pl.when(b == bt_n - 1)
    def _():
        z = _nn(hn_s[...], wz_ref[...])
        x1_ref[...] = _gate_and_out(y_s[...], xc_s[:, 0:SSM_WIDTH], z, x_ref[...],
                                    dskip_ref, gssm_ref, wout_ref, gpost_ref, mix_s)


def _ffn_kernel(x_ref, gpre_ref, wg_ref, wu_ref, wd_ref, gpost_ref, o_ref):
    x = x_ref[...]
    f = _rms(x, gpre_ref[...]).astype(BF16)
    d_ff = wg_ref.shape[1]
    acc = jnp.zeros(x.shape, F32)
    for j in range(d_ff // FF_CHUNK):
        sl = slice(j * FF_CHUNK, (j + 1) * FF_CHUNK)
        gate = _nn(f, wg_ref[:, sl])
        up = _nn(f, wu_ref[:, sl])
        acc = acc + _nn((_silu(gate) * up).astype(BF16), wd_ref[sl, :])
    o_ref[...] = x + _rms(acc, gpost_ref[...])


def _const_spec(shape):
    nd = len(shape)
    return pl.BlockSpec(shape, lambda *_: (0,) * nd, pipeline_mode=pl.Buffered(1))


def _rope_tables(pos):
    half = ROT_DIM // 2
    inv = ROPE_THETA ** (-jnp.arange(half, dtype=F32) * 2.0 / ROT_DIM)
    ang = pos.astype(F32)[:, None] * inv[None, :]
    cos = jnp.cos(ang)
    sin = jnp.sin(ang)
    n = pos.shape[0]
    pad = HEAD_DIM - ROT_DIM
    c = jnp.concatenate([cos, cos, jnp.ones((n, pad), F32)], axis=1)
    s1 = jnp.concatenate([-sin, jnp.zeros((n, half + pad), F32)], axis=1)
    s2 = jnp.concatenate([jnp.zeros((n, half), F32), sin, jnp.zeros((n, pad), F32)], axis=1)
    rep = LANES // HEAD_DIM
    return jnp.tile(c, (1, rep)), jnp.tile(s1, (1, rep)), jnp.tile(s2, (1, rep))


def _layer_params(g_pre_mix, w_in, attn_sinks, conv_w, conv_b, dt_bias, a_log, d_skip, g_ssm_out, w_out, g_post_mix):
    qperm = np.arange(ATT_WIDTH).reshape(ATT_KV_HEADS, ATT_GROUP, HEAD_DIM).transpose(1, 0, 2).reshape(-1)
    o = 0
    wq = w_in[:, o:o + ATT_WIDTH][:, qperm].astype(BF16); o += ATT_WIDTH
    wkv = w_in[:, o:o + 2 * KV_WIDTH].astype(BF16); o += 2 * KV_WIDTH
    wz = w_in[:, o:o + SSM_WIDTH].astype(BF16); o += SSM_WIDTH
    wxbc = w_in[:, o:o + CONV_DIM].astype(BF16); o += CONV_DIM
    wdt_f = w_in[:, o:o + SSM_HEADS]
    wdt = jnp.pad(wdt_f, ((0, 0), (0, LANES - SSM_HEADS))).astype(BF16)
    wdtT = wdt_f.T.astype(BF16)
    wvT = w_in[:, ATT_WIDTH + KV_WIDTH:ATT_WIDTH + 2 * KV_WIDTH].T.astype(BF16)
    wout = jnp.concatenate([w_out[:ATT_WIDTH][qperm], w_out[ATT_WIDTH:]], axis=0).astype(BF16)
    pad16 = ((0, 0), (0, LANES - SSM_HEADS))
    expand = (np.arange(LANES)[:, None] == (np.arange(SSM_WIDTH)[None, :] // SSM_HEAD_DIM)).astype(np.float32)
    return dict(
        gpre=g_pre_mix.reshape(1, D_MODEL), wq=wq, wkv=wkv, wz=wz, wxbc=wxbc, wdt=wdt, wdtT=wdtT, wvT=wvT,
        convw=conv_w, convb=conv_b.reshape(1, CONV_DIM),
        dtb_row=jnp.pad(dt_bias.reshape(1, SSM_HEADS), pad16), dtb_col=dt_bias.reshape(SSM_HEADS, 1),
        alog_row=jnp.pad(a_log.reshape(1, SSM_HEADS), pad16), alog_col=a_log.reshape(SSM_HEADS, 1),
        dskip=jnp.repeat(d_skip, SSM_HEAD_DIM).reshape(1, SSM_WIDTH), gssm=g_ssm_out.reshape(1, SSM_WIDTH),
        wout=wout, gpost=g_post_mix.reshape(1, D_MODEL), expand2=jnp.asarray(np.concatenate([expand, expand], axis=0), BF16),
        sinks_gk=attn_sinks.reshape(ATT_KV_HEADS, ATT_GROUP).T,
    )


_WEIGHT_ORDER = ("gpre", "wq", "wkv", "wz", "wxbc", "wdt", "wdtT")
_TAIL_ORDER = ("convw", "convb", "dtb_row", "dtb_col", "alog_row", "alog_col",
               "dskip", "gssm", "wout", "gpost", "expand2")


def _prompt_mixer(x, p):
    bsz, seq, _ = x.shape
    tm = SEQ_TILE
    nchunk = tm // CHUNK
    c, s1, s2 = _rope_tables(jnp.arange(seq, dtype=jnp.int32))
    sink = jnp.repeat(p["sinks_gk"].T, CHUNK, axis=1)[:, None, :]
    wvT = p["wvT"]
    jj = np.arange(2 * CHUNK)[:, None]
    tt = np.arange(CHUNK)[None, :]
    vis = (jj >= tt) & (jj <= tt + WINDOW)
    bias = np.stack([np.where(vis, 0.0, NEG), np.where(vis & (jj >= CHUNK), 0.0, NEG)]).astype(np.float32)
    bias = jnp.asarray(np.tile(bias, (1, 1, ATT_GROUP)))
    consts = [p[n] for n in _WEIGHT_ORDER]
    tail = [p[n] for n in _TAIL_ORDER]
    in_specs = ([pl.BlockSpec((None, tm, D_MODEL), lambda b, s: (b, s, 0))]
                + [_const_spec(a.shape) for a in consts]
                + [pl.BlockSpec((tm, LANES), lambda b, s: (s, 0))] * 3
                + [_const_spec(sink.shape), _const_spec(wvT.shape), _const_spec(bias.shape)]
                + [_const_spec(a.shape) for a in tail])
    out_shape = (
        jax.ShapeDtypeStruct((bsz, seq, D_MODEL), F32),
        jax.ShapeDtypeStruct((bsz, WINDOW, KV_WIDTH), F32),
        jax.ShapeDtypeStruct((bsz, WINDOW, KV_WIDTH), F32),
        jax.ShapeDtypeStruct((bsz, CONV_W - 1, CONV_DIM), F32),
        jax.ShapeDtypeStruct((bsz, SSM_WIDTH, D_STATE), F32),
    )
    out_specs = (
        pl.BlockSpec((None, tm, D_MODEL), lambda b, s: (b, s, 0)),
        pl.BlockSpec((None, WINDOW, KV_WIDTH), lambda b, s: (b, 0, 0)),
        pl.BlockSpec((None, WINDOW, KV_WIDTH), lambda b, s: (b, 0, 0)),
        pl.BlockSpec((None, CONV_W - 1, CONV_DIM), lambda b, s: (b, 0, 0)),
        pl.BlockSpec((None, SSM_WIDTH, D_STATE), lambda b, s: (b, 0, 0)),
    )
    scratch = [
        pltpu.VMEM((tm, D_MODEL), BF16),
        pltpu.VMEM((tm, ATT_WIDTH), BF16),
        pltpu.VMEM((CHUNK + tm, KV_WIDTH), BF16),
        pltpu.VMEM((nchunk + 1, KV_WIDTH, CHUNK), BF16),
        pltpu.VMEM((SUBLANES + tm, CONV_DIM), F32),
        pltpu.VMEM((tm, CONV_DIM), F32),
        pltpu.VMEM((tm, LANES), F32),
        pltpu.VMEM((nchunk, SSM_HEADS, CHUNK), F32),
        pltpu.VMEM((tm, SSM_WIDTH), F32),
        pltpu.VMEM((tm, MIX_WIDTH), BF16),
        pltpu.VMEM((D_STATE, SSM_WIDTH), F32),
    ]
    return pl.pallas_call(
        _prompt_kernel,
        grid=(bsz, seq // tm),
        in_specs=in_specs,
        out_specs=out_specs,
        out_shape=out_shape,
        scratch_shapes=scratch,
        compiler_params=pltpu.CompilerParams(
            dimension_semantics=("arbitrary", "arbitrary"), vmem_limit_bytes=VMEM_LIMIT_BYTES),
        name="prompt_mixer",
    )(x, *consts, c, s1, s2, sink, wvT, bias, *tail)


def _sample_mixer(x, cache_k, cache_v, state_conv, state_ssm, p):
    nb, tdec, _ = x.shape
    bt = SAMPLE_BT
    rows = bt * tdec
    c, s1, s2 = _rope_tables(PAST_LEN + jnp.arange(tdec, dtype=jnp.int32))
    c, s1, s2 = (jnp.tile(t, (bt, 1)) for t in (c, s1, s2))
    sink = jnp.repeat(p["sinks_gk"].reshape(-1), tdec).reshape(rows, 1)
    consts = [p[n] for n in _WEIGHT_ORDER]
    tail = [p[n] for n in _TAIL_ORDER]
    x2 = x.reshape(nb * tdec, D_MODEL)
    ck = cache_k.reshape(nb, WINDOW, KV_WIDTH)
    cv = cache_v.reshape(nb, WINDOW, KV_WIDTH)
    ssm = state_ssm.reshape(nb, SSM_WIDTH, D_STATE)
    tmap = lambda i, j: (i, 0, 0)
    pmap = lambda i, j: (i * bt + j, 0, 0)
    in_specs = ([pl.BlockSpec((rows, D_MODEL), lambda i, j: (i, 0)),
                 pl.BlockSpec((None, WINDOW, KV_WIDTH), pmap),
                 pl.BlockSpec((None, WINDOW, KV_WIDTH), pmap),
                 pl.BlockSpec((bt, CONV_W - 1, CONV_DIM), tmap),
                 pl.BlockSpec((None, SSM_WIDTH, D_STATE), pmap)]
                + [_const_spec(a.shape) for a in consts]
                + [_const_spec(c.shape)] * 3
                + [_const_spec(sink.shape)]
                + [_const_spec(a.shape) for a in tail])
    out_shape = (
        jax.ShapeDtypeStruct((nb * tdec, D_MODEL), F32),
        jax.ShapeDtypeStruct((nb, WINDOW, KV_WIDTH), F32),
        jax.ShapeDtypeStruct((nb, WINDOW, KV_WIDTH), F32),
        jax.ShapeDtypeStruct((nb, CONV_W - 1, CONV_DIM), F32),
        jax.ShapeDtypeStruct((nb, SSM_WIDTH, D_STATE), F32),
    )
    out_specs = (
        pl.BlockSpec((rows, D_MODEL), lambda i, j: (i, 0)),
        pl.BlockSpec((None, WINDOW, KV_WIDTH), pmap),
        pl.BlockSpec((None, WINDOW, KV_WIDTH), pmap),
        pl.BlockSpec((bt, CONV_W - 1, CONV_DIM), tmap),
        pl.BlockSpec((None, SSM_WIDTH, D_STATE), pmap),
    )
    scratch = [
        pltpu.VMEM((rows, D_MODEL), BF16),
        pltpu.VMEM((rows, ATT_WIDTH), F32),
        pltpu.VMEM((rows, KV_WIDTH), F32),
        pltpu.VMEM((rows, KV_WIDTH), F32),
        pltpu.VMEM((bt, 2 * SUBLANES, CONV_DIM), F32),
        pltpu.VMEM((rows, CONV_DIM), F32),
        pltpu.VMEM((rows, LANES), F32),
        pltpu.VMEM((rows, SSM_WIDTH), F32),
        pltpu.VMEM((SSM_WIDTH, rows), BF16),
        pltpu.VMEM((rows, SSM_WIDTH), F32),
        pltpu.VMEM((rows, MIX_WIDTH), F32),
    ]
    return pl.pallas_call(
        _sample_kernel,
        grid=(nb // bt, bt),
        in_specs=in_specs,
        out_specs=out_specs,
        out_shape=out_shape,
        scratch_shapes=scratch,
        compiler_params=pltpu.CompilerParams(
            dimension_semantics=("arbitrary", "arbitrary"), vmem_limit_bytes=VMEM_LIMIT_BYTES),
        name="sample_mixer",
    )(x2, ck, cv, state_conv, ssm, *consts, c, s1, s2, sink, *tail)


def _ffn(x2, gpre, wg, wu, wd, gpost):
    n = x2.shape[0]
    tf = FFN_TILE
    consts = [gpre, wg, wu, wd, gpost]
    return pl.pallas_call(
        _ffn_kernel,
        grid=(n // tf,),
        in_specs=[pl.BlockSpec((tf, D_MODEL), lambda i: (i, 0))] + [_const_spec(a.shape) for a in consts],
        out_specs=pl.BlockSpec((tf, D_MODEL), lambda i: (i, 0)),
        out_shape=jax.ShapeDtypeStruct((n, D_MODEL), F32),
        compiler_params=pltpu.CompilerParams(
            dimension_semantics=("arbitrary",), vmem_limit_bytes=VMEM_LIMIT_BYTES),
        name="ffn",
    )(x2, *consts)


def kernel(x_prompt, x_sample, cache_k_win, cache_v_win, state_conv, state_ssm, g_pre_mix, w_in, attn_sinks, conv_w, conv_b, dt_bias, a_log, d_skip, g_ssm_out, w_out, g_post_mix, g_pre_ffn, w_gate, w_up, w_down, g_post_ffn):
    depth = w_in.shape[0]
    bp, lp, _ = x_prompt.shape
    nb, ts, _ = x_sample.shape
    hp, hs = x_prompt, x_sample
    outs = [[] for _ in range(8)]
    for l in range(depth):
        p = _layer_params(g_pre_mix[l], w_in[l], attn_sinks[l], conv_w[l], conv_b[l], dt_bias[l], a_log[l],
                          d_skip[l], g_ssm_out[l], w_out[l], g_post_mix[l])
        ffn_w = (g_pre_ffn[l].reshape(1, D_MODEL), w_gate[l].astype(BF16), w_up[l].astype(BF16),
                 w_down[l].astype(BF16), g_post_ffn[l].reshape(1, D_MODEL))
        x1p, kp, vp, cp, sp = _prompt_mixer(hp, p)
        x1s, ksm, vsm, csm, ssm = _sample_mixer(hs, cache_k_win[l], cache_v_win[l], state_conv[l], state_ssm[l], p)
        hp = _ffn(x1p.reshape(bp * lp, D_MODEL), *ffn_w).reshape(bp, lp, D_MODEL)
        hs = _ffn(x1s, *ffn_w).reshape(nb, ts, D_MODEL)
        kv_shape = (WINDOW, ATT_KV_HEADS, HEAD_DIM)
        ssm_shape = (SSM_HEADS, SSM_HEAD_DIM, D_STATE)
        for lst, val in zip(outs, (kp.reshape((bp,) + kv_shape), vp.reshape((bp,) + kv_shape), cp,
                                   sp.reshape((bp,) + ssm_shape),
                                   ksm.reshape((nb,) + kv_shape), vsm.reshape((nb,) + kv_shape), csm,
                                   ssm.reshape((nb,) + ssm_shape))):
            lst.append(val)
    return (hp, hs) + tuple(jnp.stack(o) for o in outs)
```

```python
import functools
import math

import numpy as np
import jax
import jax.numpy as jnp
from jax import lax
from jax.experimental import pallas as pl
from jax.experimental.pallas import tpu as pltpu

F32 = jnp.float32
BF16 = jnp.bfloat16

D_MODEL = 1024
ATT_HEADS = 16
ATT_KV_HEADS = 4
ATT_GROUP = ATT_HEADS // ATT_KV_HEADS
HEAD_DIM = 64
ATT_WIDTH = ATT_HEADS * HEAD_DIM
KV_WIDTH = ATT_KV_HEADS * HEAD_DIM
WINDOW = 128
ROT_DIM = HEAD_DIM // 4
ROPE_THETA = 500000.0
SSM_HEADS = 16
SSM_HEAD_DIM = 64
SSM_WIDTH = SSM_HEADS * SSM_HEAD_DIM
SSM_GROUPS = 2
SSM_HPG = SSM_HEADS // SSM_GROUPS
SSM_GROUP_W = SSM_WIDTH // SSM_GROUPS
D_STATE = 128
CONV_W = 4
CONV_DIM = SSM_WIDTH + 2 * SSM_GROUPS * D_STATE
MIX_WIDTH = ATT_WIDTH + SSM_WIDTH
EPS = 1e-6
PAST_LEN = 8192

LANES = 128
SUBLANES = 8
VMEM_LIMIT_BYTES = 60 * 1024 * 1024

CHUNK = 128
NEG = -1e30
LOG2E = math.log2(math.e)
Q_SCALE = HEAD_DIM ** -0.5 * LOG2E
SEQ_TILE = 512
SAMPLE_BT = 16
SAMPLE_PB = 4
FFN_TILE = 512
FF_CHUNK = 256


def _nn(a, b):
    return jnp.dot(a, b, preferred_element_type=F32)


def _nt(a, b):
    return lax.dot_general(a, b, (((1,), (1,)), ((), ())), preferred_element_type=F32)


def _split_bf16(x, n):
    parts = []
    r = x
    for i in range(n):
        p = r.astype(BF16)
        parts.append(p)
        if i + 1 < n:
            r = r - p.astype(F32)
    return parts


def _expand_heads(x, expand2_ref):
    hi, mid = _split_bf16(x, 2)
    return _nn(jnp.concatenate([hi, mid], axis=1), expand2_ref[...])


def _cumsum_cols(m01, x):
    w = x.shape[1]
    r = _nn(m01, jnp.concatenate(_split_bf16(x, 3), axis=1))
    return r[:, :w] + r[:, w:2 * w] + r[:, 2 * w:]


def _cumsum_rows(x, m01):
    h = x.shape[0]
    r = _nt(jnp.concatenate(_split_bf16(x, 3), axis=0), m01)
    return r[:h] + r[h:2 * h] + r[2 * h:]


def _rms(x, g):
    ms = jnp.mean(x * x, axis=-1, keepdims=True)
    return x * lax.rsqrt(ms + EPS) * g


def _silu(x):
    h = 0.5 * x
    return h + h * jnp.tanh(h)


def _conv_silu(xpad_ref, row0, rows, convw_ref, convb_ref, out_ref):
    rb, cb = 64, 256
    for c0 in range(0, CONV_DIM, cb):
        cs = slice(c0, c0 + cb)
        w = [convw_ref[i:i + 1, cs] for i in range(CONV_W)]
        bias = convb_ref[:, cs]
        for r0 in range(0, rows, rb):
            xh = xpad_ref[row0 + r0 - SUBLANES:row0 + r0 + rb, cs]
            acc = bias + xh[SUBLANES:] * w[CONV_W - 1]
            for i in range(CONV_W - 1):
                acc = acc + pltpu.roll(xh, CONV_W - 1 - i, 0)[SUBLANES:] * w[i]
            out_ref[r0:r0 + rb, cs] = _silu(acc)


def _softplus(x):
    return jnp.maximum(x, 0.0) + jnp.log1p(jnp.exp(-jnp.abs(x)))


def _rope(x, c, s1, s2):
    outs = []
    for j in range(x.shape[1] // LANES):
        xb = x[:, j * LANES:(j + 1) * LANES]
        outs.append(xb * c + pltpu.roll(xb, LANES - ROT_DIM // 2, 1) * s1 + pltpu.roll(xb, ROT_DIM // 2, 1) * s2)
    return outs[0] if len(outs) == 1 else jnp.concatenate(outs, axis=1)


def _iota(shape, dim):
    return lax.broadcasted_iota(jnp.int32, shape, dim)


def _project(hn, wq_ref, wkv_ref, ropec, ropes1, ropes2):
    q = _rope(_nn(hn, wq_ref[...]), ropec, ropes1, ropes2) * (HEAD_DIM ** -0.5)
    kv = _nn(hn, wkv_ref[...])
    k = _rope(kv[:, :KV_WIDTH], ropec, ropes1, ropes2)
    v = kv[:, KV_WIDTH:]
    return q, k, v


def _dt_both(hn, wdt_ref, wdtT_ref, dtb_row_ref, dtb_col_ref):
    dtc = _softplus(_nn(hn, wdt_ref[...]) + dtb_row_ref[...])
    dtr = _softplus(_nt(wdtT_ref[...], hn) + dtb_col_ref[...])
    return dtc, dtr


def _a_vectors(alog_row_ref, alog_col_ref):
    lane = _iota((1, LANES), 1)
    a_row = jnp.where(lane < SSM_HEADS, -jnp.exp(alog_row_ref[...]), 0.0)
    a_col = -jnp.exp(alog_col_ref[...])
    return a_row, a_col


def _ssd_block(xs_bf, b_all, c_all, acol, arow, dtr, mask_bool, hT_bf=None):
    quad = 4
    lane4 = _iota((1, quad * SSM_HEAD_DIM), 1) // SSM_HEAD_DIM
    zero = jnp.zeros((), BF16)
    pieces = []
    for g in range(SSM_GROUPS):
        cf = c_all[:, g * D_STATE:(g + 1) * D_STATE]
        cb = _nt(cf.astype(BF16), b_all[:, g * D_STATE:(g + 1) * D_STATE].astype(BF16))
        for qd in range(SSM_HPG // quad):
            e0 = g * SSM_HPG + quad * qd
            lanes = slice(e0 * SSM_HEAD_DIM, (e0 + quad) * SSM_HEAD_DIM)
            xq = xs_bf[:, lanes]
            lhs, rhs = [], []
            for i in range(quad):
                e = e0 + i
                a_t = jnp.broadcast_to(acol[:, e:e + 1], (CHUNK, CHUNK))
                seg = a_t - arow[e:e + 1, :]
                w = cb * jnp.exp(jnp.where(mask_bool, seg, NEG)) * dtr[e:e + 1, :]
                lhs.append(w.astype(BF16))
                rhs.append(jnp.where(lane4 == i, xq, zero))
                if hT_bf is not None:
                    lhs.append((cf * jnp.exp(a_t)).astype(BF16))
                    rhs.append(jnp.where(lane4 == i, hT_bf[:, lanes], zero))
            pieces.append(_nn(jnp.concatenate(lhs, axis=1), jnp.concatenate(rhs, axis=0)))
    return jnp.concatenate(pieces, axis=1)


def _gate_and_out(y, xs, z, x_in, dskip_ref, gssm_ref, wout_ref, gpost_ref, mix_s):
    y = y + dskip_ref[...] * xs
    gated = y * _silu(z)
    for g in range(SSM_GROUPS):
        gg = gated[:, g * SSM_GROUP_W:(g + 1) * SSM_GROUP_W]
        ms = jnp.mean(gg * gg, axis=-1, keepdims=True)
        o = gg * lax.rsqrt(ms + EPS) * gssm_ref[:, g * SSM_GROUP_W:(g + 1) * SSM_GROUP_W]
        mix_s[:, ATT_WIDTH + g * SSM_GROUP_W:ATT_WIDTH + (g + 1) * SSM_GROUP_W] = o.astype(mix_s.dtype)
    mo = _nn(mix_s[...].astype(BF16), wout_ref[...])
    return x_in + _rms(mo, gpost_ref[...])


def _prompt_kernel(x_ref, gpre_ref, wq_ref, wkv_ref, wz_ref, wxbc_ref, wdt_ref, wdtT_ref,
                   ropec_ref, ropes1_ref, ropes2_ref, sink_ref, wvT_ref, biasT_ref,
                   convw_ref, convb_ref, dtb_row_ref, dtb_col_ref, alog_row_ref, alog_col_ref,
                   dskip_ref, gssm_ref, wout_ref, gpost_ref, expand2_ref,
                   x1_ref, nk_ref, nv_ref, nconv_ref, nssm_ref,
                   hn_s, q_s, kbuf, vT_s, xbc_s, xc_s, dtc_s, dtr_s, y_s, mix_s, hT_s):
    tm = x_ref.shape[0]
    nchunk = tm // CHUNK
    s = pl.program_id(1)
    last = pl.num_programs(1) - 1

    @pl.when(s == 0)
    def _():
        kbuf[0:CHUNK, :] = jnp.zeros((CHUNK, KV_WIDTH), BF16)
        vT_s[0] = jnp.zeros((KV_WIDTH, CHUNK), BF16)
        xbc_s[0:SUBLANES, :] = jnp.zeros((SUBLANES, CONV_DIM), F32)
        hT_s[...] = jnp.zeros_like(hT_s)

    xt = x_ref[...]
    hn = _rms(xt, gpre_ref[...]).astype(BF16)
    hn_s[...] = hn

    xbc_s[SUBLANES:SUBLANES + tm, :] = _nn(hn, wxbc_ref[...])
    _conv_silu(xbc_s, SUBLANES, tm, convw_ref, convb_ref, xc_s)

    ropec, ropes1, ropes2 = ropec_ref[...], ropes1_ref[...], ropes2_ref[...]
    q_s[...] = (_rope(_nn(hn, wq_ref[...]), ropec, ropes1, ropes2) * Q_SCALE).astype(BF16)
    k = _rope(_nn(hn, wkv_ref[:, 0:KV_WIDTH]), ropec, ropes1, ropes2)
    kbuf[CHUNK:CHUNK + tm, :] = k.astype(BF16)
    vT = _nt(wvT_ref[...], hn).astype(BF16)
    for j in range(nchunk):
        vT_s[1 + j] = vT[:, j * CHUNK:(j + 1) * CHUNK]

    dtc, dtr = _dt_both(hn, wdt_ref, wdtT_ref, dtb_row_ref, dtb_col_ref)
    dtc_s[...] = dtc
    for j in range(nchunk):
        dtr_s[j] = dtr[:, j * CHUNK:(j + 1) * CHUNK]

    a_row, a_col = _a_vectors(alog_row_ref, alog_col_ref)

    r2 = _iota((CHUNK, CHUNK), 0)
    c2 = _iota((CHUNK, CHUNK), 1)
    tril = c2 <= r2
    tril_bf = tril.astype(BF16)
    lane_kv = _iota((1, KV_WIDTH), 1) // HEAD_DIM
    ones_rows = jnp.ones((2 * SUBLANES, 2 * CHUNK), BF16)

    def chunk_body(c, carry):
        r0 = pl.multiple_of(c * CHUNK, CHUNK)
        first = jnp.logical_and(s == 0, c == 0).astype(jnp.int32)
        bias = biasT_ref[first]
        qcat = jnp.concatenate([q_s[pl.ds(r0, CHUNK), g * KV_WIDTH:(g + 1) * KV_WIDTH]
                                for g in range(ATT_GROUP)], axis=0)
        kwin = kbuf[pl.ds(r0, 2 * CHUNK), :]
        kstack = jnp.concatenate([jnp.where(lane_kv == kvh, kwin, jnp.zeros((), BF16))
                                  for kvh in range(ATT_KV_HEADS)], axis=0)
        sT = _nt(kstack, qcat)
        vT_win = jnp.concatenate([vT_s[c], vT_s[c + 1]], axis=1)
        o_rows = []
        for kvh in range(ATT_KV_HEADS):
            blk = sT[kvh * 2 * CHUNK:(kvh + 1) * 2 * CHUNK] + bias
            sink = sink_ref[kvh] * LOG2E
            m = jnp.maximum(jnp.max(blk, axis=0, keepdims=True), sink)
            p = jnp.exp2(blk - m).astype(BF16)
            lhs = jnp.concatenate([vT_win[kvh * HEAD_DIM:(kvh + 1) * HEAD_DIM], ones_rows], axis=0)
            oT = _nn(lhs, p)
            den = oT[HEAD_DIM:HEAD_DIM + 1] + jnp.exp2(sink - m)
            o_rows.append(oT[:HEAD_DIM] * (1.0 / den))
        oT_all = jnp.concatenate(o_rows, axis=0)
        for g in range(ATT_GROUP):
            mix_s[pl.ds(r0, CHUNK), g * KV_WIDTH:(g + 1) * KV_WIDTH] = (
                oT_all[:, g * CHUNK:(g + 1) * CHUNK].T.astype(BF16))

        dtc_c = dtc_s[pl.ds(r0, CHUNK), :]
        dtr_c = dtr_s[c]
        acol = _cumsum_cols(tril_bf, dtc_c * a_row)
        arow = _cumsum_rows(dtr_c * a_col, tril_bf)
        a_end = acol[CHUNK - 1:CHUNK, :]
        tailc = jnp.exp(a_end - acol) * dtc_c
        ex = _expand_heads(
            jnp.concatenate([tailc, jnp.broadcast_to(jnp.exp(a_end), (SUBLANES, LANES))], axis=0), expand2_ref)
        tlx = ex[:CHUNK]
        dec_row = ex[CHUNK:CHUNK + 1]
        xs = xc_s[pl.ds(r0, CHUNK), 0:SSM_WIDTH]
        b_all = xc_s[pl.ds(r0, CHUNK), SSM_WIDTH:SSM_WIDTH + SSM_GROUPS * D_STATE]
        c_all = xc_s[pl.ds(r0, CHUNK), SSM_WIDTH + SSM_GROUPS * D_STATE:CONV_DIM]
        hT = hT_s[...]
        y_s[pl.ds(r0, CHUNK), :] = _ssd_block(xs.astype(BF16), b_all, c_all, acol, arow, dtr_c, tril,
                                              hT.astype(BF16))
        xtl = (xs * tlx).astype(BF16)
        for g in range(SSM_GROUPS):
            sl = slice(g * SSM_GROUP_W, (g + 1) * SSM_GROUP_W)
            bt = b_all[:, g * D_STATE:(g + 1) * D_STATE].T.astype(BF16)
            hT_s[:, sl] = hT[:, sl] * dec_row[:, sl] + _nn(bt, xtl[:, sl])
        return carry

    lax.fori_loop(0, nchunk, chunk_body, 0)

    kbuf[0:CHUNK, :] = kbuf[tm:tm + CHUNK, :]
    vT_s[0] = vT_s[nchunk]
    xbc_s[0:SUBLANES, :] = xbc_s[tm:tm + SUBLANES, :]

    z = _nn(hn_s[...], wz_ref[...])
    x1_ref[...] = _gate_and_out(y_s[...], xc_s[:, 0:SSM_WIDTH], z, x_ref[...],
                                dskip_ref, gssm_ref, wout_ref, gpost_ref, mix_s)

    @pl.when(s == last)
    def _():
        hn_w = hn_s[tm - WINDOW:, :]
        nk_ref[...] = _rope(_nn(hn_w, wkv_ref[:, 0:KV_WIDTH]), ropec_ref[tm - WINDOW:, :],
                            ropes1_ref[tm - WINDOW:, :], ropes2_ref[tm - WINDOW:, :])
        nv_ref[...] = _nn(hn_w, wkv_ref[:, KV_WIDTH:2 * KV_WIDTH])
        nconv_ref[...] = xbc_s[SUBLANES - (CONV_W - 1):SUBLANES, :]
        nssm_ref[...] = hT_s[...].T


def _sample_kernel(x_ref, ck_ref, cv_ref, sconv_ref, sssm_ref,
                   gpre_ref, wq_ref, wkv_ref, wz_ref, wxbc_ref, wdt_ref, wdtT_ref,
                   ropec_ref, ropes1_ref, ropes2_ref, sink_ref, biasc_ref, biasn_ref,
                   convw_ref, convb_ref, dtb_row_ref, dtb_col_ref, alog_row_ref, alog_col_ref,
                   dskip_ref, gssm_ref, wout_ref, gpost_ref, expand2_ref,
                   x1_ref, nk_ref, nv_ref, nconv_ref, nssm_ref,
                   hn_s, q_s, kn_s, vn_s, knb_s, vnb_s, xpad_s, xc_s, bb_s, ea_s, eax_s, xT_s, y_s, yoff_s, mix_s):
    bt_n = sconv_ref.shape[0]
    pb_n = ck_ref.shape[0]
    m_rows = x_ref.shape[0]
    tdec = m_rows // bt_n
    j = pl.program_id(1)

    @pl.when(j == 0)
    def _():
        hn = _rms(x_ref[...], gpre_ref[...]).astype(BF16)
        hn_s[...] = hn
        q, k, v = _project(hn, wq_ref, wkv_ref, ropec_ref[...], ropes1_ref[...], ropes2_ref[...])
        q_s[...] = q
        kn_s[...] = k
        vn_s[...] = v
        knb_s[...] = k.astype(BF16)
        vnb_s[...] = v.astype(BF16)

        xbc = _nn(hn, wxbc_ref[...])
        xpad_s[:, 0:SUBLANES - 3, :] = jnp.zeros((bt_n, SUBLANES - 3, CONV_DIM), F32)
        xpad_s[:, SUBLANES - 3:SUBLANES, :] = sconv_ref[...]
        xpad_s[:, SUBLANES:2 * SUBLANES, :] = xbc.reshape(bt_n, tdec, CONV_DIM)
        nconv_ref[...] = xpad_s[:, 2 * SUBLANES - 3:2 * SUBLANES, :]
        cb = 256
        for c0 in range(0, CONV_DIM, cb):
            cs = slice(c0, c0 + cb)
            xh = xpad_s[:, :, cs].reshape(bt_n * 2 * SUBLANES, cb)

            def new_rows(a):
                return a.reshape(bt_n, 2 * SUBLANES, cb)[:, SUBLANES:, :].reshape(m_rows, cb)

            acc = convb_ref[:, cs] + new_rows(xh) * convw_ref[CONV_W - 1:CONV_W, cs]
            for i in range(CONV_W - 1):
                acc = acc + new_rows(pltpu.roll(xh, CONV_W - 1 - i, 0)) * convw_ref[i:i + 1, cs]
            xc_s[:, cs] = _silu(acc)
        xs = xc_s[:, 0:SSM_WIDTH]
        b_all = xc_s[:, SSM_WIDTH:SSM_WIDTH + SSM_GROUPS * D_STATE]
        c_all = xc_s[:, SSM_WIDTH + SSM_GROUPS * D_STATE:CONV_DIM]
        bb_s[...] = b_all.astype(BF16)

        dtc, dtr = _dt_both(hn, wdt_ref, wdtT_ref, dtb_row_ref, dtb_col_ref)
        a_row, a_col = _a_vectors(alog_row_ref, alog_col_ref)

        r2 = _iota((m_rows, m_rows), 0)
        c2 = _iota((m_rows, m_rows), 1)
        same = (r2 // tdec) == (c2 // tdec)
        causal = jnp.logical_and(same, c2 <= r2)
        causal_bf = causal.astype(BF16)
        same_bf = same.astype(BF16)

        dac = dtc * a_row
        acol = _cumsum_cols(causal_bf, dac)
        alast = _cumsum_cols(same_bf, dac)
        arow = _cumsum_rows(dtr * a_col, causal_bf)
        tailc = jnp.exp(alast - acol) * dtc
        ex = _expand_heads(jnp.concatenate([jnp.exp(acol), tailc], axis=0), expand2_ref)
        ea_s[...] = jnp.exp(alast)
        eax_s[...] = ex[:m_rows]
        y_s[...] = _ssd_block(xs.astype(BF16), b_all, c_all, acol, arow, dtr, causal)
        xtl = xs * ex[m_rows:]
        for jj in range(SSM_WIDTH // LANES):
            xT_s[jj * LANES:(jj + 1) * LANES, :] = xtl[:, jj * LANES:(jj + 1) * LANES].T.astype(BF16)

    lane_kv = _iota((1, KV_WIDTH), 1) // HEAD_DIM
    row_b = _iota((m_rows, 1), 0) // tdec
    sink = sink_ref[...]

    for pb in range(pb_n):
        b = j * pb_n + pb
        r = pl.multiple_of(b * tdec, tdec)
        q8 = q_s[pl.ds(r, tdec), :]
        qbd = jnp.concatenate(
            [jnp.where(lane_kv == kvh, q8[:, g * KV_WIDTH:(g + 1) * KV_WIDTH], 0.0)
             for g in range(ATT_GROUP) for kvh in range(ATT_KV_HEADS)], axis=0).astype(BF16)
        kc = ck_ref[pb]
        vc = cv_ref[pb]
        sc_c = _nt(qbd, kc.astype(BF16)) + biasc_ref[...]
        sc_n = _nt(qbd, knb_s[...]) + biasn_ref[b]
        m = jnp.maximum(jnp.maximum(jnp.max(sc_c, axis=1, keepdims=True), jnp.max(sc_n, axis=1, keepdims=True)),
                        sink)
        p_c = jnp.exp(sc_c - m)
        p_n = jnp.exp(sc_n - m)
        den = jnp.sum(p_c, axis=1, keepdims=True) + jnp.sum(p_n, axis=1, keepdims=True) + jnp.exp(sink - m)
        o = (_nn(p_c.astype(BF16), vc.astype(BF16)) + _nn(p_n.astype(BF16), vnb_s[...])) * (1.0 / den)
        for g in range(ATT_GROUP):
            og = None
            for kvh in range(ATT_KV_HEADS):
                i0 = (g * ATT_KV_HEADS + kvh) * tdec
                t = jnp.where(lane_kv == kvh, o[i0:i0 + tdec, :], 0.0)
                og = t if og is None else og + t
            mix_s[pl.ds(r, tdec), g * KV_WIDTH:(g + 1) * KV_WIDTH] = og
        nk_ref[pb, 0:WINDOW - tdec, :] = kc[tdec:, :]
        nk_ref[pb, WINDOW - tdec:WINDOW, :] = kn_s[pl.ds(r, tdec), :]
        nv_ref[pb, 0:WINDOW - tdec, :] = vc[tdec:, :]
        nv_ref[pb, WINDOW - tdec:WINDOW, :] = vn_s[pl.ds(r, tdec), :]

        in_b = row_b == b
        ea_b = ea_s[pl.ds(r, 1), :]
        for g in range(SSM_GROUPS):
            sl = slice(g * SSM_GROUP_W, (g + 1) * SSM_GROUP_W)
            hg = sssm_ref[pb, sl, :]
            bm = jnp.where(in_b, bb_s[:, g * D_STATE:(g + 1) * D_STATE], jnp.zeros((), BF16))
            upd = _nn(xT_s[sl, :], bm)
            dec = jnp.concatenate(
                [jnp.broadcast_to(ea_b[:, g * SSM_HPG + e:g * SSM_HPG + e + 1], (SSM_HEAD_DIM, D_STATE))
                 for e in range(SSM_HPG)], axis=0)
            nssm_ref[pb, sl, :] = hg * dec + upd
            c0 = SSM_WIDTH + (SSM_GROUPS + g) * D_STATE
            cc = xc_s[pl.ds(r, tdec), c0:c0 + D_STATE].astype(BF16)
            yoff = _nt(cc, hg.astype(BF16)) * eax_s[pl.ds(r, tdec), sl]
            yoff_s[pl.ds(r, tdec), sl] = yoff

    @pl.when(j == pl.num_programs(1) - 1)
    def _():
        z = _nn(hn_s[...], wz_ref[...])
        x1_ref[...] = _gate_and_out(y_s[...] + yoff_s[...], xc_s[:, 0:SSM_WIDTH], z, x_ref[...],
                                    dskip_ref, gssm_ref, wout_ref, gpost_ref, mix_s)


def _ffn_kernel(x_ref, gpre_ref, wg_ref, wu_ref, wd_ref, gpost_ref, o_ref):
    x = x_ref[...]
    f = _rms(x, gpre_ref[...]).astype(BF16)
    d_ff = wg_ref.shape[1]
    acc = jnp.zeros(x.shape, F32)
    for j in range(d_ff // FF_CHUNK):
        sl = slice(j * FF_CHUNK, (j + 1) * FF_CHUNK)
        gate = _nn(f, wg_ref[:, sl])
        up = _nn(f, wu_ref[:, sl])
        acc = acc + _nn((_silu(gate) * up).astype(BF16), wd_ref[sl, :])
    o_ref[...] = x + _rms(acc, gpost_ref[...])


def _const_spec(shape):
    nd = len(shape)
    return pl.BlockSpec(shape, lambda *_: (0,) * nd, pipeline_mode=pl.Buffered(1))


def _rope_tables(pos):
    half = ROT_DIM // 2
    inv = ROPE_THETA ** (-jnp.arange(half, dtype=F32) * 2.0 / ROT_DIM)
    ang = pos.astype(F32)[:, None] * inv[None, :]
    cos = jnp.cos(ang)
    sin = jnp.sin(ang)
    n = pos.shape[0]
    pad = HEAD_DIM - ROT_DIM
    c = jnp.concatenate([cos, cos, jnp.ones((n, pad), F32)], axis=1)
    s1 = jnp.concatenate([-sin, jnp.zeros((n, half + pad), F32)], axis=1)
    s2 = jnp.concatenate([jnp.zeros((n, half), F32), sin, jnp.zeros((n, pad), F32)], axis=1)
    rep = LANES // HEAD_DIM
    return jnp.tile(c, (1, rep)), jnp.tile(s1, (1, rep)), jnp.tile(s2, (1, rep))


def _layer_params(g_pre_mix, w_in, attn_sinks, conv_w, conv_b, dt_bias, a_log, d_skip, g_ssm_out, w_out, g_post_mix):
    qperm = np.arange(ATT_WIDTH).reshape(ATT_KV_HEADS, ATT_GROUP, HEAD_DIM).transpose(1, 0, 2).reshape(-1)
    o = 0
    wq = w_in[:, o:o + ATT_WIDTH][:, qperm].astype(BF16); o += ATT_WIDTH
    wkv = w_in[:, o:o + 2 * KV_WIDTH].astype(BF16); o += 2 * KV_WIDTH
    wz = w_in[:, o:o + SSM_WIDTH].astype(BF16); o += SSM_WIDTH
    wxbc = w_in[:, o:o + CONV_DIM].astype(BF16); o += CONV_DIM
    wdt_f = w_in[:, o:o + SSM_HEADS]
    wdt = jnp.pad(wdt_f, ((0, 0), (0, LANES - SSM_HEADS))).astype(BF16)
    wdtT = wdt_f.T.astype(BF16)
    wvT = w_in[:, ATT_WIDTH + KV_WIDTH:ATT_WIDTH + 2 * KV_WIDTH].T.astype(BF16)
    wout = jnp.concatenate([w_out[:ATT_WIDTH][qperm], w_out[ATT_WIDTH:]], axis=0).astype(BF16)
    pad16 = ((0, 0), (0, LANES - SSM_HEADS))
    expand = (np.arange(LANES)[:, None] == (np.arange(SSM_WIDTH)[None, :] // SSM_HEAD_DIM)).astype(np.float32)
    return dict(
        gpre=g_pre_mix.reshape(1, D_MODEL), wq=wq, wkv=wkv, wz=wz, wxbc=wxbc, wdt=wdt, wdtT=wdtT, wvT=wvT,
        convw=conv_w, convb=conv_b.reshape(1, CONV_DIM),
        dtb_row=jnp.pad(dt_bias.reshape(1, SSM_HEADS), pad16), dtb_col=dt_bias.reshape(SSM_HEADS, 1),
        alog_row=jnp.pad(a_log.reshape(1, SSM_HEADS), pad16), alog_col=a_log.reshape(SSM_HEADS, 1),
        dskip=jnp.repeat(d_skip, SSM_HEAD_DIM).reshape(1, SSM_WIDTH), gssm=g_ssm_out.reshape(1, SSM_WIDTH),
        wout=wout, gpost=g_post_mix.reshape(1, D_MODEL), expand2=jnp.asarray(np.concatenate([expand, expand], axis=0), BF16),
        sinks_gk=attn_sinks.reshape(ATT_KV_HEADS, ATT_GROUP).T,
    )


_WEIGHT_ORDER = ("gpre", "wq", "wkv", "wz", "wxbc", "wdt", "wdtT")
_TAIL_ORDER = ("convw", "convb", "dtb_row", "dtb_col", "alog_row", "alog_col",
               "dskip", "gssm", "wout", "gpost", "expand2")


def _prompt_mixer(x, p):
    bsz, seq, _ = x.shape
    tm = SEQ_TILE
    nchunk = tm // CHUNK
    c, s1, s2 = _rope_tables(jnp.arange(seq, dtype=jnp.int32))
    sink = jnp.repeat(p["sinks_gk"].T, CHUNK, axis=1)[:, None, :]
    wvT = p["wvT"]
    jj = np.arange(2 * CHUNK)[:, None]
    tt = np.arange(CHUNK)[None, :]
    vis = (jj >= tt) & (jj <= tt + WINDOW)
    bias = np.stack([np.where(vis, 0.0, NEG), np.where(vis & (jj >= CHUNK), 0.0, NEG)]).astype(np.float32)
    bias = jnp.asarray(np.tile(bias, (1, 1, ATT_GROUP)))
    consts = [p[n] for n in _WEIGHT_ORDER]
    tail = [p[n] for n in _TAIL_ORDER]
    in_specs = ([pl.BlockSpec((None, tm, D_MODEL), lambda b, s: (b, s, 0))]
                + [_const_spec(a.shape) for a in consts]
                + [pl.BlockSpec((tm, LANES), lambda b, s: (s, 0))] * 3
                + [_const_spec(sink.shape), _const_spec(wvT.shape), _const_spec(bias.shape)]
                + [_const_spec(a.shape) for a in tail])
    out_shape = (
        jax.ShapeDtypeStruct((bsz, seq, D_MODEL), F32),
        jax.ShapeDtypeStruct((bsz, WINDOW, KV_WIDTH), F32),
        jax.ShapeDtypeStruct((bsz, WINDOW, KV_WIDTH), F32),
        jax.ShapeDtypeStruct((bsz, CONV_W - 1, CONV_DIM), F32),
        jax.ShapeDtypeStruct((bsz, SSM_WIDTH, D_STATE), F32),
    )
    out_specs = (
        pl.BlockSpec((None, tm, D_MODEL), lambda b, s: (b, s, 0)),
        pl.BlockSpec((None, WINDOW, KV_WIDTH), lambda b, s: (b, 0, 0)),
        pl.BlockSpec((None, WINDOW, KV_WIDTH), lambda b, s: (b, 0, 0)),
        pl.BlockSpec((None, CONV_W - 1, CONV_DIM), lambda b, s: (b, 0, 0)),
        pl.BlockSpec((None, SSM_WIDTH, D_STATE), lambda b, s: (b, 0, 0)),
    )
    scratch = [
        pltpu.VMEM((tm, D_MODEL), BF16),
        pltpu.VMEM((tm, ATT_WIDTH), BF16),
        pltpu.VMEM((CHUNK + tm, KV_WIDTH), BF16),
        pltpu.VMEM((nchunk + 1, KV_WIDTH, CHUNK), BF16),
        pltpu.VMEM((SUBLANES + tm, CONV_DIM), F32),
        pltpu.VMEM((tm, CONV_DIM), F32),
        pltpu.VMEM((tm, LANES), F32),
        pltpu.VMEM((nchunk, SSM_HEADS, CHUNK), F32),
        pltpu.VMEM((tm, SSM_WIDTH), F32),
        pltpu.VMEM((tm, MIX_WIDTH), BF16),
        pltpu.VMEM((D_STATE, SSM_WIDTH), F32),
    ]
    return pl.pallas_call(
        _prompt_kernel,
        grid=(bsz, seq // tm),
        in_specs=in_specs,
        out_specs=out_specs,
        out_shape=out_shape,
        scratch_shapes=scratch,
        compiler_params=pltpu.CompilerParams(
            dimension_semantics=("arbitrary", "arbitrary"), vmem_limit_bytes=VMEM_LIMIT_BYTES),
        name="prompt_mixer",
    )(x, *consts, c, s1, s2, sink, wvT, bias, *tail)


def _sample_mixer(x, cache_k, cache_v, state_conv, state_ssm, p):
    nb, tdec, _ = x.shape
    bt = SAMPLE_BT
    rows = bt * tdec
    c, s1, s2 = _rope_tables(PAST_LEN + jnp.arange(tdec, dtype=jnp.int32))
    c, s1, s2 = (jnp.tile(t, (bt, 1)) for t in (c, s1, s2))
    sink = jnp.repeat(p["sinks_gk"].reshape(-1), tdec).reshape(rows, 1)
    consts = [p[n] for n in _WEIGHT_ORDER]
    tail = [p[n] for n in _TAIL_ORDER]
    x2 = x.reshape(nb * tdec, D_MODEL)
    ck = cache_k.reshape(nb, WINDOW, KV_WIDTH)
    cv = cache_v.reshape(nb, WINDOW, KV_WIDTH)
    ssm = state_ssm.reshape(nb, SSM_WIDTH, D_STATE)
    pb = SAMPLE_PB
    steps = bt // pb
    t_of_row = (np.arange(rows) % tdec)[:, None]
    col = np.arange(rows)[None, :]
    bias_c = jnp.asarray(np.where(col >= t_of_row, 0.0, NEG).astype(np.float32))
    bias_n = jnp.asarray(np.stack([np.where((col // tdec == b) & (col % tdec <= t_of_row), 0.0, NEG)
                                   for b in range(bt)]).astype(np.float32))
    tmap = lambda i, j: (i, 0, 0)
    pmap = lambda i, j: (i * steps + j, 0, 0)
    in_specs = ([pl.BlockSpec((rows, D_MODEL), lambda i, j: (i, 0)),
                 pl.BlockSpec((pb, WINDOW, KV_WIDTH), pmap),
                 pl.BlockSpec((pb, WINDOW, KV_WIDTH), pmap),
                 pl.BlockSpec((bt, CONV_W - 1, CONV_DIM), tmap),
                 pl.BlockSpec((pb, SSM_WIDTH, D_STATE), pmap)]
                + [_const_spec(a.shape) for a in consts]
                + [_const_spec(c.shape)] * 3
                + [_const_spec(sink.shape), _const_spec(bias_c.shape), _const_spec(bias_n.shape)]
                + [_const_spec(a.shape) for a in tail])
    out_shape = (
        jax.ShapeDtypeStruct((nb * tdec, D_MODEL), F32),
        jax.ShapeDtypeStruct((nb, WINDOW, KV_WIDTH), F32),
        jax.ShapeDtypeStruct((nb, WINDOW, KV_WIDTH), F32),
        jax.ShapeDtypeStruct((nb, CONV_W - 1, CONV_DIM), F32),
        jax.ShapeDtypeStruct((nb, SSM_WIDTH, D_STATE), F32),
    )
    out_specs = (
        pl.BlockSpec((rows, D_MODEL), lambda i, j: (i, 0)),
        pl.BlockSpec((pb, WINDOW, KV_WIDTH), pmap),
        pl.BlockSpec((pb, WINDOW, KV_WIDTH), pmap),
        pl.BlockSpec((bt, CONV_W - 1, CONV_DIM), tmap),
        pl.BlockSpec((pb, SSM_WIDTH, D_STATE), pmap),
    )
    scratch = [
        pltpu.VMEM((rows, D_MODEL), BF16),
        pltpu.VMEM((rows, ATT_WIDTH), F32),
        pltpu.VMEM((rows, KV_WIDTH), F32),
        pltpu.VMEM((rows, KV_WIDTH), F32),
        pltpu.VMEM((rows, KV_WIDTH), BF16),
        pltpu.VMEM((rows, KV_WIDTH), BF16),
        pltpu.VMEM((bt, 2 * SUBLANES, CONV_DIM), F32),
        pltpu.VMEM((rows, CONV_DIM), F32),
        pltpu.VMEM((rows, SSM_GROUPS * D_STATE), BF16),
        pltpu.VMEM((rows, LANES), F32),
        pltpu.VMEM((rows, SSM_WIDTH), F32),
        pltpu.VMEM((SSM_WIDTH, rows), BF16),
        pltpu.VMEM((rows, SSM_WIDTH), F32),
        pltpu.VMEM((rows, SSM_WIDTH), F32),
        pltpu.VMEM((rows, MIX_WIDTH), F32),
    ]
    return pl.pallas_call(
        _sample_kernel,
        grid=(nb // bt, steps),
        in_specs=in_specs,
        out_specs=out_specs,
        out_shape=out_shape,
        scratch_shapes=scratch,
        compiler_params=pltpu.CompilerParams(
            dimension_semantics=("arbitrary", "arbitrary"), vmem_limit_bytes=VMEM_LIMIT_BYTES),
        name="sample_mixer",
    )(x2, ck, cv, state_conv, ssm, *consts, c, s1, s2, sink, bias_c, bias_n, *tail)


def _ffn(x2, gpre, wg, wu, wd, gpost):
    n = x2.shape[0]
    tf = FFN_TILE
    consts = [gpre, wg, wu, wd, gpost]
    return pl.pallas_call(
        _ffn_kernel,
        grid=(n // tf,),
        in_specs=[pl.BlockSpec((tf, D_MODEL), lambda i: (i, 0))] + [_const_spec(a.shape) for a in consts],
        out_specs=pl.BlockSpec((tf, D_MODEL), lambda i: (i, 0)),
        out_shape=jax.ShapeDtypeStruct((n, D_MODEL), F32),
        compiler_params=pltpu.CompilerParams(
            dimension_semantics=("arbitrary",), vmem_limit_bytes=VMEM_LIMIT_BYTES),
        name="ffn",
    )(x2, *consts)


def kernel(x_prompt, x_sample, cache_k_win, cache_v_win, state_conv, state_ssm, g_pre_mix, w_in, attn_sinks, conv_w, conv_b, dt_bias, a_log, d_skip, g_ssm_out, w_out, g_post_mix, g_pre_ffn, w_gate, w_up, w_down, g_post_ffn):
    depth = w_in.shape[0]
    bp, lp, _ = x_prompt.shape
    nb, ts, _ = x_sample.shape
    hp, hs = x_prompt, x_sample
    outs = [[] for _ in range(8)]
    for l in range(depth):
        p = _layer_params(g_pre_mix[l], w_in[l], attn_sinks[l], conv_w[l], conv_b[l], dt_bias[l], a_log[l],
                          d_skip[l], g_ssm_out[l], w_out[l], g_post_mix[l])
        ffn_w = (g_pre_ffn[l].reshape(1, D_MODEL), w_gate[l].astype(BF16), w_up[l].astype(BF16),
                 w_down[l].astype(BF16), g_post_ffn[l].reshape(1, D_MODEL))
        x1p, kp, vp, cp, sp = _prompt_mixer(hp, p)
        x1s, ksm, vsm, csm, ssm = _sample_mixer(hs, cache_k_win[l], cache_v_win[l], state_conv[l], state_ssm[l], p)
        hp = _ffn(x1p.reshape(bp * lp, D_MODEL), *ffn_w).reshape(bp, lp, D_MODEL)
        hs = _ffn(x1s, *ffn_w).reshape(nb, ts, D_MODEL)
        kv_shape = (WINDOW, ATT_KV_HEADS, HEAD_DIM)
        ssm_shape = (SSM_HEADS, SSM_HEAD_DIM, D_STATE)
        for lst, val in zip(outs, (kp.reshape((bp,) + kv_shape), vp.reshape((bp,) + kv_shape), cp,
                                   sp.reshape((bp,) + ssm_shape),
                                   ksm.reshape((nb,) + kv_shape), vsm.reshape((nb,) + kv_shape), csm,
                                   ssm.reshape((nb,) + ssm_shape))):
            lst.append(val)
    return (hp, hs) + tuple(jnp.stack(o) for o in outs)
```

```python
import functools
import math

import numpy as np
import jax
import jax.numpy as jnp
from jax import lax
from jax.experimental import pallas as pl
from jax.experimental.pallas import tpu as pltpu

F32 = jnp.float32
BF16 = jnp.bfloat16

D_MODEL = 1024
ATT_HEADS = 16
ATT_KV_HEADS = 4
ATT_GROUP = ATT_HEADS // ATT_KV_HEADS
HEAD_DIM = 64
ATT_WIDTH = ATT_HEADS * HEAD_DIM
KV_WIDTH = ATT_KV_HEADS * HEAD_DIM
WINDOW = 128
ROT_DIM = HEAD_DIM // 4
ROPE_THETA = 500000.0
SSM_HEADS = 16
SSM_HEAD_DIM = 64
SSM_WIDTH = SSM_HEADS * SSM_HEAD_DIM
SSM_GROUPS = 2
SSM_HPG = SSM_HEADS // SSM_GROUPS
SSM_GROUP_W = SSM_WIDTH // SSM_GROUPS
D_STATE = 128
CONV_W = 4
CONV_DIM = SSM_WIDTH + 2 * SSM_GROUPS * D_STATE
MIX_WIDTH = ATT_WIDTH + SSM_WIDTH
EPS = 1e-6
PAST_LEN = 8192

LANES = 128
SUBLANES = 8
VMEM_LIMIT_BYTES = 60 * 1024 * 1024

CHUNK = 128
NEG = -1e30
LOG2E = math.log2(math.e)
Q_SCALE = HEAD_DIM ** -0.5 * LOG2E
SEQ_TILE = 512
SAMPLE_BT = 16
SAMPLE_PB = 4
FFN_TILE = 512
ROW_BLOCK = 32
FF_CHUNK = 256


def _nn(a, b):
    return jnp.dot(a, b, preferred_element_type=F32)


def _nt(a, b):
    return lax.dot_general(a, b, (((1,), (1,)), ((), ())), preferred_element_type=F32)


def _split_bf16(x, n):
    parts = []
    r = x
    for i in range(n):
        p = r.astype(BF16)
        parts.append(p)
        if i + 1 < n:
            r = r - p.astype(F32)
    return parts


def _expand_heads(x, expand2_ref):
    hi, mid = _split_bf16(x, 2)
    return _nn(jnp.concatenate([hi, mid], axis=1), expand2_ref[...])


def _cumsum_cols(m01, x):
    w = x.shape[1]
    r = _nn(m01, jnp.concatenate(_split_bf16(x, 3), axis=1))
    return r[:, :w] + r[:, w:2 * w] + r[:, 2 * w:]


def _cumsum_rows(x, m01):
    h = x.shape[0]
    r = _nt(jnp.concatenate(_split_bf16(x, 3), axis=0), m01)
    return r[:h] + r[h:2 * h] + r[2 * h:]


def _rms(x, g):
    ms = jnp.mean(x * x, axis=-1, keepdims=True)
    return x * lax.rsqrt(ms + EPS) * g


def _rms_rows(src, g_ref, dst_ref, res_ref=None):
    g = g_ref[...]
    for r0 in range(0, dst_ref.shape[0], ROW_BLOCK):
        rs = slice(r0, r0 + ROW_BLOCK)
        y = _rms(src[rs, :], g)
        if res_ref is not None:
            y = res_ref[rs, :] + y
        dst_ref[rs, :] = y.astype(dst_ref.dtype)


def _silu(x):
    h = 0.5 * x
    return h + h * jnp.tanh(h)


CONV_CB = 256
CONV_RB = 64


def _conv_silu_cols(xpad_ref, row0, rows, convw_ref, convb_ref, out_ref, c0):
    cs = slice(c0, c0 + CONV_CB)
    w = [convw_ref[i:i + 1, cs] for i in range(CONV_W)]
    bias = convb_ref[:, cs]
    for r0 in range(0, rows, CONV_RB):
        xh = xpad_ref[row0 + r0 - SUBLANES:row0 + r0 + CONV_RB, cs]
        acc = bias + xh[SUBLANES:] * w[CONV_W - 1]
        for i in range(CONV_W - 1):
            acc = acc + pltpu.roll(xh, CONV_W - 1 - i, 0)[SUBLANES:] * w[i]
        out_ref[r0:r0 + CONV_RB, cs] = _silu(acc)


def _softplus(x):
    return jnp.maximum(x, 0.0) + jnp.log1p(jnp.exp(-jnp.abs(x)))


def _rope(x, c, s1, s2):
    outs = []
    for j in range(x.shape[1] // LANES):
        xb = x[:, j * LANES:(j + 1) * LANES]
        outs.append(xb * c + pltpu.roll(xb, LANES - ROT_DIM // 2, 1) * s1 + pltpu.roll(xb, ROT_DIM // 2, 1) * s2)
    return outs[0] if len(outs) == 1 else jnp.concatenate(outs, axis=1)


def _iota(shape, dim):
    return lax.broadcasted_iota(jnp.int32, shape, dim)


def _project(hn, wq_ref, wkv_ref, ropec, ropes1, ropes2):
    q = _rope(_nn(hn, wq_ref[...]), ropec, ropes1, ropes2) * (HEAD_DIM ** -0.5)
    kv = _nn(hn, wkv_ref[...])
    k = _rope(kv[:, :KV_WIDTH], ropec, ropes1, ropes2)
    v = kv[:, KV_WIDTH:]
    return q, k, v


def _dt_both(hn, wdt_ref, wdtT_ref, dtb_row_ref, dtb_col_ref):
    dtc = _softplus(_nn(hn, wdt_ref[...]) + dtb_row_ref[...])
    dtr = _softplus(_nt(wdtT_ref[...], hn) + dtb_col_ref[...])
    return dtc, dtr


def _a_vectors(alog_row_ref, alog_col_ref):
    lane = _iota((1, LANES), 1)
    a_row = jnp.where(lane < SSM_HEADS, -jnp.exp(alog_row_ref[...]), 0.0)
    a_col = -jnp.exp(alog_col_ref[...])
    return a_row, a_col


def _log2_decay(acol, arow, dtr):
    return acol * LOG2E, (arow - jnp.log(dtr)) * LOG2E


SSD_QUAD = 4
SSD_NQUAD = SSM_HEADS // SSD_QUAD


def _ssd_cb(b_all, c_all):
    return [_nt(c_all[:, g * D_STATE:(g + 1) * D_STATE].astype(BF16),
                b_all[:, g * D_STATE:(g + 1) * D_STATE].astype(BF16)) for g in range(SSM_GROUPS)]


def _ssd_quad(qi, xs_bf, c_all, cbs, acol2, arow2, mask_bool, hT_bf=None):
    lane4 = _iota((1, SSD_QUAD * SSM_HEAD_DIM), 1) // SSM_HEAD_DIM
    zero = jnp.zeros((), BF16)
    e0 = SSD_QUAD * qi
    g = e0 // SSM_HPG
    cf = c_all[:, g * D_STATE:(g + 1) * D_STATE]
    lanes = slice(e0 * SSM_HEAD_DIM, (e0 + SSD_QUAD) * SSM_HEAD_DIM)
    xq = xs_bf[:, lanes]
    lhs, rhs = [], []
    for i in range(SSD_QUAD):
        e = e0 + i
        a_t = jnp.broadcast_to(acol2[:, e:e + 1], (CHUNK, CHUNK))
        w = cbs[g] * jnp.exp2(jnp.where(mask_bool, a_t - arow2[e:e + 1, :], NEG))
        lhs.append(w.astype(BF16))
        rhs.append(jnp.where(lane4 == i, xq, zero))
        if hT_bf is not None:
            lhs.append((cf * jnp.exp2(a_t)).astype(BF16))
            rhs.append(jnp.where(lane4 == i, hT_bf[:, lanes], zero))
    return _nn(jnp.concatenate(lhs, axis=1), jnp.concatenate(rhs, axis=0))


def _ssd_block(xs_bf, b_all, c_all, acol2, arow2, mask_bool):
    cbs = _ssd_cb(b_all, c_all)
    return jnp.concatenate([_ssd_quad(qi, xs_bf, c_all, cbs, acol2, arow2, mask_bool)
                            for qi in range(SSD_NQUAD)], axis=1)


def _gate_and_out(y_refs, xc_s, z, x_ref, dskip_ref, gssm_ref, wout_ref, gpost_ref, mix_s, out_ref):
    dskip = dskip_ref[...]
    for r0 in range(0, out_ref.shape[0], ROW_BLOCK):
        rs = slice(r0, r0 + ROW_BLOCK)
        y = y_refs[0][rs, :]
        for extra in y_refs[1:]:
            y = y + extra[rs, :]
        gated = (y + dskip * xc_s[rs, 0:SSM_WIDTH]) * _silu(z[rs, :])
        for g in range(SSM_GROUPS):
            gs = slice(g * SSM_GROUP_W, (g + 1) * SSM_GROUP_W)
            gg = gated[:, gs]
            ms = jnp.mean(gg * gg, axis=-1, keepdims=True)
            o = gg * lax.rsqrt(ms + EPS) * gssm_ref[:, gs]
            mix_s[rs, ATT_WIDTH + g * SSM_GROUP_W:ATT_WIDTH + (g + 1) * SSM_GROUP_W] = o.astype(mix_s.dtype)
    mo = _nn(mix_s[...].astype(BF16), wout_ref[...])
    _rms_rows(mo, gpost_ref, out_ref, res_ref=x_ref)


def _prompt_kernel(x_ref, gpre_ref, wq_ref, wkv_ref, wz_ref, wxbc_ref, wdt_ref, wdtT_ref,
                   ropec_ref, ropes1_ref, ropes2_ref, sink_ref, wvT_ref, biasT_ref,
                   convw_ref, convb_ref, dtb_row_ref, dtb_col_ref, alog_row_ref, alog_col_ref,
                   dskip_ref, gssm_ref, wout_ref, gpost_ref, expand2_ref,
                   x1_ref, nk_ref, nv_ref, nconv_ref, nssm_ref,
                   hn_s, q_s, kbuf, vT_s, xbc_s, xc_s, dtc_s, dtr_s, y_s, mix_s, hT_s):
    tm = x_ref.shape[0]
    nchunk = tm // CHUNK
    s = pl.program_id(1)
    last = pl.num_programs(1) - 1

    @pl.when(s == 0)
    def _():
        kbuf[0:CHUNK, :] = jnp.zeros((CHUNK, KV_WIDTH), BF16)
        vT_s[0] = jnp.zeros((KV_WIDTH, CHUNK), BF16)
        xbc_s[0:SUBLANES, :] = jnp.zeros((SUBLANES, CONV_DIM), F32)
        hT_s[...] = jnp.zeros_like(hT_s)

    _rms_rows(x_ref, gpre_ref, hn_s)
    hn = hn_s[...]

    ropec, ropes1, ropes2 = ropec_ref[...], ropes1_ref[...], ropes2_ref[...]

    def proj_xbc(c0):
        xbc_s[SUBLANES:SUBLANES + tm, c0:c0 + CONV_CB] = _nn(hn, wxbc_ref[:, c0:c0 + CONV_CB])

    def proj_q(c0):
        q_s[:, c0:c0 + KV_WIDTH] = (
            _rope(_nn(hn, wq_ref[:, c0:c0 + KV_WIDTH]), ropec, ropes1, ropes2) * Q_SCALE).astype(BF16)

    def proj_k():
        kbuf[CHUNK:CHUNK + tm, :] = _rope(_nn(hn, wkv_ref[:, 0:KV_WIDTH]), ropec, ropes1, ropes2).astype(BF16)

    def proj_v_dt():
        vT = _nt(wvT_ref[...], hn).astype(BF16)
        for j in range(nchunk):
            vT_s[1 + j] = vT[:, j * CHUNK:(j + 1) * CHUNK]
        dtc, dtr = _dt_both(hn, wdt_ref, wdtT_ref, dtb_row_ref, dtb_col_ref)
        dtc_s[...] = dtc
        for j in range(nchunk):
            dtr_s[j] = dtr[:, j * CHUNK:(j + 1) * CHUNK]

    others = [functools.partial(proj_q, c0) for c0 in range(0, ATT_WIDTH, KV_WIDTH)] + [proj_k, proj_v_dt]
    conv_cols = list(range(0, CONV_DIM, CONV_CB))
    proj_xbc(conv_cols[0])
    for n, c0 in enumerate(conv_cols):
        if n + 1 < len(conv_cols):
            proj_xbc(conv_cols[n + 1])
        _conv_silu_cols(xbc_s, SUBLANES, tm, convw_ref, convb_ref, xc_s, c0)
        if n < len(others):
            others[n]()
    for f in others[len(conv_cols):]:
        f()

    a_row, a_col = _a_vectors(alog_row_ref, alog_col_ref)

    r2 = _iota((CHUNK, CHUNK), 0)
    c2 = _iota((CHUNK, CHUNK), 1)
    tril = c2 <= r2
    tril_bf = tril.astype(BF16)
    lane_kv = _iota((1, KV_WIDTH), 1) // HEAD_DIM
    ones_rows = jnp.ones((2 * SUBLANES, 2 * CHUNK), BF16)

    def chunk_body(c, carry):
        r0 = pl.multiple_of(c * CHUNK, CHUNK)
        first = jnp.logical_and(s == 0, c == 0).astype(jnp.int32)
        bias = biasT_ref[first]
        qcat = jnp.concatenate([q_s[pl.ds(r0, CHUNK), g * KV_WIDTH:(g + 1) * KV_WIDTH]
                                for g in range(ATT_GROUP)], axis=0)
        kwin = kbuf[pl.ds(r0, 2 * CHUNK), :]
        kstack = jnp.concatenate([jnp.where(lane_kv == kvh, kwin, jnp.zeros((), BF16))
                                  for kvh in range(ATT_KV_HEADS)], axis=0)
        sT = _nt(kstack, qcat)
        vT_win = jnp.concatenate([vT_s[c], vT_s[c + 1]], axis=1)

        dtc_c = dtc_s[pl.ds(r0, CHUNK), :]
        dtr_c = dtr_s[c]
        acol = _cumsum_cols(tril_bf, dtc_c * a_row)
        arow = _cumsum_rows(dtr_c * a_col, tril_bf)
        a_end = acol[CHUNK - 1:CHUNK, :]
        tailc = jnp.exp(a_end - acol) * dtc_c
        ex = _expand_heads(
            jnp.concatenate([tailc, jnp.broadcast_to(jnp.exp(a_end), (SUBLANES, LANES))], axis=0), expand2_ref)
        tlx = ex[:CHUNK]
        dec_row = ex[CHUNK:CHUNK + 1]
        xs = xc_s[pl.ds(r0, CHUNK), 0:SSM_WIDTH]
        b_all = xc_s[pl.ds(r0, CHUNK), SSM_WIDTH:SSM_WIDTH + SSM_GROUPS * D_STATE]
        c_all = xc_s[pl.ds(r0, CHUNK), SSM_WIDTH + SSM_GROUPS * D_STATE:CONV_DIM]
        hT = hT_s[...]
        acol2, arow2 = _log2_decay(acol, arow, dtr_c)
        xs_bf = xs.astype(BF16)
        hT_bf = hT.astype(BF16)
        cbs = _ssd_cb(b_all, c_all)

        o_rows = []
        for i in range(ATT_KV_HEADS):
            blk = sT[i * 2 * CHUNK:(i + 1) * 2 * CHUNK] + bias
            sink = sink_ref[i] * LOG2E
            m = jnp.maximum(jnp.max(blk, axis=0, keepdims=True), sink)
            p = jnp.exp2(blk - m).astype(BF16)
            lhs = jnp.concatenate([vT_win[i * HEAD_DIM:(i + 1) * HEAD_DIM], ones_rows], axis=0)
            oT = _nn(lhs, p)
            den = oT[HEAD_DIM:HEAD_DIM + 1] + jnp.exp2(sink - m)
            o_rows.append(oT[:HEAD_DIM] * (1.0 / den))
            for qi in range(i * SSD_NQUAD // ATT_KV_HEADS, (i + 1) * SSD_NQUAD // ATT_KV_HEADS):
                lanes = slice(qi * SSD_QUAD * SSM_HEAD_DIM, (qi + 1) * SSD_QUAD * SSM_HEAD_DIM)
                y_s[pl.ds(r0, CHUNK), lanes] = _ssd_quad(qi, xs_bf, c_all, cbs, acol2, arow2, tril, hT_bf)
        oT_all = jnp.concatenate(o_rows, axis=0)
        for g in range(ATT_GROUP):
            mix_s[pl.ds(r0, CHUNK), g * KV_WIDTH:(g + 1) * KV_WIDTH] = (
                oT_all[:, g * CHUNK:(g + 1) * CHUNK].T.astype(BF16))

        xtl = (xs * tlx).astype(BF16)
        for g in range(SSM_GROUPS):
            sl = slice(g * SSM_GROUP_W, (g + 1) * SSM_GROUP_W)
            bt = b_all[:, g * D_STATE:(g + 1) * D_STATE].T.astype(BF16)
            hT_s[:, sl] = hT[:, sl] * dec_row[:, sl] + _nn(bt, xtl[:, sl])
        return carry

    lax.fori_loop(0, nchunk, chunk_body, 0)

    kbuf[0:CHUNK, :] = kbuf[tm:tm + CHUNK, :]
    vT_s[0] = vT_s[nchunk]
    xbc_s[0:SUBLANES, :] = xbc_s[tm:tm + SUBLANES, :]

    z = _nn(hn_s[...], wz_ref[...])
    _gate_and_out((y_s,), xc_s, z, x_ref, dskip_ref, gssm_ref, wout_ref, gpost_ref, mix_s, x1_ref)

    @pl.when(s == last)
    def _():
        hn_w = hn_s[tm - WINDOW:, :]
        nk_ref[...] = _rope(_nn(hn_w, wkv_ref[:, 0:KV_WIDTH]), ropec_ref[tm - WINDOW:, :],
                            ropes1_ref[tm - WINDOW:, :], ropes2_ref[tm - WINDOW:, :])
        nv_ref[...] = _nn(hn_w, wkv_ref[:, KV_WIDTH:2 * KV_WIDTH])
        nconv_ref[...] = xbc_s[SUBLANES - (CONV_W - 1):SUBLANES, :]
        nssm_ref[...] = hT_s[...].T


def _sample_kernel(x_ref, ck_ref, cv_ref, sconv_ref, sssm_ref,
                   gpre_ref, wq_ref, wkv_ref, wz_ref, wxbc_ref, wdt_ref, wdtT_ref,
                   ropec_ref, ropes1_ref, ropes2_ref, sink_ref, biasc_ref, biasn_ref,
                   convw_ref, convb_ref, dtb_row_ref, dtb_col_ref, alog_row_ref, alog_col_ref,
                   dskip_ref, gssm_ref, wout_ref, gpost_ref, expand2_ref,
                   x1_ref, nk_ref, nv_ref, nconv_ref, nssm_ref,
                   hn_s, q_s, kn_s, vn_s, knb_s, vnb_s, xpad_s, xc_s, bb_s, ea_s, eax_s, xT_s, y_s, yoff_s, mix_s):
    bt_n = sconv_ref.shape[0]
    pb_n = ck_ref.shape[0]
    m_rows = x_ref.shape[0]
    tdec = m_rows // bt_n
    j = pl.program_id(1)

    @pl.when(j == 0)
    def _():
        _rms_rows(x_ref, gpre_ref, hn_s)
        hn = hn_s[...]
        q, k, v = _project(hn, wq_ref, wkv_ref, ropec_ref[...], ropes1_ref[...], ropes2_ref[...])
        q_s[...] = q
        kn_s[...] = k
        vn_s[...] = v
        knb_s[...] = k.astype(BF16)
        vnb_s[...] = v.astype(BF16)

        xbc = _nn(hn, wxbc_ref[...])
        xpad_s[:, 0:SUBLANES - 3, :] = jnp.zeros((bt_n, SUBLANES - 3, CONV_DIM), F32)
        xpad_s[:, SUBLANES - 3:SUBLANES, :] = sconv_ref[...]
        xpad_s[:, SUBLANES:2 * SUBLANES, :] = xbc.reshape(bt_n, tdec, CONV_DIM)
        nconv_ref[...] = xpad_s[:, 2 * SUBLANES - 3:2 * SUBLANES, :]
        cb = 256
        for c0 in range(0, CONV_DIM, cb):
            cs = slice(c0, c0 + cb)
            xh = xpad_s[:, :, cs].reshape(bt_n * 2 * SUBLANES, cb)

            def new_rows(a):
                return a.reshape(bt_n, 2 * SUBLANES, cb)[:, SUBLANES:, :].reshape(m_rows, cb)

            acc = convb_ref[:, cs] + new_rows(xh) * convw_ref[CONV_W - 1:CONV_W, cs]
            for i in range(CONV_W - 1):
                acc = acc + new_rows(pltpu.roll(xh, CONV_W - 1 - i, 0)) * convw_ref[i:i + 1, cs]
            xc_s[:, cs] = _silu(acc)
        xs = xc_s[:, 0:SSM_WIDTH]
        b_all = xc_s[:, SSM_WIDTH:SSM_WIDTH + SSM_GROUPS * D_STATE]
        c_all = xc_s[:, SSM_WIDTH + SSM_GROUPS * D_STATE:CONV_DIM]
        bb_s[...] = b_all.astype(BF16)

        dtc, dtr = _dt_both(hn, wdt_ref, wdtT_ref, dtb_row_ref, dtb_col_ref)
        a_row, a_col = _a_vectors(alog_row_ref, alog_col_ref)

        r2 = _iota((m_rows, m_rows), 0)
        c2 = _iota((m_rows, m_rows), 1)
        same = (r2 // tdec) == (c2 // tdec)
        causal = jnp.logical_and(same, c2 <= r2)
        causal_bf = causal.astype(BF16)
        same_bf = same.astype(BF16)

        dac = dtc * a_row
        acol = _cumsum_cols(causal_bf, dac)
        alast = _cumsum_cols(same_bf, dac)
        arow = _cumsum_rows(dtr * a_col, causal_bf)
        tailc = jnp.exp(alast - acol) * dtc
        ex = _expand_heads(jnp.concatenate([jnp.exp(acol), tailc], axis=0), expand2_ref)
        ea_s[...] = jnp.exp(alast)
        eax_s[...] = ex[:m_rows]
        acol2, arow2 = _log2_decay(acol, arow, dtr)
        y_s[...] = _ssd_block(xs.astype(BF16), b_all, c_all, acol2, arow2, causal)
        xtl = xs * ex[m_rows:]
        for jj in range(SSM_WIDTH // LANES):
            xT_s[jj * LANES:(jj + 1) * LANES, :] = xtl[:, jj * LANES:(jj + 1) * LANES].T.astype(BF16)

    lane_kv = _iota((1, KV_WIDTH), 1) // HEAD_DIM
    row_b = _iota((m_rows, 1), 0) // tdec
    sink = sink_ref[...]

    pbs = range(pb_n)
    bs = [j * pb_n + pb for pb in pbs]
    rs = [pl.multiple_of(b * tdec, tdec) for b in bs]

    qbd, kc, vc, sc_c, sc_n = [], [], [], [], []
    for pb in pbs:
        q8 = q_s[pl.ds(rs[pb], tdec), :]
        qbd.append(jnp.concatenate(
            [jnp.where(lane_kv == kvh, q8[:, g * KV_WIDTH:(g + 1) * KV_WIDTH], 0.0)
             for g in range(ATT_GROUP) for kvh in range(ATT_KV_HEADS)], axis=0).astype(BF16))
        kc.append(ck_ref[pb])
        vc.append(cv_ref[pb])
    for pb in pbs:
        sc_c.append(_nt(qbd[pb], kc[pb].astype(BF16)) + biasc_ref[...])
        sc_n.append(_nt(qbd[pb], knb_s[...]) + biasn_ref[bs[pb]])
    p_c, p_n, inv = [], [], []
    for pb in pbs:
        m = jnp.maximum(jnp.maximum(jnp.max(sc_c[pb], axis=1, keepdims=True),
                                    jnp.max(sc_n[pb], axis=1, keepdims=True)), sink)
        pc = jnp.exp(sc_c[pb] - m)
        pn = jnp.exp(sc_n[pb] - m)
        den = jnp.sum(pc, axis=1, keepdims=True) + jnp.sum(pn, axis=1, keepdims=True) + jnp.exp(sink - m)
        p_c.append(pc.astype(BF16))
        p_n.append(pn.astype(BF16))
        inv.append(1.0 / den)
    for pb in pbs:
        o = (_nn(p_c[pb], vc[pb].astype(BF16)) + _nn(p_n[pb], vnb_s[...])) * inv[pb]
        for g in range(ATT_GROUP):
            og = None
            for kvh in range(ATT_KV_HEADS):
                i0 = (g * ATT_KV_HEADS + kvh) * tdec
                t = jnp.where(lane_kv == kvh, o[i0:i0 + tdec, :], 0.0)
                og = t if og is None else og + t
            mix_s[pl.ds(rs[pb], tdec), g * KV_WIDTH:(g + 1) * KV_WIDTH] = og
        nk_ref[pb, 0:WINDOW - tdec, :] = kc[pb][tdec:, :]
        nk_ref[pb, WINDOW - tdec:WINDOW, :] = kn_s[pl.ds(rs[pb], tdec), :]
        nv_ref[pb, 0:WINDOW - tdec, :] = vc[pb][tdec:, :]
        nv_ref[pb, WINDOW - tdec:WINDOW, :] = vn_s[pl.ds(rs[pb], tdec), :]

    for g in range(SSM_GROUPS):
        sl = slice(g * SSM_GROUP_W, (g + 1) * SSM_GROUP_W)
        c0 = SSM_WIDTH + (SSM_GROUPS + g) * D_STATE
        hg, upd = [], []
        for pb in pbs:
            hg.append(sssm_ref[pb, sl, :])
            bm = jnp.where(row_b == bs[pb], bb_s[:, g * D_STATE:(g + 1) * D_STATE], jnp.zeros((), BF16))
            upd.append(_nn(xT_s[sl, :], bm))
        for pb in pbs:
            cc = xc_s[pl.ds(rs[pb], tdec), c0:c0 + D_STATE].astype(BF16)
            yoff_s[pl.ds(rs[pb], tdec), sl] = (_nt(cc, hg[pb].astype(BF16))
                                               * eax_s[pl.ds(rs[pb], tdec), sl])
        for pb in pbs:
            ea_b = ea_s[pl.ds(rs[pb], 1), :]
            dec = jnp.concatenate(
                [jnp.broadcast_to(ea_b[:, g * SSM_HPG + e:g * SSM_HPG + e + 1], (SSM_HEAD_DIM, D_STATE))
                 for e in range(SSM_HPG)], axis=0)
            nssm_ref[pb, sl, :] = hg[pb] * dec + upd[pb]

    @pl.when(j == pl.num_programs(1) - 1)
    def _():
        z = _nn(hn_s[...], wz_ref[...])
        _gate_and_out((y_s, yoff_s), xc_s, z, x_ref, dskip_ref, gssm_ref, wout_ref, gpost_ref, mix_s, x1_ref)


def _ffn_kernel(x_ref, gpre_ref, wg_ref, wu_ref, wd_ref, gpost_ref, o_ref, f_s):
    _rms_rows(x_ref, gpre_ref, f_s)
    f = f_s[...]
    d_ff = wg_ref.shape[1]
    acc = jnp.zeros(x_ref.shape, F32)
    for j in range(d_ff // FF_CHUNK):
        sl = slice(j * FF_CHUNK, (j + 1) * FF_CHUNK)
        gate = _nn(f, wg_ref[:, sl])
        up = _nn(f, wu_ref[:, sl])
        acc = acc + _nn((_silu(gate) * up).astype(BF16), wd_ref[sl, :])
    _rms_rows(acc, gpost_ref, o_ref, res_ref=x_ref)


def _const_spec(shape):
    nd = len(shape)
    return pl.BlockSpec(shape, lambda *_: (0,) * nd, pipeline_mode=pl.Buffered(1))


def _rope_tables(pos):
    half = ROT_DIM // 2
    inv = ROPE_THETA ** (-jnp.arange(half, dtype=F32) * 2.0 / ROT_DIM)
    ang = pos.astype(F32)[:, None] * inv[None, :]
    cos = jnp.cos(ang)
    sin = jnp.sin(ang)
    n = pos.shape[0]
    pad = HEAD_DIM - ROT_DIM
    c = jnp.concatenate([cos, cos, jnp.ones((n, pad), F32)], axis=1)
    s1 = jnp.concatenate([-sin, jnp.zeros((n, half + pad), F32)], axis=1)
    s2 = jnp.concatenate([jnp.zeros((n, half), F32), sin, jnp.zeros((n, pad), F32)], axis=1)
    rep = LANES // HEAD_DIM
    return jnp.tile(c, (1, rep)), jnp.tile(s1, (1, rep)), jnp.tile(s2, (1, rep))


def _layer_params(g_pre_mix, w_in, attn_sinks, conv_w, conv_b, dt_bias, a_log, d_skip, g_ssm_out, w_out, g_post_mix):
    qperm = np.arange(ATT_WIDTH).reshape(ATT_KV_HEADS, ATT_GROUP, HEAD_DIM).transpose(1, 0, 2).reshape(-1)
    o = 0
    wq = w_in[:, o:o + ATT_WIDTH][:, qperm].astype(BF16); o += ATT_WIDTH
    wkv = w_in[:, o:o + 2 * KV_WIDTH].astype(BF16); o += 2 * KV_WIDTH
    wz = w_in[:, o:o + SSM_WIDTH].astype(BF16); o += SSM_WIDTH
    wxbc = w_in[:, o:o + CONV_DIM].astype(BF16); o += CONV_DIM
    wdt_f = w_in[:, o:o + SSM_HEADS]
    wdt = jnp.pad(wdt_f, ((0, 0), (0, LANES - SSM_HEADS))).astype(BF16)
    wdtT = wdt_f.T.astype(BF16)
    wvT = w_in[:, ATT_WIDTH + KV_WIDTH:ATT_WIDTH + 2 * KV_WIDTH].T.astype(BF16)
    wout = jnp.concatenate([w_out[:ATT_WIDTH][qperm], w_out[ATT_WIDTH:]], axis=0).astype(BF16)
    pad16 = ((0, 0), (0, LANES - SSM_HEADS))
    expand = (np.arange(LANES)[:, None] == (np.arange(SSM_WIDTH)[None, :] // SSM_HEAD_DIM)).astype(np.float32)
    return dict(
        gpre=g_pre_mix.reshape(1, D_MODEL), wq=wq, wkv=wkv, wz=wz, wxbc=wxbc, wdt=wdt, wdtT=wdtT, wvT=wvT,
        convw=conv_w, convb=conv_b.reshape(1, CONV_DIM),
        dtb_row=jnp.pad(dt_bias.reshape(1, SSM_HEADS), pad16), dtb_col=dt_bias.reshape(SSM_HEADS, 1),
        alog_row=jnp.pad(a_log.reshape(1, SSM_HEADS), pad16), alog_col=a_log.reshape(SSM_HEADS, 1),
        dskip=jnp.repeat(d_skip, SSM_HEAD_DIM).reshape(1, SSM_WIDTH), gssm=g_ssm_out.reshape(1, SSM_WIDTH),
        wout=wout, gpost=g_post_mix.reshape(1, D_MODEL), expand2=jnp.asarray(np.concatenate([expand, expand], axis=0), BF16),
        sinks_gk=attn_sinks.reshape(ATT_KV_HEADS, ATT_GROUP).T,
    )


_WEIGHT_ORDER = ("gpre", "wq", "wkv", "wz", "wxbc", "wdt", "wdtT")
_TAIL_ORDER = ("convw", "convb", "dtb_row", "dtb_col", "alog_row", "alog_col",
               "dskip", "gssm", "wout", "gpost", "expand2")


def _prompt_mixer(x, p):
    bsz, seq, _ = x.shape
    tm = SEQ_TILE
    nchunk = tm // CHUNK
    c, s1, s2 = _rope_tables(jnp.arange(seq, dtype=jnp.int32))
    sink = jnp.repeat(p["sinks_gk"].T, CHUNK, axis=1)[:, None, :]
    wvT = p["wvT"]
    jj = np.arange(2 * CHUNK)[:, None]
    tt = np.arange(CHUNK)[None, :]
    vis = (jj >= tt) & (jj <= tt + WINDOW)
    bias = np.stack([np.where(vis, 0.0, NEG), np.where(vis & (jj >= CHUNK), 0.0, NEG)]).astype(np.float32)
    bias = jnp.asarray(np.tile(bias, (1, 1, ATT_GROUP)))
    consts = [p[n] for n in _WEIGHT_ORDER]
    tail = [p[n] for n in _TAIL_ORDER]
    in_specs = ([pl.BlockSpec((None, tm, D_MODEL), lambda b, s: (b, s, 0))]
                + [_const_spec(a.shape) for a in consts]
                + [pl.BlockSpec((tm, LANES), lambda b, s: (s, 0))] * 3
                + [_const_spec(sink.shape), _const_spec(wvT.shape), _const_spec(bias.shape)]
                + [_const_spec(a.shape) for a in tail])
    out_shape = (
        jax.ShapeDtypeStruct((bsz, seq, D_MODEL), F32),
        jax.ShapeDtypeStruct((bsz, WINDOW, KV_WIDTH), F32),
        jax.ShapeDtypeStruct((bsz, WINDOW, KV_WIDTH), F32),
        jax.ShapeDtypeStruct((bsz, CONV_W - 1, CONV_DIM), F32),
        jax.ShapeDtypeStruct((bsz, SSM_WIDTH, D_STATE), F32),
    )
    out_specs = (
        pl.BlockSpec((None, tm, D_MODEL), lambda b, s: (b, s, 0)),
        pl.BlockSpec((None, WINDOW, KV_WIDTH), lambda b, s: (b, 0, 0)),
        pl.BlockSpec((None, WINDOW, KV_WIDTH), lambda b, s: (b, 0, 0)),
        pl.BlockSpec((None, CONV_W - 1, CONV_DIM), lambda b, s: (b, 0, 0)),
        pl.BlockSpec((None, SSM_WIDTH, D_STATE), lambda b, s: (b, 0, 0)),
    )
    scratch = [
        pltpu.VMEM((tm, D_MODEL), BF16),
        pltpu.VMEM((tm, ATT_WIDTH), BF16),
        pltpu.VMEM((CHUNK + tm, KV_WIDTH), BF16),
        pltpu.VMEM((nchunk + 1, KV_WIDTH, CHUNK), BF16),
        pltpu.VMEM((SUBLANES + tm, CONV_DIM), F32),
        pltpu.VMEM((tm, CONV_DIM), F32),
        pltpu.VMEM((tm, LANES), F32),
        pltpu.VMEM((nchunk, SSM_HEADS, CHUNK), F32),
        pltpu.VMEM((tm, SSM_WIDTH), F32),
        pltpu.VMEM((tm, MIX_WIDTH), BF16),
        pltpu.VMEM((D_STATE, SSM_WIDTH), F32),
    ]
    return pl.pallas_call(
        _prompt_kernel,
        grid=(bsz, seq // tm),
        in_specs=in_specs,
        out_specs=out_specs,
        out_shape=out_shape,
        scratch_shapes=scratch,
        compiler_params=pltpu.CompilerParams(
            dimension_semantics=("arbitrary", "arbitrary"), vmem_limit_bytes=VMEM_LIMIT_BYTES),
        name="prompt_mixer",
    )(x, *consts, c, s1, s2, sink, wvT, bias, *tail)


def _sample_mixer(x, cache_k, cache_v, state_conv, state_ssm, p):
    nb, tdec, _ = x.shape
    bt = SAMPLE_BT
    rows = bt * tdec
    c, s1, s2 = _rope_tables(PAST_LEN + jnp.arange(tdec, dtype=jnp.int32))
    c, s1, s2 = (jnp.tile(t, (bt, 1)) for t in (c, s1, s2))
    sink = jnp.repeat(p["sinks_gk"].reshape(-1), tdec).reshape(rows, 1)
    consts = [p[n] for n in _WEIGHT_ORDER]
    tail = [p[n] for n in _TAIL_ORDER]
    x2 = x.reshape(nb * tdec, D_MODEL)
    ck = cache_k.reshape(nb, WINDOW, KV_WIDTH)
    cv = cache_v.reshape(nb, WINDOW, KV_WIDTH)
    ssm = state_ssm.reshape(nb, SSM_WIDTH, D_STATE)
    pb = SAMPLE_PB
    steps = bt // pb
    t_of_row = (np.arange(rows) % tdec)[:, None]
    col = np.arange(rows)[None, :]
    bias_c = jnp.asarray(np.where(col >= t_of_row, 0.0, NEG).astype(np.float32))
    bias_n = jnp.asarray(np.stack([np.where((col // tdec == b) & (col % tdec <= t_of_row), 0.0, NEG)
                                   for b in range(bt)]).astype(np.float32))
    tmap = lambda i, j: (i, 0, 0)
    pmap = lambda i, j: (i * steps + j, 0, 0)
    in_specs = ([pl.BlockSpec((rows, D_MODEL), lambda i, j: (i, 0)),
                 pl.BlockSpec((pb, WINDOW, KV_WIDTH), pmap),
                 pl.BlockSpec((pb, WINDOW, KV_WIDTH), pmap),
                 pl.BlockSpec((bt, CONV_W - 1, CONV_DIM), tmap),
                 pl.BlockSpec((pb, SSM_WIDTH, D_STATE), pmap)]
                + [_const_spec(a.shape) for a in consts]
                + [_const_spec(c.shape)] * 3
                + [_const_spec(sink.shape), _const_spec(bias_c.shape), _const_spec(bias_n.shape)]
                + [_const_spec(a.shape) for a in tail])
    out_shape = (
        jax.ShapeDtypeStruct((nb * tdec, D_MODEL), F32),
        jax.ShapeDtypeStruct((nb, WINDOW, KV_WIDTH), F32),
        jax.ShapeDtypeStruct((nb, WINDOW, KV_WIDTH), F32),
        jax.ShapeDtypeStruct((nb, CONV_W - 1, CONV_DIM), F32),
        jax.ShapeDtypeStruct((nb, SSM_WIDTH, D_STATE), F32),
    )
    out_specs = (
        pl.BlockSpec((rows, D_MODEL), lambda i, j: (i, 0)),
        pl.BlockSpec((pb, WINDOW, KV_WIDTH), pmap),
        pl.BlockSpec((pb, WINDOW, KV_WIDTH), pmap),
        pl.BlockSpec((bt, CONV_W - 1, CONV_DIM), tmap),
        pl.BlockSpec((pb, SSM_WIDTH, D_STATE), pmap),
    )
    scratch = [
        pltpu.VMEM((rows, D_MODEL), BF16),
        pltpu.VMEM((rows, ATT_WIDTH), F32),
        pltpu.VMEM((rows, KV_WIDTH), F32),
        pltpu.VMEM((rows, KV_WIDTH), F32),
        pltpu.VMEM((rows, KV_WIDTH), BF16),
        pltpu.VMEM((rows, KV_WIDTH), BF16),
        pltpu.VMEM((bt, 2 * SUBLANES, CONV_DIM), F32),
        pltpu.VMEM((rows, CONV_DIM), F32),
        pltpu.VMEM((rows, SSM_GROUPS * D_STATE), BF16),
        pltpu.VMEM((rows, LANES), F32),
        pltpu.VMEM((rows, SSM_WIDTH), F32),
        pltpu.VMEM((SSM_WIDTH, rows), BF16),
        pltpu.VMEM((rows, SSM_WIDTH), F32),
        pltpu.VMEM((rows, SSM_WIDTH), F32),
        pltpu.VMEM((rows, MIX_WIDTH), F32),
    ]
    return pl.pallas_call(
        _sample_kernel,
        grid=(nb // bt, steps),
        in_specs=in_specs,
        out_specs=out_specs,
        out_shape=out_shape,
        scratch_shapes=scratch,
        compiler_params=pltpu.CompilerParams(
            dimension_semantics=("arbitrary", "arbitrary"), vmem_limit_bytes=VMEM_LIMIT_BYTES),
        name="sample_mixer",
    )(x2, ck, cv, state_conv, ssm, *consts, c, s1, s2, sink, bias_c, bias_n, *tail)


def _ffn(x2, gpre, wg, wu, wd, gpost):
    n = x2.shape[0]
    tf = FFN_TILE
    consts = [gpre, wg, wu, wd, gpost]
    return pl.pallas_call(
        _ffn_kernel,
        grid=(n // tf,),
        in_specs=[pl.BlockSpec((tf, D_MODEL), lambda i: (i, 0))] + [_const_spec(a.shape) for a in consts],
        out_specs=pl.BlockSpec((tf, D_MODEL), lambda i: (i, 0)),
        out_shape=jax.ShapeDtypeStruct((n, D_MODEL), F32),
        scratch_shapes=[pltpu.VMEM((tf, D_MODEL), BF16)],
        compiler_params=pltpu.CompilerParams(
            dimension_semantics=("arbitrary",), vmem_limit_bytes=VMEM_LIMIT_BYTES),
        name="ffn",
    )(x2, *consts)


def kernel(x_prompt, x_sample, cache_k_win, cache_v_win, state_conv, state_ssm, g_pre_mix, w_in, attn_sinks, conv_w, conv_b, dt_bias, a_log, d_skip, g_ssm_out, w_out, g_post_mix, g_pre_ffn, w_gate, w_up, w_down, g_post_ffn):
    depth = w_in.shape[0]
    bp, lp, _ = x_prompt.shape
    nb, ts, _ = x_sample.shape
    hp, hs = x_prompt, x_sample
    outs = [[] for _ in range(8)]
    for l in range(depth):
        p = _layer_params(g_pre_mix[l], w_in[l], attn_sinks[l], conv_w[l], conv_b[l], dt_bias[l], a_log[l],
                          d_skip[l], g_ssm_out[l], w_out[l], g_post_mix[l])
        ffn_w = (g_pre_ffn[l].reshape(1, D_MODEL), w_gate[l].astype(BF16), w_up[l].astype(BF16),
                 w_down[l].astype(BF16), g_post_ffn[l].reshape(1, D_MODEL))
        x1p, kp, vp, cp, sp = _prompt_mixer(hp, p)
        x1s, ksm, vsm, csm, ssm = _sample_mixer(hs, cache_k_win[l], cache_v_win[l], state_conv[l], state_ssm[l], p)
        hp = _ffn(x1p.reshape(bp * lp, D_MODEL), *ffn_w).reshape(bp, lp, D_MODEL)
        hs = _ffn(x1s, *ffn_w).reshape(nb, ts, D_MODEL)
        kv_shape = (WINDOW, ATT_KV_HEADS, HEAD_DIM)
        ssm_shape = (SSM_HEADS, SSM_HEAD_DIM, D_STATE)
        for lst, val in zip(outs, (kp.reshape((bp,) + kv_shape), vp.reshape((bp,) + kv_shape), cp,
                                   sp.reshape((bp,) + ssm_shape),
                                   ksm.reshape((nb,) + kv_shape), vsm.reshape((nb,) + kv_shape), csm,
                                   ssm.reshape((nb,) + ssm_shape))):
            lst.append(val)
    return (hp, hs) + tuple(jnp.stack(o) for o in outs)
```

```python
import functools
import math

import numpy as np
import jax
import jax.numpy as jnp
from jax import lax
from jax.experimental import pallas as pl
from jax.experimental.pallas import tpu as pltpu

F32 = jnp.float32
BF16 = jnp.bfloat16

D_MODEL = 1024
ATT_HEADS = 16
ATT_KV_HEADS = 4
ATT_GROUP = ATT_HEADS // ATT_KV_HEADS
HEAD_DIM = 64
ATT_WIDTH = ATT_HEADS * HEAD_DIM
KV_WIDTH = ATT_KV_HEADS * HEAD_DIM
WINDOW = 128
ROT_DIM = HEAD_DIM // 4
ROPE_THETA = 500000.0
SSM_HEADS = 16
SSM_HEAD_DIM = 64
SSM_WIDTH = SSM_HEADS * SSM_HEAD_DIM
SSM_GROUPS = 2
SSM_HPG = SSM_HEADS // SSM_GROUPS
SSM_GROUP_W = SSM_WIDTH // SSM_GROUPS
D_STATE = 128
CONV_W = 4
CONV_DIM = SSM_WIDTH + 2 * SSM_GROUPS * D_STATE
MIX_WIDTH = ATT_WIDTH + SSM_WIDTH
EPS = 1e-6
PAST_LEN = 8192

LANES = 128
SUBLANES = 8
VMEM_LIMIT_BYTES = 60 * 1024 * 1024

CHUNK = 128
NEG = -1e30
LOG2E = math.log2(math.e)
Q_SCALE = HEAD_DIM ** -0.5 * LOG2E
SEQ_TILE = 512
SAMPLE_BT = 16
SAMPLE_PB = 4
FFN_TILE = 512
ROW_BLOCK = 32
FF_CHUNK = 256


def _nn(a, b):
    return jnp.dot(a, b, preferred_element_type=F32)


def _nt(a, b):
    return lax.dot_general(a, b, (((1,), (1,)), ((), ())), preferred_element_type=F32)


def _split_bf16(x, n):
    parts = []
    r = x
    for i in range(n):
        p = r.astype(BF16)
        parts.append(p)
        if i + 1 < n:
            r = r - p.astype(F32)
    return parts


def _expand_heads(x, expand2_ref):
    hi, mid = _split_bf16(x, 2)
    return _nn(jnp.concatenate([hi, mid], axis=1), expand2_ref[...])


def _cumsum_cols(m01, x):
    w = x.shape[1]
    r = _nn(m01, jnp.concatenate(_split_bf16(x, 3), axis=1))
    return r[:, :w] + r[:, w:2 * w] + r[:, 2 * w:]


def _heads_to_rows(x):
    return x.T[:SSM_HEADS]


def _rms(x, g):
    ms = jnp.mean(x * x, axis=-1, keepdims=True)
    return x * lax.rsqrt(ms + EPS) * g


def _rms_rows(src, g_ref, dst_ref, res_ref=None):
    g = g_ref[...]
    for r0 in range(0, dst_ref.shape[0], ROW_BLOCK):
        rs = slice(r0, r0 + ROW_BLOCK)
        y = _rms(src[rs, :], g)
        if res_ref is not None:
            y = res_ref[rs, :] + y
        dst_ref[rs, :] = y.astype(dst_ref.dtype)


def _silu(x):
    h = 0.5 * x
    return h + h * jnp.tanh(h)


R_K = 0
R_V = R_K + KV_WIDTH
R_Z = R_V + KV_WIDTH
R_XBC = R_Z + SSM_WIDTH
R_END = R_XBC + CONV_DIM

CONV_CB = 256
CONV_RB = 64


def _conv_silu_cols(xpad_ref, row0, rows, convw_ref, convb_ref, out_ref, c0):
    cs = slice(c0, c0 + CONV_CB)
    w = [convw_ref[i:i + 1, cs] for i in range(CONV_W)]
    bias = convb_ref[:, cs]
    for r0 in range(0, rows, CONV_RB):
        xh = xpad_ref[row0 + r0 - SUBLANES:row0 + r0 + CONV_RB, cs]
        acc = bias + xh[SUBLANES:] * w[CONV_W - 1]
        for i in range(CONV_W - 1):
            acc = acc + pltpu.roll(xh, CONV_W - 1 - i, 0)[SUBLANES:] * w[i]
        out_ref[r0:r0 + CONV_RB, cs] = _silu(acc)


def _softplus(x):
    return jnp.maximum(x, 0.0) + jnp.log1p(jnp.exp(-jnp.abs(x)))


def _rope(x, c, s1, s2):
    outs = []
    for j in range(x.shape[1] // LANES):
        xb = x[:, j * LANES:(j + 1) * LANES]
        outs.append(xb * c + pltpu.roll(xb, LANES - ROT_DIM // 2, 1) * s1 + pltpu.roll(xb, ROT_DIM // 2, 1) * s2)
    return outs[0] if len(outs) == 1 else jnp.concatenate(outs, axis=1)


def _iota(shape, dim):
    return lax.broadcasted_iota(jnp.int32, shape, dim)


def _project(hn, wq_ref, wr_ref, ropec, ropes1, ropes2):
    q = _rope(_nn(hn, wq_ref[...]), ropec, ropes1, ropes2) * (HEAD_DIM ** -0.5)
    kv = _nn(hn, wr_ref[:, R_K:R_Z])
    k = _rope(kv[:, :KV_WIDTH], ropec, ropes1, ropes2)
    v = kv[:, KV_WIDTH:]
    return q, k, v


def _dt_cols(hn, wdt_ref, dtb_row_ref):
    return _softplus(_nn(hn, wdt_ref[...]) + dtb_row_ref[...])


def _a_row(alog_row_ref):
    lane = _iota((1, LANES), 1)
    return jnp.where(lane < SSM_HEADS, -jnp.exp(alog_row_ref[...]), 0.0)


def _log2_decay(acol, arow, dtr):
    return acol * LOG2E, (arow - jnp.log(dtr)) * LOG2E


SSD_QUAD = 4
SSD_NQUAD = SSM_HEADS // SSD_QUAD


def _ssd_cb(b_all, c_all):
    return [_nt(c_all[:, g * D_STATE:(g + 1) * D_STATE].astype(BF16),
                b_all[:, g * D_STATE:(g + 1) * D_STATE].astype(BF16)) for g in range(SSM_GROUPS)]


def _ssd_quad(qi, xs_bf, c_all, cbs, acol2, arow2, mask_bool, hT_bf=None):
    lane4 = _iota((1, SSD_QUAD * SSM_HEAD_DIM), 1) // SSM_HEAD_DIM
    zero = jnp.zeros((), BF16)
    e0 = SSD_QUAD * qi
    g = e0 // SSM_HPG
    cf = c_all[:, g * D_STATE:(g + 1) * D_STATE]
    lanes = slice(e0 * SSM_HEAD_DIM, (e0 + SSD_QUAD) * SSM_HEAD_DIM)
    xq = xs_bf[:, lanes]
    lhs, rhs = [], []
    for i in range(SSD_QUAD):
        e = e0 + i
        a_t = jnp.broadcast_to(acol2[:, e:e + 1], (CHUNK, CHUNK))
        w = cbs[g] * jnp.exp2(jnp.where(mask_bool, a_t - arow2[e:e + 1, :], NEG))
        lhs.append(w.astype(BF16))
        rhs.append(jnp.where(lane4 == i, xq, zero))
        if hT_bf is not None:
            lhs.append((cf * jnp.exp2(a_t)).astype(BF16))
            rhs.append(jnp.where(lane4 == i, hT_bf[:, lanes], zero))
    return _nn(jnp.concatenate(lhs, axis=1), jnp.concatenate(rhs, axis=0))


def _ssd_block(xs_bf, b_all, c_all, acol2, arow2, mask_bool):
    cbs = _ssd_cb(b_all, c_all)
    return jnp.concatenate([_ssd_quad(qi, xs_bf, c_all, cbs, acol2, arow2, mask_bool)
                            for qi in range(SSD_NQUAD)], axis=1)


def _gate_and_out(y_refs, xc_s, z, x_ref, dskip_ref, gssm_ref, wout_ref, gpost_ref, mix_s, out_ref):
    dskip = dskip_ref[...]
    for r0 in range(0, out_ref.shape[0], ROW_BLOCK):
        rs = slice(r0, r0 + ROW_BLOCK)
        y = y_refs[0][rs, :]
        for extra in y_refs[1:]:
            y = y + extra[rs, :]
        gated = (y + dskip * xc_s[rs, 0:SSM_WIDTH]) * _silu(z[rs, :])
        for g in range(SSM_GROUPS):
            gs = slice(g * SSM_GROUP_W, (g + 1) * SSM_GROUP_W)
            gg = gated[:, gs]
            ms = jnp.mean(gg * gg, axis=-1, keepdims=True)
            o = gg * lax.rsqrt(ms + EPS) * gssm_ref[:, gs]
            mix_s[rs, ATT_WIDTH + g * SSM_GROUP_W:ATT_WIDTH + (g + 1) * SSM_GROUP_W] = o.astype(mix_s.dtype)
    mo = _nn(mix_s[...].astype(BF16), wout_ref[...])
    _rms_rows(mo, gpost_ref, out_ref, res_ref=x_ref)


def _prompt_kernel(x_ref, gpre_ref, wq_ref, wr_ref, wdt_ref,
                   ropec_ref, ropes1_ref, ropes2_ref, sink_ref, biasT_ref,
                   convw_ref, convb_ref, dtb_row_ref, alog_row_ref,
                   dskip_ref, gssm_ref, wout_ref, gpost_ref, expand2_ref,
                   x1_ref, nk_ref, nv_ref, nconv_ref, nssm_ref,
                   hn_s, q_s, kbuf, vT_s, xbc_s, xc_s, dtc_s, y_s, mix_s, hT_s):
    tm = x_ref.shape[0]
    nchunk = tm // CHUNK
    s = pl.program_id(1)
    last = pl.num_programs(1) - 1

    @pl.when(s == 0)
    def _():
        kbuf[0:CHUNK, :] = jnp.zeros((CHUNK, KV_WIDTH), BF16)
        vT_s[0] = jnp.zeros((KV_WIDTH, CHUNK), BF16)
        xbc_s[0:SUBLANES, :] = jnp.zeros((SUBLANES, CONV_DIM), F32)
        hT_s[...] = jnp.zeros_like(hT_s)

    _rms_rows(x_ref, gpre_ref, hn_s)
    hn = hn_s[...]

    ropec, ropes1, ropes2 = ropec_ref[...], ropes1_ref[...], ropes2_ref[...]

    def proj_xbc(c0):
        xbc_s[SUBLANES:SUBLANES + tm, c0:c0 + CONV_CB] = _nn(hn, wr_ref[:, R_XBC + c0:R_XBC + c0 + CONV_CB])

    def proj_q(c0):
        q_s[:, c0:c0 + KV_WIDTH] = (
            _rope(_nn(hn, wq_ref[:, c0:c0 + KV_WIDTH]), ropec, ropes1, ropes2) * Q_SCALE).astype(BF16)

    def proj_k():
        kbuf[CHUNK:CHUNK + tm, :] = _rope(_nn(hn, wr_ref[:, R_K:R_V]), ropec, ropes1, ropes2).astype(BF16)

    def proj_v_dt():
        v = _nn(hn, wr_ref[:, R_V:R_Z])
        for j in range(nchunk):
            vT_s[1 + j] = v[j * CHUNK:(j + 1) * CHUNK, :].T.astype(BF16)
        dtc_s[...] = _dt_cols(hn, wdt_ref, dtb_row_ref)

    others = [functools.partial(proj_q, c0) for c0 in range(0, ATT_WIDTH, KV_WIDTH)] + [proj_k, proj_v_dt]
    conv_cols = list(range(0, CONV_DIM, CONV_CB))
    proj_xbc(conv_cols[0])
    for n, c0 in enumerate(conv_cols):
        if n + 1 < len(conv_cols):
            proj_xbc(conv_cols[n + 1])
        _conv_silu_cols(xbc_s, SUBLANES, tm, convw_ref, convb_ref, xc_s, c0)
        if n < len(others):
            others[n]()
    for f in others[len(conv_cols):]:
        f()

    a_row = _a_row(alog_row_ref)

    r2 = _iota((CHUNK, CHUNK), 0)
    c2 = _iota((CHUNK, CHUNK), 1)
    tril = c2 <= r2
    tril_bf = tril.astype(BF16)
    lane_kv = _iota((1, KV_WIDTH), 1) // HEAD_DIM
    ones_rows = jnp.ones((2 * SUBLANES, 2 * CHUNK), BF16)

    def chunk_body(c, carry):
        r0 = pl.multiple_of(c * CHUNK, CHUNK)
        first = jnp.logical_and(s == 0, c == 0).astype(jnp.int32)
        bias = biasT_ref[first]
        qcat = jnp.concatenate([q_s[pl.ds(r0, CHUNK), g * KV_WIDTH:(g + 1) * KV_WIDTH]
                                for g in range(ATT_GROUP)], axis=0)
        kwin = kbuf[pl.ds(r0, 2 * CHUNK), :]
        kstack = jnp.concatenate([jnp.where(lane_kv == kvh, kwin, jnp.zeros((), BF16))
                                  for kvh in range(ATT_KV_HEADS)], axis=0)
        sT = _nt(kstack, qcat)
        vT_win = jnp.concatenate([vT_s[c], vT_s[c + 1]], axis=1)

        dtc_c = dtc_s[pl.ds(r0, CHUNK), :]
        dtr_c = _heads_to_rows(dtc_c)
        acol = _cumsum_cols(tril_bf, dtc_c * a_row)
        arow = _heads_to_rows(acol)
        a_end = acol[CHUNK - 1:CHUNK, :]
        tailc = jnp.exp(a_end - acol) * dtc_c
        ex = _expand_heads(
            jnp.concatenate([tailc, jnp.broadcast_to(jnp.exp(a_end), (SUBLANES, LANES))], axis=0), expand2_ref)
        tlx = ex[:CHUNK]
        dec_row = ex[CHUNK:CHUNK + 1]
        xs = xc_s[pl.ds(r0, CHUNK), 0:SSM_WIDTH]
        b_all = xc_s[pl.ds(r0, CHUNK), SSM_WIDTH:SSM_WIDTH + SSM_GROUPS * D_STATE]
        c_all = xc_s[pl.ds(r0, CHUNK), SSM_WIDTH + SSM_GROUPS * D_STATE:CONV_DIM]
        hT = hT_s[...]
        acol2, arow2 = _log2_decay(acol, arow, dtr_c)
        xs_bf = xs.astype(BF16)
        hT_bf = hT.astype(BF16)
        cbs = _ssd_cb(b_all, c_all)

        o_rows = []
        for i in range(ATT_KV_HEADS):
            blk = sT[i * 2 * CHUNK:(i + 1) * 2 * CHUNK] + bias
            sink = sink_ref[i] * LOG2E
            m = jnp.maximum(jnp.max(blk, axis=0, keepdims=True), sink)
            p = jnp.exp2(blk - m).astype(BF16)
            lhs = jnp.concatenate([vT_win[i * HEAD_DIM:(i + 1) * HEAD_DIM], ones_rows], axis=0)
            oT = _nn(lhs, p)
            den = oT[HEAD_DIM:HEAD_DIM + 1] + jnp.exp2(sink - m)
            o_rows.append(oT[:HEAD_DIM] * (1.0 / den))
            for qi in range(i * SSD_NQUAD // ATT_KV_HEADS, (i + 1) * SSD_NQUAD // ATT_KV_HEADS):
                lanes = slice(qi * SSD_QUAD * SSM_HEAD_DIM, (qi + 1) * SSD_QUAD * SSM_HEAD_DIM)
                y_s[pl.ds(r0, CHUNK), lanes] = _ssd_quad(qi, xs_bf, c_all, cbs, acol2, arow2, tril, hT_bf)
        oT_all = jnp.concatenate(o_rows, axis=0)
        for g in range(ATT_GROUP):
            mix_s[pl.ds(r0, CHUNK), g * KV_WIDTH:(g + 1) * KV_WIDTH] = (
                oT_all[:, g * CHUNK:(g + 1) * CHUNK].T.astype(BF16))

        xtl = (xs * tlx).astype(BF16)
        for g in range(SSM_GROUPS):
            sl = slice(g * SSM_GROUP_W, (g + 1) * SSM_GROUP_W)
            bt = b_all[:, g * D_STATE:(g + 1) * D_STATE].T.astype(BF16)
            hT_s[:, sl] = hT[:, sl] * dec_row[:, sl] + _nn(bt, xtl[:, sl])
        return carry

    lax.fori_loop(0, nchunk, chunk_body, 0)

    kbuf[0:CHUNK, :] = kbuf[tm:tm + CHUNK, :]
    vT_s[0] = vT_s[nchunk]
    xbc_s[0:SUBLANES, :] = xbc_s[tm:tm + SUBLANES, :]

    z = _nn(hn_s[...], wr_ref[:, R_Z:R_XBC])
    _gate_and_out((y_s,), xc_s, z, x_ref, dskip_ref, gssm_ref, wout_ref, gpost_ref, mix_s, x1_ref)

    @pl.when(s == last)
    def _():
        hn_w = hn_s[tm - WINDOW:, :]
        nk_ref[...] = _rope(_nn(hn_w, wr_ref[:, R_K:R_V]), ropec_ref[tm - WINDOW:, :],
                            ropes1_ref[tm - WINDOW:, :], ropes2_ref[tm - WINDOW:, :])
        nv_ref[...] = _nn(hn_w, wr_ref[:, R_V:R_Z])
        nconv_ref[...] = xbc_s[SUBLANES - (CONV_W - 1):SUBLANES, :]
        nssm_ref[...] = hT_s[...].T


def _sample_kernel(x_ref, ck_ref, cv_ref, sconv_ref, sssm_ref,
                   gpre_ref, wq_ref, wr_ref, wdt_ref,
                   ropec_ref, ropes1_ref, ropes2_ref, sink_ref, biasc_ref, biasn_ref,
                   convw_ref, convb_ref, dtb_row_ref, alog_row_ref,
                   dskip_ref, gssm_ref, wout_ref, gpost_ref, expand2_ref,
                   x1_ref, nk_ref, nv_ref, nconv_ref, nssm_ref,
                   hn_s, q_s, kn_s, vn_s, knb_s, vnb_s, xpad_s, xc_s, bb_s, ea_s, eax_s, xT_s, y_s, yoff_s, mix_s):
    bt_n = sconv_ref.shape[0]
    pb_n = ck_ref.shape[0]
    m_rows = x_ref.shape[0]
    tdec = m_rows // bt_n
    j = pl.program_id(1)

    @pl.when(j == 0)
    def _():
        _rms_rows(x_ref, gpre_ref, hn_s)
        hn = hn_s[...]
        q, k, v = _project(hn, wq_ref, wr_ref, ropec_ref[...], ropes1_ref[...], ropes2_ref[...])
        q_s[...] = q
        kn_s[...] = k
        vn_s[...] = v
        knb_s[...] = k.astype(BF16)
        vnb_s[...] = v.astype(BF16)

        xbc = _nn(hn, wr_ref[:, R_XBC:R_END])
        xpad_s[:, 0:SUBLANES - 3, :] = jnp.zeros((bt_n, SUBLANES - 3, CONV_DIM), F32)
        xpad_s[:, SUBLANES - 3:SUBLANES, :] = sconv_ref[...]
        xpad_s[:, SUBLANES:2 * SUBLANES, :] = xbc.reshape(bt_n, tdec, CONV_DIM)
        nconv_ref[...] = xpad_s[:, 2 * SUBLANES - 3:2 * SUBLANES, :]
        cb = 256
        for c0 in range(0, CONV_DIM, cb):
            cs = slice(c0, c0 + cb)
            xh = xpad_s[:, :, cs].reshape(bt_n * 2 * SUBLANES, cb)

            def new_rows(a):
                return a.reshape(bt_n, 2 * SUBLANES, cb)[:, SUBLANES:, :].reshape(m_rows, cb)

            acc = convb_ref[:, cs] + new_rows(xh) * convw_ref[CONV_W - 1:CONV_W, cs]
            for i in range(CONV_W - 1):
                acc = acc + new_rows(pltpu.roll(xh, CONV_W - 1 - i, 0)) * convw_ref[i:i + 1, cs]
            xc_s[:, cs] = _silu(acc)
        xs = xc_s[:, 0:SSM_WIDTH]
        b_all = xc_s[:, SSM_WIDTH:SSM_WIDTH + SSM_GROUPS * D_STATE]
        c_all = xc_s[:, SSM_WIDTH + SSM_GROUPS * D_STATE:CONV_DIM]
        bb_s[...] = b_all.astype(BF16)

        dtc = _dt_cols(hn, wdt_ref, dtb_row_ref)
        dtr = _heads_to_rows(dtc)
        a_row = _a_row(alog_row_ref)

        r2 = _iota((m_rows, m_rows), 0)
        c2 = _iota((m_rows, m_rows), 1)
        same = (r2 // tdec) == (c2 // tdec)
        causal = jnp.logical_and(same, c2 <= r2)
        causal_bf = causal.astype(BF16)
        same_bf = same.astype(BF16)

        dac = dtc * a_row
        acol = _cumsum_cols(causal_bf, dac)
        alast = _cumsum_cols(same_bf, dac)
        arow = _heads_to_rows(acol)
        tailc = jnp.exp(alast - acol) * dtc
        ex = _expand_heads(jnp.concatenate([jnp.exp(acol), tailc], axis=0), expand2_ref)
        ea_s[...] = jnp.exp(alast)
        eax_s[...] = ex[:m_rows]
        acol2, arow2 = _log2_decay(acol, arow, dtr)
        y_s[...] = _ssd_block(xs.astype(BF16), b_all, c_all, acol2, arow2, causal)
        xtl = xs * ex[m_rows:]
        for jj in range(SSM_WIDTH // LANES):
            xT_s[jj * LANES:(jj + 1) * LANES, :] = xtl[:, jj * LANES:(jj + 1) * LANES].T.astype(BF16)

    lane_kv = _iota((1, KV_WIDTH), 1) // HEAD_DIM
    row_b = _iota((m_rows, 1), 0) // tdec
    sink = sink_ref[...]

    pbs = range(pb_n)
    bs = [j * pb_n + pb for pb in pbs]
    rs = [pl.multiple_of(b * tdec, tdec) for b in bs]

    qbd, kc, vc, sc_c, sc_n = [], [], [], [], []
    for pb in pbs:
        q8 = q_s[pl.ds(rs[pb], tdec), :]
        qbd.append(jnp.concatenate(
            [jnp.where(lane_kv == kvh, q8[:, g * KV_WIDTH:(g + 1) * KV_WIDTH], 0.0)
             for g in range(ATT_GROUP) for kvh in range(ATT_KV_HEADS)], axis=0).astype(BF16))
        kc.append(ck_ref[pb])
        vc.append(cv_ref[pb])
    for pb in pbs:
        sc_c.append(_nt(qbd[pb], kc[pb].astype(BF16)) + biasc_ref[...])
        sc_n.append(_nt(qbd[pb], knb_s[...]) + biasn_ref[bs[pb]])
    p_c, p_n, inv = [], [], []
    for pb in pbs:
        m = jnp.maximum(jnp.maximum(jnp.max(sc_c[pb], axis=1, keepdims=True),
                                    jnp.max(sc_n[pb], axis=1, keepdims=True)), sink)
        pc = jnp.exp(sc_c[pb] - m)
        pn = jnp.exp(sc_n[pb] - m)
        den = jnp.sum(pc, axis=1, keepdims=True) + jnp.sum(pn, axis=1, keepdims=True) + jnp.exp(sink - m)
        p_c.append(pc.astype(BF16))
        p_n.append(pn.astype(BF16))
        inv.append(1.0 / den)
    for pb in pbs:
        o = (_nn(p_c[pb], vc[pb].astype(BF16)) + _nn(p_n[pb], vnb_s[...])) * inv[pb]
        for g in range(ATT_GROUP):
            og = None
            for kvh in range(ATT_KV_HEADS):
                i0 = (g * ATT_KV_HEADS + kvh) * tdec
                t = jnp.where(lane_kv == kvh, o[i0:i0 + tdec, :], 0.0)
                og = t if og is None else og + t
            mix_s[pl.ds(rs[pb], tdec), g * KV_WIDTH:(g + 1) * KV_WIDTH] = og
        nk_ref[pb, 0:WINDOW - tdec, :] = kc[pb][tdec:, :]
        nk_ref[pb, WINDOW - tdec:WINDOW, :] = kn_s[pl.ds(rs[pb], tdec), :]
        nv_ref[pb, 0:WINDOW - tdec, :] = vc[pb][tdec:, :]
        nv_ref[pb, WINDOW - tdec:WINDOW, :] = vn_s[pl.ds(rs[pb], tdec), :]

    for g in range(SSM_GROUPS):
        sl = slice(g * SSM_GROUP_W, (g + 1) * SSM_GROUP_W)
        c0 = SSM_WIDTH + (SSM_GROUPS + g) * D_STATE
        hg, upd = [], []
        for pb in pbs:
            hg.append(sssm_ref[pb, sl, :])
            bm = jnp.where(row_b == bs[pb], bb_s[:, g * D_STATE:(g + 1) * D_STATE], jnp.zeros((), BF16))
            upd.append(_nn(xT_s[sl, :], bm))
        for pb in pbs:
            cc = xc_s[pl.ds(rs[pb], tdec), c0:c0 + D_STATE].astype(BF16)
            yoff_s[pl.ds(rs[pb], tdec), sl] = (_nt(cc, hg[pb].astype(BF16))
                                               * eax_s[pl.ds(rs[pb], tdec), sl])
        for pb in pbs:
            ea_b = ea_s[pl.ds(rs[pb], 1), :]
            dec = jnp.concatenate(
                [jnp.broadcast_to(ea_b[:, g * SSM_HPG + e:g * SSM_HPG + e + 1], (SSM_HEAD_DIM, D_STATE))
                 for e in range(SSM_HPG)], axis=0)
            nssm_ref[pb, sl, :] = hg[pb] * dec + upd[pb]

    @pl.when(j == pl.num_programs(1) - 1)
    def _():
        z = _nn(hn_s[...], wr_ref[:, R_Z:R_XBC])
        _gate_and_out((y_s, yoff_s), xc_s, z, x_ref, dskip_ref, gssm_ref, wout_ref, gpost_ref, mix_s, x1_ref)


def _ffn_kernel(x_ref, gpre_ref, wg_ref, wu_ref, wd_ref, gpost_ref, o_ref, f_s):
    _rms_rows(x_ref, gpre_ref, f_s)
    f = f_s[...]
    d_ff = wg_ref.shape[1]
    acc = jnp.zeros(x_ref.shape, F32)
    for j in range(d_ff // FF_CHUNK):
        sl = slice(j * FF_CHUNK, (j + 1) * FF_CHUNK)
        gate = _nn(f, wg_ref[:, sl])
        up = _nn(f, wu_ref[:, sl])
        acc = acc + _nn((_silu(gate) * up).astype(BF16), wd_ref[sl, :])
    _rms_rows(acc, gpost_ref, o_ref, res_ref=x_ref)


def _const_spec(shape):
    nd = len(shape)
    return pl.BlockSpec(shape, lambda *_: (0,) * nd, pipeline_mode=pl.Buffered(1))


def _rope_tables(pos):
    half = ROT_DIM // 2
    inv = ROPE_THETA ** (-np.arange(half, dtype=np.float64) * 2.0 / ROT_DIM)
    ang = pos.astype(np.float64)[:, None] * inv[None, :]
    cos = np.cos(ang).astype(np.float32)
    sin = np.sin(ang).astype(np.float32)
    n = pos.shape[0]
    pad = HEAD_DIM - ROT_DIM
    c = np.concatenate([cos, cos, np.ones((n, pad), np.float32)], axis=1)
    s1 = np.concatenate([-sin, np.zeros((n, half + pad), np.float32)], axis=1)
    s2 = np.concatenate([np.zeros((n, half), np.float32), sin, np.zeros((n, pad), np.float32)], axis=1)
    rep = LANES // HEAD_DIM
    return tuple(jnp.asarray(np.tile(t, (1, rep))) for t in (c, s1, s2))


def _layer_params(g_pre_mix, w_in, attn_sinks, conv_w, conv_b, dt_bias, a_log, d_skip, g_ssm_out, w_out, g_post_mix):
    kgd = (ATT_KV_HEADS, ATT_GROUP, HEAD_DIM)
    wq = (w_in[:, :ATT_WIDTH].reshape((D_MODEL,) + kgd).transpose(0, 2, 1, 3)
          .reshape(D_MODEL, ATT_WIDTH).astype(BF16))
    wr = w_in[:, ATT_WIDTH:ATT_WIDTH + R_END].astype(BF16)
    wdt_f = w_in[:, ATT_WIDTH + R_END:]
    wdt = jnp.pad(wdt_f, ((0, 0), (0, LANES - SSM_HEADS))).astype(BF16)
    wout = jnp.concatenate(
        [w_out[:ATT_WIDTH].reshape(kgd + (D_MODEL,)).transpose(1, 0, 2, 3).reshape(ATT_WIDTH, D_MODEL),
         w_out[ATT_WIDTH:]], axis=0).astype(BF16)
    pad16 = ((0, 0), (0, LANES - SSM_HEADS))
    expand = (np.arange(LANES)[:, None] == (np.arange(SSM_WIDTH)[None, :] // SSM_HEAD_DIM)).astype(np.float32)
    return dict(
        gpre=g_pre_mix.reshape(1, D_MODEL), wq=wq, wr=wr, wdt=wdt,
        convw=conv_w, convb=conv_b.reshape(1, CONV_DIM),
        dtb_row=jnp.pad(dt_bias.reshape(1, SSM_HEADS), pad16),
        alog_row=jnp.pad(a_log.reshape(1, SSM_HEADS), pad16),
        dskip=jnp.repeat(d_skip, SSM_HEAD_DIM).reshape(1, SSM_WIDTH), gssm=g_ssm_out.reshape(1, SSM_WIDTH),
        wout=wout, gpost=g_post_mix.reshape(1, D_MODEL), expand2=jnp.asarray(np.concatenate([expand, expand], axis=0), BF16),
        sinks_gk=attn_sinks.reshape(ATT_KV_HEADS, ATT_GROUP).T,
    )


_WEIGHT_ORDER = ("gpre", "wq", "wr", "wdt")
_TAIL_ORDER = ("convw", "convb", "dtb_row", "alog_row",
               "dskip", "gssm", "wout", "gpost", "expand2")


def _prompt_mixer(x, p):
    bsz, seq, _ = x.shape
    tm = SEQ_TILE
    nchunk = tm // CHUNK
    c, s1, s2 = _rope_tables(np.arange(seq))
    sink = jnp.repeat(p["sinks_gk"].T, CHUNK, axis=1)[:, None, :]
    jj = np.arange(2 * CHUNK)[:, None]
    tt = np.arange(CHUNK)[None, :]
    vis = (jj >= tt) & (jj <= tt + WINDOW)
    bias = np.stack([np.where(vis, 0.0, NEG), np.where(vis & (jj >= CHUNK), 0.0, NEG)]).astype(np.float32)
    bias = jnp.asarray(np.tile(bias, (1, 1, ATT_GROUP)))
    consts = [p[n] for n in _WEIGHT_ORDER]
    tail = [p[n] for n in _TAIL_ORDER]
    in_specs = ([pl.BlockSpec((None, tm, D_MODEL), lambda b, s: (b, s, 0))]
                + [_const_spec(a.shape) for a in consts]
                + [pl.BlockSpec((tm, LANES), lambda b, s: (s, 0))] * 3
                + [_const_spec(sink.shape), _const_spec(bias.shape)]
                + [_const_spec(a.shape) for a in tail])
    out_shape = (
        jax.ShapeDtypeStruct((bsz, seq, D_MODEL), F32),
        jax.ShapeDtypeStruct((bsz, WINDOW, KV_WIDTH), F32),
        jax.ShapeDtypeStruct((bsz, WINDOW, KV_WIDTH), F32),
        jax.ShapeDtypeStruct((bsz, CONV_W - 1, CONV_DIM), F32),
        jax.ShapeDtypeStruct((bsz, SSM_WIDTH, D_STATE), F32),
    )
    out_specs = (
        pl.BlockSpec((None, tm, D_MODEL), lambda b, s: (b, s, 0)),
        pl.BlockSpec((None, WINDOW, KV_WIDTH), lambda b, s: (b, 0, 0)),
        pl.BlockSpec((None, WINDOW, KV_WIDTH), lambda b, s: (b, 0, 0)),
        pl.BlockSpec((None, CONV_W - 1, CONV_DIM), lambda b, s: (b, 0, 0)),
        pl.BlockSpec((None, SSM_WIDTH, D_STATE), lambda b, s: (b, 0, 0)),
    )
    scratch = [
        pltpu.VMEM((tm, D_MODEL), BF16),
        pltpu.VMEM((tm, ATT_WIDTH), BF16),
        pltpu.VMEM((CHUNK + tm, KV_WIDTH), BF16),
        pltpu.VMEM((nchunk + 1, KV_WIDTH, CHUNK), BF16),
        pltpu.VMEM((SUBLANES + tm, CONV_DIM), F32),
        pltpu.VMEM((tm, CONV_DIM), F32),
        pltpu.VMEM((tm, LANES), F32),
        pltpu.VMEM((tm, SSM_WIDTH), F32),
        pltpu.VMEM((tm, MIX_WIDTH), BF16),
        pltpu.VMEM((D_STATE, SSM_WIDTH), F32),
    ]
    return pl.pallas_call(
        _prompt_kernel,
        grid=(bsz, seq // tm),
        in_specs=in_specs,
        out_specs=out_specs,
        out_shape=out_shape,
        scratch_shapes=scratch,
        compiler_params=pltpu.CompilerParams(
            dimension_semantics=("arbitrary", "arbitrary"), vmem_limit_bytes=VMEM_LIMIT_BYTES),
        name="prompt_mixer",
    )(x, *consts, c, s1, s2, sink, bias, *tail)


def _sample_mixer(x, cache_k, cache_v, state_conv, state_ssm, p):
    nb, tdec, _ = x.shape
    bt = SAMPLE_BT
    rows = bt * tdec
    c, s1, s2 = _rope_tables(np.tile(PAST_LEN + np.arange(tdec), bt))
    sink = jnp.repeat(p["sinks_gk"].reshape(-1), tdec).reshape(rows, 1)
    consts = [p[n] for n in _WEIGHT_ORDER]
    tail = [p[n] for n in _TAIL_ORDER]
    x2 = x.reshape(nb * tdec, D_MODEL)
    ck = cache_k.reshape(nb, WINDOW, KV_WIDTH)
    cv = cache_v.reshape(nb, WINDOW, KV_WIDTH)
    ssm = state_ssm.reshape(nb, SSM_WIDTH, D_STATE)
    pb = SAMPLE_PB
    steps = bt // pb
    t_of_row = (np.arange(rows) % tdec)[:, None]
    col = np.arange(rows)[None, :]
    bias_c = jnp.asarray(np.where(col >= t_of_row, 0.0, NEG).astype(np.float32))
    bias_n = jnp.asarray(np.stack([np.where((col // tdec == b) & (col % tdec <= t_of_row), 0.0, NEG)
                                   for b in range(bt)]).astype(np.float32))
    tmap = lambda i, j: (i, 0, 0)
    pmap = lambda i, j: (i * steps + j, 0, 0)
    in_specs = ([pl.BlockSpec((rows, D_MODEL), lambda i, j: (i, 0)),
                 pl.BlockSpec((pb, WINDOW, KV_WIDTH), pmap),
                 pl.BlockSpec((pb, WINDOW, KV_WIDTH), pmap),
                 pl.BlockSpec((bt, CONV_W - 1, CONV_DIM), tmap),
                 pl.BlockSpec((pb, SSM_WIDTH, D_STATE), pmap)]
                + [_const_spec(a.shape) for a in consts]
                + [_const_spec(c.shape)] * 3
                + [_const_spec(sink.shape), _const_spec(bias_c.shape), _const_spec(bias_n.shape)]
                + [_const_spec(a.shape) for a in tail])
    out_shape = (
        jax.ShapeDtypeStruct((nb * tdec, D_MODEL), F32),
        jax.ShapeDtypeStruct((nb, WINDOW, KV_WIDTH), F32),
        jax.ShapeDtypeStruct((nb, WINDOW, KV_WIDTH), F32),
        jax.ShapeDtypeStruct((nb, CONV_W - 1, CONV_DIM), F32),
        jax.ShapeDtypeStruct((nb, SSM_WIDTH, D_STATE), F32),
    )
    out_specs = (
        pl.BlockSpec((rows, D_MODEL), lambda i, j: (i, 0)),
        pl.BlockSpec((pb, WINDOW, KV_WIDTH), pmap),
        pl.BlockSpec((pb, WINDOW, KV_WIDTH), pmap),
        pl.BlockSpec((bt, CONV_W - 1, CONV_DIM), tmap),
        pl.BlockSpec((pb, SSM_WIDTH, D_STATE), pmap),
    )
    scratch = [
        pltpu.VMEM((rows, D_MODEL), BF16),
        pltpu.VMEM((rows, ATT_WIDTH), F32),
        pltpu.VMEM((rows, KV_WIDTH), F32),
        pltpu.VMEM((rows, KV_WIDTH), F32),
        pltpu.VMEM((rows, KV_WIDTH), BF16),
        pltpu.VMEM((rows, KV_WIDTH), BF16),
        pltpu.VMEM((bt, 2 * SUBLANES, CONV_DIM), F32),
        pltpu.VMEM((rows, CONV_DIM), F32),
        pltpu.VMEM((rows, SSM_GROUPS * D_STATE), BF16),
        pltpu.VMEM((rows, LANES), F32),
        pltpu.VMEM((rows, SSM_WIDTH), F32),
        pltpu.VMEM((SSM_WIDTH, rows), BF16),
        pltpu.VMEM((rows, SSM_WIDTH), F32),
        pltpu.VMEM((rows, SSM_WIDTH), F32),
        pltpu.VMEM((rows, MIX_WIDTH), F32),
    ]
    return pl.pallas_call(
        _sample_kernel,
        grid=(nb // bt, steps),
        in_specs=in_specs,
        out_specs=out_specs,
        out_shape=out_shape,
        scratch_shapes=scratch,
        compiler_params=pltpu.CompilerParams(
            dimension_semantics=("arbitrary", "arbitrary"), vmem_limit_bytes=VMEM_LIMIT_BYTES),
        name="sample_mixer",
    )(x2, ck, cv, state_conv, ssm, *consts, c, s1, s2, sink, bias_c, bias_n, *tail)


def _ffn(x2, gpre, wg, wu, wd, gpost):
    n = x2.shape[0]
    tf = FFN_TILE
    consts = [gpre, wg, wu, wd, gpost]
    return pl.pallas_call(
        _ffn_kernel,
        grid=(n // tf,),
        in_specs=[pl.BlockSpec((tf, D_MODEL), lambda i: (i, 0))] + [_const_spec(a.shape) for a in consts],
        out_specs=pl.BlockSpec((tf, D_MODEL), lambda i: (i, 0)),
        out_shape=jax.ShapeDtypeStruct((n, D_MODEL), F32),
        scratch_shapes=[pltpu.VMEM((tf, D_MODEL), BF16)],
        compiler_params=pltpu.CompilerParams(
            dimension_semantics=("arbitrary",), vmem_limit_bytes=VMEM_LIMIT_BYTES),
        name="ffn",
    )(x2, *consts)


def kernel(x_prompt, x_sample, cache_k_win, cache_v_win, state_conv, state_ssm, g_pre_mix, w_in, attn_sinks, conv_w, conv_b, dt_bias, a_log, d_skip, g_ssm_out, w_out, g_post_mix, g_pre_ffn, w_gate, w_up, w_down, g_post_ffn):
    depth = w_in.shape[0]
    bp, lp, _ = x_prompt.shape
    nb, ts, _ = x_sample.shape
    hp, hs = x_prompt, x_sample
    outs = [[] for _ in range(8)]
    for l in range(depth):
        p = _layer_params(g_pre_mix[l], w_in[l], attn_sinks[l], conv_w[l], conv_b[l], dt_bias[l], a_log[l],
                          d_skip[l], g_ssm_out[l], w_out[l], g_post_mix[l])
        ffn_w = (g_pre_ffn[l].reshape(1, D_MODEL), w_gate[l].astype(BF16), w_up[l].astype(BF16),
                 w_down[l].astype(BF16), g_post_ffn[l].reshape(1, D_MODEL))
        x1p, kp, vp, cp, sp = _prompt_mixer(hp, p)
        x1s, ksm, vsm, csm, ssm = _sample_mixer(hs, cache_k_win[l], cache_v_win[l], state_conv[l], state_ssm[l], p)
        hp = _ffn(x1p.reshape(bp * lp, D_MODEL), *ffn_w).reshape(bp, lp, D_MODEL)
        hs = _ffn(x1s, *ffn_w).reshape(nb, ts, D_MODEL)
        kv_shape = (WINDOW, ATT_KV_HEADS, HEAD_DIM)
        ssm_shape = (SSM_HEADS, SSM_HEAD_DIM, D_STATE)
        for lst, val in zip(outs, (kp.reshape((bp,) + kv_shape), vp.reshape((bp,) + kv_shape), cp,
                                   sp.reshape((bp,) + ssm_shape),
                                   ksm.reshape((nb,) + kv_shape), vsm.reshape((nb,) + kv_shape), csm,
                                   ssm.reshape((nb,) + ssm_shape))):
            lst.append(val)
    return (hp, hs) + tuple(jnp.stack(o) for o in outs)
```

```python
import functools
import math

import numpy as np
import jax
import jax.numpy as jnp
from jax import lax
from jax.experimental import pallas as pl
from jax.experimental.pallas import tpu as pltpu

F32 = jnp.float32
BF16 = jnp.bfloat16

D_MODEL = 1024
ATT_HEADS = 16
ATT_KV_HEADS = 4
ATT_GROUP = ATT_HEADS // ATT_KV_HEADS
HEAD_DIM = 64
ATT_WIDTH = ATT_HEADS * HEAD_DIM
KV_WIDTH = ATT_KV_HEADS * HEAD_DIM
WINDOW = 128
ROT_DIM = HEAD_DIM // 4
ROPE_THETA = 500000.0
SSM_HEADS = 16
SSM_HEAD_DIM = 64
SSM_WIDTH = SSM_HEADS * SSM_HEAD_DIM
SSM_GROUPS = 2
SSM_HPG = SSM_HEADS // SSM_GROUPS
SSM_GROUP_W = SSM_WIDTH // SSM_GROUPS
D_STATE = 128
CONV_W = 4
CONV_DIM = SSM_WIDTH + 2 * SSM_GROUPS * D_STATE
MIX_WIDTH = ATT_WIDTH + SSM_WIDTH
EPS = 1e-6
PAST_LEN = 8192

LANES = 128
SUBLANES = 8
VMEM_LIMIT_BYTES = 60 * 1024 * 1024

CHUNK = 128
NEG = -1e30
LOG2E = math.log2(math.e)
Q_SCALE = HEAD_DIM ** -0.5 * LOG2E
SEQ_TILE = 512
SAMPLE_BT = 16
SAMPLE_PB = 4
FFN_TILE = 512
ROW_BLOCK = 32
FF_CHUNK = 256


def _nn(a, b):
    return jnp.dot(a, b, preferred_element_type=F32)


def _nt(a, b):
    return lax.dot_general(a, b, (((1,), (1,)), ((), ())), preferred_element_type=F32)


def _split_bf16(x, n):
    parts = []
    r = x
    for i in range(n):
        p = r.astype(BF16)
        parts.append(p)
        if i + 1 < n:
            r = r - p.astype(F32)
    return parts


def _expand_heads(x, expand2_ref):
    hi, mid = _split_bf16(x, 2)
    return _nn(jnp.concatenate([hi, mid], axis=1), expand2_ref[...])


def _cumsum_cols(m01, x):
    w = x.shape[1]
    r = _nn(m01, jnp.concatenate(_split_bf16(x, 3), axis=1))
    return r[:, :w] + r[:, w:2 * w] + r[:, 2 * w:]


def _heads_to_rows(x):
    return x.T[:SSM_HEADS]


def _rms(x, g):
    ms = jnp.mean(x * x, axis=-1, keepdims=True)
    return x * lax.rsqrt(ms + EPS) * g


def _rms_rows(src, g_ref, dst_ref, res_ref=None):
    g = g_ref[...]
    for r0 in range(0, dst_ref.shape[0], ROW_BLOCK):
        rs = slice(r0, r0 + ROW_BLOCK)
        y = _rms(src[rs, :], g)
        if res_ref is not None:
            y = res_ref[rs, :] + y
        dst_ref[rs, :] = y.astype(dst_ref.dtype)


def _silu(x):
    h = 0.5 * x
    return h + h * jnp.tanh(h)


R_K = ATT_WIDTH
R_V = R_K + KV_WIDTH
R_Z = R_V + KV_WIDTH
R_XBC = R_Z + SSM_WIDTH
R_END = R_XBC + CONV_DIM

CONV_CB = 256
CONV_RB = 64


def _conv_silu_cols(xpad_ref, row0, rows, convw_ref, convb_ref, out_ref, c0):
    cs = slice(c0, c0 + CONV_CB)
    w = [convw_ref[i:i + 1, cs] for i in range(CONV_W)]
    bias = convb_ref[:, cs]
    for r0 in range(0, rows, CONV_RB):
        xh = xpad_ref[row0 + r0 - SUBLANES:row0 + r0 + CONV_RB, cs]
        acc = bias + xh[SUBLANES:] * w[CONV_W - 1]
        for i in range(CONV_W - 1):
            acc = acc + pltpu.roll(xh, CONV_W - 1 - i, 0)[SUBLANES:] * w[i]
        out_ref[r0:r0 + CONV_RB, cs] = _silu(acc)


def _softplus(x):
    return jnp.maximum(x, 0.0) + jnp.log1p(jnp.exp(-jnp.abs(x)))


def _rope(x, c, s1, s2):
    outs = []
    for j in range(x.shape[1] // LANES):
        xb = x[:, j * LANES:(j + 1) * LANES]
        outs.append(xb * c + pltpu.roll(xb, LANES - ROT_DIM // 2, 1) * s1 + pltpu.roll(xb, ROT_DIM // 2, 1) * s2)
    return outs[0] if len(outs) == 1 else jnp.concatenate(outs, axis=1)


def _iota(shape, dim):
    return lax.broadcasted_iota(jnp.int32, shape, dim)


def _head_blocks(c_out):
    out = []
    for half in range(2):
        g, kvh = divmod(2 * c_out + half, ATT_KV_HEADS)
        b_in = kvh * ATT_GROUP + g
        out.append((b_in // 2, b_in % 2))
    return out


def _permute_q_weight(wr_ref, wq_s):
    low = _iota((1, LANES), 1) < HEAD_DIM
    for c_out in range(ATT_WIDTH // LANES):
        halves = []
        for half, (c_in, src_half) in enumerate(_head_blocks(c_out)):
            col = wr_ref[:, c_in * LANES:(c_in + 1) * LANES]
            halves.append(col if src_half == half else pltpu.roll(col, HEAD_DIM, 1))
        wq_s[:, c_out * LANES:(c_out + 1) * LANES] = jnp.where(low, halves[0], halves[1])


def _project(hn, wq_ref, wr_ref, ropec, ropes1, ropes2):
    q = _rope(_nn(hn, wq_ref[...]), ropec, ropes1, ropes2) * (HEAD_DIM ** -0.5)
    kv = _nn(hn, wr_ref[:, R_K:R_Z])
    k = _rope(kv[:, :KV_WIDTH], ropec, ropes1, ropes2)
    v = kv[:, KV_WIDTH:]
    return q, k, v


def _dt_cols(hn, wdt_ref, dtb_row_ref):
    return _softplus(_nn(hn, wdt_ref[...]) + dtb_row_ref[...])


def _a_row(alog_row_ref):
    lane = _iota((1, LANES), 1)
    return jnp.where(lane < SSM_HEADS, -jnp.exp(alog_row_ref[...]), 0.0)


def _log2_decay(acol, arow, dtr):
    return acol * LOG2E, (arow - jnp.log(dtr)) * LOG2E


SSD_QUAD = 4
SSD_NQUAD = SSM_HEADS // SSD_QUAD


def _ssd_cb(b_all, c_all):
    return [_nt(c_all[:, g * D_STATE:(g + 1) * D_STATE].astype(BF16),
                b_all[:, g * D_STATE:(g + 1) * D_STATE].astype(BF16)) for g in range(SSM_GROUPS)]


def _ssd_quad(qi, xs_bf, c_all, cbs, acol2, arow2, mask_bool, hT_bf=None):
    lane4 = _iota((1, SSD_QUAD * SSM_HEAD_DIM), 1) // SSM_HEAD_DIM
    zero = jnp.zeros((), BF16)
    e0 = SSD_QUAD * qi
    g = e0 // SSM_HPG
    cf = c_all[:, g * D_STATE:(g + 1) * D_STATE]
    lanes = slice(e0 * SSM_HEAD_DIM, (e0 + SSD_QUAD) * SSM_HEAD_DIM)
    xq = xs_bf[:, lanes]
    lhs, rhs = [], []
    for i in range(SSD_QUAD):
        e = e0 + i
        a_t = jnp.broadcast_to(acol2[:, e:e + 1], (CHUNK, CHUNK))
        w = cbs[g] * jnp.exp2(jnp.where(mask_bool, a_t - arow2[e:e + 1, :], NEG))
        lhs.append(w.astype(BF16))
        rhs.append(jnp.where(lane4 == i, xq, zero))
        if hT_bf is not None:
            lhs.append((cf * jnp.exp2(a_t)).astype(BF16))
            rhs.append(jnp.where(lane4 == i, hT_bf[:, lanes], zero))
    return _nn(jnp.concatenate(lhs, axis=1), jnp.concatenate(rhs, axis=0))


def _ssd_block(xs_bf, b_all, c_all, acol2, arow2, mask_bool):
    cbs = _ssd_cb(b_all, c_all)
    return jnp.concatenate([_ssd_quad(qi, xs_bf, c_all, cbs, acol2, arow2, mask_bool)
                            for qi in range(SSD_NQUAD)], axis=1)


def _gate_and_out(y_refs, xc_s, z, x_ref, dskip_ref, gssm_ref, wout_ref, gpost_ref, mix_s, out_ref):
    dskip = dskip_ref[...]
    for r0 in range(0, out_ref.shape[0], ROW_BLOCK):
        rs = slice(r0, r0 + ROW_BLOCK)
        y = y_refs[0][rs, :]
        for extra in y_refs[1:]:
            y = y + extra[rs, :]
        gated = (y + dskip * xc_s[rs, 0:SSM_WIDTH]) * _silu(z[rs, :])
        for g in range(SSM_GROUPS):
            gs = slice(g * SSM_GROUP_W, (g + 1) * SSM_GROUP_W)
            gg = gated[:, gs]
            ms = jnp.mean(gg * gg, axis=-1, keepdims=True)
            o = gg * lax.rsqrt(ms + EPS) * gssm_ref[:, gs]
            mix_s[rs, ATT_WIDTH + g * SSM_GROUP_W:ATT_WIDTH + (g + 1) * SSM_GROUP_W] = o.astype(mix_s.dtype)
    mo = _nn(mix_s[...].astype(BF16), wout_ref[...])
    _rms_rows(mo, gpost_ref, out_ref, res_ref=x_ref)


def _prompt_kernel(x_ref, gpre_ref, wr_ref, wdt_ref,
                   ropec_ref, ropes1_ref, ropes2_ref, sink_ref, biasT_ref,
                   convw_ref, convb_ref, dtb_row_ref, alog_row_ref,
                   dskip_ref, gssm_ref, wout_ref, gpost_ref, expand2_ref,
                   x1_ref, nk_ref, nv_ref, nconv_ref, nssm_ref,
                   hn_s, q_s, kbuf, vT_s, xbc_s, xc_s, dtc_s, y_s, mix_s, hT_s, wq_s):
    tm = x_ref.shape[0]
    nchunk = tm // CHUNK
    s = pl.program_id(1)
    last = pl.num_programs(1) - 1

    @pl.when(jnp.logical_and(pl.program_id(0) == 0, s == 0))
    def _():
        _permute_q_weight(wr_ref, wq_s)

    @pl.when(s == 0)
    def _():
        kbuf[0:CHUNK, :] = jnp.zeros((CHUNK, KV_WIDTH), BF16)
        vT_s[0] = jnp.zeros((KV_WIDTH, CHUNK), BF16)
        xbc_s[0:SUBLANES, :] = jnp.zeros((SUBLANES, CONV_DIM), F32)
        hT_s[...] = jnp.zeros_like(hT_s)

    _rms_rows(x_ref, gpre_ref, hn_s)
    hn = hn_s[...]

    ropec, ropes1, ropes2 = ropec_ref[...], ropes1_ref[...], ropes2_ref[...]

    def proj_xbc(c0):
        xbc_s[SUBLANES:SUBLANES + tm, c0:c0 + CONV_CB] = _nn(hn, wr_ref[:, R_XBC + c0:R_XBC + c0 + CONV_CB])

    def proj_q(c0):
        q_s[:, c0:c0 + KV_WIDTH] = (
            _rope(_nn(hn, wq_s[:, c0:c0 + KV_WIDTH]), ropec, ropes1, ropes2) * Q_SCALE).astype(BF16)

    def proj_k():
        kbuf[CHUNK:CHUNK + tm, :] = _rope(_nn(hn, wr_ref[:, R_K:R_V]), ropec, ropes1, ropes2).astype(BF16)

    def proj_v_dt():
        v = _nn(hn, wr_ref[:, R_V:R_Z])
        for j in range(nchunk):
            vT_s[1 + j] = v[j * CHUNK:(j + 1) * CHUNK, :].T.astype(BF16)
        dtc_s[...] = _dt_cols(hn, wdt_ref, dtb_row_ref)

    others = [functools.partial(proj_q, c0) for c0 in range(0, ATT_WIDTH, KV_WIDTH)] + [proj_k, proj_v_dt]
    conv_cols = list(range(0, CONV_DIM, CONV_CB))
    proj_xbc(conv_cols[0])
    for n, c0 in enumerate(conv_cols):
        if n + 1 < len(conv_cols):
            proj_xbc(conv_cols[n + 1])
        _conv_silu_cols(xbc_s, SUBLANES, tm, convw_ref, convb_ref, xc_s, c0)
        if n < len(others):
            others[n]()
    for f in others[len(conv_cols):]:
        f()

    a_row = _a_row(alog_row_ref)

    r2 = _iota((CHUNK, CHUNK), 0)
    c2 = _iota((CHUNK, CHUNK), 1)
    tril = c2 <= r2
    tril_bf = tril.astype(BF16)
    lane_kv = _iota((1, KV_WIDTH), 1) // HEAD_DIM
    ones_rows = jnp.ones((2 * SUBLANES, 2 * CHUNK), BF16)

    def chunk_body(c, carry):
        r0 = pl.multiple_of(c * CHUNK, CHUNK)
        first = jnp.logical_and(s == 0, c == 0).astype(jnp.int32)
        bias = biasT_ref[first]
        qcat = jnp.concatenate([q_s[pl.ds(r0, CHUNK), g * KV_WIDTH:(g + 1) * KV_WIDTH]
                                for g in range(ATT_GROUP)], axis=0)
        kwin = kbuf[pl.ds(r0, 2 * CHUNK), :]
        kstack = jnp.concatenate([jnp.where(lane_kv == kvh, kwin, jnp.zeros((), BF16))
                                  for kvh in range(ATT_KV_HEADS)], axis=0)
        sT = _nt(kstack, qcat)
        vT_win = jnp.concatenate([vT_s[c], vT_s[c + 1]], axis=1)

        dtc_c = dtc_s[pl.ds(r0, CHUNK), :]
        dtr_c = _heads_to_rows(dtc_c)
        acol = _cumsum_cols(tril_bf, dtc_c * a_row)
        arow = _heads_to_rows(acol)
        a_end = acol[CHUNK - 1:CHUNK, :]
        tailc = jnp.exp(a_end - acol) * dtc_c
        ex = _expand_heads(
            jnp.concatenate([tailc, jnp.broadcast_to(jnp.exp(a_end), (SUBLANES, LANES))], axis=0), expand2_ref)
        tlx = ex[:CHUNK]
        dec_row = ex[CHUNK:CHUNK + 1]
        xs = xc_s[pl.ds(r0, CHUNK), 0:SSM_WIDTH]
        b_all = xc_s[pl.ds(r0, CHUNK), SSM_WIDTH:SSM_WIDTH + SSM_GROUPS * D_STATE]
        c_all = xc_s[pl.ds(r0, CHUNK), SSM_WIDTH + SSM_GROUPS * D_STATE:CONV_DIM]
        hT = hT_s[...]
        acol2, arow2 = _log2_decay(acol, arow, dtr_c)
        xs_bf = xs.astype(BF16)
        hT_bf = hT.astype(BF16)
        cbs = _ssd_cb(b_all, c_all)

        o_rows = []
        for i in range(ATT_KV_HEADS):
            blk = sT[i * 2 * CHUNK:(i + 1) * 2 * CHUNK] + bias
            sink = sink_ref[i] * LOG2E
            m = jnp.maximum(jnp.max(blk, axis=0, keepdims=True), sink)
            p = jnp.exp2(blk - m).astype(BF16)
            lhs = jnp.concatenate([vT_win[i * HEAD_DIM:(i + 1) * HEAD_DIM], ones_rows], axis=0)
            oT = _nn(lhs, p)
            den = oT[HEAD_DIM:HEAD_DIM + 1] + jnp.exp2(sink - m)
            o_rows.append(oT[:HEAD_DIM] * (1.0 / den))
            for qi in range(i * SSD_NQUAD // ATT_KV_HEADS, (i + 1) * SSD_NQUAD // ATT_KV_HEADS):
                lanes = slice(qi * SSD_QUAD * SSM_HEAD_DIM, (qi + 1) * SSD_QUAD * SSM_HEAD_DIM)
                y_s[pl.ds(r0, CHUNK), lanes] = _ssd_quad(qi, xs_bf, c_all, cbs, acol2, arow2, tril, hT_bf)
        for c_out in range(ATT_WIDTH // LANES):
            kvh, g0 = divmod(2 * c_out, ATT_GROUP)
            two = jnp.concatenate([o_rows[kvh][:, g * CHUNK:(g + 1) * CHUNK] for g in (g0, g0 + 1)], axis=0)
            mix_s[pl.ds(r0, CHUNK), c_out * LANES:(c_out + 1) * LANES] = two.T.astype(BF16)

        xtl = (xs * tlx).astype(BF16)
        for g in range(SSM_GROUPS):
            sl = slice(g * SSM_GROUP_W, (g + 1) * SSM_GROUP_W)
            bt = b_all[:, g * D_STATE:(g + 1) * D_STATE].T.astype(BF16)
            hT_s[:, sl] = hT[:, sl] * dec_row[:, sl] + _nn(bt, xtl[:, sl])
        return carry

    lax.fori_loop(0, nchunk, chunk_body, 0)

    kbuf[0:CHUNK, :] = kbuf[tm:tm + CHUNK, :]
    vT_s[0] = vT_s[nchunk]
    xbc_s[0:SUBLANES, :] = xbc_s[tm:tm + SUBLANES, :]

    z = _nn(hn_s[...], wr_ref[:, R_Z:R_XBC])
    _gate_and_out((y_s,), xc_s, z, x_ref, dskip_ref, gssm_ref, wout_ref, gpost_ref, mix_s, x1_ref)

    @pl.when(s == last)
    def _():
        hn_w = hn_s[tm - WINDOW:, :]
        nk_ref[...] = _rope(_nn(hn_w, wr_ref[:, R_K:R_V]), ropec_ref[tm - WINDOW:, :],
                            ropes1_ref[tm - WINDOW:, :], ropes2_ref[tm - WINDOW:, :])
        nv_ref[...] = _nn(hn_w, wr_ref[:, R_V:R_Z])
        nconv_ref[...] = xbc_s[SUBLANES - (CONV_W - 1):SUBLANES, :]
        nssm_ref[...] = hT_s[...].T


def _sample_kernel(x_ref, ck_ref, cv_ref, sconv_ref, sssm_ref,
                   gpre_ref, wr_ref, wdt_ref,
                   ropec_ref, ropes1_ref, ropes2_ref, sink_ref, biasc_ref, biasn_ref,
                   convw_ref, convb_ref, dtb_row_ref, alog_row_ref,
                   dskip_ref, gssm_ref, wout_ref, gpost_ref, expand2_ref,
                   x1_ref, nk_ref, nv_ref, nconv_ref, nssm_ref,
                   hn_s, q_s, kn_s, vn_s, knb_s, vnb_s, xpad_s, xc_s, bb_s, ea_s, eax_s, xT_s, y_s, yoff_s, mix_s, wq_s):
    bt_n = sconv_ref.shape[0]
    pb_n = ck_ref.shape[0]
    m_rows = x_ref.shape[0]
    tdec = m_rows // bt_n
    j = pl.program_id(1)

    @pl.when(jnp.logical_and(pl.program_id(0) == 0, j == 0))
    def _():
        _permute_q_weight(wr_ref, wq_s)

    @pl.when(j == 0)
    def _():
        _rms_rows(x_ref, gpre_ref, hn_s)
        hn = hn_s[...]
        q, k, v = _project(hn, wq_s, wr_ref, ropec_ref[...], ropes1_ref[...], ropes2_ref[...])
        q_s[...] = q
        kn_s[...] = k
        vn_s[...] = v
        knb_s[...] = k.astype(BF16)
        vnb_s[...] = v.astype(BF16)

        xbc = _nn(hn, wr_ref[:, R_XBC:R_END])
        xpad_s[:, 0:SUBLANES - 3, :] = jnp.zeros((bt_n, SUBLANES - 3, CONV_DIM), F32)
        xpad_s[:, SUBLANES - 3:SUBLANES, :] = sconv_ref[...]
        xpad_s[:, SUBLANES:2 * SUBLANES, :] = xbc.reshape(bt_n, tdec, CONV_DIM)
        nconv_ref[...] = xpad_s[:, 2 * SUBLANES - 3:2 * SUBLANES, :]
        cb = 256
        for c0 in range(0, CONV_DIM, cb):
            cs = slice(c0, c0 + cb)
            xh = xpad_s[:, :, cs].reshape(bt_n * 2 * SUBLANES, cb)

            def new_rows(a):
                return a.reshape(bt_n, 2 * SUBLANES, cb)[:, SUBLANES:, :].reshape(m_rows, cb)

            acc = convb_ref[:, cs] + new_rows(xh) * convw_ref[CONV_W - 1:CONV_W, cs]
            for i in range(CONV_W - 1):
                acc = acc + new_rows(pltpu.roll(xh, CONV_W - 1 - i, 0)) * convw_ref[i:i + 1, cs]
            xc_s[:, cs] = _silu(acc)
        xs = xc_s[:, 0:SSM_WIDTH]
        b_all = xc_s[:, SSM_WIDTH:SSM_WIDTH + SSM_GROUPS * D_STATE]
        c_all = xc_s[:, SSM_WIDTH + SSM_GROUPS * D_STATE:CONV_DIM]
        bb_s[...] = b_all.astype(BF16)

        dtc = _dt_cols(hn, wdt_ref, dtb_row_ref)
        dtr = _heads_to_rows(dtc)
        a_row = _a_row(alog_row_ref)

        r2 = _iota((m_rows, m_rows), 0)
        c2 = _iota((m_rows, m_rows), 1)
        same = (r2 // tdec) == (c2 // tdec)
        causal = jnp.logical_and(same, c2 <= r2)
        causal_bf = causal.astype(BF16)
        same_bf = same.astype(BF16)

        dac = dtc * a_row
        acol = _cumsum_cols(causal_bf, dac)
        alast = _cumsum_cols(same_bf, dac)
        arow = _heads_to_rows(acol)
        tailc = jnp.exp(alast - acol) * dtc
        ex = _expand_heads(jnp.concatenate([jnp.exp(acol), tailc], axis=0), expand2_ref)
        ea_s[...] = jnp.exp(alast)
        eax_s[...] = ex[:m_rows]
        acol2, arow2 = _log2_decay(acol, arow, dtr)
        y_s[...] = _ssd_block(xs.astype(BF16), b_all, c_all, acol2, arow2, causal)
        xtl = xs * ex[m_rows:]
        for jj in range(SSM_WIDTH // LANES):
            xT_s[jj * LANES:(jj + 1) * LANES, :] = xtl[:, jj * LANES:(jj + 1) * LANES].T.astype(BF16)

    lane_kv = _iota((1, KV_WIDTH), 1) // HEAD_DIM
    row_b = _iota((m_rows, 1), 0) // tdec
    low_half = _iota((1, LANES), 1) < HEAD_DIM
    sink = sink_ref[...]

    pbs = range(pb_n)
    bs = [j * pb_n + pb for pb in pbs]
    rs = [pl.multiple_of(b * tdec, tdec) for b in bs]

    qbd, kc, vc, sc_c, sc_n = [], [], [], [], []
    for pb in pbs:
        q8 = q_s[pl.ds(rs[pb], tdec), :]
        qbd.append(jnp.concatenate(
            [jnp.where(lane_kv == kvh, q8[:, g * KV_WIDTH:(g + 1) * KV_WIDTH], 0.0)
             for g in range(ATT_GROUP) for kvh in range(ATT_KV_HEADS)], axis=0).astype(BF16))
        kc.append(ck_ref[pb])
        vc.append(cv_ref[pb])
    for pb in pbs:
        sc_c.append(_nt(qbd[pb], kc[pb].astype(BF16)) + biasc_ref[...])
        sc_n.append(_nt(qbd[pb], knb_s[...]) + biasn_ref[bs[pb]])
    p_c, p_n, inv = [], [], []
    for pb in pbs:
        m = jnp.maximum(jnp.maximum(jnp.max(sc_c[pb], axis=1, keepdims=True),
                                    jnp.max(sc_n[pb], axis=1, keepdims=True)), sink)
        pc = jnp.exp(sc_c[pb] - m)
        pn = jnp.exp(sc_n[pb] - m)
        den = jnp.sum(pc, axis=1, keepdims=True) + jnp.sum(pn, axis=1, keepdims=True) + jnp.exp(sink - m)
        p_c.append(pc.astype(BF16))
        p_n.append(pn.astype(BF16))
        inv.append(1.0 / den)
    for pb in pbs:
        o = (_nn(p_c[pb], vc[pb].astype(BF16)) + _nn(p_n[pb], vnb_s[...])) * inv[pb]
        for c_out in range(ATT_WIDTH // LANES):
            kvh, g0 = divmod(2 * c_out, ATT_GROUP)
            halves = []
            for half in range(2):
                i0 = ((g0 + half) * ATT_KV_HEADS + kvh) * tdec
                piece = o[i0:i0 + tdec, (kvh // 2) * LANES:(kvh // 2 + 1) * LANES]
                halves.append(piece if kvh % 2 == half else pltpu.roll(piece, HEAD_DIM, 1))
            mix_s[pl.ds(rs[pb], tdec), c_out * LANES:(c_out + 1) * LANES] = jnp.where(low_half, halves[0], halves[1])
        nk_ref[pb, 0:WINDOW - tdec, :] = kc[pb][tdec:, :]
        nk_ref[pb, WINDOW - tdec:WINDOW, :] = kn_s[pl.ds(rs[pb], tdec), :]
        nv_ref[pb, 0:WINDOW - tdec, :] = vc[pb][tdec:, :]
        nv_ref[pb, WINDOW - tdec:WINDOW, :] = vn_s[pl.ds(rs[pb], tdec), :]

    for g in range(SSM_GROUPS):
        sl = slice(g * SSM_GROUP_W, (g + 1) * SSM_GROUP_W)
        c0 = SSM_WIDTH + (SSM_GROUPS + g) * D_STATE
        hg, upd = [], []
        for pb in pbs:
            hg.append(sssm_ref[pb, sl, :])
            bm = jnp.where(row_b == bs[pb], bb_s[:, g * D_STATE:(g + 1) * D_STATE], jnp.zeros((), BF16))
            upd.append(_nn(xT_s[sl, :], bm))
        for pb in pbs:
            cc = xc_s[pl.ds(rs[pb], tdec), c0:c0 + D_STATE].astype(BF16)
            yoff_s[pl.ds(rs[pb], tdec), sl] = (_nt(cc, hg[pb].astype(BF16))
                                               * eax_s[pl.ds(rs[pb], tdec), sl])
        for pb in pbs:
            ea_b = ea_s[pl.ds(rs[pb], 1), :]
            dec = jnp.concatenate(
                [jnp.broadcast_to(ea_b[:, g * SSM_HPG + e:g * SSM_HPG + e + 1], (SSM_HEAD_DIM, D_STATE))
                 for e in range(SSM_HPG)], axis=0)
            nssm_ref[pb, sl, :] = hg[pb] * dec + upd[pb]

    @pl.when(j == pl.num_programs(1) - 1)
    def _():
        z = _nn(hn_s[...], wr_ref[:, R_Z:R_XBC])
        _gate_and_out((y_s, yoff_s), xc_s, z, x_ref, dskip_ref, gssm_ref, wout_ref, gpost_ref, mix_s, x1_ref)


def _ffn_kernel(x_ref, gpre_ref, wg_ref, wu_ref, wd_ref, gpost_ref, o_ref, f_s):
    _rms_rows(x_ref, gpre_ref, f_s)
    f = f_s[...]
    d_ff = wg_ref.shape[1]
    acc = jnp.zeros(x_ref.shape, F32)
    for j in range(d_ff // FF_CHUNK):
        sl = slice(j * FF_CHUNK, (j + 1) * FF_CHUNK)
        gate = _nn(f, wg_ref[:, sl])
        up = _nn(f, wu_ref[:, sl])
        acc = acc + _nn((_silu(gate) * up).astype(BF16), wd_ref[sl, :])
    _rms_rows(acc, gpost_ref, o_ref, res_ref=x_ref)


def _const_spec(shape):
    nd = len(shape)
    return pl.BlockSpec(shape, lambda *_: (0,) * nd, pipeline_mode=pl.Buffered(1))


def _rope_tables(pos):
    half = ROT_DIM // 2
    inv = ROPE_THETA ** (-np.arange(half, dtype=np.float64) * 2.0 / ROT_DIM)
    ang = pos.astype(np.float64)[:, None] * inv[None, :]
    cos = np.cos(ang).astype(np.float32)
    sin = np.sin(ang).astype(np.float32)
    n = pos.shape[0]
    pad = HEAD_DIM - ROT_DIM
    c = np.concatenate([cos, cos, np.ones((n, pad), np.float32)], axis=1)
    s1 = np.concatenate([-sin, np.zeros((n, half + pad), np.float32)], axis=1)
    s2 = np.concatenate([np.zeros((n, half), np.float32), sin, np.zeros((n, pad), np.float32)], axis=1)
    rep = LANES // HEAD_DIM
    return tuple(jnp.asarray(np.tile(t, (1, rep))) for t in (c, s1, s2))


def _layer_params(g_pre_mix, w_in, attn_sinks, conv_w, conv_b, dt_bias, a_log, d_skip, g_ssm_out, w_out, g_post_mix):
    wr = w_in.astype(BF16)
    wdt = jnp.pad(w_in[:, R_END:], ((0, 0), (0, LANES - SSM_HEADS))).astype(BF16)
    wout = w_out.astype(BF16)
    pad16 = ((0, 0), (0, LANES - SSM_HEADS))
    expand = (np.arange(LANES)[:, None] == (np.arange(SSM_WIDTH)[None, :] // SSM_HEAD_DIM)).astype(np.float32)
    return dict(
        gpre=g_pre_mix.reshape(1, D_MODEL), wr=wr, wdt=wdt,
        convw=conv_w, convb=conv_b.reshape(1, CONV_DIM),
        dtb_row=jnp.pad(dt_bias.reshape(1, SSM_HEADS), pad16),
        alog_row=jnp.pad(a_log.reshape(1, SSM_HEADS), pad16),
        dskip=jnp.repeat(d_skip, SSM_HEAD_DIM).reshape(1, SSM_WIDTH), gssm=g_ssm_out.reshape(1, SSM_WIDTH),
        wout=wout, gpost=g_post_mix.reshape(1, D_MODEL), expand2=jnp.asarray(np.concatenate([expand, expand], axis=0), BF16),
        sinks_gk=attn_sinks.reshape(ATT_KV_HEADS, ATT_GROUP).T,
    )


_WEIGHT_ORDER = ("gpre", "wr", "wdt")
_TAIL_ORDER = ("convw", "convb", "dtb_row", "alog_row",
               "dskip", "gssm", "wout", "gpost", "expand2")


def _prompt_mixer(x, p):
    bsz, seq, _ = x.shape
    tm = SEQ_TILE
    nchunk = tm // CHUNK
    c, s1, s2 = _rope_tables(np.arange(seq))
    sink = jnp.repeat(p["sinks_gk"].T, CHUNK, axis=1)[:, None, :]
    jj = np.arange(2 * CHUNK)[:, None]
    tt = np.arange(CHUNK)[None, :]
    vis = (jj >= tt) & (jj <= tt + WINDOW)
    bias = np.stack([np.where(vis, 0.0, NEG), np.where(vis & (jj >= CHUNK), 0.0, NEG)]).astype(np.float32)
    bias = jnp.asarray(np.tile(bias, (1, 1, ATT_GROUP)))
    consts = [p[n] for n in _WEIGHT_ORDER]
    tail = [p[n] for n in _TAIL_ORDER]
    in_specs = ([pl.BlockSpec((None, tm, D_MODEL), lambda b, s: (b, s, 0))]
                + [_const_spec(a.shape) for a in consts]
                + [pl.BlockSpec((tm, LANES), lambda b, s: (s, 0))] * 3
                + [_const_spec(sink.shape), _const_spec(bias.shape)]
                + [_const_spec(a.shape) for a in tail])
    out_shape = (
        jax.ShapeDtypeStruct((bsz, seq, D_MODEL), F32),
        jax.ShapeDtypeStruct((bsz, WINDOW, KV_WIDTH), F32),
        jax.ShapeDtypeStruct((bsz, WINDOW, KV_WIDTH), F32),
        jax.ShapeDtypeStruct((bsz, CONV_W - 1, CONV_DIM), F32),
        jax.ShapeDtypeStruct((bsz, SSM_WIDTH, D_STATE), F32),
    )
    out_specs = (
        pl.BlockSpec((None, tm, D_MODEL), lambda b, s: (b, s, 0)),
        pl.BlockSpec((None, WINDOW, KV_WIDTH), lambda b, s: (b, 0, 0)),
        pl.BlockSpec((None, WINDOW, KV_WIDTH), lambda b, s: (b, 0, 0)),
        pl.BlockSpec((None, CONV_W - 1, CONV_DIM), lambda b, s: (b, 0, 0)),
        pl.BlockSpec((None, SSM_WIDTH, D_STATE), lambda b, s: (b, 0, 0)),
    )
    scratch = [
        pltpu.VMEM((tm, D_MODEL), BF16),
        pltpu.VMEM((tm, ATT_WIDTH), BF16),
        pltpu.VMEM((CHUNK + tm, KV_WIDTH), BF16),
        pltpu.VMEM((nchunk + 1, KV_WIDTH, CHUNK), BF16),
        pltpu.VMEM((SUBLANES + tm, CONV_DIM), F32),
        pltpu.VMEM((tm, CONV_DIM), F32),
        pltpu.VMEM((tm, LANES), F32),
        pltpu.VMEM((tm, SSM_WIDTH), F32),
        pltpu.VMEM((tm, MIX_WIDTH), BF16),
        pltpu.VMEM((D_STATE, SSM_WIDTH), F32),
        pltpu.VMEM((D_MODEL, ATT_WIDTH), BF16),
    ]
    return pl.pallas_call(
        _prompt_kernel,
        grid=(bsz, seq // tm),
        in_specs=in_specs,
        out_specs=out_specs,
        out_shape=out_shape,
        scratch_shapes=scratch,
        compiler_params=pltpu.CompilerParams(
            dimension_semantics=("arbitrary", "arbitrary"), vmem_limit_bytes=VMEM_LIMIT_BYTES),
        name="prompt_mixer",
    )(x, *consts, c, s1, s2, sink, bias, *tail)


def _sample_mixer(x, cache_k, cache_v, state_conv, state_ssm, p):
    nb, tdec, _ = x.shape
    bt = SAMPLE_BT
    rows = bt * tdec
    c, s1, s2 = _rope_tables(np.tile(PAST_LEN + np.arange(tdec), bt))
    sink = jnp.repeat(p["sinks_gk"].reshape(-1), tdec).reshape(rows, 1)
    consts = [p[n] for n in _WEIGHT_ORDER]
    tail = [p[n] for n in _TAIL_ORDER]
    x2 = x.reshape(nb * tdec, D_MODEL)
    ck = cache_k.reshape(nb, WINDOW, KV_WIDTH)
    cv = cache_v.reshape(nb, WINDOW, KV_WIDTH)
    ssm = state_ssm.reshape(nb, SSM_WIDTH, D_STATE)
    pb = SAMPLE_PB
    steps = bt // pb
    t_of_row = (np.arange(rows) % tdec)[:, None]
    col = np.arange(rows)[None, :]
    bias_c = jnp.asarray(np.where(col >= t_of_row, 0.0, NEG).astype(np.float32))
    bias_n = jnp.asarray(np.stack([np.where((col // tdec == b) & (col % tdec <= t_of_row), 0.0, NEG)
                                   for b in range(bt)]).astype(np.float32))
    tmap = lambda i, j: (i, 0, 0)
    pmap = lambda i, j: (i * steps + j, 0, 0)
    in_specs = ([pl.BlockSpec((rows, D_MODEL), lambda i, j: (i, 0)),
                 pl.BlockSpec((pb, WINDOW, KV_WIDTH), pmap),
                 pl.BlockSpec((pb, WINDOW, KV_WIDTH), pmap),
                 pl.BlockSpec((bt, CONV_W - 1, CONV_DIM), tmap),
                 pl.BlockSpec((pb, SSM_WIDTH, D_STATE), pmap)]
                + [_const_spec(a.shape) for a in consts]
                + [_const_spec(c.shape)] * 3
                + [_const_spec(sink.shape), _const_spec(bias_c.shape), _const_spec(bias_n.shape)]
                + [_const_spec(a.shape) for a in tail])
    out_shape = (
        jax.ShapeDtypeStruct((nb * tdec, D_MODEL), F32),
        jax.ShapeDtypeStruct((nb, WINDOW, KV_WIDTH), F32),
        jax.ShapeDtypeStruct((nb, WINDOW, KV_WIDTH), F32),
        jax.ShapeDtypeStruct((nb, CONV_W - 1, CONV_DIM), F32),
        jax.ShapeDtypeStruct((nb, SSM_WIDTH, D_STATE), F32),
    )
    out_specs = (
        pl.BlockSpec((rows, D_MODEL), lambda i, j: (i, 0)),
        pl.BlockSpec((pb, WINDOW, KV_WIDTH), pmap),
        pl.BlockSpec((pb, WINDOW, KV_WIDTH), pmap),
        pl.BlockSpec((bt, CONV_W - 1, CONV_DIM), tmap),
        pl.BlockSpec((pb, SSM_WIDTH, D_STATE), pmap),
    )
    scratch = [
        pltpu.VMEM((rows, D_MODEL), BF16),
        pltpu.VMEM((rows, ATT_WIDTH), F32),
        pltpu.VMEM((rows, KV_WIDTH), F32),
        pltpu.VMEM((rows, KV_WIDTH), F32),
        pltpu.VMEM((rows, KV_WIDTH), BF16),
        pltpu.VMEM((rows, KV_WIDTH), BF16),
        pltpu.VMEM((bt, 2 * SUBLANES, CONV_DIM), F32),
        pltpu.VMEM((rows, CONV_DIM), F32),
        pltpu.VMEM((rows, SSM_GROUPS * D_STATE), BF16),
        pltpu.VMEM((rows, LANES), F32),
        pltpu.VMEM((rows, SSM_WIDTH), F32),
        pltpu.VMEM((SSM_WIDTH, rows), BF16),
        pltpu.VMEM((rows, SSM_WIDTH), F32),
        pltpu.VMEM((rows, SSM_WIDTH), F32),
        pltpu.VMEM((rows, MIX_WIDTH), F32),
        pltpu.VMEM((D_MODEL, ATT_WIDTH), BF16),
    ]
    return pl.pallas_call(
        _sample_kernel,
        grid=(nb // bt, steps),
        in_specs=in_specs,
        out_specs=out_specs,
        out_shape=out_shape,
        scratch_shapes=scratch,
        compiler_params=pltpu.CompilerParams(
            dimension_semantics=("arbitrary", "arbitrary"), vmem_limit_bytes=VMEM_LIMIT_BYTES),
        name="sample_mixer",
    )(x2, ck, cv, state_conv, ssm, *consts, c, s1, s2, sink, bias_c, bias_n, *tail)


def _ffn(x2, gpre, wg, wu, wd, gpost):
    n = x2.shape[0]
    tf = FFN_TILE
    consts = [gpre, wg, wu, wd, gpost]
    return pl.pallas_call(
        _ffn_kernel,
        grid=(n // tf,),
        in_specs=[pl.BlockSpec((tf, D_MODEL), lambda i: (i, 0))] + [_const_spec(a.shape) for a in consts],
        out_specs=pl.BlockSpec((tf, D_MODEL), lambda i: (i, 0)),
        out_shape=jax.ShapeDtypeStruct((n, D_MODEL), F32),
        scratch_shapes=[pltpu.VMEM((tf, D_MODEL), BF16)],
        compiler_params=pltpu.CompilerParams(
            dimension_semantics=("arbitrary",), vmem_limit_bytes=VMEM_LIMIT_BYTES),
        name="ffn",
    )(x2, *consts)


def kernel(x_prompt, x_sample, cache_k_win, cache_v_win, state_conv, state_ssm, g_pre_mix, w_in, attn_sinks, conv_w, conv_b, dt_bias, a_log, d_skip, g_ssm_out, w_out, g_post_mix, g_pre_ffn, w_gate, w_up, w_down, g_post_ffn):
    depth = w_in.shape[0]
    bp, lp, _ = x_prompt.shape
    nb, ts, _ = x_sample.shape
    hp, hs = x_prompt, x_sample
    outs = [[] for _ in range(8)]
    for l in range(depth):
        p = _layer_params(g_pre_mix[l], w_in[l], attn_sinks[l], conv_w[l], conv_b[l], dt_bias[l], a_log[l],
                          d_skip[l], g_ssm_out[l], w_out[l], g_post_mix[l])
        ffn_w = (g_pre_ffn[l].reshape(1, D_MODEL), w_gate[l].astype(BF16), w_up[l].astype(BF16),
                 w_down[l].astype(BF16), g_post_ffn[l].reshape(1, D_MODEL))
        x1p, kp, vp, cp, sp = _prompt_mixer(hp, p)
        x1s, ksm, vsm, csm, ssm = _sample_mixer(hs, cache_k_win[l], cache_v_win[l], state_conv[l], state_ssm[l], p)
        hp = _ffn(x1p.reshape(bp * lp, D_MODEL), *ffn_w).reshape(bp, lp, D_MODEL)
        hs = _ffn(x1s, *ffn_w).reshape(nb, ts, D_MODEL)
        kv_shape = (WINDOW, ATT_KV_HEADS, HEAD_DIM)
        ssm_shape = (SSM_HEADS, SSM_HEAD_DIM, D_STATE)
        for lst, val in zip(outs, (kp.reshape((bp,) + kv_shape), vp.reshape((bp,) + kv_shape), cp,
                                   sp.reshape((bp,) + ssm_shape),
                                   ksm.reshape((nb,) + kv_shape), vsm.reshape((nb,) + kv_shape), csm,
                                   ssm.reshape((nb,) + ssm_shape))):
            lst.append(val)
    return (hp, hs) + tuple(jnp.stack(o) for o in outs)
```

```python
import functools
import math

import numpy as np
import jax
import jax.numpy as jnp
from jax import lax
from jax.experimental import pallas as pl
from jax.experimental.pallas import tpu as pltpu

F32 = jnp.float32
BF16 = jnp.bfloat16

D_MODEL = 1024
ATT_HEADS = 16
ATT_KV_HEADS = 4
ATT_GROUP = ATT_HEADS // ATT_KV_HEADS
HEAD_DIM = 64
ATT_WIDTH = ATT_HEADS * HEAD_DIM
KV_WIDTH = ATT_KV_HEADS * HEAD_DIM
WINDOW = 128
ROT_DIM = HEAD_DIM // 4
ROPE_THETA = 500000.0
SSM_HEADS = 16
SSM_HEAD_DIM = 64
SSM_WIDTH = SSM_HEADS * SSM_HEAD_DIM
SSM_GROUPS = 2
SSM_HPG = SSM_HEADS // SSM_GROUPS
SSM_GROUP_W = SSM_WIDTH // SSM_GROUPS
D_STATE = 128
CONV_W = 4
CONV_DIM = SSM_WIDTH + 2 * SSM_GROUPS * D_STATE
MIX_WIDTH = ATT_WIDTH + SSM_WIDTH
EPS = 1e-6
PAST_LEN = 8192

LANES = 128
SUBLANES = 8
VMEM_LIMIT_BYTES = 60 * 1024 * 1024

CHUNK = 128
NEG = -1e30
LOG2E = math.log2(math.e)
Q_SCALE = HEAD_DIM ** -0.5 * LOG2E
SEQ_TILE = 512
SAMPLE_BT = 16
SAMPLE_PB = 4
FFN_TILE = 512
ROW_BLOCK = 32
FF_CHUNK = 256


def _nn(a, b):
    return jnp.dot(a, b, preferred_element_type=F32)


def _nt(a, b):
    return lax.dot_general(a, b, (((1,), (1,)), ((), ())), preferred_element_type=F32)


def _split_bf16(x, n):
    parts = []
    r = x
    for i in range(n):
        p = r.astype(BF16)
        parts.append(p)
        if i + 1 < n:
            r = r - p.astype(F32)
    return parts


def _expand_heads(x, expand2_ref):
    hi, mid = _split_bf16(x, 2)
    return _nn(jnp.concatenate([hi, mid], axis=1), expand2_ref[...])


def _cumsum_cols(m01, x):
    w = x.shape[1]
    r = _nn(m01, jnp.concatenate(_split_bf16(x, 3), axis=1))
    return r[:, :w] + r[:, w:2 * w] + r[:, 2 * w:]


def _heads_to_rows(x):
    return x.T[:SSM_HEADS]


def _rms(x, g):
    ms = jnp.mean(x * x, axis=-1, keepdims=True)
    return x * lax.rsqrt(ms + EPS) * g


def _rms_rows(src, g_ref, dst_ref, res_ref=None):
    g = g_ref[...]
    for r0 in range(0, dst_ref.shape[0], ROW_BLOCK):
        rs = slice(r0, r0 + ROW_BLOCK)
        y = _rms(src[rs, :], g)
        if res_ref is not None:
            y = res_ref[rs, :] + y
        dst_ref[rs, :] = y.astype(dst_ref.dtype)


def _silu(x):
    h = 0.5 * x
    return h + h * jnp.tanh(h)


R_K = ATT_WIDTH
R_V = R_K + KV_WIDTH
R_Z = R_V + KV_WIDTH
R_XBC = R_Z + SSM_WIDTH
R_END = R_XBC + CONV_DIM

CONV_CB = 256
CONV_RB = 64


def _conv_silu_cols(xpad_ref, row0, rows, convw_ref, convb_ref, out_ref, c0):
    cs = slice(c0, c0 + CONV_CB)
    w = [convw_ref[i:i + 1, cs] for i in range(CONV_W)]
    bias = convb_ref[:, cs]
    for r0 in range(0, rows, CONV_RB):
        xh = xpad_ref[row0 + r0 - SUBLANES:row0 + r0 + CONV_RB, cs]
        acc = bias + xh[SUBLANES:] * w[CONV_W - 1]
        for i in range(CONV_W - 1):
            acc = acc + pltpu.roll(xh, CONV_W - 1 - i, 0)[SUBLANES:] * w[i]
        out_ref[r0:r0 + CONV_RB, cs] = _silu(acc)


def _softplus(x):
    return jnp.maximum(x, 0.0) + jnp.log1p(jnp.exp(-jnp.abs(x)))


def _rope(x, c, s1, s2):
    outs = []
    for j in range(x.shape[1] // LANES):
        xb = x[:, j * LANES:(j + 1) * LANES]
        outs.append(xb * c + pltpu.roll(xb, LANES - ROT_DIM // 2, 1) * s1 + pltpu.roll(xb, ROT_DIM // 2, 1) * s2)
    return outs[0] if len(outs) == 1 else jnp.concatenate(outs, axis=1)


def _iota(shape, dim):
    return lax.broadcasted_iota(jnp.int32, shape, dim)


def _head_blocks(c_out):
    out = []
    for half in range(2):
        g, kvh = divmod(2 * c_out + half, ATT_KV_HEADS)
        b_in = kvh * ATT_GROUP + g
        out.append((b_in // 2, b_in % 2))
    return out


def _permute_q_weight(wr_ref, wq_s):
    low = _iota((1, LANES), 1) < HEAD_DIM
    for c_out in range(ATT_WIDTH // LANES):
        halves = []
        for half, (c_in, src_half) in enumerate(_head_blocks(c_out)):
            col = wr_ref[:, c_in * LANES:(c_in + 1) * LANES]
            halves.append(col if src_half == half else pltpu.roll(col, HEAD_DIM, 1))
        wq_s[:, c_out * LANES:(c_out + 1) * LANES] = jnp.where(low, halves[0], halves[1])


def _project(hn, wq_ref, wr_ref, ropec, ropes1, ropes2):
    q = _rope(_nn(hn, wq_ref[...]), ropec, ropes1, ropes2) * (HEAD_DIM ** -0.5)
    kv = _nn(hn, wr_ref[:, R_K:R_Z])
    k = _rope(kv[:, :KV_WIDTH], ropec, ropes1, ropes2)
    v = kv[:, KV_WIDTH:]
    return q, k, v


def _dt_cols(hn, wdt_ref, dtb_row_ref):
    return _softplus(_nn(hn, wdt_ref[...]) + dtb_row_ref[...])


def _a_row(alog_row_ref):
    lane = _iota((1, LANES), 1)
    return jnp.where(lane < SSM_HEADS, -jnp.exp(alog_row_ref[...]), 0.0)


def _log2_decay(acol, arow, dtr):
    return acol * LOG2E, (arow - jnp.log(dtr)) * LOG2E


SSD_QUAD = 4
SSD_NQUAD = SSM_HEADS // SSD_QUAD


def _ssd_cb(b_all, c_all):
    return [_nt(c_all[:, g * D_STATE:(g + 1) * D_STATE].astype(BF16),
                b_all[:, g * D_STATE:(g + 1) * D_STATE].astype(BF16)) for g in range(SSM_GROUPS)]


def _ssd_quad(qi, xs_bf, c_all, cbs, acol2, arow2, mask_bool, hT_bf=None):
    lane4 = _iota((1, SSD_QUAD * SSM_HEAD_DIM), 1) // SSM_HEAD_DIM
    zero = jnp.zeros((), BF16)
    e0 = SSD_QUAD * qi
    g = e0 // SSM_HPG
    cf = c_all[:, g * D_STATE:(g + 1) * D_STATE]
    lanes = slice(e0 * SSM_HEAD_DIM, (e0 + SSD_QUAD) * SSM_HEAD_DIM)
    xq = xs_bf[:, lanes]
    lhs, rhs = [], []
    for i in range(SSD_QUAD):
        e = e0 + i
        a_t = jnp.broadcast_to(acol2[:, e:e + 1], (CHUNK, CHUNK))
        w = cbs[g] * jnp.exp2(jnp.where(mask_bool, a_t - arow2[e:e + 1, :], NEG))
        lhs.append(w.astype(BF16))
        rhs.append(jnp.where(lane4 == i, xq, zero))
        if hT_bf is not None:
            lhs.append((cf * jnp.exp2(a_t)).astype(BF16))
            rhs.append(jnp.where(lane4 == i, hT_bf[:, lanes], zero))
    return _nn(jnp.concatenate(lhs, axis=1), jnp.concatenate(rhs, axis=0))


def _ssd_block(xs_bf, b_all, c_all, acol2, arow2, mask_bool):
    cbs = _ssd_cb(b_all, c_all)
    return jnp.concatenate([_ssd_quad(qi, xs_bf, c_all, cbs, acol2, arow2, mask_bool)
                            for qi in range(SSD_NQUAD)], axis=1)


def _gate_and_out(y_refs, xc_s, z, x_ref, dskip_ref, gssm_ref, wout_ref, gpost_ref, mix_s, out_ref):
    dskip = dskip_ref[...]
    for r0 in range(0, out_ref.shape[0], ROW_BLOCK):
        rs = slice(r0, r0 + ROW_BLOCK)
        y = y_refs[0][rs, :]
        for extra in y_refs[1:]:
            y = y + extra[rs, :]
        gated = (y + dskip * xc_s[rs, 0:SSM_WIDTH]) * _silu(z[rs, :])
        for g in range(SSM_GROUPS):
            gs = slice(g * SSM_GROUP_W, (g + 1) * SSM_GROUP_W)
            gg = gated[:, gs]
            ms = jnp.mean(gg * gg, axis=-1, keepdims=True)
            o = gg * lax.rsqrt(ms + EPS) * gssm_ref[:, gs]
            mix_s[rs, ATT_WIDTH + g * SSM_GROUP_W:ATT_WIDTH + (g + 1) * SSM_GROUP_W] = o.astype(mix_s.dtype)
    mo = _nn(mix_s[...].astype(BF16), wout_ref[...])
    _rms_rows(mo, gpost_ref, out_ref, res_ref=x_ref)


def _prompt_kernel(x_ref, gpre_ref, wr_ref, wdt_ref,
                   ropec_ref, ropes1_ref, ropes2_ref, sink_ref, biasT_ref,
                   convw_ref, convb_ref, dtb_row_ref, alog_row_ref,
                   dskip_ref, gssm_ref, wout_ref, gpost_ref, expand2_ref,
                   x1_ref, nk_ref, nv_ref, nconv_ref, nssm_ref,
                   hn_s, q_s, kbuf, vT_s, xbc_s, xc_s, dtc_s, y_s, mix_s, hT_s, wq_s):
    tm = x_ref.shape[0]
    nchunk = tm // CHUNK
    s = pl.program_id(1)
    last = pl.num_programs(1) - 1

    @pl.when(jnp.logical_and(pl.program_id(0) == 0, s == 0))
    def _():
        _permute_q_weight(wr_ref, wq_s)

    @pl.when(s == 0)
    def _():
        kbuf[0:CHUNK, :] = jnp.zeros((CHUNK, KV_WIDTH), BF16)
        vT_s[0] = jnp.zeros((KV_WIDTH, CHUNK), BF16)
        xbc_s[0:SUBLANES, :] = jnp.zeros((SUBLANES, CONV_DIM), F32)
        hT_s[...] = jnp.zeros_like(hT_s)

    _rms_rows(x_ref, gpre_ref, hn_s)
    hn = hn_s[...]

    ropec, ropes1, ropes2 = ropec_ref[...], ropes1_ref[...], ropes2_ref[...]

    def proj_xbc(c0):
        xbc_s[SUBLANES:SUBLANES + tm, c0:c0 + CONV_CB] = _nn(hn, wr_ref[:, R_XBC + c0:R_XBC + c0 + CONV_CB])

    def proj_q(c0):
        q_s[:, c0:c0 + KV_WIDTH] = (
            _rope(_nn(hn, wq_s[:, c0:c0 + KV_WIDTH]), ropec, ropes1, ropes2) * Q_SCALE).astype(BF16)

    def proj_k():
        kbuf[CHUNK:CHUNK + tm, :] = _rope(_nn(hn, wr_ref[:, R_K:R_V]), ropec, ropes1, ropes2).astype(BF16)

    def proj_v_dt():
        v = _nn(hn, wr_ref[:, R_V:R_Z])
        for j in range(nchunk):
            vT_s[1 + j] = v[j * CHUNK:(j + 1) * CHUNK, :].T.astype(BF16)
        dtc_s[...] = _dt_cols(hn, wdt_ref, dtb_row_ref)

    others = [functools.partial(proj_q, c0) for c0 in range(0, ATT_WIDTH, KV_WIDTH)] + [proj_k, proj_v_dt]
    conv_cols = list(range(0, CONV_DIM, CONV_CB))
    proj_xbc(conv_cols[0])
    for n, c0 in enumerate(conv_cols):
        if n + 1 < len(conv_cols):
            proj_xbc(conv_cols[n + 1])
        _conv_silu_cols(xbc_s, SUBLANES, tm, convw_ref, convb_ref, xc_s, c0)
        if n < len(others):
            others[n]()
    for f in others[len(conv_cols):]:
        f()

    a_row = _a_row(alog_row_ref)

    r2 = _iota((CHUNK, CHUNK), 0)
    c2 = _iota((CHUNK, CHUNK), 1)
    tril = c2 <= r2
    tril_bf = tril.astype(BF16)
    lane_kv = _iota((1, KV_WIDTH), 1) // HEAD_DIM
    ones_rows = jnp.ones((2 * SUBLANES, 2 * CHUNK), BF16)

    def chunk_body(c, carry):
        r0 = c * CHUNK
        first = jnp.logical_and(s == 0, c == 0).astype(jnp.int32)
        bias = biasT_ref[first]
        qcat = jnp.concatenate([q_s[pl.ds(r0, CHUNK), g * KV_WIDTH:(g + 1) * KV_WIDTH]
                                for g in range(ATT_GROUP)], axis=0)
        kwin = kbuf[pl.ds(r0, 2 * CHUNK), :]
        kstack = jnp.concatenate([jnp.where(lane_kv == kvh, kwin, jnp.zeros((), BF16))
                                  for kvh in range(ATT_KV_HEADS)], axis=0)
        sT = _nt(kstack, qcat)
        vT_win = jnp.concatenate([vT_s[c], vT_s[c + 1]], axis=1)

        dtc_c = dtc_s[pl.ds(r0, CHUNK), :]
        dtr_c = _heads_to_rows(dtc_c)
        acol = _cumsum_cols(tril_bf, dtc_c * a_row)
        arow = _heads_to_rows(acol)
        a_end = acol[CHUNK - 1:CHUNK, :]
        tailc = jnp.exp(a_end - acol) * dtc_c
        ex = _expand_heads(
            jnp.concatenate([tailc, jnp.broadcast_to(jnp.exp(a_end), (SUBLANES, LANES))], axis=0), expand2_ref)
        tlx = ex[:CHUNK]
        dec_row = ex[CHUNK:CHUNK + 1]
        xs = xc_s[pl.ds(r0, CHUNK), 0:SSM_WIDTH]
        b_all = xc_s[pl.ds(r0, CHUNK), SSM_WIDTH:SSM_WIDTH + SSM_GROUPS * D_STATE]
        c_all = xc_s[pl.ds(r0, CHUNK), SSM_WIDTH + SSM_GROUPS * D_STATE:CONV_DIM]
        hT = hT_s[...]
        acol2, arow2 = _log2_decay(acol, arow, dtr_c)
        xs_bf = xs.astype(BF16)
        hT_bf = hT.astype(BF16)
        cbs = _ssd_cb(b_all, c_all)

        o_rows = []
        for i in range(ATT_KV_HEADS):
            blk = sT[i * 2 * CHUNK:(i + 1) * 2 * CHUNK] + bias
            sink = sink_ref[i] * LOG2E
            m = jnp.maximum(jnp.max(blk, axis=0, keepdims=True), sink)
            p = jnp.exp2(blk - m).astype(BF16)
            lhs = jnp.concatenate([vT_win[i * HEAD_DIM:(i + 1) * HEAD_DIM], ones_rows], axis=0)
            oT = _nn(lhs, p)
            den = oT[HEAD_DIM:HEAD_DIM + 1] + jnp.exp2(sink - m)
            o_rows.append(oT[:HEAD_DIM] * (1.0 / den))
            for qi in range(i * SSD_NQUAD // ATT_KV_HEADS, (i + 1) * SSD_NQUAD // ATT_KV_HEADS):
                lanes = slice(qi * SSD_QUAD * SSM_HEAD_DIM, (qi + 1) * SSD_QUAD * SSM_HEAD_DIM)
                y_s[pl.ds(r0, CHUNK), lanes] = _ssd_quad(qi, xs_bf, c_all, cbs, acol2, arow2, tril, hT_bf)
        for c_out in range(ATT_WIDTH // LANES):
            kvh, g0 = divmod(2 * c_out, ATT_GROUP)
            two = jnp.concatenate([o_rows[kvh][:, g * CHUNK:(g + 1) * CHUNK] for g in (g0, g0 + 1)], axis=0)
            mix_s[pl.ds(r0, CHUNK), c_out * LANES:(c_out + 1) * LANES] = two.T.astype(BF16)

        xtl = (xs * tlx).astype(BF16)
        for g in range(SSM_GROUPS):
            sl = slice(g * SSM_GROUP_W, (g + 1) * SSM_GROUP_W)
            bt = b_all[:, g * D_STATE:(g + 1) * D_STATE].T.astype(BF16)
            hT_s[:, sl] = hT[:, sl] * dec_row[:, sl] + _nn(bt, xtl[:, sl])
        return carry

    for c in range(nchunk):
        chunk_body(c, 0)

    kbuf[0:CHUNK, :] = kbuf[tm:tm + CHUNK, :]
    vT_s[0] = vT_s[nchunk]
    xbc_s[0:SUBLANES, :] = xbc_s[tm:tm + SUBLANES, :]

    z = _nn(hn_s[...], wr_ref[:, R_Z:R_XBC])
    _gate_and_out((y_s,), xc_s, z, x_ref, dskip_ref, gssm_ref, wout_ref, gpost_ref, mix_s, x1_ref)

    @pl.when(s == last)
    def _():
        hn_w = hn_s[tm - WINDOW:, :]
        nk_ref[...] = _rope(_nn(hn_w, wr_ref[:, R_K:R_V]), ropec_ref[tm - WINDOW:, :],
                            ropes1_ref[tm - WINDOW:, :], ropes2_ref[tm - WINDOW:, :])
        nv_ref[...] = _nn(hn_w, wr_ref[:, R_V:R_Z])
        nconv_ref[...] = xbc_s[SUBLANES - (CONV_W - 1):SUBLANES, :]
        nssm_ref[...] = hT_s[...].T


def _sample_kernel(x_ref, ck_ref, cv_ref, sconv_ref, sssm_ref,
                   gpre_ref, wr_ref, wdt_ref,
                   ropec_ref, ropes1_ref, ropes2_ref, sink_ref, biasc_ref, biasn_ref,
                   convw_ref, convb_ref, dtb_row_ref, alog_row_ref,
                   dskip_ref, gssm_ref, wout_ref, gpost_ref, expand2_ref,
                   x1_ref, nk_ref, nv_ref, nconv_ref, nssm_ref,
                   hn_s, q_s, kn_s, vn_s, knb_s, vnb_s, xpad_s, xc_s, bb_s, ea_s, eax_s, xT_s, y_s, yoff_s, mix_s, wq_s):
    bt_n = sconv_ref.shape[0]
    pb_n = ck_ref.shape[0]
    m_rows = x_ref.shape[0]
    tdec = m_rows // bt_n
    j = pl.program_id(1)

    @pl.when(jnp.logical_and(pl.program_id(0) == 0, j == 0))
    def _():
        _permute_q_weight(wr_ref, wq_s)

    @pl.when(j == 0)
    def _():
        _rms_rows(x_ref, gpre_ref, hn_s)
        hn = hn_s[...]
        q, k, v = _project(hn, wq_s, wr_ref, ropec_ref[...], ropes1_ref[...], ropes2_ref[...])
        q_s[...] = q
        kn_s[...] = k
        vn_s[...] = v
        knb_s[...] = k.astype(BF16)
        vnb_s[...] = v.astype(BF16)

        xbc = _nn(hn, wr_ref[:, R_XBC:R_END])
        xpad_s[:, 0:SUBLANES - 3, :] = jnp.zeros((bt_n, SUBLANES - 3, CONV_DIM), F32)
        xpad_s[:, SUBLANES - 3:SUBLANES, :] = sconv_ref[...]
        xpad_s[:, SUBLANES:2 * SUBLANES, :] = xbc.reshape(bt_n, tdec, CONV_DIM)
        nconv_ref[...] = xpad_s[:, 2 * SUBLANES - 3:2 * SUBLANES, :]
        cb = 256
        for c0 in range(0, CONV_DIM, cb):
            cs = slice(c0, c0 + cb)
            xh = xpad_s[:, :, cs].reshape(bt_n * 2 * SUBLANES, cb)

            def new_rows(a):
                return a.reshape(bt_n, 2 * SUBLANES, cb)[:, SUBLANES:, :].reshape(m_rows, cb)

            acc = convb_ref[:, cs] + new_rows(xh) * convw_ref[CONV_W - 1:CONV_W, cs]
            for i in range(CONV_W - 1):
                acc = acc + new_rows(pltpu.roll(xh, CONV_W - 1 - i, 0)) * convw_ref[i:i + 1, cs]
            xc_s[:, cs] = _silu(acc)
        xs = xc_s[:, 0:SSM_WIDTH]
        b_all = xc_s[:, SSM_WIDTH:SSM_WIDTH + SSM_GROUPS * D_STATE]
        c_all = xc_s[:, SSM_WIDTH + SSM_GROUPS * D_STATE:CONV_DIM]
        bb_s[...] = b_all.astype(BF16)

        dtc = _dt_cols(hn, wdt_ref, dtb_row_ref)
        dtr = _heads_to_rows(dtc)
        a_row = _a_row(alog_row_ref)

        r2 = _iota((m_rows, m_rows), 0)
        c2 = _iota((m_rows, m_rows), 1)
        same = (r2 // tdec) == (c2 // tdec)
        causal = jnp.logical_and(same, c2 <= r2)
        causal_bf = causal.astype(BF16)
        same_bf = same.astype(BF16)

        dac = dtc * a_row
        acol = _cumsum_cols(causal_bf, dac)
        alast = _cumsum_cols(same_bf, dac)
        arow = _heads_to_rows(acol)
        tailc = jnp.exp(alast - acol) * dtc
        ex = _expand_heads(jnp.concatenate([jnp.exp(acol), tailc], axis=0), expand2_ref)
        ea_s[...] = jnp.exp(alast)
        eax_s[...] = ex[:m_rows]
        acol2, arow2 = _log2_decay(acol, arow, dtr)
        y_s[...] = _ssd_block(xs.astype(BF16), b_all, c_all, acol2, arow2, causal)
        xtl = xs * ex[m_rows:]
        for jj in range(SSM_WIDTH // LANES):
            xT_s[jj * LANES:(jj + 1) * LANES, :] = xtl[:, jj * LANES:(jj + 1) * LANES].T.astype(BF16)

    lane_kv = _iota((1, KV_WIDTH), 1) // HEAD_DIM
    row_b = _iota((m_rows, 1), 0) // tdec
    low_half = _iota((1, LANES), 1) < HEAD_DIM
    sink = sink_ref[...]

    pbs = range(pb_n)
    bs = [j * pb_n + pb for pb in pbs]
    rs = [pl.multiple_of(b * tdec, tdec) for b in bs]

    qbd, kc, vc, sc_c, sc_n = [], [], [], [], []
    for pb in pbs:
        q8 = q_s[pl.ds(rs[pb], tdec), :]
        qbd.append(jnp.concatenate(
            [jnp.where(lane_kv == kvh, q8[:, g * KV_WIDTH:(g + 1) * KV_WIDTH], 0.0)
             for g in range(ATT_GROUP) for kvh in range(ATT_KV_HEADS)], axis=0).astype(BF16))
        kc.append(ck_ref[pb])
        vc.append(cv_ref[pb])
    for pb in pbs:
        sc_c.append(_nt(qbd[pb], kc[pb].astype(BF16)) + biasc_ref[...])
        sc_n.append(_nt(qbd[pb], knb_s[...]) + biasn_ref[bs[pb]])
    p_c, p_n, inv = [], [], []
    for pb in pbs:
        m = jnp.maximum(jnp.maximum(jnp.max(sc_c[pb], axis=1, keepdims=True),
                                    jnp.max(sc_n[pb], axis=1, keepdims=True)), sink)
        pc = jnp.exp(sc_c[pb] - m)
        pn = jnp.exp(sc_n[pb] - m)
        den = jnp.sum(pc, axis=1, keepdims=True) + jnp.sum(pn, axis=1, keepdims=True) + jnp.exp(sink - m)
        p_c.append(pc.astype(BF16))
        p_n.append(pn.astype(BF16))
        inv.append(1.0 / den)
    for pb in pbs:
        o = (_nn(p_c[pb], vc[pb].astype(BF16)) + _nn(p_n[pb], vnb_s[...])) * inv[pb]
        for c_out in range(ATT_WIDTH // LANES):
            kvh, g0 = divmod(2 * c_out, ATT_GROUP)
            halves = []
            for half in range(2):
                i0 = ((g0 + half) * ATT_KV_HEADS + kvh) * tdec
                piece = o[i0:i0 + tdec, (kvh // 2) * LANES:(kvh // 2 + 1) * LANES]
                halves.append(piece if kvh % 2 == half else pltpu.roll(piece, HEAD_DIM, 1))
            mix_s[pl.ds(rs[pb], tdec), c_out * LANES:(c_out + 1) * LANES] = jnp.where(low_half, halves[0], halves[1])
        nk_ref[pb, 0:WINDOW - tdec, :] = kc[pb][tdec:, :]
        nk_ref[pb, WINDOW - tdec:WINDOW, :] = kn_s[pl.ds(rs[pb], tdec), :]
        nv_ref[pb, 0:WINDOW - tdec, :] = vc[pb][tdec:, :]
        nv_ref[pb, WINDOW - tdec:WINDOW, :] = vn_s[pl.ds(rs[pb], tdec), :]

    for g in range(SSM_GROUPS):
        sl = slice(g * SSM_GROUP_W, (g + 1) * SSM_GROUP_W)
        c0 = SSM_WIDTH + (SSM_GROUPS + g) * D_STATE
        hg, upd = [], []
        for pb in pbs:
            hg.append(sssm_ref[pb, sl, :])
            bm = jnp.where(row_b == bs[pb], bb_s[:, g * D_STATE:(g + 1) * D_STATE], jnp.zeros((), BF16))
            upd.append(_nn(xT_s[sl, :], bm))
        for pb in pbs:
            cc = xc_s[pl.ds(rs[pb], tdec), c0:c0 + D_STATE].astype(BF16)
            yoff_s[pl.ds(rs[pb], tdec), sl] = (_nt(cc, hg[pb].astype(BF16))
                                               * eax_s[pl.ds(rs[pb], tdec), sl])
        for pb in pbs:
            ea_b = ea_s[pl.ds(rs[pb], 1), :]
            dec = jnp.concatenate(
                [jnp.broadcast_to(ea_b[:, g * SSM_HPG + e:g * SSM_HPG + e + 1], (SSM_HEAD_DIM, D_STATE))
                 for e in range(SSM_HPG)], axis=0)
            nssm_ref[pb, sl, :] = hg[pb] * dec + upd[pb]

    @pl.when(j == pl.num_programs(1) - 1)
    def _():
        z = _nn(hn_s[...], wr_ref[:, R_Z:R_XBC])
        _gate_and_out((y_s, yoff_s), xc_s, z, x_ref, dskip_ref, gssm_ref, wout_ref, gpost_ref, mix_s, x1_ref)


def _ffn_kernel(x_ref, gpre_ref, wg_ref, wu_ref, wd_ref, gpost_ref, o_ref, f_s):
    _rms_rows(x_ref, gpre_ref, f_s)
    f = f_s[...]
    d_ff = wg_ref.shape[1]
    acc = jnp.zeros(x_ref.shape, F32)
    for j in range(d_ff // FF_CHUNK):
        sl = slice(j * FF_CHUNK, (j + 1) * FF_CHUNK)
        gate = _nn(f, wg_ref[:, sl])
        up = _nn(f, wu_ref[:, sl])
        acc = acc + _nn((_silu(gate) * up).astype(BF16), wd_ref[sl, :])
    _rms_rows(acc, gpost_ref, o_ref, res_ref=x_ref)


def _const_spec(shape):
    nd = len(shape)
    return pl.BlockSpec(shape, lambda *_: (0,) * nd, pipeline_mode=pl.Buffered(1))


def _rope_tables(pos):
    half = ROT_DIM // 2
    inv = ROPE_THETA ** (-np.arange(half, dtype=np.float64) * 2.0 / ROT_DIM)
    ang = pos.astype(np.float64)[:, None] * inv[None, :]
    cos = np.cos(ang).astype(np.float32)
    sin = np.sin(ang).astype(np.float32)
    n = pos.shape[0]
    pad = HEAD_DIM - ROT_DIM
    c = np.concatenate([cos, cos, np.ones((n, pad), np.float32)], axis=1)
    s1 = np.concatenate([-sin, np.zeros((n, half + pad), np.float32)], axis=1)
    s2 = np.concatenate([np.zeros((n, half), np.float32), sin, np.zeros((n, pad), np.float32)], axis=1)
    rep = LANES // HEAD_DIM
    return tuple(jnp.asarray(np.tile(t, (1, rep))) for t in (c, s1, s2))


def _layer_params(g_pre_mix, w_in, attn_sinks, conv_w, conv_b, dt_bias, a_log, d_skip, g_ssm_out, w_out, g_post_mix):
    wr = w_in.astype(BF16)
    wdt = jnp.pad(w_in[:, R_END:], ((0, 0), (0, LANES - SSM_HEADS))).astype(BF16)
    wout = w_out.astype(BF16)
    pad16 = ((0, 0), (0, LANES - SSM_HEADS))
    expand = (np.arange(LANES)[:, None] == (np.arange(SSM_WIDTH)[None, :] // SSM_HEAD_DIM)).astype(np.float32)
    return dict(
        gpre=g_pre_mix.reshape(1, D_MODEL), wr=wr, wdt=wdt,
        convw=conv_w, convb=conv_b.reshape(1, CONV_DIM),
        dtb_row=jnp.pad(dt_bias.reshape(1, SSM_HEADS), pad16),
        alog_row=jnp.pad(a_log.reshape(1, SSM_HEADS), pad16),
        dskip=jnp.repeat(d_skip, SSM_HEAD_DIM).reshape(1, SSM_WIDTH), gssm=g_ssm_out.reshape(1, SSM_WIDTH),
        wout=wout, gpost=g_post_mix.reshape(1, D_MODEL), expand2=jnp.asarray(np.concatenate([expand, expand], axis=0), BF16),
        sinks_gk=attn_sinks.reshape(ATT_KV_HEADS, ATT_GROUP).T,
    )


_WEIGHT_ORDER = ("gpre", "wr", "wdt")
_TAIL_ORDER = ("convw", "convb", "dtb_row", "alog_row",
               "dskip", "gssm", "wout", "gpost", "expand2")


def _prompt_mixer(x, p):
    bsz, seq, _ = x.shape
    tm = SEQ_TILE
    nchunk = tm // CHUNK
    c, s1, s2 = _rope_tables(np.arange(seq))
    sink = jnp.repeat(p["sinks_gk"].T, CHUNK, axis=1)[:, None, :]
    jj = np.arange(2 * CHUNK)[:, None]
    tt = np.arange(CHUNK)[None, :]
    vis = (jj >= tt) & (jj <= tt + WINDOW)
    bias = np.stack([np.where(vis, 0.0, NEG), np.where(vis & (jj >= CHUNK), 0.0, NEG)]).astype(np.float32)
    bias = jnp.asarray(np.tile(bias, (1, 1, ATT_GROUP)))
    consts = [p[n] for n in _WEIGHT_ORDER]
    tail = [p[n] for n in _TAIL_ORDER]
    in_specs = ([pl.BlockSpec((None, tm, D_MODEL), lambda b, s: (b, s, 0))]
                + [_const_spec(a.shape) for a in consts]
                + [pl.BlockSpec((tm, LANES), lambda b, s: (s, 0))] * 3
                + [_const_spec(sink.shape), _const_spec(bias.shape)]
                + [_const_spec(a.shape) for a in tail])
    out_shape = (
        jax.ShapeDtypeStruct((bsz, seq, D_MODEL), F32),
        jax.ShapeDtypeStruct((bsz, WINDOW, KV_WIDTH), F32),
        jax.ShapeDtypeStruct((bsz, WINDOW, KV_WIDTH), F32),
        jax.ShapeDtypeStruct((bsz, CONV_W - 1, CONV_DIM), F32),
        jax.ShapeDtypeStruct((bsz, SSM_WIDTH, D_STATE), F32),
    )
    out_specs = (
        pl.BlockSpec((None, tm, D_MODEL), lambda b, s: (b, s, 0)),
        pl.BlockSpec((None, WINDOW, KV_WIDTH), lambda b, s: (b, 0, 0)),
        pl.BlockSpec((None, WINDOW, KV_WIDTH), lambda b, s: (b, 0, 0)),
        pl.BlockSpec((None, CONV_W - 1, CONV_DIM), lambda b, s: (b, 0, 0)),
        pl.BlockSpec((None, SSM_WIDTH, D_STATE), lambda b, s: (b, 0, 0)),
    )
    scratch = [
        pltpu.VMEM((tm, D_MODEL), BF16),
        pltpu.VMEM((tm, ATT_WIDTH), BF16),
        pltpu.VMEM((CHUNK + tm, KV_WIDTH), BF16),
        pltpu.VMEM((nchunk + 1, KV_WIDTH, CHUNK), BF16),
        pltpu.VMEM((SUBLANES + tm, CONV_DIM), F32),
        pltpu.VMEM((tm, CONV_DIM), F32),
        pltpu.VMEM((tm, LANES), F32),
        pltpu.VMEM((tm, SSM_WIDTH), F32),
        pltpu.VMEM((tm, MIX_WIDTH), BF16),
        pltpu.VMEM((D_STATE, SSM_WIDTH), F32),
        pltpu.VMEM((D_MODEL, ATT_WIDTH), BF16),
    ]
    return pl.pallas_call(
        _prompt_kernel,
        grid=(bsz, seq // tm),
        in_specs=in_specs,
        out_specs=out_specs,
        out_shape=out_shape,
        scratch_shapes=scratch,
        compiler_params=pltpu.CompilerParams(
            dimension_semantics=("arbitrary", "arbitrary"), vmem_limit_bytes=VMEM_LIMIT_BYTES),
        name="prompt_mixer",
    )(x, *consts, c, s1, s2, sink, bias, *tail)


def _sample_mixer(x, cache_k, cache_v, state_conv, state_ssm, p):
    nb, tdec, _ = x.shape
    bt = SAMPLE_BT
    rows = bt * tdec
    c, s1, s2 = _rope_tables(np.tile(PAST_LEN + np.arange(tdec), bt))
    sink = jnp.repeat(p["sinks_gk"].reshape(-1), tdec).reshape(rows, 1)
    consts = [p[n] for n in _WEIGHT_ORDER]
    tail = [p[n] for n in _TAIL_ORDER]
    x2 = x.reshape(nb * tdec, D_MODEL)
    ck = cache_k.reshape(nb, WINDOW, KV_WIDTH)
    cv = cache_v.reshape(nb, WINDOW, KV_WIDTH)
    ssm = state_ssm.reshape(nb, SSM_WIDTH, D_STATE)
    pb = SAMPLE_PB
    steps = bt // pb
    t_of_row = (np.arange(rows) % tdec)[:, None]
    col = np.arange(rows)[None, :]
    bias_c = jnp.asarray(np.where(col >= t_of_row, 0.0, NEG).astype(np.float32))
    bias_n = jnp.asarray(np.stack([np.where((col // tdec == b) & (col % tdec <= t_of_row), 0.0, NEG)
                                   for b in range(bt)]).astype(np.float32))
    tmap = lambda i, j: (i, 0, 0)
    pmap = lambda i, j: (i * steps + j, 0, 0)
    in_specs = ([pl.BlockSpec((rows, D_MODEL), lambda i, j: (i, 0)),
                 pl.BlockSpec((pb, WINDOW, KV_WIDTH), pmap),
                 pl.BlockSpec((pb, WINDOW, KV_WIDTH), pmap),
                 pl.BlockSpec((bt, CONV_W - 1, CONV_DIM), tmap),
                 pl.BlockSpec((pb, SSM_WIDTH, D_STATE), pmap)]
                + [_const_spec(a.shape) for a in consts]
                + [_const_spec(c.shape)] * 3
                + [_const_spec(sink.shape), _const_spec(bias_c.shape), _const_spec(bias_n.shape)]
                + [_const_spec(a.shape) for a in tail])
    out_shape = (
        jax.ShapeDtypeStruct((nb * tdec, D_MODEL), F32),
        jax.ShapeDtypeStruct((nb, WINDOW, KV_WIDTH), F32),
        jax.ShapeDtypeStruct((nb, WINDOW, KV_WIDTH), F32),
        jax.ShapeDtypeStruct((nb, CONV_W - 1, CONV_DIM), F32),
        jax.ShapeDtypeStruct((nb, SSM_WIDTH, D_STATE), F32),
    )
    out_specs = (
        pl.BlockSpec((rows, D_MODEL), lambda i, j: (i, 0)),
        pl.BlockSpec((pb, WINDOW, KV_WIDTH), pmap),
        pl.BlockSpec((pb, WINDOW, KV_WIDTH), pmap),
        pl.BlockSpec((bt, CONV_W - 1, CONV_DIM), tmap),
        pl.BlockSpec((pb, SSM_WIDTH, D_STATE), pmap),
    )
    scratch = [
        pltpu.VMEM((rows, D_MODEL), BF16),
        pltpu.VMEM((rows, ATT_WIDTH), F32),
        pltpu.VMEM((rows, KV_WIDTH), F32),
        pltpu.VMEM((rows, KV_WIDTH), F32),
        pltpu.VMEM((rows, KV_WIDTH), BF16),
        pltpu.VMEM((rows, KV_WIDTH), BF16),
        pltpu.VMEM((bt, 2 * SUBLANES, CONV_DIM), F32),
        pltpu.VMEM((rows, CONV_DIM), F32),
        pltpu.VMEM((rows, SSM_GROUPS * D_STATE), BF16),
        pltpu.VMEM((rows, LANES), F32),
        pltpu.VMEM((rows, SSM_WIDTH), F32),
        pltpu.VMEM((SSM_WIDTH, rows), BF16),
        pltpu.VMEM((rows, SSM_WIDTH), F32),
        pltpu.VMEM((rows, SSM_WIDTH), F32),
        pltpu.VMEM((rows, MIX_WIDTH), F32),
        pltpu.VMEM((D_MODEL, ATT_WIDTH), BF16),
    ]
    return pl.pallas_call(
        _sample_kernel,
        grid=(nb // bt, steps),
        in_specs=in_specs,
        out_specs=out_specs,
        out_shape=out_shape,
        scratch_shapes=scratch,
        compiler_params=pltpu.CompilerParams(
            dimension_semantics=("arbitrary", "arbitrary"), vmem_limit_bytes=VMEM_LIMIT_BYTES),
        name="sample_mixer",
    )(x2, ck, cv, state_conv, ssm, *consts, c, s1, s2, sink, bias_c, bias_n, *tail)


def _ffn(x2, gpre, wg, wu, wd, gpost):
    n = x2.shape[0]
    tf = FFN_TILE
    consts = [gpre, wg, wu, wd, gpost]
    return pl.pallas_call(
        _ffn_kernel,
        grid=(n // tf,),
        in_specs=[pl.BlockSpec((tf, D_MODEL), lambda i: (i, 0))] + [_const_spec(a.shape) for a in consts],
        out_specs=pl.BlockSpec((tf, D_MODEL), lambda i: (i, 0)),
        out_shape=jax.ShapeDtypeStruct((n, D_MODEL), F32),
        scratch_shapes=[pltpu.VMEM((tf, D_MODEL), BF16)],
        compiler_params=pltpu.CompilerParams(
            dimension_semantics=("arbitrary",), vmem_limit_bytes=VMEM_LIMIT_BYTES),
        name="ffn",
    )(x2, *consts)


def kernel(x_prompt, x_sample, cache_k_win, cache_v_win, state_conv, state_ssm, g_pre_mix, w_in, attn_sinks, conv_w, conv_b, dt_bias, a_log, d_skip, g_ssm_out, w_out, g_post_mix, g_pre_ffn, w_gate, w_up, w_down, g_post_ffn):
    depth = w_in.shape[0]
    bp, lp, _ = x_prompt.shape
    nb, ts, _ = x_sample.shape
    hp, hs = x_prompt, x_sample
    outs = [[] for _ in range(8)]
    for l in range(depth):
        p = _layer_params(g_pre_mix[l], w_in[l], attn_sinks[l], conv_w[l], conv_b[l], dt_bias[l], a_log[l],
                          d_skip[l], g_ssm_out[l], w_out[l], g_post_mix[l])
        ffn_w = (g_pre_ffn[l].reshape(1, D_MODEL), w_gate[l].astype(BF16), w_up[l].astype(BF16),
                 w_down[l].astype(BF16), g_post_ffn[l].reshape(1, D_MODEL))
        x1p, kp, vp, cp, sp = _prompt_mixer(hp, p)
        x1s, ksm, vsm, csm, ssm = _sample_mixer(hs, cache_k_win[l], cache_v_win[l], state_conv[l], state_ssm[l], p)
        hp = _ffn(x1p.reshape(bp * lp, D_MODEL), *ffn_w).reshape(bp, lp, D_MODEL)
        hs = _ffn(x1s, *ffn_w).reshape(nb, ts, D_MODEL)
        kv_shape = (WINDOW, ATT_KV_HEADS, HEAD_DIM)
        ssm_shape = (SSM_HEADS, SSM_HEAD_DIM, D_STATE)
        for lst, val in zip(outs, (kp.reshape((bp,) + kv_shape), vp.reshape((bp,) + kv_shape), cp,
                                   sp.reshape((bp,) + ssm_shape),
                                   ksm.reshape((nb,) + kv_shape), vsm.reshape((nb,) + kv_shape), csm,
                                   ssm.reshape((nb,) + ssm_shape))):
            lst.append(val)
    return (hp, hs) + tuple(jnp.stack(o) for o in outs)
```

```python
import functools
import math

import numpy as np
import jax
import jax.numpy as jnp
from jax import lax
from jax.experimental import pallas as pl
from jax.experimental.pallas import tpu as pltpu

F32 = jnp.float32
BF16 = jnp.bfloat16

D_MODEL = 1024
ATT_HEADS = 16
ATT_KV_HEADS = 4
ATT_GROUP = ATT_HEADS // ATT_KV_HEADS
HEAD_DIM = 64
ATT_WIDTH = ATT_HEADS * HEAD_DIM
KV_WIDTH = ATT_KV_HEADS * HEAD_DIM
WINDOW = 128
ROT_DIM = HEAD_DIM // 4
ROPE_THETA = 500000.0
SSM_HEADS = 16
SSM_HEAD_DIM = 64
SSM_WIDTH = SSM_HEADS * SSM_HEAD_DIM
SSM_GROUPS = 2
SSM_HPG = SSM_HEADS // SSM_GROUPS
SSM_GROUP_W = SSM_WIDTH // SSM_GROUPS
D_STATE = 128
CONV_W = 4
CONV_DIM = SSM_WIDTH + 2 * SSM_GROUPS * D_STATE
MIX_WIDTH = ATT_WIDTH + SSM_WIDTH
EPS = 1e-6
PAST_LEN = 8192

LANES = 128
SUBLANES = 8
VMEM_LIMIT_BYTES = 60 * 1024 * 1024

CHUNK = 128
NEG = -1e30
LOG2E = math.log2(math.e)
Q_SCALE = HEAD_DIM ** -0.5 * LOG2E
SEQ_TILE = 512
SAMPLE_BT = 16
SAMPLE_PB = 4
FFN_TILE = 512
ROW_BLOCK = 32
FF_CHUNK = 256


def _nn(a, b):
    return jnp.dot(a, b, preferred_element_type=F32)


def _nt(a, b):
    return lax.dot_general(a, b, (((1,), (1,)), ((), ())), preferred_element_type=F32)


def _split_bf16(x, n):
    parts = []
    r = x
    for i in range(n):
        p = r.astype(BF16)
        parts.append(p)
        if i + 1 < n:
            r = r - p.astype(F32)
    return parts


def _expand_heads(x, expand2_ref):
    hi, mid = _split_bf16(x, 2)
    return _nn(jnp.concatenate([hi, mid], axis=1), expand2_ref[...])


def _cumsum_cols(m01, x):
    w = x.shape[1]
    r = _nn(m01, jnp.concatenate(_split_bf16(x, 3), axis=1))
    return r[:, :w] + r[:, w:2 * w] + r[:, 2 * w:]


def _heads_to_rows(x):
    return x.T[:SSM_HEADS]


def _rms(x, g):
    ms = jnp.mean(x * x, axis=-1, keepdims=True)
    return x * lax.rsqrt(ms + EPS) * g


def _rms_rows(src, g_ref, dst_ref, res_ref=None):
    g = g_ref[...]
    for r0 in range(0, dst_ref.shape[0], ROW_BLOCK):
        rs = slice(r0, r0 + ROW_BLOCK)
        y = _rms(src[rs, :], g)
        if res_ref is not None:
            y = res_ref[rs, :] + y
        dst_ref[rs, :] = y.astype(dst_ref.dtype)


def _silu(x):
    h = 0.5 * x
    return h + h * jnp.tanh(h)


R_K = ATT_WIDTH
R_V = R_K + KV_WIDTH
R_Z = R_V + KV_WIDTH
R_XBC = R_Z + SSM_WIDTH
R_END = R_XBC + CONV_DIM

CONV_CB = 256
CONV_RB = 64


def _conv_silu_cols(xpad_ref, row0, rows, convw_ref, convb_ref, out_ref, c0):
    cs = slice(c0, c0 + CONV_CB)
    w = [convw_ref[i:i + 1, cs] for i in range(CONV_W)]
    bias = convb_ref[:, cs]
    for r0 in range(0, rows, CONV_RB):
        xh = xpad_ref[row0 + r0 - SUBLANES:row0 + r0 + CONV_RB, cs]
        acc = bias + xh[SUBLANES:] * w[CONV_W - 1]
        for i in range(CONV_W - 1):
            acc = acc + pltpu.roll(xh, CONV_W - 1 - i, 0)[SUBLANES:] * w[i]
        out_ref[r0:r0 + CONV_RB, cs] = _silu(acc)


def _softplus(x):
    return jnp.maximum(x, 0.0) + jnp.log1p(jnp.exp(-jnp.abs(x)))


def _rope(x, c, s1, s2):
    outs = []
    for j in range(x.shape[1] // LANES):
        xb = x[:, j * LANES:(j + 1) * LANES]
        outs.append(xb * c + pltpu.roll(xb, LANES - ROT_DIM // 2, 1) * s1 + pltpu.roll(xb, ROT_DIM // 2, 1) * s2)
    return outs[0] if len(outs) == 1 else jnp.concatenate(outs, axis=1)


def _iota(shape, dim):
    return lax.broadcasted_iota(jnp.int32, shape, dim)


def _head_blocks(c_out):
    out = []
    for half in range(2):
        g, kvh = divmod(2 * c_out + half, ATT_KV_HEADS)
        b_in = kvh * ATT_GROUP + g
        out.append((b_in // 2, b_in % 2))
    return out


def _permute_q_weight(wr_ref, wq_s):
    low = _iota((1, LANES), 1) < HEAD_DIM
    for c_out in range(ATT_WIDTH // LANES):
        halves = []
        for half, (c_in, src_half) in enumerate(_head_blocks(c_out)):
            col = wr_ref[:, c_in * LANES:(c_in + 1) * LANES]
            halves.append(col if src_half == half else pltpu.roll(col, HEAD_DIM, 1))
        wq_s[:, c_out * LANES:(c_out + 1) * LANES] = jnp.where(low, halves[0], halves[1])


def _project(hn, wq_ref, wr_ref, ropec, ropes1, ropes2):
    q = _rope(_nn(hn, wq_ref[...]), ropec, ropes1, ropes2) * (HEAD_DIM ** -0.5)
    kv = _nn(hn, wr_ref[:, R_K:R_Z])
    k = _rope(kv[:, :KV_WIDTH], ropec, ropes1, ropes2)
    v = kv[:, KV_WIDTH:]
    return q, k, v


def _dt_cols(hn, wdt_ref, dtb_row_ref):
    return _softplus(_nn(hn, wdt_ref[...]) + dtb_row_ref[...])


def _a_row(alog_row_ref):
    lane = _iota((1, LANES), 1)
    return jnp.where(lane < SSM_HEADS, -jnp.exp(alog_row_ref[...]), 0.0)


def _log2_decay(acol, arow, dtr):
    return acol * LOG2E, (arow - jnp.log(dtr)) * LOG2E


SSD_QUAD = 4
SSD_NQUAD = SSM_HEADS // SSD_QUAD


def _ssd_cb(b_all, c_all):
    return [_nt(c_all[:, g * D_STATE:(g + 1) * D_STATE].astype(BF16),
                b_all[:, g * D_STATE:(g + 1) * D_STATE].astype(BF16)) for g in range(SSM_GROUPS)]


def _ssd_quad(qi, xs_bf, c_all, cbs, acol2, arow2, mask_bool, hT_bf=None):
    lane4 = _iota((1, SSD_QUAD * SSM_HEAD_DIM), 1) // SSM_HEAD_DIM
    zero = jnp.zeros((), BF16)
    e0 = SSD_QUAD * qi
    g = e0 // SSM_HPG
    cf = c_all[:, g * D_STATE:(g + 1) * D_STATE]
    lanes = slice(e0 * SSM_HEAD_DIM, (e0 + SSD_QUAD) * SSM_HEAD_DIM)
    xq = xs_bf[:, lanes]
    lhs, rhs = [], []
    for i in range(SSD_QUAD):
        e = e0 + i
        a_t = jnp.broadcast_to(acol2[:, e:e + 1], (CHUNK, CHUNK))
        w = cbs[g] * jnp.exp2(jnp.where(mask_bool, a_t - arow2[e:e + 1, :], NEG))
        lhs.append(w.astype(BF16))
        rhs.append(jnp.where(lane4 == i, xq, zero))
        if hT_bf is not None:
            lhs.append((cf * jnp.exp2(a_t)).astype(BF16))
            rhs.append(jnp.where(lane4 == i, hT_bf[:, lanes], zero))
    return _nn(jnp.concatenate(lhs, axis=1), jnp.concatenate(rhs, axis=0))


def _ssd_block(xs_bf, b_all, c_all, acol2, arow2, mask_bool):
    cbs = _ssd_cb(b_all, c_all)
    return jnp.concatenate([_ssd_quad(qi, xs_bf, c_all, cbs, acol2, arow2, mask_bool)
                            for qi in range(SSD_NQUAD)], axis=1)


def _gate_and_out(y_refs, xc_s, z, x_ref, dskip_ref, gssm_ref, wout_ref, gpost_ref, mix_s, out_ref):
    dskip = dskip_ref[...]
    for r0 in range(0, out_ref.shape[0], ROW_BLOCK):
        rs = slice(r0, r0 + ROW_BLOCK)
        y = y_refs[0][rs, :]
        for extra in y_refs[1:]:
            y = y + extra[rs, :]
        gated = (y + dskip * xc_s[rs, 0:SSM_WIDTH]) * _silu(z[rs, :])
        for g in range(SSM_GROUPS):
            gs = slice(g * SSM_GROUP_W, (g + 1) * SSM_GROUP_W)
            gg = gated[:, gs]
            ms = jnp.mean(gg * gg, axis=-1, keepdims=True)
            o = gg * lax.rsqrt(ms + EPS) * gssm_ref[:, gs]
            mix_s[rs, ATT_WIDTH + g * SSM_GROUP_W:ATT_WIDTH + (g + 1) * SSM_GROUP_W] = o.astype(mix_s.dtype)
    mo = _nn(mix_s[...].astype(BF16), wout_ref[...])
    _rms_rows(mo, gpost_ref, out_ref, res_ref=x_ref)


def _prompt_kernel(x_ref, gpre_ref, wr_ref, wdt_ref,
                   ropec_ref, ropes1_ref, ropes2_ref, sink_ref, biasT_ref,
                   convw_ref, convb_ref, dtb_row_ref, alog_row_ref,
                   dskip_ref, gssm_ref, wout_ref, gpost_ref, expand2_ref,
                   x1_ref, nk_ref, nv_ref, nconv_ref, nssm_ref,
                   hn_s, q_s, kbuf, vT_s, xbc_s, xc_s, dtc_s, y_s, mix_s, hT_s, wq_s):
    tm = x_ref.shape[0]
    nchunk = tm // CHUNK
    s = pl.program_id(1)
    last = pl.num_programs(1) - 1

    @pl.when(jnp.logical_and(pl.program_id(0) == 0, s == 0))
    def _():
        _permute_q_weight(wr_ref, wq_s)

    @pl.when(s == 0)
    def _():
        kbuf[0:CHUNK, :] = jnp.zeros((CHUNK, KV_WIDTH), BF16)
        vT_s[0] = jnp.zeros((KV_WIDTH, CHUNK), BF16)
        xbc_s[0:SUBLANES, :] = jnp.zeros((SUBLANES, CONV_DIM), F32)
        hT_s[...] = jnp.zeros_like(hT_s)

    _rms_rows(x_ref, gpre_ref, hn_s)
    hn = hn_s[...]

    ropec, ropes1, ropes2 = ropec_ref[...], ropes1_ref[...], ropes2_ref[...]

    def proj_xbc(c0):
        xbc_s[SUBLANES:SUBLANES + tm, c0:c0 + CONV_CB] = _nn(hn, wr_ref[:, R_XBC + c0:R_XBC + c0 + CONV_CB])

    def proj_q(c0):
        q_s[:, c0:c0 + KV_WIDTH] = (
            _rope(_nn(hn, wq_s[:, c0:c0 + KV_WIDTH]), ropec, ropes1, ropes2) * Q_SCALE).astype(BF16)

    def proj_k():
        kbuf[CHUNK:CHUNK + tm, :] = _rope(_nn(hn, wr_ref[:, R_K:R_V]), ropec, ropes1, ropes2).astype(BF16)

    def proj_v_dt():
        v = _nn(hn, wr_ref[:, R_V:R_Z])
        for j in range(nchunk):
            vT_s[1 + j] = v[j * CHUNK:(j + 1) * CHUNK, :].T.astype(BF16)
        dtc_s[...] = _dt_cols(hn, wdt_ref, dtb_row_ref)

    others = [functools.partial(proj_q, c0) for c0 in range(0, ATT_WIDTH, KV_WIDTH)] + [proj_k, proj_v_dt]
    conv_cols = list(range(0, CONV_DIM, CONV_CB))
    proj_xbc(conv_cols[0])
    for n, c0 in enumerate(conv_cols):
        if n + 1 < len(conv_cols):
            proj_xbc(conv_cols[n + 1])
        _conv_silu_cols(xbc_s, SUBLANES, tm, convw_ref, convb_ref, xc_s, c0)
        if n < len(others):
            others[n]()
    for f in others[len(conv_cols):]:
        f()

    a_row = _a_row(alog_row_ref)

    r2 = _iota((CHUNK, CHUNK), 0)
    c2 = _iota((CHUNK, CHUNK), 1)
    tril = c2 <= r2
    tril_bf = tril.astype(BF16)
    lane_kv = _iota((1, KV_WIDTH), 1) // HEAD_DIM
    ones_rows = jnp.ones((2 * SUBLANES, 2 * CHUNK), BF16)

    def chunk_body(c, carry):
        r0 = c * CHUNK
        first = jnp.logical_and(s == 0, c == 0).astype(jnp.int32)
        bias = biasT_ref[first]
        qcat = jnp.concatenate([q_s[pl.ds(r0, CHUNK), g * KV_WIDTH:(g + 1) * KV_WIDTH]
                                for g in range(ATT_GROUP)], axis=0)
        kwin = kbuf[pl.ds(r0, 2 * CHUNK), :]
        kstack = jnp.concatenate([jnp.where(lane_kv == kvh, kwin, jnp.zeros((), BF16))
                                  for kvh in range(ATT_KV_HEADS)], axis=0)
        sT = _nt(kstack, qcat)
        vT_win = jnp.concatenate([vT_s[c], vT_s[c + 1]], axis=1)

        dtc_c = dtc_s[pl.ds(r0, CHUNK), :]
        dtr_c = _heads_to_rows(dtc_c)
        acol = _cumsum_cols(tril_bf, dtc_c * a_row)
        arow = _heads_to_rows(acol)
        a_end = acol[CHUNK - 1:CHUNK, :]
        tailc = jnp.exp(a_end - acol) * dtc_c
        ex = _expand_heads(
            jnp.concatenate([tailc, jnp.broadcast_to(jnp.exp(a_end), (SUBLANES, LANES))], axis=0), expand2_ref)
        tlx = ex[:CHUNK]
        dec_row = ex[CHUNK:CHUNK + 1]
        xs = xc_s[pl.ds(r0, CHUNK), 0:SSM_WIDTH]
        b_all = xc_s[pl.ds(r0, CHUNK), SSM_WIDTH:SSM_WIDTH + SSM_GROUPS * D_STATE]
        c_all = xc_s[pl.ds(r0, CHUNK), SSM_WIDTH + SSM_GROUPS * D_STATE:CONV_DIM]
        hT = hT_s[...]
        acol2, arow2 = _log2_decay(acol, arow, dtr_c)
        xs_bf = xs.astype(BF16)
        hT_bf = hT.astype(BF16)
        cbs = _ssd_cb(b_all, c_all)

        o_rows = []
        for i in range(ATT_KV_HEADS):
            blk = sT[i * 2 * CHUNK:(i + 1) * 2 * CHUNK] + bias
            sink = sink_ref[i] * LOG2E
            m = jnp.maximum(jnp.max(blk, axis=0, keepdims=True), sink)
            p = jnp.exp2(blk - m).astype(BF16)
            lhs = jnp.concatenate([vT_win[i * HEAD_DIM:(i + 1) * HEAD_DIM], ones_rows], axis=0)
            oT = _nn(lhs, p)
            den = oT[HEAD_DIM:HEAD_DIM + 1] + jnp.exp2(sink - m)
            o_rows.append(oT[:HEAD_DIM] * (1.0 / den))
            for qi in range(i * SSD_NQUAD // ATT_KV_HEADS, (i + 1) * SSD_NQUAD // ATT_KV_HEADS):
                lanes = slice(qi * SSD_QUAD * SSM_HEAD_DIM, (qi + 1) * SSD_QUAD * SSM_HEAD_DIM)
                y_s[pl.ds(r0, CHUNK), lanes] = _ssd_quad(qi, xs_bf, c_all, cbs, acol2, arow2, tril, hT_bf)
        for c_out in range(ATT_WIDTH // LANES):
            kvh, g0 = divmod(2 * c_out, ATT_GROUP)
            two = jnp.concatenate([o_rows[kvh][:, g * CHUNK:(g + 1) * CHUNK] for g in (g0, g0 + 1)], axis=0)
            mix_s[pl.ds(r0, CHUNK), c_out * LANES:(c_out + 1) * LANES] = two.T.astype(BF16)

        xtl = (xs * tlx).astype(BF16)
        for g in range(SSM_GROUPS):
            sl = slice(g * SSM_GROUP_W, (g + 1) * SSM_GROUP_W)
            bt = b_all[:, g * D_STATE:(g + 1) * D_STATE].T.astype(BF16)
            hT_s[:, sl] = hT[:, sl] * dec_row[:, sl] + _nn(bt, xtl[:, sl])
        return carry

    for c in range(nchunk):
        chunk_body(c, 0)

    kbuf[0:CHUNK, :] = kbuf[tm:tm + CHUNK, :]
    vT_s[0] = vT_s[nchunk]
    xbc_s[0:SUBLANES, :] = xbc_s[tm:tm + SUBLANES, :]

    z = _nn(hn_s[...], wr_ref[:, R_Z:R_XBC])
    _gate_and_out((y_s,), xc_s, z, x_ref, dskip_ref, gssm_ref, wout_ref, gpost_ref, mix_s, x1_ref)

    @pl.when(s == last)
    def _():
        hn_w = hn_s[tm - WINDOW:, :]
        nk_ref[...] = _rope(_nn(hn_w, wr_ref[:, R_K:R_V]), ropec_ref[tm - WINDOW:, :],
                            ropes1_ref[tm - WINDOW:, :], ropes2_ref[tm - WINDOW:, :])
        nv_ref[...] = _nn(hn_w, wr_ref[:, R_V:R_Z])
        nconv_ref[...] = xbc_s[SUBLANES - (CONV_W - 1):SUBLANES, :]
        nssm_ref[...] = hT_s[...].T


def _sample_kernel(x_ref, ck_ref, cv_ref, sconv_ref, sssm_ref,
                   gpre_ref, wr_ref, wdt_ref,
                   ropec_ref, ropes1_ref, ropes2_ref, sink_ref, bias_ref,
                   convw_ref, convb_ref, dtb_row_ref, alog_row_ref,
                   dskip_ref, gssm_ref, wout_ref, gpost_ref, expand2_ref,
                   x1_ref, nk_ref, nv_ref, nconv_ref, nssm_ref,
                   hn_s, q_s, kn_s, vn_s, knb_s, vnb_s, xpad_s, xc_s, bb_s, ea_s, eax_s, xT_s, y_s, yoff_s, mix_s, wq_s):
    bt_n = sconv_ref.shape[0]
    pb_n = ck_ref.shape[0]
    m_rows = x_ref.shape[0]
    tdec = m_rows // bt_n
    j = pl.program_id(1)

    @pl.when(jnp.logical_and(pl.program_id(0) == 0, j == 0))
    def _():
        _permute_q_weight(wr_ref, wq_s)

    @pl.when(j == 0)
    def _():
        _rms_rows(x_ref, gpre_ref, hn_s)
        hn = hn_s[...]
        q, k, v = _project(hn, wq_s, wr_ref, ropec_ref[...], ropes1_ref[...], ropes2_ref[...])
        q_s[...] = q
        kn_s[...] = k
        vn_s[...] = v
        knb_s[...] = k.astype(BF16)
        vnb_s[...] = v.astype(BF16)

        xbc = _nn(hn, wr_ref[:, R_XBC:R_END])
        xpad_s[:, 0:SUBLANES - 3, :] = jnp.zeros((bt_n, SUBLANES - 3, CONV_DIM), F32)
        xpad_s[:, SUBLANES - 3:SUBLANES, :] = sconv_ref[...]
        xpad_s[:, SUBLANES:2 * SUBLANES, :] = xbc.reshape(bt_n, tdec, CONV_DIM)
        nconv_ref[...] = xpad_s[:, 2 * SUBLANES - 3:2 * SUBLANES, :]
        cb = 256
        for c0 in range(0, CONV_DIM, cb):
            cs = slice(c0, c0 + cb)
            xh = xpad_s[:, :, cs].reshape(bt_n * 2 * SUBLANES, cb)

            def new_rows(a):
                return a.reshape(bt_n, 2 * SUBLANES, cb)[:, SUBLANES:, :].reshape(m_rows, cb)

            acc = convb_ref[:, cs] + new_rows(xh) * convw_ref[CONV_W - 1:CONV_W, cs]
            for i in range(CONV_W - 1):
                acc = acc + new_rows(pltpu.roll(xh, CONV_W - 1 - i, 0)) * convw_ref[i:i + 1, cs]
            xc_s[:, cs] = _silu(acc)
        xs = xc_s[:, 0:SSM_WIDTH]
        b_all = xc_s[:, SSM_WIDTH:SSM_WIDTH + SSM_GROUPS * D_STATE]
        c_all = xc_s[:, SSM_WIDTH + SSM_GROUPS * D_STATE:CONV_DIM]
        bb_s[...] = b_all.astype(BF16)

        dtc = _dt_cols(hn, wdt_ref, dtb_row_ref)
        dtr = _heads_to_rows(dtc)
        a_row = _a_row(alog_row_ref)

        r2 = _iota((m_rows, m_rows), 0)
        c2 = _iota((m_rows, m_rows), 1)
        same = (r2 // tdec) == (c2 // tdec)
        causal = jnp.logical_and(same, c2 <= r2)
        causal_bf = causal.astype(BF16)
        same_bf = same.astype(BF16)

        dac = dtc * a_row
        acol = _cumsum_cols(causal_bf, dac)
        alast = _cumsum_cols(same_bf, dac)
        arow = _heads_to_rows(acol)
        tailc = jnp.exp(alast - acol) * dtc
        ex = _expand_heads(jnp.concatenate([jnp.exp(acol), tailc], axis=0), expand2_ref)
        ea_s[...] = jnp.exp(alast)
        eax_s[...] = ex[:m_rows]
        acol2, arow2 = _log2_decay(acol, arow, dtr)
        y_s[...] = _ssd_block(xs.astype(BF16), b_all, c_all, acol2, arow2, causal)
        xtl = xs * ex[m_rows:]
        for jj in range(SSM_WIDTH // LANES):
            xT_s[jj * LANES:(jj + 1) * LANES, :] = xtl[:, jj * LANES:(jj + 1) * LANES].T.astype(BF16)

    lane_kv = _iota((1, KV_WIDTH), 1) // HEAD_DIM
    row_b = _iota((m_rows, 1), 0) // tdec
    low_half = _iota((1, LANES), 1) < HEAD_DIM
    sink = sink_ref[...]

    pbs = range(pb_n)
    bs = [j * pb_n + pb for pb in pbs]
    rs = [pl.multiple_of(b * tdec, tdec) for b in bs]

    qbd, kc, vc, sc = [], [], [], []
    for pb in pbs:
        q8 = q_s[pl.ds(rs[pb], tdec), :]
        qbd.append(jnp.concatenate(
            [jnp.where(lane_kv == kvh, q8[:, g * KV_WIDTH:(g + 1) * KV_WIDTH], 0.0)
             for g in range(ATT_GROUP) for kvh in range(ATT_KV_HEADS)], axis=0).astype(BF16))
        kc.append(ck_ref[pb])
        vc.append(cv_ref[pb])
    for pb in pbs:
        keys = jnp.concatenate([kc[pb].astype(BF16), knb_s[...]], axis=0)
        sc.append(_nt(qbd[pb], keys) + bias_ref[bs[pb]])
    p, inv = [], []
    for pb in pbs:
        m = jnp.maximum(jnp.max(sc[pb], axis=1, keepdims=True), sink)
        e = jnp.exp(sc[pb] - m)
        inv.append(1.0 / (jnp.sum(e, axis=1, keepdims=True) + jnp.exp(sink - m)))
        p.append(e.astype(BF16))
    for pb in pbs:
        o = _nn(p[pb], jnp.concatenate([vc[pb].astype(BF16), vnb_s[...]], axis=0)) * inv[pb]
        for c_out in range(ATT_WIDTH // LANES):
            kvh, g0 = divmod(2 * c_out, ATT_GROUP)
            halves = []
            for half in range(2):
                i0 = ((g0 + half) * ATT_KV_HEADS + kvh) * tdec
                piece = o[i0:i0 + tdec, (kvh // 2) * LANES:(kvh // 2 + 1) * LANES]
                halves.append(piece if kvh % 2 == half else pltpu.roll(piece, HEAD_DIM, 1))
            mix_s[pl.ds(rs[pb], tdec), c_out * LANES:(c_out + 1) * LANES] = jnp.where(low_half, halves[0], halves[1])
        nk_ref[pb, 0:WINDOW - tdec, :] = kc[pb][tdec:, :]
        nk_ref[pb, WINDOW - tdec:WINDOW, :] = kn_s[pl.ds(rs[pb], tdec), :]
        nv_ref[pb, 0:WINDOW - tdec, :] = vc[pb][tdec:, :]
        nv_ref[pb, WINDOW - tdec:WINDOW, :] = vn_s[pl.ds(rs[pb], tdec), :]

    for g in range(SSM_GROUPS):
        sl = slice(g * SSM_GROUP_W, (g + 1) * SSM_GROUP_W)
        c0 = SSM_WIDTH + (SSM_GROUPS + g) * D_STATE
        hg, upd = [], []
        for pb in pbs:
            hg.append(sssm_ref[pb, sl, :])
            bm = jnp.where(row_b == bs[pb], bb_s[:, g * D_STATE:(g + 1) * D_STATE], jnp.zeros((), BF16))
            upd.append(_nn(xT_s[sl, :], bm))
        for pb in pbs:
            cc = xc_s[pl.ds(rs[pb], tdec), c0:c0 + D_STATE].astype(BF16)
            yoff_s[pl.ds(rs[pb], tdec), sl] = (_nt(cc, hg[pb].astype(BF16))
                                               * eax_s[pl.ds(rs[pb], tdec), sl])
        for pb in pbs:
            ea_b = ea_s[pl.ds(rs[pb], 1), :]
            dec = jnp.concatenate(
                [jnp.broadcast_to(ea_b[:, g * SSM_HPG + e:g * SSM_HPG + e + 1], (SSM_HEAD_DIM, D_STATE))
                 for e in range(SSM_HPG)], axis=0)
            nssm_ref[pb, sl, :] = hg[pb] * dec + upd[pb]

    @pl.when(j == pl.num_programs(1) - 1)
    def _():
        z = _nn(hn_s[...], wr_ref[:, R_Z:R_XBC])
        _gate_and_out((y_s, yoff_s), xc_s, z, x_ref, dskip_ref, gssm_ref, wout_ref, gpost_ref, mix_s, x1_ref)


def _ffn_kernel(xa_ref, xb_ref, gpre_ref, wg_ref, wu_ref, wd_ref, gpost_ref, oa_ref, ob_ref, f_s, *, steps_a):
    def tile(x_ref, o_ref):
        _rms_rows(x_ref, gpre_ref, f_s)
        f = f_s[...]
        d_ff = wg_ref.shape[1]
        acc = jnp.zeros(x_ref.shape, F32)
        for j in range(d_ff // FF_CHUNK):
            sl = slice(j * FF_CHUNK, (j + 1) * FF_CHUNK)
            gate = _nn(f, wg_ref[:, sl])
            up = _nn(f, wu_ref[:, sl])
            acc = acc + _nn((_silu(gate) * up).astype(BF16), wd_ref[sl, :])
        _rms_rows(acc, gpost_ref, o_ref, res_ref=x_ref)

    i = pl.program_id(0)

    @pl.when(i < steps_a)
    def _():
        tile(xa_ref, oa_ref)

    @pl.when(i >= steps_a)
    def _():
        tile(xb_ref, ob_ref)


def _const_spec(shape):
    nd = len(shape)
    return pl.BlockSpec(shape, lambda *_: (0,) * nd, pipeline_mode=pl.Buffered(1))


def _rope_tables(pos):
    half = ROT_DIM // 2
    inv = ROPE_THETA ** (-np.arange(half, dtype=np.float64) * 2.0 / ROT_DIM)
    ang = pos.astype(np.float64)[:, None] * inv[None, :]
    cos = np.cos(ang).astype(np.float32)
    sin = np.sin(ang).astype(np.float32)
    n = pos.shape[0]
    pad = HEAD_DIM - ROT_DIM
    c = np.concatenate([cos, cos, np.ones((n, pad), np.float32)], axis=1)
    s1 = np.concatenate([-sin, np.zeros((n, half + pad), np.float32)], axis=1)
    s2 = np.concatenate([np.zeros((n, half), np.float32), sin, np.zeros((n, pad), np.float32)], axis=1)
    rep = LANES // HEAD_DIM
    return tuple(jnp.asarray(np.tile(t, (1, rep))) for t in (c, s1, s2))


def _layer_params(g_pre_mix, w_in, attn_sinks, conv_w, conv_b, dt_bias, a_log, d_skip, g_ssm_out, w_out, g_post_mix):
    wr = w_in.astype(BF16)
    wdt = jnp.pad(w_in[:, R_END:], ((0, 0), (0, LANES - SSM_HEADS))).astype(BF16)
    wout = w_out.astype(BF16)
    pad16 = ((0, 0), (0, LANES - SSM_HEADS))
    expand = (np.arange(LANES)[:, None] == (np.arange(SSM_WIDTH)[None, :] // SSM_HEAD_DIM)).astype(np.float32)
    return dict(
        gpre=g_pre_mix.reshape(1, D_MODEL), wr=wr, wdt=wdt,
        convw=conv_w, convb=conv_b.reshape(1, CONV_DIM),
        dtb_row=jnp.pad(dt_bias.reshape(1, SSM_HEADS), pad16),
        alog_row=jnp.pad(a_log.reshape(1, SSM_HEADS), pad16),
        dskip=jnp.repeat(d_skip, SSM_HEAD_DIM).reshape(1, SSM_WIDTH), gssm=g_ssm_out.reshape(1, SSM_WIDTH),
        wout=wout, gpost=g_post_mix.reshape(1, D_MODEL), expand2=jnp.asarray(np.concatenate([expand, expand], axis=0), BF16),
        sinks_gk=attn_sinks.reshape(ATT_KV_HEADS, ATT_GROUP).T,
    )


_WEIGHT_ORDER = ("gpre", "wr", "wdt")
_TAIL_ORDER = ("convw", "convb", "dtb_row", "alog_row",
               "dskip", "gssm", "wout", "gpost", "expand2")


def _prompt_mixer(x, p):
    bsz, seq, _ = x.shape
    tm = SEQ_TILE
    nchunk = tm // CHUNK
    c, s1, s2 = _rope_tables(np.arange(seq))
    sink = jnp.repeat(p["sinks_gk"].T, CHUNK, axis=1)[:, None, :]
    jj = np.arange(2 * CHUNK)[:, None]
    tt = np.arange(CHUNK)[None, :]
    vis = (jj >= tt) & (jj <= tt + WINDOW)
    bias = np.stack([np.where(vis, 0.0, NEG), np.where(vis & (jj >= CHUNK), 0.0, NEG)]).astype(np.float32)
    bias = jnp.asarray(np.tile(bias, (1, 1, ATT_GROUP)))
    consts = [p[n] for n in _WEIGHT_ORDER]
    tail = [p[n] for n in _TAIL_ORDER]
    in_specs = ([pl.BlockSpec((None, tm, D_MODEL), lambda b, s: (b, s, 0))]
                + [_const_spec(a.shape) for a in consts]
                + [pl.BlockSpec((tm, LANES), lambda b, s: (s, 0))] * 3
                + [_const_spec(sink.shape), _const_spec(bias.shape)]
                + [_const_spec(a.shape) for a in tail])
    out_shape = (
        jax.ShapeDtypeStruct((bsz, seq, D_MODEL), F32),
        jax.ShapeDtypeStruct((bsz, WINDOW, KV_WIDTH), F32),
        jax.ShapeDtypeStruct((bsz, WINDOW, KV_WIDTH), F32),
        jax.ShapeDtypeStruct((bsz, CONV_W - 1, CONV_DIM), F32),
        jax.ShapeDtypeStruct((bsz, SSM_WIDTH, D_STATE), F32),
    )
    out_specs = (
        pl.BlockSpec((None, tm, D_MODEL), lambda b, s: (b, s, 0)),
        pl.BlockSpec((None, WINDOW, KV_WIDTH), lambda b, s: (b, 0, 0)),
        pl.BlockSpec((None, WINDOW, KV_WIDTH), lambda b, s: (b, 0, 0)),
        pl.BlockSpec((None, CONV_W - 1, CONV_DIM), lambda b, s: (b, 0, 0)),
        pl.BlockSpec((None, SSM_WIDTH, D_STATE), lambda b, s: (b, 0, 0)),
    )
    scratch = [
        pltpu.VMEM((tm, D_MODEL), BF16),
        pltpu.VMEM((tm, ATT_WIDTH), BF16),
        pltpu.VMEM((CHUNK + tm, KV_WIDTH), BF16),
        pltpu.VMEM((nchunk + 1, KV_WIDTH, CHUNK), BF16),
        pltpu.VMEM((SUBLANES + tm, CONV_DIM), F32),
        pltpu.VMEM((tm, CONV_DIM), F32),
        pltpu.VMEM((tm, LANES), F32),
        pltpu.VMEM((tm, SSM_WIDTH), F32),
        pltpu.VMEM((tm, MIX_WIDTH), BF16),
        pltpu.VMEM((D_STATE, SSM_WIDTH), F32),
        pltpu.VMEM((D_MODEL, ATT_WIDTH), BF16),
    ]
    return pl.pallas_call(
        _prompt_kernel,
        grid=(bsz, seq // tm),
        in_specs=in_specs,
        out_specs=out_specs,
        out_shape=out_shape,
        scratch_shapes=scratch,
        compiler_params=pltpu.CompilerParams(
            dimension_semantics=("arbitrary", "arbitrary"), vmem_limit_bytes=VMEM_LIMIT_BYTES),
        name="prompt_mixer",
    )(x, *consts, c, s1, s2, sink, bias, *tail)


def _sample_mixer(x, cache_k, cache_v, state_conv, state_ssm, p):
    nb, tdec, _ = x.shape
    bt = SAMPLE_BT
    rows = bt * tdec
    c, s1, s2 = _rope_tables(np.tile(PAST_LEN + np.arange(tdec), bt))
    sink = jnp.repeat(p["sinks_gk"].reshape(-1), tdec).reshape(rows, 1)
    consts = [p[n] for n in _WEIGHT_ORDER]
    tail = [p[n] for n in _TAIL_ORDER]
    x2 = x.reshape(nb * tdec, D_MODEL)
    ck = cache_k.reshape(nb, WINDOW, KV_WIDTH)
    cv = cache_v.reshape(nb, WINDOW, KV_WIDTH)
    ssm = state_ssm.reshape(nb, SSM_WIDTH, D_STATE)
    pb = SAMPLE_PB
    steps = bt // pb
    t_of_row = (np.arange(rows) % tdec)[:, None]
    col = np.arange(rows)[None, :]
    vis_c = col >= t_of_row
    bias = jnp.asarray(np.stack(
        [np.where(np.concatenate([vis_c, (col // tdec == b) & (col % tdec <= t_of_row)], axis=1), 0.0, NEG)
         for b in range(bt)]).astype(np.float32))
    tmap = lambda i, j: (i, 0, 0)
    pmap = lambda i, j: (i * steps + j, 0, 0)
    in_specs = ([pl.BlockSpec((rows, D_MODEL), lambda i, j: (i, 0)),
                 pl.BlockSpec((pb, WINDOW, KV_WIDTH), pmap),
                 pl.BlockSpec((pb, WINDOW, KV_WIDTH), pmap),
                 pl.BlockSpec((bt, CONV_W - 1, CONV_DIM), tmap),
                 pl.BlockSpec((pb, SSM_WIDTH, D_STATE), pmap)]
                + [_const_spec(a.shape) for a in consts]
                + [_const_spec(c.shape)] * 3
                + [_const_spec(sink.shape), _const_spec(bias.shape)]
                + [_const_spec(a.shape) for a in tail])
    out_shape = (
        jax.ShapeDtypeStruct((nb * tdec, D_MODEL), F32),
        jax.ShapeDtypeStruct((nb, WINDOW, KV_WIDTH), F32),
        jax.ShapeDtypeStruct((nb, WINDOW, KV_WIDTH), F32),
        jax.ShapeDtypeStruct((nb, CONV_W - 1, CONV_DIM), F32),
        jax.ShapeDtypeStruct((nb, SSM_WIDTH, D_STATE), F32),
    )
    out_specs = (
        pl.BlockSpec((rows, D_MODEL), lambda i, j: (i, 0)),
        pl.BlockSpec((pb, WINDOW, KV_WIDTH), pmap),
        pl.BlockSpec((pb, WINDOW, KV_WIDTH), pmap),
        pl.BlockSpec((bt, CONV_W - 1, CONV_DIM), tmap),
        pl.BlockSpec((pb, SSM_WIDTH, D_STATE), pmap),
    )
    scratch = [
        pltpu.VMEM((rows, D_MODEL), BF16),
        pltpu.VMEM((rows, ATT_WIDTH), F32),
        pltpu.VMEM((rows, KV_WIDTH), F32),
        pltpu.VMEM((rows, KV_WIDTH), F32),
        pltpu.VMEM((rows, KV_WIDTH), BF16),
        pltpu.VMEM((rows, KV_WIDTH), BF16),
        pltpu.VMEM((bt, 2 * SUBLANES, CONV_DIM), F32),
        pltpu.VMEM((rows, CONV_DIM), F32),
        pltpu.VMEM((rows, SSM_GROUPS * D_STATE), BF16),
        pltpu.VMEM((rows, LANES), F32),
        pltpu.VMEM((rows, SSM_WIDTH), F32),
        pltpu.VMEM((SSM_WIDTH, rows), BF16),
        pltpu.VMEM((rows, SSM_WIDTH), F32),
        pltpu.VMEM((rows, SSM_WIDTH), F32),
        pltpu.VMEM((rows, MIX_WIDTH), F32),
        pltpu.VMEM((D_MODEL, ATT_WIDTH), BF16),
    ]
    return pl.pallas_call(
        _sample_kernel,
        grid=(nb // bt, steps),
        in_specs=in_specs,
        out_specs=out_specs,
        out_shape=out_shape,
        scratch_shapes=scratch,
        compiler_params=pltpu.CompilerParams(
            dimension_semantics=("arbitrary", "arbitrary"), vmem_limit_bytes=VMEM_LIMIT_BYTES),
        name="sample_mixer",
    )(x2, ck, cv, state_conv, ssm, *consts, c, s1, s2, sink, bias, *tail)


def _ffn(xa, xb, gpre, wg, wu, wd, gpost):
    tf = FFN_TILE
    steps_a, steps_b = xa.shape[0] // tf, xb.shape[0] // tf
    consts = [gpre, wg, wu, wd, gpost]
    amap = lambda i: (jnp.minimum(i, steps_a - 1), 0)
    bmap = lambda i: (jnp.maximum(i - steps_a, 0), 0)
    return pl.pallas_call(
        functools.partial(_ffn_kernel, steps_a=steps_a),
        grid=(steps_a + steps_b,),
        in_specs=([pl.BlockSpec((tf, D_MODEL), amap), pl.BlockSpec((tf, D_MODEL), bmap)]
                  + [_const_spec(a.shape) for a in consts]),
        out_specs=(pl.BlockSpec((tf, D_MODEL), amap), pl.BlockSpec((tf, D_MODEL), bmap)),
        out_shape=(jax.ShapeDtypeStruct(xa.shape, F32), jax.ShapeDtypeStruct(xb.shape, F32)),
        scratch_shapes=[pltpu.VMEM((tf, D_MODEL), BF16)],
        compiler_params=pltpu.CompilerParams(
            dimension_semantics=("arbitrary",), vmem_limit_bytes=VMEM_LIMIT_BYTES),
        name="ffn",
    )(xa, xb, *consts)


def kernel(x_prompt, x_sample, cache_k_win, cache_v_win, state_conv, state_ssm, g_pre_mix, w_in, attn_sinks, conv_w, conv_b, dt_bias, a_log, d_skip, g_ssm_out, w_out, g_post_mix, g_pre_ffn, w_gate, w_up, w_down, g_post_ffn):
    depth = w_in.shape[0]
    bp, lp, _ = x_prompt.shape
    nb, ts, _ = x_sample.shape
    hp, hs = x_prompt, x_sample
    outs = [[] for _ in range(8)]
    for l in range(depth):
        p = _layer_params(g_pre_mix[l], w_in[l], attn_sinks[l], conv_w[l], conv_b[l], dt_bias[l], a_log[l],
                          d_skip[l], g_ssm_out[l], w_out[l], g_post_mix[l])
        ffn_w = (g_pre_ffn[l].reshape(1, D_MODEL), w_gate[l].astype(BF16), w_up[l].astype(BF16),
                 w_down[l].astype(BF16), g_post_ffn[l].reshape(1, D_MODEL))
        x1p, kp, vp, cp, sp = _prompt_mixer(hp, p)
        x1s, ksm, vsm, csm, ssm = _sample_mixer(hs, cache_k_win[l], cache_v_win[l], state_conv[l], state_ssm[l], p)
        hp, hs = _ffn(x1p.reshape(bp * lp, D_MODEL), x1s, *ffn_w)
        hp = hp.reshape(bp, lp, D_MODEL)
        hs = hs.reshape(nb, ts, D_MODEL)
        kv_shape = (WINDOW, ATT_KV_HEADS, HEAD_DIM)
        ssm_shape = (SSM_HEADS, SSM_HEAD_DIM, D_STATE)
        for lst, val in zip(outs, (kp.reshape((bp,) + kv_shape), vp.reshape((bp,) + kv_shape), cp,
                                   sp.reshape((bp,) + ssm_shape),
                                   ksm.reshape((nb,) + kv_shape), vsm.reshape((nb,) + kv_shape), csm,
                                   ssm.reshape((nb,) + ssm_shape))):
            lst.append(val)
    return (hp, hs) + tuple(jnp.stack(o) for o in outs)
```

```python
import functools
import math

import numpy as np
import jax
import jax.numpy as jnp
from jax import lax
from jax.experimental import pallas as pl
from jax.experimental.pallas import tpu as pltpu

F32 = jnp.float32
BF16 = jnp.bfloat16

D_MODEL = 1024
ATT_HEADS = 16
ATT_KV_HEADS = 4
ATT_GROUP = ATT_HEADS // ATT_KV_HEADS
HEAD_DIM = 64
ATT_WIDTH = ATT_HEADS * HEAD_DIM
KV_WIDTH = ATT_KV_HEADS * HEAD_DIM
WINDOW = 128
ROT_DIM = HEAD_DIM // 4
ROPE_THETA = 500000.0
SSM_HEADS = 16
SSM_HEAD_DIM = 64
SSM_WIDTH = SSM_HEADS * SSM_HEAD_DIM
SSM_GROUPS = 2
SSM_HPG = SSM_HEADS // SSM_GROUPS
SSM_GROUP_W = SSM_WIDTH // SSM_GROUPS
D_STATE = 128
CONV_W = 4
CONV_DIM = SSM_WIDTH + 2 * SSM_GROUPS * D_STATE
MIX_WIDTH = ATT_WIDTH + SSM_WIDTH
EPS = 1e-6
PAST_LEN = 8192

LANES = 128
SUBLANES = 8
VMEM_LIMIT_BYTES = 60 * 1024 * 1024

CHUNK = 128
NEG = -1e30
LOG2E = math.log2(math.e)
Q_SCALE = HEAD_DIM ** -0.5 * LOG2E
SEQ_TILE = 512
SAMPLE_BT = 16
SAMPLE_PB = 4
FFN_TILE = 1024
ROW_BLOCK = 32
FF_CHUNK = 256


def _nn(a, b):
    return jnp.dot(a, b, preferred_element_type=F32)


def _nt(a, b):
    return lax.dot_general(a, b, (((1,), (1,)), ((), ())), preferred_element_type=F32)


def _split_bf16(x, n):
    parts = []
    r = x
    for i in range(n):
        p = r.astype(BF16)
        parts.append(p)
        if i + 1 < n:
            r = r - p.astype(F32)
    return parts


def _expand_heads(x, expand2_ref):
    hi, mid = _split_bf16(x, 2)
    return _nn(jnp.concatenate([hi, mid], axis=1), expand2_ref[...])


def _cumsum_cols(m01, x):
    w = x.shape[1]
    r = _nn(m01, jnp.concatenate(_split_bf16(x, 3), axis=1))
    return r[:, :w] + r[:, w:2 * w] + r[:, 2 * w:]


def _heads_to_rows(x):
    return x.T[:SSM_HEADS]


def _rms(x, g):
    ms = jnp.mean(x * x, axis=-1, keepdims=True)
    return x * lax.rsqrt(ms + EPS) * g


def _rms_rows(src, g_ref, dst_ref, res_ref=None):
    g = g_ref[...]
    for r0 in range(0, dst_ref.shape[0], ROW_BLOCK):
        rs = slice(r0, r0 + ROW_BLOCK)
        y = _rms(src[rs, :], g)
        if res_ref is not None:
            y = res_ref[rs, :] + y
        dst_ref[rs, :] = y.astype(dst_ref.dtype)


def _silu(x):
    h = 0.5 * x
    return h + h * jnp.tanh(h)


R_K = ATT_WIDTH
R_V = R_K + KV_WIDTH
R_Z = R_V + KV_WIDTH
R_XBC = R_Z + SSM_WIDTH
R_END = R_XBC + CONV_DIM

CONV_CB = 256
CONV_RB = 64


def _conv_silu_cols(xpad_ref, row0, rows, convw_ref, convb_ref, out_ref, c0):
    cs = slice(c0, c0 + CONV_CB)
    w = [convw_ref[i:i + 1, cs] for i in range(CONV_W)]
    bias = convb_ref[:, cs]
    for r0 in range(0, rows, CONV_RB):
        xh = xpad_ref[row0 + r0 - SUBLANES:row0 + r0 + CONV_RB, cs]
        acc = bias + xh[SUBLANES:] * w[CONV_W - 1]
        for i in range(CONV_W - 1):
            acc = acc + pltpu.roll(xh, CONV_W - 1 - i, 0)[SUBLANES:] * w[i]
        out_ref[r0:r0 + CONV_RB, cs] = _silu(acc)


def _softplus(x):
    return jnp.maximum(x, 0.0) + jnp.log1p(jnp.exp(-jnp.abs(x)))


def _rope(x, c, s1, s2):
    outs = []
    for j in range(x.shape[1] // LANES):
        xb = x[:, j * LANES:(j + 1) * LANES]
        outs.append(xb * c + pltpu.roll(xb, LANES - ROT_DIM // 2, 1) * s1 + pltpu.roll(xb, ROT_DIM // 2, 1) * s2)
    return outs[0] if len(outs) == 1 else jnp.concatenate(outs, axis=1)


def _iota(shape, dim):
    return lax.broadcasted_iota(jnp.int32, shape, dim)


def _head_blocks(c_out):
    out = []
    for half in range(2):
        g, kvh = divmod(2 * c_out + half, ATT_KV_HEADS)
        b_in = kvh * ATT_GROUP + g
        out.append((b_in // 2, b_in % 2))
    return out


def _permute_q_weight(wr_ref, wq_s):
    low = _iota((1, LANES), 1) < HEAD_DIM
    for c_out in range(ATT_WIDTH // LANES):
        halves = []
        for half, (c_in, src_half) in enumerate(_head_blocks(c_out)):
            col = wr_ref[:, c_in * LANES:(c_in + 1) * LANES]
            halves.append(col if src_half == half else pltpu.roll(col, HEAD_DIM, 1))
        wq_s[:, c_out * LANES:(c_out + 1) * LANES] = jnp.where(low, halves[0], halves[1])


def _project(hn, wq_ref, wr_ref, ropec, ropes1, ropes2):
    q = _rope(_nn(hn, wq_ref[...]), ropec, ropes1, ropes2) * (HEAD_DIM ** -0.5)
    kv = _nn(hn, wr_ref[:, R_K:R_Z])
    k = _rope(kv[:, :KV_WIDTH], ropec, ropes1, ropes2)
    v = kv[:, KV_WIDTH:]
    return q, k, v


def _dt_cols(hn, wdt_ref, dtb_row_ref):
    return _softplus(_nn(hn, wdt_ref[...]) + dtb_row_ref[...])


def _a_row(alog_row_ref):
    lane = _iota((1, LANES), 1)
    return jnp.where(lane < SSM_HEADS, -jnp.exp(alog_row_ref[...]), 0.0)


def _log2_decay(acol, arow, dtr):
    return acol * LOG2E, (arow - jnp.log(dtr)) * LOG2E


SSD_QUAD = 4
SSD_NQUAD = SSM_HEADS // SSD_QUAD


def _ssd_cb(b_all, c_all):
    return [_nt(c_all[:, g * D_STATE:(g + 1) * D_STATE].astype(BF16),
                b_all[:, g * D_STATE:(g + 1) * D_STATE].astype(BF16)) for g in range(SSM_GROUPS)]


def _ssd_quad(qi, xs_bf, c_all, cbs, acol2, arow2, mask_bool, hT_bf=None):
    lane4 = _iota((1, SSD_QUAD * SSM_HEAD_DIM), 1) // SSM_HEAD_DIM
    zero = jnp.zeros((), BF16)
    e0 = SSD_QUAD * qi
    g = e0 // SSM_HPG
    cf = c_all[:, g * D_STATE:(g + 1) * D_STATE]
    lanes = slice(e0 * SSM_HEAD_DIM, (e0 + SSD_QUAD) * SSM_HEAD_DIM)
    xq = xs_bf[:, lanes]
    lhs, rhs = [], []
    for i in range(SSD_QUAD):
        e = e0 + i
        a_t = jnp.broadcast_to(acol2[:, e:e + 1], (CHUNK, CHUNK))
        w = cbs[g] * jnp.exp2(jnp.where(mask_bool, a_t - arow2[e:e + 1, :], NEG))
        lhs.append(w.astype(BF16))
        rhs.append(jnp.where(lane4 == i, xq, zero))
        if hT_bf is not None:
            lhs.append((cf * jnp.exp2(a_t)).astype(BF16))
            rhs.append(jnp.where(lane4 == i, hT_bf[:, lanes], zero))
    return _nn(jnp.concatenate(lhs, axis=1), jnp.concatenate(rhs, axis=0))


def _ssd_block(xs_bf, b_all, c_all, acol2, arow2, mask_bool):
    cbs = _ssd_cb(b_all, c_all)
    return jnp.concatenate([_ssd_quad(qi, xs_bf, c_all, cbs, acol2, arow2, mask_bool)
                            for qi in range(SSD_NQUAD)], axis=1)


def _gate_and_out(y_refs, xc_s, z, x_ref, dskip_ref, gssm_ref, wout_ref, gpost_ref, mix_s, out_ref):
    dskip = dskip_ref[...]
    for r0 in range(0, out_ref.shape[0], ROW_BLOCK):
        rs = slice(r0, r0 + ROW_BLOCK)
        y = y_refs[0][rs, :]
        for extra in y_refs[1:]:
            y = y + extra[rs, :]
        gated = (y + dskip * xc_s[rs, 0:SSM_WIDTH]) * _silu(z[rs, :])
        for g in range(SSM_GROUPS):
            gs = slice(g * SSM_GROUP_W, (g + 1) * SSM_GROUP_W)
            gg = gated[:, gs]
            ms = jnp.mean(gg * gg, axis=-1, keepdims=True)
            o = gg * lax.rsqrt(ms + EPS) * gssm_ref[:, gs]
            mix_s[rs, ATT_WIDTH + g * SSM_GROUP_W:ATT_WIDTH + (g + 1) * SSM_GROUP_W] = o.astype(mix_s.dtype)
    mo = _nn(mix_s[...].astype(BF16), wout_ref[...])
    _rms_rows(mo, gpost_ref, out_ref, res_ref=x_ref)


def _prompt_kernel(x_ref, gpre_ref, wr_ref, wdt_ref,
                   ropec_ref, ropes1_ref, ropes2_ref, sink_ref, biasT_ref,
                   convw_ref, convb_ref, dtb_row_ref, alog_row_ref,
                   dskip_ref, gssm_ref, wout_ref, gpost_ref, expand2_ref,
                   x1_ref, nk_ref, nv_ref, nconv_ref, nssm_ref,
                   hn_s, q_s, kbuf, vT_s, xbc_s, xc_s, dtc_s, y_s, mix_s, hT_s, wq_s):
    tm = x_ref.shape[0]
    nchunk = tm // CHUNK
    s = pl.program_id(1)
    last = pl.num_programs(1) - 1

    @pl.when(jnp.logical_and(pl.program_id(0) == 0, s == 0))
    def _():
        _permute_q_weight(wr_ref, wq_s)

    @pl.when(s == 0)
    def _():
        kbuf[0:CHUNK, :] = jnp.zeros((CHUNK, KV_WIDTH), BF16)
        vT_s[0] = jnp.zeros((KV_WIDTH, CHUNK), BF16)
        xbc_s[0:SUBLANES, :] = jnp.zeros((SUBLANES, CONV_DIM), F32)
        hT_s[...] = jnp.zeros_like(hT_s)

    _rms_rows(x_ref, gpre_ref, hn_s)
    hn = hn_s[...]

    ropec, ropes1, ropes2 = ropec_ref[...], ropes1_ref[...], ropes2_ref[...]

    def proj_xbc(c0):
        xbc_s[SUBLANES:SUBLANES + tm, c0:c0 + CONV_CB] = _nn(hn, wr_ref[:, R_XBC + c0:R_XBC + c0 + CONV_CB])

    def proj_q(c0):
        q_s[:, c0:c0 + KV_WIDTH] = (
            _rope(_nn(hn, wq_s[:, c0:c0 + KV_WIDTH]), ropec, ropes1, ropes2) * Q_SCALE).astype(BF16)

    def proj_k():
        kbuf[CHUNK:CHUNK + tm, :] = _rope(_nn(hn, wr_ref[:, R_K:R_V]), ropec, ropes1, ropes2).astype(BF16)

    def proj_v_dt():
        v = _nn(hn, wr_ref[:, R_V:R_Z])
        for j in range(nchunk):
            vT_s[1 + j] = v[j * CHUNK:(j + 1) * CHUNK, :].T.astype(BF16)
        dtc_s[...] = _dt_cols(hn, wdt_ref, dtb_row_ref)

    others = [functools.partial(proj_q, c0) for c0 in range(0, ATT_WIDTH, KV_WIDTH)] + [proj_k, proj_v_dt]
    conv_cols = list(range(0, CONV_DIM, CONV_CB))
    proj_xbc(conv_cols[0])
    for n, c0 in enumerate(conv_cols):
        if n + 1 < len(conv_cols):
            proj_xbc(conv_cols[n + 1])
        _conv_silu_cols(xbc_s, SUBLANES, tm, convw_ref, convb_ref, xc_s, c0)
        if n < len(others):
            others[n]()
    for f in others[len(conv_cols):]:
        f()

    a_row = _a_row(alog_row_ref)

    r2 = _iota((CHUNK, CHUNK), 0)
    c2 = _iota((CHUNK, CHUNK), 1)
    tril = c2 <= r2
    tril_bf = tril.astype(BF16)
    lane_kv = _iota((1, KV_WIDTH), 1) // HEAD_DIM
    ones_rows = jnp.ones((2 * SUBLANES, 2 * CHUNK), BF16)

    def chunk_body(c, carry):
        r0 = c * CHUNK
        first = jnp.logical_and(s == 0, c == 0).astype(jnp.int32)
        bias = biasT_ref[first]
        qcat = jnp.concatenate([q_s[pl.ds(r0, CHUNK), g * KV_WIDTH:(g + 1) * KV_WIDTH]
                                for g in range(ATT_GROUP)], axis=0)
        kwin = kbuf[pl.ds(r0, 2 * CHUNK), :]
        kstack = jnp.concatenate([jnp.where(lane_kv == kvh, kwin, jnp.zeros((), BF16))
                                  for kvh in range(ATT_KV_HEADS)], axis=0)
        sT = _nt(kstack, qcat)
        vT_win = jnp.concatenate([vT_s[c], vT_s[c + 1]], axis=1)

        dtc_c = dtc_s[pl.ds(r0, CHUNK), :]
        dtr_c = _heads_to_rows(dtc_c)
        acol = _cumsum_cols(tril_bf, dtc_c * a_row)
        arow = _heads_to_rows(acol)
        a_end = acol[CHUNK - 1:CHUNK, :]
        tailc = jnp.exp(a_end - acol) * dtc_c
        ex = _expand_heads(
            jnp.concatenate([tailc, jnp.broadcast_to(jnp.exp(a_end), (SUBLANES, LANES))], axis=0), expand2_ref)
        tlx = ex[:CHUNK]
        dec_row = ex[CHUNK:CHUNK + 1]
        xs = xc_s[pl.ds(r0, CHUNK), 0:SSM_WIDTH]
        b_all = xc_s[pl.ds(r0, CHUNK), SSM_WIDTH:SSM_WIDTH + SSM_GROUPS * D_STATE]
        c_all = xc_s[pl.ds(r0, CHUNK), SSM_WIDTH + SSM_GROUPS * D_STATE:CONV_DIM]
        hT = hT_s[...]
        acol2, arow2 = _log2_decay(acol, arow, dtr_c)
        xs_bf = xs.astype(BF16)
        hT_bf = hT.astype(BF16)
        cbs = _ssd_cb(b_all, c_all)

        o_rows = []
        for i in range(ATT_KV_HEADS):
            blk = sT[i * 2 * CHUNK:(i + 1) * 2 * CHUNK] + bias
            sink = sink_ref[i] * LOG2E
            m = jnp.maximum(jnp.max(blk, axis=0, keepdims=True), sink)
            p = jnp.exp2(blk - m).astype(BF16)
            lhs = jnp.concatenate([vT_win[i * HEAD_DIM:(i + 1) * HEAD_DIM], ones_rows], axis=0)
            oT = _nn(lhs, p)
            den = oT[HEAD_DIM:HEAD_DIM + 1] + jnp.exp2(sink - m)
            o_rows.append(oT[:HEAD_DIM] * (1.0 / den))
            for qi in range(i * SSD_NQUAD // ATT_KV_HEADS, (i + 1) * SSD_NQUAD // ATT_KV_HEADS):
                lanes = slice(qi * SSD_QUAD * SSM_HEAD_DIM, (qi + 1) * SSD_QUAD * SSM_HEAD_DIM)
                y_s[pl.ds(r0, CHUNK), lanes] = _ssd_quad(qi, xs_bf, c_all, cbs, acol2, arow2, tril, hT_bf)
        for c_out in range(ATT_WIDTH // LANES):
            kvh, g0 = divmod(2 * c_out, ATT_GROUP)
            two = jnp.concatenate([o_rows[kvh][:, g * CHUNK:(g + 1) * CHUNK] for g in (g0, g0 + 1)], axis=0)
            mix_s[pl.ds(r0, CHUNK), c_out * LANES:(c_out + 1) * LANES] = two.T.astype(BF16)

        xtl = (xs * tlx).astype(BF16)
        for g in range(SSM_GROUPS):
            sl = slice(g * SSM_GROUP_W, (g + 1) * SSM_GROUP_W)
            bt = b_all[:, g * D_STATE:(g + 1) * D_STATE].T.astype(BF16)
            hT_s[:, sl] = hT[:, sl] * dec_row[:, sl] + _nn(bt, xtl[:, sl])
        return carry

    for c in range(nchunk):
        chunk_body(c, 0)

    kbuf[0:CHUNK, :] = kbuf[tm:tm + CHUNK, :]
    vT_s[0] = vT_s[nchunk]
    xbc_s[0:SUBLANES, :] = xbc_s[tm:tm + SUBLANES, :]

    z = _nn(hn_s[...], wr_ref[:, R_Z:R_XBC])
    _gate_and_out((y_s,), xc_s, z, x_ref, dskip_ref, gssm_ref, wout_ref, gpost_ref, mix_s, x1_ref)

    @pl.when(s == last)
    def _():
        hn_w = hn_s[tm - WINDOW:, :]
        nk_ref[...] = _rope(_nn(hn_w, wr_ref[:, R_K:R_V]), ropec_ref[tm - WINDOW:, :],
                            ropes1_ref[tm - WINDOW:, :], ropes2_ref[tm - WINDOW:, :])
        nv_ref[...] = _nn(hn_w, wr_ref[:, R_V:R_Z])
        nconv_ref[...] = xbc_s[SUBLANES - (CONV_W - 1):SUBLANES, :]
        nssm_ref[...] = hT_s[...].T


def _sample_kernel(x_ref, ck_ref, cv_ref, sconv_ref, sssm_ref,
                   gpre_ref, wr_ref, wdt_ref,
                   ropec_ref, ropes1_ref, ropes2_ref, sink_ref, bias_ref,
                   convw_ref, convb_ref, dtb_row_ref, alog_row_ref,
                   dskip_ref, gssm_ref, wout_ref, gpost_ref, expand2_ref,
                   x1_ref, nk_ref, nv_ref, nconv_ref, nssm_ref,
                   hn_s, q_s, kn_s, vn_s, knb_s, vnb_s, xpad_s, xc_s, bb_s, ea_s, eax_s, xT_s, y_s, yoff_s, mix_s, wq_s):
    bt_n = sconv_ref.shape[0]
    pb_n = ck_ref.shape[0]
    m_rows = x_ref.shape[0]
    tdec = m_rows // bt_n
    j = pl.program_id(1)

    @pl.when(jnp.logical_and(pl.program_id(0) == 0, j == 0))
    def _():
        _permute_q_weight(wr_ref, wq_s)

    @pl.when(j == 0)
    def _():
        _rms_rows(x_ref, gpre_ref, hn_s)
        hn = hn_s[...]
        q, k, v = _project(hn, wq_s, wr_ref, ropec_ref[...], ropes1_ref[...], ropes2_ref[...])
        q_s[...] = q
        kn_s[...] = k
        vn_s[...] = v
        knb_s[...] = k.astype(BF16)
        vnb_s[...] = v.astype(BF16)

        xbc = _nn(hn, wr_ref[:, R_XBC:R_END])
        xpad_s[:, 0:SUBLANES - 3, :] = jnp.zeros((bt_n, SUBLANES - 3, CONV_DIM), F32)
        xpad_s[:, SUBLANES - 3:SUBLANES, :] = sconv_ref[...]
        xpad_s[:, SUBLANES:2 * SUBLANES, :] = xbc.reshape(bt_n, tdec, CONV_DIM)
        nconv_ref[...] = xpad_s[:, 2 * SUBLANES - 3:2 * SUBLANES, :]
        cb = 256
        for c0 in range(0, CONV_DIM, cb):
            cs = slice(c0, c0 + cb)
            xh = xpad_s[:, :, cs].reshape(bt_n * 2 * SUBLANES, cb)

            def new_rows(a):
                return a.reshape(bt_n, 2 * SUBLANES, cb)[:, SUBLANES:, :].reshape(m_rows, cb)

            acc = convb_ref[:, cs] + new_rows(xh) * convw_ref[CONV_W - 1:CONV_W, cs]
            for i in range(CONV_W - 1):
                acc = acc + new_rows(pltpu.roll(xh, CONV_W - 1 - i, 0)) * convw_ref[i:i + 1, cs]
            xc_s[:, cs] = _silu(acc)
        xs = xc_s[:, 0:SSM_WIDTH]
        b_all = xc_s[:, SSM_WIDTH:SSM_WIDTH + SSM_GROUPS * D_STATE]
        c_all = xc_s[:, SSM_WIDTH + SSM_GROUPS * D_STATE:CONV_DIM]
        bb_s[...] = b_all.astype(BF16)

        dtc = _dt_cols(hn, wdt_ref, dtb_row_ref)
        dtr = _heads_to_rows(dtc)
        a_row = _a_row(alog_row_ref)

        r2 = _iota((m_rows, m_rows), 0)
        c2 = _iota((m_rows, m_rows), 1)
        same = (r2 // tdec) == (c2 // tdec)
        causal = jnp.logical_and(same, c2 <= r2)
        causal_bf = causal.astype(BF16)
        same_bf = same.astype(BF16)

        dac = dtc * a_row
        acol = _cumsum_cols(causal_bf, dac)
        alast = _cumsum_cols(same_bf, dac)
        arow = _heads_to_rows(acol)
        tailc = jnp.exp(alast - acol) * dtc
        ex = _expand_heads(jnp.concatenate([jnp.exp(acol), tailc], axis=0), expand2_ref)
        ea_s[...] = jnp.exp(alast)
        eax_s[...] = ex[:m_rows]
        acol2, arow2 = _log2_decay(acol, arow, dtr)
        y_s[...] = _ssd_block(xs.astype(BF16), b_all, c_all, acol2, arow2, causal)
        xtl = xs * ex[m_rows:]
        for jj in range(SSM_WIDTH // LANES):
            xT_s[jj * LANES:(jj + 1) * LANES, :] = xtl[:, jj * LANES:(jj + 1) * LANES].T.astype(BF16)

    lane_kv = _iota((1, KV_WIDTH), 1) // HEAD_DIM
    row_b = _iota((m_rows, 1), 0) // tdec
    low_half = _iota((1, LANES), 1) < HEAD_DIM
    sink = sink_ref[...]

    pbs = range(pb_n)
    bs = [j * pb_n + pb for pb in pbs]
    rs = [pl.multiple_of(b * tdec, tdec) for b in bs]

    qbd, kc, vc, sc = [], [], [], []
    for pb in pbs:
        q8 = q_s[pl.ds(rs[pb], tdec), :]
        qbd.append(jnp.concatenate(
            [jnp.where(lane_kv == kvh, q8[:, g * KV_WIDTH:(g + 1) * KV_WIDTH], 0.0)
             for g in range(ATT_GROUP) for kvh in range(ATT_KV_HEADS)], axis=0).astype(BF16))
        kc.append(ck_ref[pb])
        vc.append(cv_ref[pb])
    for pb in pbs:
        keys = jnp.concatenate([kc[pb].astype(BF16), knb_s[...]], axis=0)
        sc.append(_nt(qbd[pb], keys) + bias_ref[bs[pb]])
    p, inv = [], []
    for pb in pbs:
        m = jnp.maximum(jnp.max(sc[pb], axis=1, keepdims=True), sink)
        e = jnp.exp(sc[pb] - m)
        inv.append(1.0 / (jnp.sum(e, axis=1, keepdims=True) + jnp.exp(sink - m)))
        p.append(e.astype(BF16))
    for pb in pbs:
        o = _nn(p[pb], jnp.concatenate([vc[pb].astype(BF16), vnb_s[...]], axis=0)) * inv[pb]
        for c_out in range(ATT_WIDTH // LANES):
            kvh, g0 = divmod(2 * c_out, ATT_GROUP)
            halves = []
            for half in range(2):
                i0 = ((g0 + half) * ATT_KV_HEADS + kvh) * tdec
                piece = o[i0:i0 + tdec, (kvh // 2) * LANES:(kvh // 2 + 1) * LANES]
                halves.append(piece if kvh % 2 == half else pltpu.roll(piece, HEAD_DIM, 1))
            mix_s[pl.ds(rs[pb], tdec), c_out * LANES:(c_out + 1) * LANES] = jnp.where(low_half, halves[0], halves[1])
        nk_ref[pb, 0:WINDOW - tdec, :] = kc[pb][tdec:, :]
        nk_ref[pb, WINDOW - tdec:WINDOW, :] = kn_s[pl.ds(rs[pb], tdec), :]
        nv_ref[pb, 0:WINDOW - tdec, :] = vc[pb][tdec:, :]
        nv_ref[pb, WINDOW - tdec:WINDOW, :] = vn_s[pl.ds(rs[pb], tdec), :]

    for g in range(SSM_GROUPS):
        sl = slice(g * SSM_GROUP_W, (g + 1) * SSM_GROUP_W)
        c0 = SSM_WIDTH + (SSM_GROUPS + g) * D_STATE
        hg, upd = [], []
        for pb in pbs:
            hg.append(sssm_ref[pb, sl, :])
            bm = jnp.where(row_b == bs[pb], bb_s[:, g * D_STATE:(g + 1) * D_STATE], jnp.zeros((), BF16))
            upd.append(_nn(xT_s[sl, :], bm))
        for pb in pbs:
            cc = xc_s[pl.ds(rs[pb], tdec), c0:c0 + D_STATE].astype(BF16)
            yoff_s[pl.ds(rs[pb], tdec), sl] = (_nt(cc, hg[pb].astype(BF16))
                                               * eax_s[pl.ds(rs[pb], tdec), sl])
        for pb in pbs:
            ea_b = ea_s[pl.ds(rs[pb], 1), :]
            dec = jnp.concatenate(
                [jnp.broadcast_to(ea_b[:, g * SSM_HPG + e:g * SSM_HPG + e + 1], (SSM_HEAD_DIM, D_STATE))
                 for e in range(SSM_HPG)], axis=0)
            nssm_ref[pb, sl, :] = hg[pb] * dec + upd[pb]

    @pl.when(j == pl.num_programs(1) - 1)
    def _():
        z = _nn(hn_s[...], wr_ref[:, R_Z:R_XBC])
        _gate_and_out((y_s, yoff_s), xc_s, z, x_ref, dskip_ref, gssm_ref, wout_ref, gpost_ref, mix_s, x1_ref)


def _ffn_kernel(xa_ref, xb_ref, gpre_ref, wg_ref, wu_ref, wd_ref, gpost_ref, oa_ref, ob_ref, f_s, acc_s, *, steps_a):
    def tile(x_ref, o_ref):
        rows = x_ref.shape[0]
        half = rows // 2
        g_pre, g_post = gpre_ref[...], gpost_ref[...]
        n_chunks = wg_ref.shape[1] // FF_CHUNK

        def pre_block(r0):
            f_s[r0:r0 + ROW_BLOCK, :] = _rms(x_ref[r0:r0 + ROW_BLOCK, :], g_pre).astype(BF16)

        def post_block(r0):
            rs = slice(r0, r0 + ROW_BLOCK)
            o_ref[rs, :] = x_ref[rs, :] + _rms(acc_s[rs, :], g_post)

        def matmuls(r0, fill):
            f = f_s[r0:r0 + half, :]
            acc = jnp.zeros((half, x_ref.shape[1]), F32)
            per = -(-len(fill) // n_chunks)
            for j in range(n_chunks):
                sl = slice(j * FF_CHUNK, (j + 1) * FF_CHUNK)
                gate = _nn(f, wg_ref[:, sl])
                up = _nn(f, wu_ref[:, sl])
                acc = acc + _nn((_silu(gate) * up).astype(BF16), wd_ref[sl, :])
                for t in fill[j * per:(j + 1) * per]:
                    t()
            acc_s[r0:r0 + half, :] = acc

        for r0 in range(0, half, ROW_BLOCK):
            pre_block(r0)
        matmuls(0, [functools.partial(pre_block, r0) for r0 in range(half, rows, ROW_BLOCK)])
        matmuls(half, [functools.partial(post_block, r0) for r0 in range(0, half, ROW_BLOCK)])
        for r0 in range(half, rows, ROW_BLOCK):
            post_block(r0)

    i = pl.program_id(0)

    @pl.when(i < steps_a)
    def _():
        tile(xa_ref, oa_ref)

    @pl.when(i >= steps_a)
    def _():
        tile(xb_ref, ob_ref)


def _const_spec(shape):
    nd = len(shape)
    return pl.BlockSpec(shape, lambda *_: (0,) * nd, pipeline_mode=pl.Buffered(1))


def _rope_tables(pos):
    half = ROT_DIM // 2
    inv = ROPE_THETA ** (-np.arange(half, dtype=np.float64) * 2.0 / ROT_DIM)
    ang = pos.astype(np.float64)[:, None] * inv[None, :]
    cos = np.cos(ang).astype(np.float32)
    sin = np.sin(ang).astype(np.float32)
    n = pos.shape[0]
    pad = HEAD_DIM - ROT_DIM
    c = np.concatenate([cos, cos, np.ones((n, pad), np.float32)], axis=1)
    s1 = np.concatenate([-sin, np.zeros((n, half + pad), np.float32)], axis=1)
    s2 = np.concatenate([np.zeros((n, half), np.float32), sin, np.zeros((n, pad), np.float32)], axis=1)
    rep = LANES // HEAD_DIM
    return tuple(jnp.asarray(np.tile(t, (1, rep))) for t in (c, s1, s2))


def _layer_params(g_pre_mix, w_in, attn_sinks, conv_w, conv_b, dt_bias, a_log, d_skip, g_ssm_out, w_out, g_post_mix):
    wr = w_in[:, :R_END].astype(BF16)
    wdt = jnp.pad(w_in[:, R_END:], ((0, 0), (0, LANES - SSM_HEADS))).astype(BF16)
    wout = w_out.astype(BF16)
    pad16 = ((0, 0), (0, LANES - SSM_HEADS))
    expand = (np.arange(LANES)[:, None] == (np.arange(SSM_WIDTH)[None, :] // SSM_HEAD_DIM)).astype(np.float32)
    return dict(
        gpre=g_pre_mix.reshape(1, D_MODEL), wr=wr, wdt=wdt,
        convw=conv_w, convb=conv_b.reshape(1, CONV_DIM),
        dtb_row=jnp.pad(dt_bias.reshape(1, SSM_HEADS), pad16),
        alog_row=jnp.pad(a_log.reshape(1, SSM_HEADS), pad16),
        dskip=jnp.repeat(d_skip, SSM_HEAD_DIM).reshape(1, SSM_WIDTH), gssm=g_ssm_out.reshape(1, SSM_WIDTH),
        wout=wout, gpost=g_post_mix.reshape(1, D_MODEL), expand2=jnp.asarray(np.concatenate([expand, expand], axis=0), BF16),
        sinks_gk=attn_sinks.reshape(ATT_KV_HEADS, ATT_GROUP).T,
    )


_WEIGHT_ORDER = ("gpre", "wr", "wdt")
_TAIL_ORDER = ("convw", "convb", "dtb_row", "alog_row",
               "dskip", "gssm", "wout", "gpost", "expand2")


def _prompt_mixer(x, p):
    bsz, seq, _ = x.shape
    tm = SEQ_TILE
    nchunk = tm // CHUNK
    c, s1, s2 = _rope_tables(np.arange(seq))
    sink = jnp.repeat(p["sinks_gk"].T, CHUNK, axis=1)[:, None, :]
    jj = np.arange(2 * CHUNK)[:, None]
    tt = np.arange(CHUNK)[None, :]
    vis = (jj >= tt) & (jj <= tt + WINDOW)
    bias = np.stack([np.where(vis, 0.0, NEG), np.where(vis & (jj >= CHUNK), 0.0, NEG)]).astype(np.float32)
    bias = jnp.asarray(np.tile(bias, (1, 1, ATT_GROUP)))
    consts = [p[n] for n in _WEIGHT_ORDER]
    tail = [p[n] for n in _TAIL_ORDER]
    in_specs = ([pl.BlockSpec((None, tm, D_MODEL), lambda b, s: (b, s, 0))]
                + [_const_spec(a.shape) for a in consts]
                + [pl.BlockSpec((tm, LANES), lambda b, s: (s, 0))] * 3
                + [_const_spec(sink.shape), _const_spec(bias.shape)]
                + [_const_spec(a.shape) for a in tail])
    out_shape = (
        jax.ShapeDtypeStruct((bsz, seq, D_MODEL), F32),
        jax.ShapeDtypeStruct((bsz, WINDOW, KV_WIDTH), F32),
        jax.ShapeDtypeStruct((bsz, WINDOW, KV_WIDTH), F32),
        jax.ShapeDtypeStruct((bsz, CONV_W - 1, CONV_DIM), F32),
        jax.ShapeDtypeStruct((bsz, SSM_WIDTH, D_STATE), F32),
    )
    out_specs = (
        pl.BlockSpec((None, tm, D_MODEL), lambda b, s: (b, s, 0)),
        pl.BlockSpec((None, WINDOW, KV_WIDTH), lambda b, s: (b, 0, 0)),
        pl.BlockSpec((None, WINDOW, KV_WIDTH), lambda b, s: (b, 0, 0)),
        pl.BlockSpec((None, CONV_W - 1, CONV_DIM), lambda b, s: (b, 0, 0)),
        pl.BlockSpec((None, SSM_WIDTH, D_STATE), lambda b, s: (b, 0, 0)),
    )
    scratch = [
        pltpu.VMEM((tm, D_MODEL), BF16),
        pltpu.VMEM((tm, ATT_WIDTH), BF16),
        pltpu.VMEM((CHUNK + tm, KV_WIDTH), BF16),
        pltpu.VMEM((nchunk + 1, KV_WIDTH, CHUNK), BF16),
        pltpu.VMEM((SUBLANES + tm, CONV_DIM), F32),
        pltpu.VMEM((tm, CONV_DIM), F32),
        pltpu.VMEM((tm, LANES), F32),
        pltpu.VMEM((tm, SSM_WIDTH), F32),
        pltpu.VMEM((tm, MIX_WIDTH), BF16),
        pltpu.VMEM((D_STATE, SSM_WIDTH), F32),
        pltpu.VMEM((D_MODEL, ATT_WIDTH), BF16),
    ]
    return pl.pallas_call(
        _prompt_kernel,
        grid=(bsz, seq // tm),
        in_specs=in_specs,
        out_specs=out_specs,
        out_shape=out_shape,
        scratch_shapes=scratch,
        compiler_params=pltpu.CompilerParams(
            dimension_semantics=("arbitrary", "arbitrary"), vmem_limit_bytes=VMEM_LIMIT_BYTES),
        name="prompt_mixer",
    )(x, *consts, c, s1, s2, sink, bias, *tail)


def _sample_mixer(x, cache_k, cache_v, state_conv, state_ssm, p):
    nb, tdec, _ = x.shape
    bt = SAMPLE_BT
    rows = bt * tdec
    c, s1, s2 = _rope_tables(np.tile(PAST_LEN + np.arange(tdec), bt))
    sink = jnp.repeat(p["sinks_gk"].reshape(-1), tdec).reshape(rows, 1)
    consts = [p[n] for n in _WEIGHT_ORDER]
    tail = [p[n] for n in _TAIL_ORDER]
    x2 = x.reshape(nb * tdec, D_MODEL)
    ck = cache_k.reshape(nb, WINDOW, KV_WIDTH)
    cv = cache_v.reshape(nb, WINDOW, KV_WIDTH)
    ssm = state_ssm.reshape(nb, SSM_WIDTH, D_STATE)
    pb = SAMPLE_PB
    steps = bt // pb
    t_of_row = (np.arange(rows) % tdec)[:, None]
    col = np.arange(rows)[None, :]
    vis_c = col >= t_of_row
    bias = jnp.asarray(np.stack(
        [np.where(np.concatenate([vis_c, (col // tdec == b) & (col % tdec <= t_of_row)], axis=1), 0.0, NEG)
         for b in range(bt)]).astype(np.float32))
    tmap = lambda i, j: (i, 0, 0)
    pmap = lambda i, j: (i * steps + j, 0, 0)
    in_specs = ([pl.BlockSpec((rows, D_MODEL), lambda i, j: (i, 0)),
                 pl.BlockSpec((pb, WINDOW, KV_WIDTH), pmap),
                 pl.BlockSpec((pb, WINDOW, KV_WIDTH), pmap),
                 pl.BlockSpec((bt, CONV_W - 1, CONV_DIM), tmap),
                 pl.BlockSpec((pb, SSM_WIDTH, D_STATE), pmap)]
                + [_const_spec(a.shape) for a in consts]
                + [_const_spec(c.shape)] * 3
                + [_const_spec(sink.shape), _const_spec(bias.shape)]
                + [_const_spec(a.shape) for a in tail])
    out_shape = (
        jax.ShapeDtypeStruct((nb * tdec, D_MODEL), F32),
        jax.ShapeDtypeStruct((nb, WINDOW, KV_WIDTH), F32),
        jax.ShapeDtypeStruct((nb, WINDOW, KV_WIDTH), F32),
        jax.ShapeDtypeStruct((nb, CONV_W - 1, CONV_DIM), F32),
        jax.ShapeDtypeStruct((nb, SSM_WIDTH, D_STATE), F32),
    )
    out_specs = (
        pl.BlockSpec((rows, D_MODEL), lambda i, j: (i, 0)),
        pl.BlockSpec((pb, WINDOW, KV_WIDTH), pmap),
        pl.BlockSpec((pb, WINDOW, KV_WIDTH), pmap),
        pl.BlockSpec((bt, CONV_W - 1, CONV_DIM), tmap),
        pl.BlockSpec((pb, SSM_WIDTH, D_STATE), pmap),
    )
    scratch = [
        pltpu.VMEM((rows, D_MODEL), BF16),
        pltpu.VMEM((rows, ATT_WIDTH), F32),
        pltpu.VMEM((rows, KV_WIDTH), F32),
        pltpu.VMEM((rows, KV_WIDTH), F32),
        pltpu.VMEM((rows, KV_WIDTH), BF16),
        pltpu.VMEM((rows, KV_WIDTH), BF16),
        pltpu.VMEM((bt, 2 * SUBLANES, CONV_DIM), F32),
        pltpu.VMEM((rows, CONV_DIM), F32),
        pltpu.VMEM((rows, SSM_GROUPS * D_STATE), BF16),
        pltpu.VMEM((rows, LANES), F32),
        pltpu.VMEM((rows, SSM_WIDTH), F32),
        pltpu.VMEM((SSM_WIDTH, rows), BF16),
        pltpu.VMEM((rows, SSM_WIDTH), F32),
        pltpu.VMEM((rows, SSM_WIDTH), F32),
        pltpu.VMEM((rows, MIX_WIDTH), F32),
        pltpu.VMEM((D_MODEL, ATT_WIDTH), BF16),
    ]
    return pl.pallas_call(
        _sample_kernel,
        grid=(nb // bt, steps),
        in_specs=in_specs,
        out_specs=out_specs,
        out_shape=out_shape,
        scratch_shapes=scratch,
        compiler_params=pltpu.CompilerParams(
            dimension_semantics=("arbitrary", "arbitrary"), vmem_limit_bytes=VMEM_LIMIT_BYTES),
        name="sample_mixer",
    )(x2, ck, cv, state_conv, ssm, *consts, c, s1, s2, sink, bias, *tail)


def _ffn(xa, xb, gpre, wg, wu, wd, gpost):
    tf = FFN_TILE
    steps_a, steps_b = xa.shape[0] // tf, xb.shape[0] // tf
    consts = [gpre, wg, wu, wd, gpost]
    amap = lambda i: (jnp.minimum(i, steps_a - 1), 0)
    bmap = lambda i: (jnp.maximum(i - steps_a, 0), 0)
    return pl.pallas_call(
        functools.partial(_ffn_kernel, steps_a=steps_a),
        grid=(steps_a + steps_b,),
        in_specs=([pl.BlockSpec((tf, D_MODEL), amap), pl.BlockSpec((tf, D_MODEL), bmap)]
                  + [_const_spec(a.shape) for a in consts]),
        out_specs=(pl.BlockSpec((tf, D_MODEL), amap), pl.BlockSpec((tf, D_MODEL), bmap)),
        out_shape=(jax.ShapeDtypeStruct(xa.shape, F32), jax.ShapeDtypeStruct(xb.shape, F32)),
        scratch_shapes=[pltpu.VMEM((tf, D_MODEL), BF16), pltpu.VMEM((tf, D_MODEL), F32)],
        compiler_params=pltpu.CompilerParams(
            dimension_semantics=("arbitrary",), vmem_limit_bytes=VMEM_LIMIT_BYTES),
        name="ffn",
    )(xa, xb, *consts)


def kernel(x_prompt, x_sample, cache_k_win, cache_v_win, state_conv, state_ssm, g_pre_mix, w_in, attn_sinks, conv_w, conv_b, dt_bias, a_log, d_skip, g_ssm_out, w_out, g_post_mix, g_pre_ffn, w_gate, w_up, w_down, g_post_ffn):
    depth = w_in.shape[0]
    bp, lp, _ = x_prompt.shape
    nb, ts, _ = x_sample.shape
    hp, hs = x_prompt, x_sample
    outs = [[] for _ in range(8)]
    for l in range(depth):
        p = _layer_params(g_pre_mix[l], w_in[l], attn_sinks[l], conv_w[l], conv_b[l], dt_bias[l], a_log[l],
                          d_skip[l], g_ssm_out[l], w_out[l], g_post_mix[l])
        ffn_w = (g_pre_ffn[l].reshape(1, D_MODEL), w_gate[l].astype(BF16), w_up[l].astype(BF16),
                 w_down[l].astype(BF16), g_post_ffn[l].reshape(1, D_MODEL))
        x1p, kp, vp, cp, sp = _prompt_mixer(hp, p)
        x1s, ksm, vsm, csm, ssm = _sample_mixer(hs, cache_k_win[l], cache_v_win[l], state_conv[l], state_ssm[l], p)
        hp, hs = _ffn(x1p.reshape(bp * lp, D_MODEL), x1s, *ffn_w)
        hp = hp.reshape(bp, lp, D_MODEL)
        hs = hs.reshape(nb, ts, D_MODEL)
        kv_shape = (WINDOW, ATT_KV_HEADS, HEAD_DIM)
        ssm_shape = (SSM_HEADS, SSM_HEAD_DIM, D_STATE)
        for lst, val in zip(outs, (kp.reshape((bp,) + kv_shape), vp.reshape((bp,) + kv_shape), cp,
                                   sp.reshape((bp,) + ssm_shape),
                                   ksm.reshape((nb,) + kv_shape), vsm.reshape((nb,) + kv_shape), csm,
                                   ssm.reshape((nb,) + ssm_shape))):
            lst.append(val)
    return (hp, hs) + tuple(jnp.stack(o) for o in outs)
```

```python
import functools
import math

import numpy as np
import jax
import jax.numpy as jnp
from jax import lax
from jax.experimental import pallas as pl
from jax.experimental.pallas import tpu as pltpu

F32 = jnp.float32
BF16 = jnp.bfloat16

D_MODEL = 1024
ATT_HEADS = 16
ATT_KV_HEADS = 4
ATT_GROUP = ATT_HEADS // ATT_KV_HEADS
HEAD_DIM = 64
ATT_WIDTH = ATT_HEADS * HEAD_DIM
KV_WIDTH = ATT_KV_HEADS * HEAD_DIM
WINDOW = 128
ROT_DIM = HEAD_DIM // 4
ROPE_THETA = 500000.0
SSM_HEADS = 16
SSM_HEAD_DIM = 64
SSM_WIDTH = SSM_HEADS * SSM_HEAD_DIM
SSM_GROUPS = 2
SSM_HPG = SSM_HEADS // SSM_GROUPS
SSM_GROUP_W = SSM_WIDTH // SSM_GROUPS
D_STATE = 128
CONV_W = 4
CONV_DIM = SSM_WIDTH + 2 * SSM_GROUPS * D_STATE
MIX_WIDTH = ATT_WIDTH + SSM_WIDTH
EPS = 1e-6
PAST_LEN = 8192

LANES = 128
SUBLANES = 8
VMEM_LIMIT_BYTES = 60 * 1024 * 1024

CHUNK = 128
NEG = -1e30
LOG2E = math.log2(math.e)
Q_SCALE = HEAD_DIM ** -0.5 * LOG2E
SEQ_TILE = 512
SAMPLE_BT = 16
SAMPLE_PB = 4
FFN_TILE = 512
PROJ_ROWS = 256
ROW_BLOCK = 32
FF_CHUNK = 256


def _nn(a, b):
    return jnp.dot(a, b, preferred_element_type=F32)


def _nt(a, b):
    return lax.dot_general(a, b, (((1,), (1,)), ((), ())), preferred_element_type=F32)


def _split_bf16(x, n):
    parts = []
    r = x
    for i in range(n):
        p = r.astype(BF16)
        parts.append(p)
        if i + 1 < n:
            r = r - p.astype(F32)
    return parts


def _expand_heads(x, expand2_ref):
    hi, mid = _split_bf16(x, 2)
    return _nn(jnp.concatenate([hi, mid], axis=1), expand2_ref[...])


def _cumsum_cols(m01, x):
    w = x.shape[1]
    r = _nn(m01, jnp.concatenate(_split_bf16(x, 3), axis=1))
    return r[:, :w] + r[:, w:2 * w] + r[:, 2 * w:]


def _heads_to_rows(x):
    return x.T[:SSM_HEADS]


def _rms(x, g):
    ms = jnp.mean(x * x, axis=-1, keepdims=True)
    return x * lax.rsqrt(ms + EPS) * g


def _rms_rows(src, g_ref, dst_ref, res_ref=None):
    g = g_ref[...]
    for r0 in range(0, dst_ref.shape[0], ROW_BLOCK):
        rs = slice(r0, r0 + ROW_BLOCK)
        y = _rms(src[rs, :], g)
        if res_ref is not None:
            y = res_ref[rs, :] + y
        dst_ref[rs, :] = y.astype(dst_ref.dtype)


def _silu(x):
    h = 0.5 * x
    return h + h * jnp.tanh(h)


R_K = ATT_WIDTH
R_V = R_K + KV_WIDTH
R_Z = R_V + KV_WIDTH
R_XBC = R_Z + SSM_WIDTH
R_END = R_XBC + CONV_DIM

CONV_CB = 256
CONV_RB = 64


def _conv_silu_cols(xpad_ref, row0, ra, rb, convw_ref, convb_ref, out_ref, c0):
    cs = slice(c0, c0 + CONV_CB)
    w = [convw_ref[i:i + 1, cs] for i in range(CONV_W)]
    bias = convb_ref[:, cs]
    for r0 in range(ra, rb, CONV_RB):
        xh = xpad_ref[row0 + r0 - SUBLANES:row0 + r0 + CONV_RB, cs]
        acc = bias + xh[SUBLANES:] * w[CONV_W - 1]
        for i in range(CONV_W - 1):
            acc = acc + pltpu.roll(xh, CONV_W - 1 - i, 0)[SUBLANES:] * w[i]
        out_ref[r0:r0 + CONV_RB, cs] = _silu(acc)


def _softplus(x):
    return jnp.maximum(x, 0.0) + jnp.log1p(jnp.exp(-jnp.abs(x)))


def _rope(x, c, s1, s2):
    outs = []
    for j in range(x.shape[1] // LANES):
        xb = x[:, j * LANES:(j + 1) * LANES]
        outs.append(xb * c + pltpu.roll(xb, LANES - ROT_DIM // 2, 1) * s1 + pltpu.roll(xb, ROT_DIM // 2, 1) * s2)
    return outs[0] if len(outs) == 1 else jnp.concatenate(outs, axis=1)


def _iota(shape, dim):
    return lax.broadcasted_iota(jnp.int32, shape, dim)


def _head_blocks(c_out):
    out = []
    for half in range(2):
        g, kvh = divmod(2 * c_out + half, ATT_KV_HEADS)
        b_in = kvh * ATT_GROUP + g
        out.append((b_in // 2, b_in % 2))
    return out


def _permute_q_weight(wr_ref, wq_s):
    low = _iota((1, LANES), 1) < HEAD_DIM
    for c_out in range(ATT_WIDTH // LANES):
        halves = []
        for half, (c_in, src_half) in enumerate(_head_blocks(c_out)):
            col = wr_ref[:, c_in * LANES:(c_in + 1) * LANES]
            halves.append(col if src_half == half else pltpu.roll(col, HEAD_DIM, 1))
        wq_s[:, c_out * LANES:(c_out + 1) * LANES] = jnp.where(low, halves[0], halves[1])


def _project(hn, wq_ref, wr_ref, ropec, ropes1, ropes2):
    q = _rope(_nn(hn, wq_ref[...]), ropec, ropes1, ropes2) * (HEAD_DIM ** -0.5)
    kv = _nn(hn, wr_ref[:, R_K:R_Z])
    k = _rope(kv[:, :KV_WIDTH], ropec, ropes1, ropes2)
    v = kv[:, KV_WIDTH:]
    return q, k, v


def _dt_cols(hn, wdt_ref, dtb_row_ref):
    return _softplus(_nn(hn, wdt_ref[...]) + dtb_row_ref[...])


def _a_row(alog_row_ref):
    lane = _iota((1, LANES), 1)
    return jnp.where(lane < SSM_HEADS, -jnp.exp(alog_row_ref[...]), 0.0)


def _log2_decay(acol, arow, dtr):
    return acol * LOG2E, (arow - jnp.log(dtr)) * LOG2E


SSD_QUAD = 4
SSD_NQUAD = SSM_HEADS // SSD_QUAD


def _ssd_cb(b_all, c_all):
    return [_nt(c_all[:, g * D_STATE:(g + 1) * D_STATE].astype(BF16),
                b_all[:, g * D_STATE:(g + 1) * D_STATE].astype(BF16)) for g in range(SSM_GROUPS)]


def _ssd_quad(qi, xs_bf, c_all, cbs, acol2, arow2, mask_bool, hT_bf=None):
    lane4 = _iota((1, SSD_QUAD * SSM_HEAD_DIM), 1) // SSM_HEAD_DIM
    zero = jnp.zeros((), BF16)
    e0 = SSD_QUAD * qi
    g = e0 // SSM_HPG
    cf = c_all[:, g * D_STATE:(g + 1) * D_STATE]
    lanes = slice(e0 * SSM_HEAD_DIM, (e0 + SSD_QUAD) * SSM_HEAD_DIM)
    xq = xs_bf[:, lanes]
    lhs, rhs = [], []
    for i in range(SSD_QUAD):
        e = e0 + i
        a_t = jnp.broadcast_to(acol2[:, e:e + 1], (CHUNK, CHUNK))
        w = cbs[g] * jnp.exp2(jnp.where(mask_bool, a_t - arow2[e:e + 1, :], NEG))
        lhs.append(w.astype(BF16))
        rhs.append(jnp.where(lane4 == i, xq, zero))
        if hT_bf is not None:
            lhs.append((cf * jnp.exp2(a_t)).astype(BF16))
            rhs.append(jnp.where(lane4 == i, hT_bf[:, lanes], zero))
    return _nn(jnp.concatenate(lhs, axis=1), jnp.concatenate(rhs, axis=0))


def _ssd_block(xs_bf, b_all, c_all, acol2, arow2, mask_bool):
    cbs = _ssd_cb(b_all, c_all)
    return jnp.concatenate([_ssd_quad(qi, xs_bf, c_all, cbs, acol2, arow2, mask_bool)
                            for qi in range(SSD_NQUAD)], axis=1)


def _gate_and_out(y_refs, xc_s, z, x_ref, dskip_ref, gssm_ref, wout_ref, gpost_ref, mix_s, out_ref):
    dskip = dskip_ref[...]
    for r0 in range(0, out_ref.shape[0], ROW_BLOCK):
        rs = slice(r0, r0 + ROW_BLOCK)
        y = y_refs[0][rs, :]
        for extra in y_refs[1:]:
            y = y + extra[rs, :]
        gated = (y + dskip * xc_s[rs, 0:SSM_WIDTH]) * _silu(z[rs, :])
        for g in range(SSM_GROUPS):
            gs = slice(g * SSM_GROUP_W, (g + 1) * SSM_GROUP_W)
            gg = gated[:, gs]
            ms = jnp.mean(gg * gg, axis=-1, keepdims=True)
            o = gg * lax.rsqrt(ms + EPS) * gssm_ref[:, gs]
            mix_s[rs, ATT_WIDTH + g * SSM_GROUP_W:ATT_WIDTH + (g + 1) * SSM_GROUP_W] = o.astype(mix_s.dtype)
    mo = _nn(mix_s[...].astype(BF16), wout_ref[...])
    _rms_rows(mo, gpost_ref, out_ref, res_ref=x_ref)


def _prompt_kernel(x_ref, gpre_ref, wr_ref, wdt_ref,
                   ropec_ref, ropes1_ref, ropes2_ref, sink_ref, biasT_ref,
                   convw_ref, convb_ref, dtb_row_ref, alog_row_ref,
                   dskip_ref, gssm_ref, wout_ref, gpost_ref, expand2_ref,
                   x1_ref, nk_ref, nv_ref, nconv_ref, nssm_ref,
                   hn_s, q_s, kbuf, vT_s, xbc_s, xc_s, dtc_s, y_s, mix_s, hT_s, wq_s):
    tm = x_ref.shape[0]
    nchunk = tm // CHUNK
    s = pl.program_id(1)
    last = pl.num_programs(1) - 1

    @pl.when(jnp.logical_and(pl.program_id(0) == 0, s == 0))
    def _():
        _permute_q_weight(wr_ref, wq_s)

    @pl.when(s == 0)
    def _():
        kbuf[0:CHUNK, :] = jnp.zeros((CHUNK, KV_WIDTH), BF16)
        vT_s[0] = jnp.zeros((KV_WIDTH, CHUNK), BF16)
        xbc_s[0:SUBLANES, :] = jnp.zeros((SUBLANES, CONV_DIM), F32)
        hT_s[...] = jnp.zeros_like(hT_s)

    _rms_rows(x_ref, gpre_ref, hn_s)

    def projection_pieces(ra, rb):
        rows = slice(ra, rb)

        def rope(x):
            return _rope(x, ropec_ref[rows, :], ropes1_ref[rows, :], ropes2_ref[rows, :])

        def proj_xbc(c0):
            xbc_s[SUBLANES + ra:SUBLANES + rb, c0:c0 + CONV_CB] = _nn(
                hn_s[rows, :], wr_ref[:, R_XBC + c0:R_XBC + c0 + CONV_CB])

        def conv(c0):
            _conv_silu_cols(xbc_s, SUBLANES, ra, rb, convw_ref, convb_ref, xc_s, c0)

        def proj_q(c0):
            q_s[rows, c0:c0 + KV_WIDTH] = (
                rope(_nn(hn_s[rows, :], wq_s[:, c0:c0 + KV_WIDTH])) * Q_SCALE).astype(BF16)

        def proj_k():
            kbuf[CHUNK + ra:CHUNK + rb, :] = rope(_nn(hn_s[rows, :], wr_ref[:, R_K:R_V])).astype(BF16)

        def proj_v_dt():
            v = _nn(hn_s[rows, :], wr_ref[:, R_V:R_Z])
            for j in range(ra // CHUNK, rb // CHUNK):
                vT_s[1 + j] = v[j * CHUNK - ra:(j + 1) * CHUNK - ra, :].T.astype(BF16)
            dtc_s[rows, :] = _dt_cols(hn_s[rows, :], wdt_ref, dtb_row_ref)

        others = [functools.partial(proj_q, c0) for c0 in range(0, ATT_WIDTH, KV_WIDTH)] + [proj_k, proj_v_dt]
        conv_cols = list(range(0, CONV_DIM, CONV_CB))
        pieces = [functools.partial(proj_xbc, conv_cols[0])]
        for n, c0 in enumerate(conv_cols):
            if n + 1 < len(conv_cols):
                pieces.append(functools.partial(proj_xbc, conv_cols[n + 1]))
            pieces.append(functools.partial(conv, c0))
            if n < len(others):
                pieces.append(others[n])
        return pieces + others[len(conv_cols):]

    a_row = _a_row(alog_row_ref)

    r2 = _iota((CHUNK, CHUNK), 0)
    c2 = _iota((CHUNK, CHUNK), 1)
    tril = c2 <= r2
    tril_bf = tril.astype(BF16)
    lane_kv = _iota((1, KV_WIDTH), 1) // HEAD_DIM
    ones_rows = jnp.ones((2 * SUBLANES, 2 * CHUNK), BF16)

    def chunk_body(c, extras):
        r0 = c * CHUNK
        per = -(-len(extras) // ATT_KV_HEADS)
        first = jnp.logical_and(s == 0, c == 0).astype(jnp.int32)
        bias = biasT_ref[first]
        qcat = jnp.concatenate([q_s[pl.ds(r0, CHUNK), g * KV_WIDTH:(g + 1) * KV_WIDTH]
                                for g in range(ATT_GROUP)], axis=0)
        kwin = kbuf[pl.ds(r0, 2 * CHUNK), :]
        kstack = jnp.concatenate([jnp.where(lane_kv == kvh, kwin, jnp.zeros((), BF16))
                                  for kvh in range(ATT_KV_HEADS)], axis=0)
        sT = _nt(kstack, qcat)
        vT_win = jnp.concatenate([vT_s[c], vT_s[c + 1]], axis=1)

        dtc_c = dtc_s[pl.ds(r0, CHUNK), :]
        dtr_c = _heads_to_rows(dtc_c)
        acol = _cumsum_cols(tril_bf, dtc_c * a_row)
        arow = _heads_to_rows(acol)
        a_end = acol[CHUNK - 1:CHUNK, :]
        tailc = jnp.exp(a_end - acol) * dtc_c
        ex = _expand_heads(
            jnp.concatenate([tailc, jnp.broadcast_to(jnp.exp(a_end), (SUBLANES, LANES))], axis=0), expand2_ref)
        tlx = ex[:CHUNK]
        dec_row = ex[CHUNK:CHUNK + 1]
        xs = xc_s[pl.ds(r0, CHUNK), 0:SSM_WIDTH]
        b_all = xc_s[pl.ds(r0, CHUNK), SSM_WIDTH:SSM_WIDTH + SSM_GROUPS * D_STATE]
        c_all = xc_s[pl.ds(r0, CHUNK), SSM_WIDTH + SSM_GROUPS * D_STATE:CONV_DIM]
        hT = hT_s[...]
        acol2, arow2 = _log2_decay(acol, arow, dtr_c)
        xs_bf = xs.astype(BF16)
        hT_bf = hT.astype(BF16)
        cbs = _ssd_cb(b_all, c_all)

        o_rows = []
        for i in range(ATT_KV_HEADS):
            blk = sT[i * 2 * CHUNK:(i + 1) * 2 * CHUNK] + bias
            sink = sink_ref[i] * LOG2E
            m = jnp.maximum(jnp.max(blk, axis=0, keepdims=True), sink)
            p = jnp.exp2(blk - m).astype(BF16)
            lhs = jnp.concatenate([vT_win[i * HEAD_DIM:(i + 1) * HEAD_DIM], ones_rows], axis=0)
            oT = _nn(lhs, p)
            den = oT[HEAD_DIM:HEAD_DIM + 1] + jnp.exp2(sink - m)
            o_rows.append(oT[:HEAD_DIM] * (1.0 / den))
            for qi in range(i * SSD_NQUAD // ATT_KV_HEADS, (i + 1) * SSD_NQUAD // ATT_KV_HEADS):
                lanes = slice(qi * SSD_QUAD * SSM_HEAD_DIM, (qi + 1) * SSD_QUAD * SSM_HEAD_DIM)
                y_s[pl.ds(r0, CHUNK), lanes] = _ssd_quad(qi, xs_bf, c_all, cbs, acol2, arow2, tril, hT_bf)
            for t in extras[i * per:(i + 1) * per]:
                t()
        for c_out in range(ATT_WIDTH // LANES):
            kvh, g0 = divmod(2 * c_out, ATT_GROUP)
            two = jnp.concatenate([o_rows[kvh][:, g * CHUNK:(g + 1) * CHUNK] for g in (g0, g0 + 1)], axis=0)
            mix_s[pl.ds(r0, CHUNK), c_out * LANES:(c_out + 1) * LANES] = two.T.astype(BF16)

        xtl = (xs * tlx).astype(BF16)
        for g in range(SSM_GROUPS):
            sl = slice(g * SSM_GROUP_W, (g + 1) * SSM_GROUP_W)
            bt = b_all[:, g * D_STATE:(g + 1) * D_STATE].T.astype(BF16)
            hT_s[:, sl] = hT[:, sl] * dec_row[:, sl] + _nn(bt, xtl[:, sl])

    groups = list(range(0, tm, PROJ_ROWS))
    cpg = PROJ_ROWS // CHUNK
    for t in projection_pieces(0, PROJ_ROWS):
        t()
    for gi, ra in enumerate(groups):
        nxt = projection_pieces(ra + PROJ_ROWS, ra + 2 * PROJ_ROWS) if gi + 1 < len(groups) else []
        share = -(-len(nxt) // cpg)
        for k in range(cpg):
            chunk_body(ra // CHUNK + k, nxt[k * share:(k + 1) * share])

    kbuf[0:CHUNK, :] = kbuf[tm:tm + CHUNK, :]
    vT_s[0] = vT_s[nchunk]
    xbc_s[0:SUBLANES, :] = xbc_s[tm:tm + SUBLANES, :]

    z = _nn(hn_s[...], wr_ref[:, R_Z:R_XBC])
    _gate_and_out((y_s,), xc_s, z, x_ref, dskip_ref, gssm_ref, wout_ref, gpost_ref, mix_s, x1_ref)

    @pl.when(s == last)
    def _():
        hn_w = hn_s[tm - WINDOW:, :]
        nk_ref[...] = _rope(_nn(hn_w, wr_ref[:, R_K:R_V]), ropec_ref[tm - WINDOW:, :],
                            ropes1_ref[tm - WINDOW:, :], ropes2_ref[tm - WINDOW:, :])
        nv_ref[...] = _nn(hn_w, wr_ref[:, R_V:R_Z])
        nconv_ref[...] = xbc_s[SUBLANES - (CONV_W - 1):SUBLANES, :]
        nssm_ref[...] = hT_s[...].T


def _sample_kernel(x_ref, ck_ref, cv_ref, sconv_ref, sssm_ref,
                   gpre_ref, wr_ref, wdt_ref,
                   ropec_ref, ropes1_ref, ropes2_ref, sink_ref, bias_ref,
                   convw_ref, convb_ref, dtb_row_ref, alog_row_ref,
                   dskip_ref, gssm_ref, wout_ref, gpost_ref, expand2_ref,
                   x1_ref, nk_ref, nv_ref, nconv_ref, nssm_ref,
                   hn_s, q_s, kn_s, vn_s, knb_s, vnb_s, xpad_s, xc_s, bb_s, ea_s, eax_s, xT_s, y_s, yoff_s, mix_s, wq_s):
    bt_n = sconv_ref.shape[0]
    pb_n = ck_ref.shape[0]
    m_rows = x_ref.shape[0]
    tdec = m_rows // bt_n
    j = pl.program_id(1)

    @pl.when(jnp.logical_and(pl.program_id(0) == 0, j == 0))
    def _():
        _permute_q_weight(wr_ref, wq_s)

    @pl.when(j == 0)
    def _():
        _rms_rows(x_ref, gpre_ref, hn_s)
        hn = hn_s[...]
        q, k, v = _project(hn, wq_s, wr_ref, ropec_ref[...], ropes1_ref[...], ropes2_ref[...])
        q_s[...] = q
        kn_s[...] = k
        vn_s[...] = v
        knb_s[...] = k.astype(BF16)
        vnb_s[...] = v.astype(BF16)

        xbc = _nn(hn, wr_ref[:, R_XBC:R_END])
        xpad_s[:, 0:SUBLANES - 3, :] = jnp.zeros((bt_n, SUBLANES - 3, CONV_DIM), F32)
        xpad_s[:, SUBLANES - 3:SUBLANES, :] = sconv_ref[...]
        xpad_s[:, SUBLANES:2 * SUBLANES, :] = xbc.reshape(bt_n, tdec, CONV_DIM)
        nconv_ref[...] = xpad_s[:, 2 * SUBLANES - 3:2 * SUBLANES, :]
        cb = 256
        for c0 in range(0, CONV_DIM, cb):
            cs = slice(c0, c0 + cb)
            xh = xpad_s[:, :, cs].reshape(bt_n * 2 * SUBLANES, cb)

            def new_rows(a):
                return a.reshape(bt_n, 2 * SUBLANES, cb)[:, SUBLANES:, :].reshape(m_rows, cb)

            acc = convb_ref[:, cs] + new_rows(xh) * convw_ref[CONV_W - 1:CONV_W, cs]
            for i in range(CONV_W - 1):
                acc = acc + new_rows(pltpu.roll(xh, CONV_W - 1 - i, 0)) * convw_ref[i:i + 1, cs]
            xc_s[:, cs] = _silu(acc)
        xs = xc_s[:, 0:SSM_WIDTH]
        b_all = xc_s[:, SSM_WIDTH:SSM_WIDTH + SSM_GROUPS * D_STATE]
        c_all = xc_s[:, SSM_WIDTH + SSM_GROUPS * D_STATE:CONV_DIM]
        bb_s[...] = b_all.astype(BF16)

        dtc = _dt_cols(hn, wdt_ref, dtb_row_ref)
        dtr = _heads_to_rows(dtc)
        a_row = _a_row(alog_row_ref)

        r2 = _iota((m_rows, m_rows), 0)
        c2 = _iota((m_rows, m_rows), 1)
        same = (r2 // tdec) == (c2 // tdec)
        causal = jnp.logical_and(same, c2 <= r2)
        causal_bf = causal.astype(BF16)
        same_bf = same.astype(BF16)

        dac = dtc * a_row
        acol = _cumsum_cols(causal_bf, dac)
        alast = _cumsum_cols(same_bf, dac)
        arow = _heads_to_rows(acol)
        tailc = jnp.exp(alast - acol) * dtc
        ex = _expand_heads(jnp.concatenate([jnp.exp(acol), tailc], axis=0), expand2_ref)
        ea_s[...] = jnp.exp(alast)
        eax_s[...] = ex[:m_rows]
        acol2, arow2 = _log2_decay(acol, arow, dtr)
        y_s[...] = _ssd_block(xs.astype(BF16), b_all, c_all, acol2, arow2, causal)
        xtl = xs * ex[m_rows:]
        for jj in range(SSM_WIDTH // LANES):
            xT_s[jj * LANES:(jj + 1) * LANES, :] = xtl[:, jj * LANES:(jj + 1) * LANES].T.astype(BF16)

    lane_kv = _iota((1, KV_WIDTH), 1) // HEAD_DIM
    row_b = _iota((m_rows, 1), 0) // tdec
    low_half = _iota((1, LANES), 1) < HEAD_DIM
    sink = sink_ref[...]

    pbs = range(pb_n)
    bs = [j * pb_n + pb for pb in pbs]
    rs = [pl.multiple_of(b * tdec, tdec) for b in bs]

    qbd, kc, vc, sc = [], [], [], []
    for pb in pbs:
        q8 = q_s[pl.ds(rs[pb], tdec), :]
        qbd.append(jnp.concatenate(
            [jnp.where(lane_kv == kvh, q8[:, g * KV_WIDTH:(g + 1) * KV_WIDTH], 0.0)
             for g in range(ATT_GROUP) for kvh in range(ATT_KV_HEADS)], axis=0).astype(BF16))
        kc.append(ck_ref[pb])
        vc.append(cv_ref[pb])
    for pb in pbs:
        keys = jnp.concatenate([kc[pb].astype(BF16), knb_s[...]], axis=0)
        sc.append(_nt(qbd[pb], keys) + bias_ref[bs[pb]])
    p, inv = [], []
    for pb in pbs:
        m = jnp.maximum(jnp.max(sc[pb], axis=1, keepdims=True), sink)
        e = jnp.exp(sc[pb] - m)
        inv.append(1.0 / (jnp.sum(e, axis=1, keepdims=True) + jnp.exp(sink - m)))
        p.append(e.astype(BF16))
    for pb in pbs:
        o = _nn(p[pb], jnp.concatenate([vc[pb].astype(BF16), vnb_s[...]], axis=0)) * inv[pb]
        for c_out in range(ATT_WIDTH // LANES):
            kvh, g0 = divmod(2 * c_out, ATT_GROUP)
            halves = []
            for half in range(2):
                i0 = ((g0 + half) * ATT_KV_HEADS + kvh) * tdec
                piece = o[i0:i0 + tdec, (kvh // 2) * LANES:(kvh // 2 + 1) * LANES]
                halves.append(piece if kvh % 2 == half else pltpu.roll(piece, HEAD_DIM, 1))
            mix_s[pl.ds(rs[pb], tdec), c_out * LANES:(c_out + 1) * LANES] = jnp.where(low_half, halves[0], halves[1])
        nk_ref[pb, 0:WINDOW - tdec, :] = kc[pb][tdec:, :]
        nk_ref[pb, WINDOW - tdec:WINDOW, :] = kn_s[pl.ds(rs[pb], tdec), :]
        nv_ref[pb, 0:WINDOW - tdec, :] = vc[pb][tdec:, :]
        nv_ref[pb, WINDOW - tdec:WINDOW, :] = vn_s[pl.ds(rs[pb], tdec), :]

    for g in range(SSM_GROUPS):
        sl = slice(g * SSM_GROUP_W, (g + 1) * SSM_GROUP_W)
        c0 = SSM_WIDTH + (SSM_GROUPS + g) * D_STATE
        hg, upd = [], []
        for pb in pbs:
            hg.append(sssm_ref[pb, sl, :])
            bm = jnp.where(row_b == bs[pb], bb_s[:, g * D_STATE:(g + 1) * D_STATE], jnp.zeros((), BF16))
            upd.append(_nn(xT_s[sl, :], bm))
        for pb in pbs:
            cc = xc_s[pl.ds(rs[pb], tdec), c0:c0 + D_STATE].astype(BF16)
            yoff_s[pl.ds(rs[pb], tdec), sl] = (_nt(cc, hg[pb].astype(BF16))
                                               * eax_s[pl.ds(rs[pb], tdec), sl])
        for pb in pbs:
            ea_b = ea_s[pl.ds(rs[pb], 1), :]
            dec = jnp.concatenate(
                [jnp.broadcast_to(ea_b[:, g * SSM_HPG + e:g * SSM_HPG + e + 1], (SSM_HEAD_DIM, D_STATE))
                 for e in range(SSM_HPG)], axis=0)
            nssm_ref[pb, sl, :] = hg[pb] * dec + upd[pb]

    @pl.when(j == pl.num_programs(1) - 1)
    def _():
        z = _nn(hn_s[...], wr_ref[:, R_Z:R_XBC])
        _gate_and_out((y_s, yoff_s), xc_s, z, x_ref, dskip_ref, gssm_ref, wout_ref, gpost_ref, mix_s, x1_ref)


def _ffn_kernel(xa_ref, xb_ref, gpre_ref, wg_ref, wu_ref, wd_ref, gpost_ref, oa_ref, ob_ref, f_s, *, steps_a):
    def tile(x_ref, o_ref):
        _rms_rows(x_ref, gpre_ref, f_s)
        f = f_s[...]
        d_ff = wg_ref.shape[1]
        acc = jnp.zeros(x_ref.shape, F32)
        for j in range(d_ff // FF_CHUNK):
            sl = slice(j * FF_CHUNK, (j + 1) * FF_CHUNK)
            gate = _nn(f, wg_ref[:, sl])
            up = _nn(f, wu_ref[:, sl])
            acc = acc + _nn((_silu(gate) * up).astype(BF16), wd_ref[sl, :])
        _rms_rows(acc, gpost_ref, o_ref, res_ref=x_ref)

    i = pl.program_id(0)

    @pl.when(i < steps_a)
    def _():
        tile(xa_ref, oa_ref)

    @pl.when(i >= steps_a)
    def _():
        tile(xb_ref, ob_ref)


def _const_spec(shape):
    nd = len(shape)
    return pl.BlockSpec(shape, lambda *_: (0,) * nd, pipeline_mode=pl.Buffered(1))


def _rope_tables(pos):
    half = ROT_DIM // 2
    inv = ROPE_THETA ** (-np.arange(half, dtype=np.float64) * 2.0 / ROT_DIM)
    ang = pos.astype(np.float64)[:, None] * inv[None, :]
    cos = np.cos(ang).astype(np.float32)
    sin = np.sin(ang).astype(np.float32)
    n = pos.shape[0]
    pad = HEAD_DIM - ROT_DIM
    c = np.concatenate([cos, cos, np.ones((n, pad), np.float32)], axis=1)
    s1 = np.concatenate([-sin, np.zeros((n, half + pad), np.float32)], axis=1)
    s2 = np.concatenate([np.zeros((n, half), np.float32), sin, np.zeros((n, pad), np.float32)], axis=1)
    rep = LANES // HEAD_DIM
    return tuple(jnp.asarray(np.tile(t, (1, rep))) for t in (c, s1, s2))


def _layer_params(g_pre_mix, w_in, attn_sinks, conv_w, conv_b, dt_bias, a_log, d_skip, g_ssm_out, w_out, g_post_mix):
    wr = w_in.astype(BF16)
    wdt = jnp.pad(w_in[:, R_END:], ((0, 0), (0, LANES - SSM_HEADS))).astype(BF16)
    wout = w_out.astype(BF16)
    pad16 = ((0, 0), (0, LANES - SSM_HEADS))
    expand = (np.arange(LANES)[:, None] == (np.arange(SSM_WIDTH)[None, :] // SSM_HEAD_DIM)).astype(np.float32)
    return dict(
        gpre=g_pre_mix.reshape(1, D_MODEL), wr=wr, wdt=wdt,
        convw=conv_w, convb=conv_b.reshape(1, CONV_DIM),
        dtb_row=jnp.pad(dt_bias.reshape(1, SSM_HEADS), pad16),
        alog_row=jnp.pad(a_log.reshape(1, SSM_HEADS), pad16),
        dskip=jnp.repeat(d_skip, SSM_HEAD_DIM).reshape(1, SSM_WIDTH), gssm=g_ssm_out.reshape(1, SSM_WIDTH),
        wout=wout, gpost=g_post_mix.reshape(1, D_MODEL), expand2=jnp.asarray(np.concatenate([expand, expand], axis=0), BF16),
        sinks_gk=attn_sinks.reshape(ATT_KV_HEADS, ATT_GROUP).T,
    )


_WEIGHT_ORDER = ("gpre", "wr", "wdt")
_TAIL_ORDER = ("convw", "convb", "dtb_row", "alog_row",
               "dskip", "gssm", "wout", "gpost", "expand2")


def _prompt_mixer(x, p):
    bsz, seq, _ = x.shape
    tm = SEQ_TILE
    nchunk = tm // CHUNK
    c, s1, s2 = _rope_tables(np.arange(seq))
    sink = jnp.repeat(p["sinks_gk"].T, CHUNK, axis=1)[:, None, :]
    jj = np.arange(2 * CHUNK)[:, None]
    tt = np.arange(CHUNK)[None, :]
    vis = (jj >= tt) & (jj <= tt + WINDOW)
    bias = np.stack([np.where(vis, 0.0, NEG), np.where(vis & (jj >= CHUNK), 0.0, NEG)]).astype(np.float32)
    bias = jnp.asarray(np.tile(bias, (1, 1, ATT_GROUP)))
    consts = [p[n] for n in _WEIGHT_ORDER]
    tail = [p[n] for n in _TAIL_ORDER]
    in_specs = ([pl.BlockSpec((None, tm, D_MODEL), lambda b, s: (b, s, 0))]
                + [_const_spec(a.shape) for a in consts]
                + [pl.BlockSpec((tm, LANES), lambda b, s: (s, 0))] * 3
                + [_const_spec(sink.shape), _const_spec(bias.shape)]
                + [_const_spec(a.shape) for a in tail])
    out_shape = (
        jax.ShapeDtypeStruct((bsz, seq, D_MODEL), F32),
        jax.ShapeDtypeStruct((bsz, WINDOW, KV_WIDTH), F32),
        jax.ShapeDtypeStruct((bsz, WINDOW, KV_WIDTH), F32),
        jax.ShapeDtypeStruct((bsz, CONV_W - 1, CONV_DIM), F32),
        jax.ShapeDtypeStruct((bsz, SSM_WIDTH, D_STATE), F32),
    )
    out_specs = (
        pl.BlockSpec((None, tm, D_MODEL), lambda b, s: (b, s, 0)),
        pl.BlockSpec((None, WINDOW, KV_WIDTH), lambda b, s: (b, 0, 0)),
        pl.BlockSpec((None, WINDOW, KV_WIDTH), lambda b, s: (b, 0, 0)),
        pl.BlockSpec((None, CONV_W - 1, CONV_DIM), lambda b, s: (b, 0, 0)),
        pl.BlockSpec((None, SSM_WIDTH, D_STATE), lambda b, s: (b, 0, 0)),
    )
    scratch = [
        pltpu.VMEM((tm, D_MODEL), BF16),
        pltpu.VMEM((tm, ATT_WIDTH), BF16),
        pltpu.VMEM((CHUNK + tm, KV_WIDTH), BF16),
        pltpu.VMEM((nchunk + 1, KV_WIDTH, CHUNK), BF16),
        pltpu.VMEM((SUBLANES + tm, CONV_DIM), F32),
        pltpu.VMEM((tm, CONV_DIM), F32),
        pltpu.VMEM((tm, LANES), F32),
        pltpu.VMEM((tm, SSM_WIDTH), F32),
        pltpu.VMEM((tm, MIX_WIDTH), BF16),
        pltpu.VMEM((D_STATE, SSM_WIDTH), F32),
        pltpu.VMEM((D_MODEL, ATT_WIDTH), BF16),
    ]
    return pl.pallas_call(
        _prompt_kernel,
        grid=(bsz, seq // tm),
        in_specs=in_specs,
        out_specs=out_specs,
        out_shape=out_shape,
        scratch_shapes=scratch,
        compiler_params=pltpu.CompilerParams(
            dimension_semantics=("arbitrary", "arbitrary"), vmem_limit_bytes=VMEM_LIMIT_BYTES),
        name="prompt_mixer",
    )(x, *consts, c, s1, s2, sink, bias, *tail)


def _sample_mixer(x, cache_k, cache_v, state_conv, state_ssm, p):
    nb, tdec, _ = x.shape
    bt = SAMPLE_BT
    rows = bt * tdec
    c, s1, s2 = _rope_tables(np.tile(PAST_LEN + np.arange(tdec), bt))
    sink = jnp.repeat(p["sinks_gk"].reshape(-1), tdec).reshape(rows, 1)
    consts = [p[n] for n in _WEIGHT_ORDER]
    tail = [p[n] for n in _TAIL_ORDER]
    x2 = x.reshape(nb * tdec, D_MODEL)
    ck = cache_k.reshape(nb, WINDOW, KV_WIDTH)
    cv = cache_v.reshape(nb, WINDOW, KV_WIDTH)
    ssm = state_ssm.reshape(nb, SSM_WIDTH, D_STATE)
    pb = SAMPLE_PB
    steps = bt // pb
    t_of_row = (np.arange(rows) % tdec)[:, None]
    col = np.arange(rows)[None, :]
    vis_c = col >= t_of_row
    bias = jnp.asarray(np.stack(
        [np.where(np.concatenate([vis_c, (col // tdec == b) & (col % tdec <= t_of_row)], axis=1), 0.0, NEG)
         for b in range(bt)]).astype(np.float32))
    tmap = lambda i, j: (i, 0, 0)
    pmap = lambda i, j: (i * steps + j, 0, 0)
    in_specs = ([pl.BlockSpec((rows, D_MODEL), lambda i, j: (i, 0)),
                 pl.BlockSpec((pb, WINDOW, KV_WIDTH), pmap),
                 pl.BlockSpec((pb, WINDOW, KV_WIDTH), pmap),
                 pl.BlockSpec((bt, CONV_W - 1, CONV_DIM), tmap),
                 pl.BlockSpec((pb, SSM_WIDTH, D_STATE), pmap)]
                + [_const_spec(a.shape) for a in consts]
                + [_const_spec(c.shape)] * 3
                + [_const_spec(sink.shape), _const_spec(bias.shape)]
                + [_const_spec(a.shape) for a in tail])
    out_shape = (
        jax.ShapeDtypeStruct((nb * tdec, D_MODEL), F32),
        jax.ShapeDtypeStruct((nb, WINDOW, KV_WIDTH), F32),
        jax.ShapeDtypeStruct((nb, WINDOW, KV_WIDTH), F32),
        jax.ShapeDtypeStruct((nb, CONV_W - 1, CONV_DIM), F32),
        jax.ShapeDtypeStruct((nb, SSM_WIDTH, D_STATE), F32),
    )
    out_specs = (
        pl.BlockSpec((rows, D_MODEL), lambda i, j: (i, 0)),
        pl.BlockSpec((pb, WINDOW, KV_WIDTH), pmap),
        pl.BlockSpec((pb, WINDOW, KV_WIDTH), pmap),
        pl.BlockSpec((bt, CONV_W - 1, CONV_DIM), tmap),
        pl.BlockSpec((pb, SSM_WIDTH, D_STATE), pmap),
    )
    scratch = [
        pltpu.VMEM((rows, D_MODEL), BF16),
        pltpu.VMEM((rows, ATT_WIDTH), F32),
        pltpu.VMEM((rows, KV_WIDTH), F32),
        pltpu.VMEM((rows, KV_WIDTH), F32),
        pltpu.VMEM((rows, KV_WIDTH), BF16),
        pltpu.VMEM((rows, KV_WIDTH), BF16),
        pltpu.VMEM((bt, 2 * SUBLANES, CONV_DIM), F32),
        pltpu.VMEM((rows, CONV_DIM), F32),
        pltpu.VMEM((rows, SSM_GROUPS * D_STATE), BF16),
        pltpu.VMEM((rows, LANES), F32),
        pltpu.VMEM((rows, SSM_WIDTH), F32),
        pltpu.VMEM((SSM_WIDTH, rows), BF16),
        pltpu.VMEM((rows, SSM_WIDTH), F32),
        pltpu.VMEM((rows, SSM_WIDTH), F32),
        pltpu.VMEM((rows, MIX_WIDTH), F32),
        pltpu.VMEM((D_MODEL, ATT_WIDTH), BF16),
    ]
    return pl.pallas_call(
        _sample_kernel,
        grid=(nb // bt, steps),
        in_specs=in_specs,
        out_specs=out_specs,
        out_shape=out_shape,
        scratch_shapes=scratch,
        compiler_params=pltpu.CompilerParams(
            dimension_semantics=("arbitrary", "arbitrary"), vmem_limit_bytes=VMEM_LIMIT_BYTES),
        name="sample_mixer",
    )(x2, ck, cv, state_conv, ssm, *consts, c, s1, s2, sink, bias, *tail)


def _ffn(xa, xb, gpre, wg, wu, wd, gpost):
    tf = FFN_TILE
    steps_a, steps_b = xa.shape[0] // tf, xb.shape[0] // tf
    consts = [gpre, wg, wu, wd, gpost]
    amap = lambda i: (jnp.minimum(i, steps_a - 1), 0)
    bmap = lambda i: (jnp.maximum(i - steps_a, 0), 0)
    return pl.pallas_call(
        functools.partial(_ffn_kernel, steps_a=steps_a),
        grid=(steps_a + steps_b,),
        in_specs=([pl.BlockSpec((tf, D_MODEL), amap), pl.BlockSpec((tf, D_MODEL), bmap)]
                  + [_const_spec(a.shape) for a in consts]),
        out_specs=(pl.BlockSpec((tf, D_MODEL), amap), pl.BlockSpec((tf, D_MODEL), bmap)),
        out_shape=(jax.ShapeDtypeStruct(xa.shape, F32), jax.ShapeDtypeStruct(xb.shape, F32)),
        scratch_shapes=[pltpu.VMEM((tf, D_MODEL), BF16)],
        compiler_params=pltpu.CompilerParams(
            dimension_semantics=("arbitrary",), vmem_limit_bytes=VMEM_LIMIT_BYTES),
        name="ffn",
    )(xa, xb, *consts)


def kernel(x_prompt, x_sample, cache_k_win, cache_v_win, state_conv, state_ssm, g_pre_mix, w_in, attn_sinks, conv_w, conv_b, dt_bias, a_log, d_skip, g_ssm_out, w_out, g_post_mix, g_pre_ffn, w_gate, w_up, w_down, g_post_ffn):
    depth = w_in.shape[0]
    bp, lp, _ = x_prompt.shape
    nb, ts, _ = x_sample.shape
    hp, hs = x_prompt, x_sample
    outs = [[] for _ in range(8)]
    for l in range(depth):
        p = _layer_params(g_pre_mix[l], w_in[l], attn_sinks[l], conv_w[l], conv_b[l], dt_bias[l], a_log[l],
                          d_skip[l], g_ssm_out[l], w_out[l], g_post_mix[l])
        ffn_w = (g_pre_ffn[l].reshape(1, D_MODEL), w_gate[l].astype(BF16), w_up[l].astype(BF16),
                 w_down[l].astype(BF16), g_post_ffn[l].reshape(1, D_MODEL))
        x1p, kp, vp, cp, sp = _prompt_mixer(hp, p)
        x1s, ksm, vsm, csm, ssm = _sample_mixer(hs, cache_k_win[l], cache_v_win[l], state_conv[l], state_ssm[l], p)
        hp, hs = _ffn(x1p.reshape(bp * lp, D_MODEL), x1s, *ffn_w)
        hp = hp.reshape(bp, lp, D_MODEL)
        hs = hs.reshape(nb, ts, D_MODEL)
        kv_shape = (WINDOW, ATT_KV_HEADS, HEAD_DIM)
        ssm_shape = (SSM_HEADS, SSM_HEAD_DIM, D_STATE)
        for lst, val in zip(outs, (kp.reshape((bp,) + kv_shape), vp.reshape((bp,) + kv_shape), cp,
                                   sp.reshape((bp,) + ssm_shape),
                                   ksm.reshape((nb,) + kv_shape), vsm.reshape((nb,) + kv_shape), csm,
                                   ssm.reshape((nb,) + ssm_shape))):
            lst.append(val)
    return (hp, hs) + tuple(jnp.stack(o) for o in outs)
```

```python
import functools
import math

import numpy as np
import jax
import jax.numpy as jnp
from jax import lax
from jax.experimental import pallas as pl
from jax.experimental.pallas import tpu as pltpu

F32 = jnp.float32
BF16 = jnp.bfloat16

D_MODEL = 1024
ATT_HEADS = 16
ATT_KV_HEADS = 4
ATT_GROUP = ATT_HEADS // ATT_KV_HEADS
HEAD_DIM = 64
ATT_WIDTH = ATT_HEADS * HEAD_DIM
KV_WIDTH = ATT_KV_HEADS * HEAD_DIM
WINDOW = 128
ROT_DIM = HEAD_DIM // 4
ROPE_THETA = 500000.0
SSM_HEADS = 16
SSM_HEAD_DIM = 64
SSM_WIDTH = SSM_HEADS * SSM_HEAD_DIM
SSM_GROUPS = 2
SSM_HPG = SSM_HEADS // SSM_GROUPS
SSM_GROUP_W = SSM_WIDTH // SSM_GROUPS
D_STATE = 128
CONV_W = 4
CONV_DIM = SSM_WIDTH + 2 * SSM_GROUPS * D_STATE
MIX_WIDTH = ATT_WIDTH + SSM_WIDTH
EPS = 1e-6
PAST_LEN = 8192

LANES = 128
SUBLANES = 8
VMEM_LIMIT_BYTES = 60 * 1024 * 1024

CHUNK = 128
NEG = -1e30
LOG2E = math.log2(math.e)
Q_SCALE = HEAD_DIM ** -0.5 * LOG2E
SEQ_TILE = 512
SAMPLE_BT = 16
SAMPLE_PB = 4
FFN_TILE = 512
PROJ_ROWS = 256
ROW_BLOCK = 32
FF_CHUNK = 256


def _nn(a, b):
    return jnp.dot(a, b, preferred_element_type=F32)


def _nt(a, b):
    return lax.dot_general(a, b, (((1,), (1,)), ((), ())), preferred_element_type=F32)


def _split_bf16(x, n):
    parts = []
    r = x
    for i in range(n):
        p = r.astype(BF16)
        parts.append(p)
        if i + 1 < n:
            r = r - p.astype(F32)
    return parts


def _expand_heads(x, expand2_ref):
    hi, mid = _split_bf16(x, 2)
    return _nn(jnp.concatenate([hi, mid], axis=1), expand2_ref[...])


def _cumsum_cols(m01, x):
    w = x.shape[1]
    r = _nn(m01, jnp.concatenate(_split_bf16(x, 3), axis=1))
    return r[:, :w] + r[:, w:2 * w] + r[:, 2 * w:]


def _heads_to_rows(x):
    return x.T[:SSM_HEADS]


def _rms(x, g):
    ms = jnp.mean(x * x, axis=-1, keepdims=True)
    return x * lax.rsqrt(ms + EPS) * g


def _rms_rows(src, g_ref, dst_ref, res_ref=None):
    g = g_ref[...]
    for r0 in range(0, dst_ref.shape[0], ROW_BLOCK):
        rs = slice(r0, r0 + ROW_BLOCK)
        y = _rms(src[rs, :], g)
        if res_ref is not None:
            y = res_ref[rs, :] + y
        dst_ref[rs, :] = y.astype(dst_ref.dtype)


def _silu(x):
    h = 0.5 * x
    return h + h * jnp.tanh(h)


R_K = ATT_WIDTH
R_V = R_K + KV_WIDTH
R_Z = R_V + KV_WIDTH
R_XBC = R_Z + SSM_WIDTH
R_END = R_XBC + CONV_DIM

def _conv4(x, w, bias):
    assert CONV_W == 4
    u = pltpu.roll(x, 2, 0)
    return bias + (x * w[3] + u * w[1]) + pltpu.roll(x * w[2] + u * w[0], 1, 0)


CONV_CB = 256
CONV_RB = 64


def _conv_silu_cols(xpad_ref, row0, ra, rb, convw_ref, convb_ref, out_ref, c0):
    cs = slice(c0, c0 + CONV_CB)
    w = [convw_ref[i:i + 1, cs] for i in range(CONV_W)]
    bias = convb_ref[:, cs]
    for r0 in range(ra, rb, CONV_RB):
        xh = xpad_ref[row0 + r0 - SUBLANES:row0 + r0 + CONV_RB, cs]
        out_ref[r0:r0 + CONV_RB, cs] = _silu(_conv4(xh, w, bias)[SUBLANES:])


def _softplus(x):
    return jnp.maximum(x, 0.0) + jnp.log1p(jnp.exp(-jnp.abs(x)))


def _rope(x, c, s):
    lane = lax.broadcasted_iota(jnp.int32, (1, LANES), 1)
    first = (lane % HEAD_DIM) < ROT_DIM // 2
    outs = []
    for j in range(x.shape[1] // LANES):
        xb = x[:, j * LANES:(j + 1) * LANES]
        partner = jnp.where(first, pltpu.roll(xb, LANES - ROT_DIM // 2, 1), pltpu.roll(xb, ROT_DIM // 2, 1))
        outs.append(xb * c + partner * s)
    return outs[0] if len(outs) == 1 else jnp.concatenate(outs, axis=1)


def _iota(shape, dim):
    return lax.broadcasted_iota(jnp.int32, shape, dim)


def _head_blocks(c_out):
    out = []
    for half in range(2):
        g, kvh = divmod(2 * c_out + half, ATT_KV_HEADS)
        b_in = kvh * ATT_GROUP + g
        out.append((b_in // 2, b_in % 2))
    return out


def _permute_q_weight(wr_ref, wq_s):
    low = _iota((1, LANES), 1) < HEAD_DIM
    for c_out in range(ATT_WIDTH // LANES):
        halves = []
        for half, (c_in, src_half) in enumerate(_head_blocks(c_out)):
            col = wr_ref[:, c_in * LANES:(c_in + 1) * LANES]
            halves.append(col if src_half == half else pltpu.roll(col, HEAD_DIM, 1))
        wq_s[:, c_out * LANES:(c_out + 1) * LANES] = jnp.where(low, halves[0], halves[1])


def _project(hn, wq_ref, wr_ref, ropec, ropes):
    q = _rope(_nn(hn, wq_ref[...]), ropec, ropes) * (HEAD_DIM ** -0.5)
    kv = _nn(hn, wr_ref[:, R_K:R_Z])
    k = _rope(kv[:, :KV_WIDTH], ropec, ropes)
    v = kv[:, KV_WIDTH:]
    return q, k, v


def _dt_cols(hn, wdt_ref, dtb_row_ref):
    return _softplus(_nn(hn, wdt_ref[...]) + dtb_row_ref[...])


def _a_row(alog_row_ref):
    lane = _iota((1, LANES), 1)
    return jnp.where(lane < SSM_HEADS, -jnp.exp(alog_row_ref[...]), 0.0)


def _log2_decay(acol, arow, dtr):
    return acol * LOG2E, (arow - jnp.log(dtr)) * LOG2E


SSD_QUAD = 4
SSD_NQUAD = SSM_HEADS // SSD_QUAD


def _ssd_cb(b_all, c_all):
    return [_nt(c_all[:, g * D_STATE:(g + 1) * D_STATE].astype(BF16),
                b_all[:, g * D_STATE:(g + 1) * D_STATE].astype(BF16)) for g in range(SSM_GROUPS)]


def _ssd_quad(qi, xs_bf, c_all, cbs, acol2, arow2, mask_bool, hT_bf=None):
    lane4 = _iota((1, SSD_QUAD * SSM_HEAD_DIM), 1) // SSM_HEAD_DIM
    zero = jnp.zeros((), BF16)
    e0 = SSD_QUAD * qi
    g = e0 // SSM_HPG
    cf = c_all[:, g * D_STATE:(g + 1) * D_STATE]
    lanes = slice(e0 * SSM_HEAD_DIM, (e0 + SSD_QUAD) * SSM_HEAD_DIM)
    xq = xs_bf[:, lanes]
    lhs, rhs = [], []
    for i in range(SSD_QUAD):
        e = e0 + i
        a_t = jnp.broadcast_to(acol2[:, e:e + 1], (CHUNK, CHUNK))
        w = cbs[g] * jnp.exp2(jnp.where(mask_bool, a_t - arow2[e:e + 1, :], NEG))
        lhs.append(w.astype(BF16))
        rhs.append(jnp.where(lane4 == i, xq, zero))
        if hT_bf is not None:
            lhs.append((cf * jnp.exp2(a_t)).astype(BF16))
            rhs.append(jnp.where(lane4 == i, hT_bf[:, lanes], zero))
    return _nn(jnp.concatenate(lhs, axis=1), jnp.concatenate(rhs, axis=0))


def _ssd_block(xs_bf, b_all, c_all, acol2, arow2, mask_bool):
    cbs = _ssd_cb(b_all, c_all)
    return jnp.concatenate([_ssd_quad(qi, xs_bf, c_all, cbs, acol2, arow2, mask_bool)
                            for qi in range(SSD_NQUAD)], axis=1)


def _gate_and_out(y_refs, xc_s, z, x_ref, dskip_ref, gssm_ref, wout_ref, gpost_ref, mix_s, out_ref):
    dskip = dskip_ref[...]
    for r0 in range(0, out_ref.shape[0], ROW_BLOCK):
        rs = slice(r0, r0 + ROW_BLOCK)
        y = y_refs[0][rs, :]
        for extra in y_refs[1:]:
            y = y + extra[rs, :]
        gated = (y + dskip * xc_s[rs, 0:SSM_WIDTH]) * _silu(z[rs, :])
        for g in range(SSM_GROUPS):
            gs = slice(g * SSM_GROUP_W, (g + 1) * SSM_GROUP_W)
            gg = gated[:, gs]
            ms = jnp.mean(gg * gg, axis=-1, keepdims=True)
            o = gg * lax.rsqrt(ms + EPS) * gssm_ref[:, gs]
            mix_s[rs, ATT_WIDTH + g * SSM_GROUP_W:ATT_WIDTH + (g + 1) * SSM_GROUP_W] = o.astype(mix_s.dtype)
    mo = _nn(mix_s[...].astype(BF16), wout_ref[...])
    _rms_rows(mo, gpost_ref, out_ref, res_ref=x_ref)


def _prompt_kernel(x_ref, gpre_ref, wr_ref, wdt_ref,
                   ropec_ref, ropes_ref, sink_ref, biasT_ref,
                   convw_ref, convb_ref, dtb_row_ref, alog_row_ref,
                   dskip_ref, gssm_ref, wout_ref, gpost_ref, expand2_ref,
                   x1_ref, nk_ref, nv_ref, nconv_ref, nssm_ref,
                   hn_s, q_s, kbuf, vT_s, xbc_s, xc_s, dtc_s, y_s, mix_s, hT_s, wq_s):
    tm = x_ref.shape[0]
    nchunk = tm // CHUNK
    s = pl.program_id(1)
    last = pl.num_programs(1) - 1

    @pl.when(jnp.logical_and(pl.program_id(0) == 0, s == 0))
    def _():
        _permute_q_weight(wr_ref, wq_s)

    @pl.when(s == 0)
    def _():
        kbuf[0:CHUNK, :] = jnp.zeros((CHUNK, KV_WIDTH), BF16)
        vT_s[0] = jnp.zeros((KV_WIDTH, CHUNK), BF16)
        xbc_s[0:SUBLANES, :] = jnp.zeros((SUBLANES, CONV_DIM), F32)
        hT_s[...] = jnp.zeros_like(hT_s)

    _rms_rows(x_ref, gpre_ref, hn_s)

    def projection_pieces(ra, rb):
        rows = slice(ra, rb)

        def rope(x):
            return _rope(x, ropec_ref[rows, :], ropes_ref[rows, :])

        def proj_xbc(c0):
            xbc_s[SUBLANES + ra:SUBLANES + rb, c0:c0 + CONV_CB] = _nn(
                hn_s[rows, :], wr_ref[:, R_XBC + c0:R_XBC + c0 + CONV_CB])

        def conv(c0):
            _conv_silu_cols(xbc_s, SUBLANES, ra, rb, convw_ref, convb_ref, xc_s, c0)

        def proj_q(c0):
            q_s[rows, c0:c0 + KV_WIDTH] = (
                rope(_nn(hn_s[rows, :], wq_s[:, c0:c0 + KV_WIDTH])) * Q_SCALE).astype(BF16)

        def proj_k():
            kbuf[CHUNK + ra:CHUNK + rb, :] = rope(_nn(hn_s[rows, :], wr_ref[:, R_K:R_V])).astype(BF16)

        def proj_v_dt():
            v = _nn(hn_s[rows, :], wr_ref[:, R_V:R_Z])
            for j in range(ra // CHUNK, rb // CHUNK):
                vT_s[1 + j] = v[j * CHUNK - ra:(j + 1) * CHUNK - ra, :].T.astype(BF16)
            dtc_s[rows, :] = _dt_cols(hn_s[rows, :], wdt_ref, dtb_row_ref)

        others = [functools.partial(proj_q, c0) for c0 in range(0, ATT_WIDTH, KV_WIDTH)] + [proj_k, proj_v_dt]
        conv_cols = list(range(0, CONV_DIM, CONV_CB))
        pieces = [functools.partial(proj_xbc, conv_cols[0])]
        for n, c0 in enumerate(conv_cols):
            if n + 1 < len(conv_cols):
                pieces.append(functools.partial(proj_xbc, conv_cols[n + 1]))
            pieces.append(functools.partial(conv, c0))
            if n < len(others):
                pieces.append(others[n])
        return pieces + others[len(conv_cols):]

    a_row = _a_row(alog_row_ref)

    r2 = _iota((CHUNK, CHUNK), 0)
    c2 = _iota((CHUNK, CHUNK), 1)
    tril = c2 <= r2
    tril_bf = tril.astype(BF16)
    lane_kv = _iota((1, KV_WIDTH), 1) // HEAD_DIM
    ones_rows = jnp.ones((2 * SUBLANES, 2 * CHUNK), BF16)

    def chunk_body(c, extras):
        r0 = c * CHUNK
        per = -(-len(extras) // ATT_KV_HEADS)
        first = jnp.logical_and(s == 0, c == 0).astype(jnp.int32)
        bias = biasT_ref[first]
        qcat = jnp.concatenate([q_s[pl.ds(r0, CHUNK), g * KV_WIDTH:(g + 1) * KV_WIDTH]
                                for g in range(ATT_GROUP)], axis=0)
        kwin = kbuf[pl.ds(r0, 2 * CHUNK), :]
        kstack = jnp.concatenate([jnp.where(lane_kv == kvh, kwin, jnp.zeros((), BF16))
                                  for kvh in range(ATT_KV_HEADS)], axis=0)
        sT = _nt(kstack, qcat)
        vT_win = jnp.concatenate([vT_s[c], vT_s[c + 1]], axis=1)

        dtc_c = dtc_s[pl.ds(r0, CHUNK), :]
        dtr_c = _heads_to_rows(dtc_c)
        acol = _cumsum_cols(tril_bf, dtc_c * a_row)
        arow = _heads_to_rows(acol)
        a_end = acol[CHUNK - 1:CHUNK, :]
        tailc = jnp.exp(a_end - acol) * dtc_c
        ex = _expand_heads(
            jnp.concatenate([tailc, jnp.broadcast_to(jnp.exp(a_end), (SUBLANES, LANES))], axis=0), expand2_ref)
        tlx = ex[:CHUNK]
        dec_row = ex[CHUNK:CHUNK + 1]
        xs = xc_s[pl.ds(r0, CHUNK), 0:SSM_WIDTH]
        b_all = xc_s[pl.ds(r0, CHUNK), SSM_WIDTH:SSM_WIDTH + SSM_GROUPS * D_STATE]
        c_all = xc_s[pl.ds(r0, CHUNK), SSM_WIDTH + SSM_GROUPS * D_STATE:CONV_DIM]
        hT = hT_s[...]
        acol2, arow2 = _log2_decay(acol, arow, dtr_c)
        xs_bf = xs.astype(BF16)
        hT_bf = hT.astype(BF16)
        cbs = _ssd_cb(b_all, c_all)

        o_rows = []
        for i in range(ATT_KV_HEADS):
            blk = sT[i * 2 * CHUNK:(i + 1) * 2 * CHUNK] + bias
            sink = sink_ref[i] * LOG2E
            m = jnp.maximum(jnp.max(blk, axis=0, keepdims=True), sink)
            p = jnp.exp2(blk - m).astype(BF16)
            lhs = jnp.concatenate([vT_win[i * HEAD_DIM:(i + 1) * HEAD_DIM], ones_rows], axis=0)
            oT = _nn(lhs, p)
            den = oT[HEAD_DIM:HEAD_DIM + 1] + jnp.exp2(sink - m)
            o_rows.append(oT[:HEAD_DIM] * (1.0 / den))
            for qi in range(i * SSD_NQUAD // ATT_KV_HEADS, (i + 1) * SSD_NQUAD // ATT_KV_HEADS):
                lanes = slice(qi * SSD_QUAD * SSM_HEAD_DIM, (qi + 1) * SSD_QUAD * SSM_HEAD_DIM)
                y_s[pl.ds(r0, CHUNK), lanes] = _ssd_quad(qi, xs_bf, c_all, cbs, acol2, arow2, tril, hT_bf)
            for t in extras[i * per:(i + 1) * per]:
                t()
        for c_out in range(ATT_WIDTH // LANES):
            kvh, g0 = divmod(2 * c_out, ATT_GROUP)
            two = jnp.concatenate([o_rows[kvh][:, g * CHUNK:(g + 1) * CHUNK] for g in (g0, g0 + 1)], axis=0)
            mix_s[pl.ds(r0, CHUNK), c_out * LANES:(c_out + 1) * LANES] = two.T.astype(BF16)

        xtl = (xs * tlx).astype(BF16)
        for g in range(SSM_GROUPS):
            sl = slice(g * SSM_GROUP_W, (g + 1) * SSM_GROUP_W)
            bt = b_all[:, g * D_STATE:(g + 1) * D_STATE].T.astype(BF16)
            hT_s[:, sl] = hT[:, sl] * dec_row[:, sl] + _nn(bt, xtl[:, sl])

    groups = list(range(0, tm, PROJ_ROWS))
    cpg = PROJ_ROWS // CHUNK
    for t in projection_pieces(0, PROJ_ROWS):
        t()
    for gi, ra in enumerate(groups):
        nxt = projection_pieces(ra + PROJ_ROWS, ra + 2 * PROJ_ROWS) if gi + 1 < len(groups) else []
        share = -(-len(nxt) // cpg)
        for k in range(cpg):
            chunk_body(ra // CHUNK + k, nxt[k * share:(k + 1) * share])

    kbuf[0:CHUNK, :] = kbuf[tm:tm + CHUNK, :]
    vT_s[0] = vT_s[nchunk]
    xbc_s[0:SUBLANES, :] = xbc_s[tm:tm + SUBLANES, :]

    z = _nn(hn_s[...], wr_ref[:, R_Z:R_XBC])
    _gate_and_out((y_s,), xc_s, z, x_ref, dskip_ref, gssm_ref, wout_ref, gpost_ref, mix_s, x1_ref)

    @pl.when(s == last)
    def _():
        hn_w = hn_s[tm - WINDOW:, :]
        nk_ref[...] = _rope(_nn(hn_w, wr_ref[:, R_K:R_V]), ropec_ref[tm - WINDOW:, :], ropes_ref[tm - WINDOW:, :])
        nv_ref[...] = _nn(hn_w, wr_ref[:, R_V:R_Z])
        nconv_ref[...] = xbc_s[SUBLANES - (CONV_W - 1):SUBLANES, :]
        nssm_ref[...] = hT_s[...].T


def _sample_kernel(x_ref, ck_ref, cv_ref, sconv_ref, sssm_ref,
                   gpre_ref, wr_ref, wdt_ref,
                   ropec_ref, ropes_ref, sink_ref, bias_ref,
                   convw_ref, convb_ref, dtb_row_ref, alog_row_ref,
                   dskip_ref, gssm_ref, wout_ref, gpost_ref, expand2_ref,
                   x1_ref, nk_ref, nv_ref, nconv_ref, nssm_ref,
                   hn_s, q_s, kn_s, vn_s, knb_s, vnb_s, xpad_s, xc_s, bb_s, ea_s, eax_s, xT_s, y_s, yoff_s, mix_s, wq_s):
    bt_n = sconv_ref.shape[0]
    pb_n = ck_ref.shape[0]
    m_rows = x_ref.shape[0]
    tdec = m_rows // bt_n
    j = pl.program_id(1)

    @pl.when(jnp.logical_and(pl.program_id(0) == 0, j == 0))
    def _():
        _permute_q_weight(wr_ref, wq_s)

    @pl.when(j == 0)
    def _():
        _rms_rows(x_ref, gpre_ref, hn_s)
        hn = hn_s[...]
        q, k, v = _project(hn, wq_s, wr_ref, ropec_ref[...], ropes_ref[...])
        q_s[...] = q
        kn_s[...] = k
        vn_s[...] = v
        knb_s[...] = k.astype(BF16)
        vnb_s[...] = v.astype(BF16)

        xbc = _nn(hn, wr_ref[:, R_XBC:R_END])
        xpad_s[:, 0:SUBLANES - 3, :] = jnp.zeros((bt_n, SUBLANES - 3, CONV_DIM), F32)
        xpad_s[:, SUBLANES - 3:SUBLANES, :] = sconv_ref[...]
        xpad_s[:, SUBLANES:2 * SUBLANES, :] = xbc.reshape(bt_n, tdec, CONV_DIM)
        nconv_ref[...] = xpad_s[:, 2 * SUBLANES - 3:2 * SUBLANES, :]
        for c0 in range(0, CONV_DIM, CONV_CB):
            cs = slice(c0, c0 + CONV_CB)
            xh = xpad_s[:, :, cs].reshape(bt_n * 2 * SUBLANES, CONV_CB)
            acc = _conv4(xh, [convw_ref[i:i + 1, cs] for i in range(CONV_W)], convb_ref[:, cs])
            xc_s[:, cs] = _silu(acc.reshape(bt_n, 2 * SUBLANES, CONV_CB)[:, SUBLANES:, :].reshape(m_rows, CONV_CB))
        xs = xc_s[:, 0:SSM_WIDTH]
        b_all = xc_s[:, SSM_WIDTH:SSM_WIDTH + SSM_GROUPS * D_STATE]
        c_all = xc_s[:, SSM_WIDTH + SSM_GROUPS * D_STATE:CONV_DIM]
        bb_s[...] = b_all.astype(BF16)

        dtc = _dt_cols(hn, wdt_ref, dtb_row_ref)
        dtr = _heads_to_rows(dtc)
        a_row = _a_row(alog_row_ref)

        r2 = _iota((m_rows, m_rows), 0)
        c2 = _iota((m_rows, m_rows), 1)
        same = (r2 // tdec) == (c2 // tdec)
        causal = jnp.logical_and(same, c2 <= r2)
        causal_bf = causal.astype(BF16)
        same_bf = same.astype(BF16)

        dac = dtc * a_row
        acol = _cumsum_cols(causal_bf, dac)
        alast = _cumsum_cols(same_bf, dac)
        arow = _heads_to_rows(acol)
        tailc = jnp.exp(alast - acol) * dtc
        ex = _expand_heads(jnp.concatenate([jnp.exp(acol), tailc], axis=0), expand2_ref)
        ea_s[...] = jnp.exp(alast)
        eax_s[...] = ex[:m_rows]
        acol2, arow2 = _log2_decay(acol, arow, dtr)
        y_s[...] = _ssd_block(xs.astype(BF16), b_all, c_all, acol2, arow2, causal)
        xtl = xs * ex[m_rows:]
        for jj in range(SSM_WIDTH // LANES):
            xT_s[jj * LANES:(jj + 1) * LANES, :] = xtl[:, jj * LANES:(jj + 1) * LANES].T.astype(BF16)

    lane_kv = _iota((1, KV_WIDTH), 1) // HEAD_DIM
    row_b = _iota((m_rows, 1), 0) // tdec
    low_half = _iota((1, LANES), 1) < HEAD_DIM
    sink = sink_ref[...]

    pbs = range(pb_n)
    bs = [j * pb_n + pb for pb in pbs]
    rs = [pl.multiple_of(b * tdec, tdec) for b in bs]

    qbd, kc, vc, sc = [], [], [], []
    for pb in pbs:
        q8 = q_s[pl.ds(rs[pb], tdec), :]
        qbd.append(jnp.concatenate(
            [jnp.where(lane_kv == kvh, q8[:, g * KV_WIDTH:(g + 1) * KV_WIDTH], 0.0)
             for g in range(ATT_GROUP) for kvh in range(ATT_KV_HEADS)], axis=0).astype(BF16))
        kc.append(ck_ref[pb])
        vc.append(cv_ref[pb])
    for pb in pbs:
        keys = jnp.concatenate([kc[pb].astype(BF16), knb_s[...]], axis=0)
        sc.append(_nt(qbd[pb], keys) + bias_ref[bs[pb]])
    p, inv = [], []
    for pb in pbs:
        m = jnp.maximum(jnp.max(sc[pb], axis=1, keepdims=True), sink)
        e = jnp.exp(sc[pb] - m)
        inv.append(1.0 / (jnp.sum(e, axis=1, keepdims=True) + jnp.exp(sink - m)))
        p.append(e.astype(BF16))
    for pb in pbs:
        o = _nn(p[pb], jnp.concatenate([vc[pb].astype(BF16), vnb_s[...]], axis=0)) * inv[pb]
        for c_out in range(ATT_WIDTH // LANES):
            kvh, g0 = divmod(2 * c_out, ATT_GROUP)
            halves = []
            for half in range(2):
                i0 = ((g0 + half) * ATT_KV_HEADS + kvh) * tdec
                piece = o[i0:i0 + tdec, (kvh // 2) * LANES:(kvh // 2 + 1) * LANES]
                halves.append(piece if kvh % 2 == half else pltpu.roll(piece, HEAD_DIM, 1))
            mix_s[pl.ds(rs[pb], tdec), c_out * LANES:(c_out + 1) * LANES] = jnp.where(low_half, halves[0], halves[1])
        nk_ref[pb, 0:WINDOW - tdec, :] = kc[pb][tdec:, :]
        nk_ref[pb, WINDOW - tdec:WINDOW, :] = kn_s[pl.ds(rs[pb], tdec), :]
        nv_ref[pb, 0:WINDOW - tdec, :] = vc[pb][tdec:, :]
        nv_ref[pb, WINDOW - tdec:WINDOW, :] = vn_s[pl.ds(rs[pb], tdec), :]

    for g in range(SSM_GROUPS):
        sl = slice(g * SSM_GROUP_W, (g + 1) * SSM_GROUP_W)
        c0 = SSM_WIDTH + (SSM_GROUPS + g) * D_STATE
        hg, upd = [], []
        for pb in pbs:
            hg.append(sssm_ref[pb, sl, :])
            bm = jnp.where(row_b == bs[pb], bb_s[:, g * D_STATE:(g + 1) * D_STATE], jnp.zeros((), BF16))
            upd.append(_nn(xT_s[sl, :], bm))
        for pb in pbs:
            cc = xc_s[pl.ds(rs[pb], tdec), c0:c0 + D_STATE].astype(BF16)
            yoff_s[pl.ds(rs[pb], tdec), sl] = (_nt(cc, hg[pb].astype(BF16))
                                               * eax_s[pl.ds(rs[pb], tdec), sl])
        for pb in pbs:
            ea_b = ea_s[pl.ds(rs[pb], 1), :]
            dec = jnp.concatenate(
                [jnp.broadcast_to(ea_b[:, g * SSM_HPG + e:g * SSM_HPG + e + 1], (SSM_HEAD_DIM, D_STATE))
                 for e in range(SSM_HPG)], axis=0)
            nssm_ref[pb, sl, :] = hg[pb] * dec + upd[pb]

    @pl.when(j == pl.num_programs(1) - 1)
    def _():
        z = _nn(hn_s[...], wr_ref[:, R_Z:R_XBC])
        _gate_and_out((y_s, yoff_s), xc_s, z, x_ref, dskip_ref, gssm_ref, wout_ref, gpost_ref, mix_s, x1_ref)


def _ffn_kernel(xa_ref, xb_ref, gpre_ref, wg_ref, wu_ref, wd_ref, gpost_ref, oa_ref, ob_ref, f_s, *, steps_a):
    def tile(x_ref, o_ref):
        _rms_rows(x_ref, gpre_ref, f_s)
        f = f_s[...]
        d_ff = wg_ref.shape[1]
        acc = jnp.zeros(x_ref.shape, F32)
        for j in range(d_ff // FF_CHUNK):
            sl = slice(j * FF_CHUNK, (j + 1) * FF_CHUNK)
            gate = _nn(f, wg_ref[:, sl])
            up = _nn(f, wu_ref[:, sl])
            acc = acc + _nn((_silu(gate) * up).astype(BF16), wd_ref[sl, :])
        _rms_rows(acc, gpost_ref, o_ref, res_ref=x_ref)

    i = pl.program_id(0)

    @pl.when(i < steps_a)
    def _():
        tile(xa_ref, oa_ref)

    @pl.when(i >= steps_a)
    def _():
        tile(xb_ref, ob_ref)


def _const_spec(shape):
    nd = len(shape)
    return pl.BlockSpec(shape, lambda *_: (0,) * nd, pipeline_mode=pl.Buffered(1))


def _rope_tables(pos):
    half = ROT_DIM // 2
    inv = ROPE_THETA ** (-np.arange(half, dtype=np.float64) * 2.0 / ROT_DIM)
    ang = pos.astype(np.float64)[:, None] * inv[None, :]
    cos = np.cos(ang).astype(np.float32)
    sin = np.sin(ang).astype(np.float32)
    n = pos.shape[0]
    pad = HEAD_DIM - ROT_DIM
    c = np.concatenate([cos, cos, np.ones((n, pad), np.float32)], axis=1)
    s = np.concatenate([-sin, sin, np.zeros((n, pad), np.float32)], axis=1)
    rep = LANES // HEAD_DIM
    return tuple(jnp.asarray(np.tile(t, (1, rep))) for t in (c, s))


def _layer_params(g_pre_mix, w_in, attn_sinks, conv_w, conv_b, dt_bias, a_log, d_skip, g_ssm_out, w_out, g_post_mix):
    wr = w_in.astype(BF16)
    wdt = jnp.pad(w_in[:, R_END:], ((0, 0), (0, LANES - SSM_HEADS))).astype(BF16)
    wout = w_out.astype(BF16)
    pad16 = ((0, 0), (0, LANES - SSM_HEADS))
    expand = (np.arange(LANES)[:, None] == (np.arange(SSM_WIDTH)[None, :] // SSM_HEAD_DIM)).astype(np.float32)
    return dict(
        gpre=g_pre_mix.reshape(1, D_MODEL), wr=wr, wdt=wdt,
        convw=conv_w, convb=conv_b.reshape(1, CONV_DIM),
        dtb_row=jnp.pad(dt_bias.reshape(1, SSM_HEADS), pad16),
        alog_row=jnp.pad(a_log.reshape(1, SSM_HEADS), pad16),
        dskip=jnp.repeat(d_skip, SSM_HEAD_DIM).reshape(1, SSM_WIDTH), gssm=g_ssm_out.reshape(1, SSM_WIDTH),
        wout=wout, gpost=g_post_mix.reshape(1, D_MODEL), expand2=jnp.asarray(np.concatenate([expand, expand], axis=0), BF16),
        sinks_gk=attn_sinks.reshape(ATT_KV_HEADS, ATT_GROUP).T,
    )


_WEIGHT_ORDER = ("gpre", "wr", "wdt")
_TAIL_ORDER = ("convw", "convb", "dtb_row", "alog_row",
               "dskip", "gssm", "wout", "gpost", "expand2")


def _prompt_mixer(x, p):
    bsz, seq, _ = x.shape
    tm = SEQ_TILE
    nchunk = tm // CHUNK
    c, sn = _rope_tables(np.arange(seq))
    sink = jnp.repeat(p["sinks_gk"].T, CHUNK, axis=1)[:, None, :]
    jj = np.arange(2 * CHUNK)[:, None]
    tt = np.arange(CHUNK)[None, :]
    vis = (jj >= tt) & (jj <= tt + WINDOW)
    bias = np.stack([np.where(vis, 0.0, NEG), np.where(vis & (jj >= CHUNK), 0.0, NEG)]).astype(np.float32)
    bias = jnp.asarray(np.tile(bias, (1, 1, ATT_GROUP)))
    consts = [p[n] for n in _WEIGHT_ORDER]
    tail = [p[n] for n in _TAIL_ORDER]
    in_specs = ([pl.BlockSpec((None, tm, D_MODEL), lambda b, s: (b, s, 0))]
                + [_const_spec(a.shape) for a in consts]
                + [pl.BlockSpec((tm, LANES), lambda b, s: (s, 0))] * 2
                + [_const_spec(sink.shape), _const_spec(bias.shape)]
                + [_const_spec(a.shape) for a in tail])
    out_shape = (
        jax.ShapeDtypeStruct((bsz, seq, D_MODEL), F32),
        jax.ShapeDtypeStruct((bsz, WINDOW, KV_WIDTH), F32),
        jax.ShapeDtypeStruct((bsz, WINDOW, KV_WIDTH), F32),
        jax.ShapeDtypeStruct((bsz, CONV_W - 1, CONV_DIM), F32),
        jax.ShapeDtypeStruct((bsz, SSM_WIDTH, D_STATE), F32),
    )
    out_specs = (
        pl.BlockSpec((None, tm, D_MODEL), lambda b, s: (b, s, 0)),
        pl.BlockSpec((None, WINDOW, KV_WIDTH), lambda b, s: (b, 0, 0)),
        pl.BlockSpec((None, WINDOW, KV_WIDTH), lambda b, s: (b, 0, 0)),
        pl.BlockSpec((None, CONV_W - 1, CONV_DIM), lambda b, s: (b, 0, 0)),
        pl.BlockSpec((None, SSM_WIDTH, D_STATE), lambda b, s: (b, 0, 0)),
    )
    scratch = [
        pltpu.VMEM((tm, D_MODEL), BF16),
        pltpu.VMEM((tm, ATT_WIDTH), BF16),
        pltpu.VMEM((CHUNK + tm, KV_WIDTH), BF16),
        pltpu.VMEM((nchunk + 1, KV_WIDTH, CHUNK), BF16),
        pltpu.VMEM((SUBLANES + tm, CONV_DIM), F32),
        pltpu.VMEM((tm, CONV_DIM), F32),
        pltpu.VMEM((tm, LANES), F32),
        pltpu.VMEM((tm, SSM_WIDTH), F32),
        pltpu.VMEM((tm, MIX_WIDTH), BF16),
        pltpu.VMEM((D_STATE, SSM_WIDTH), F32),
        pltpu.VMEM((D_MODEL, ATT_WIDTH), BF16),
    ]
    return pl.pallas_call(
        _prompt_kernel,
        grid=(bsz, seq // tm),
        in_specs=in_specs,
        out_specs=out_specs,
        out_shape=out_shape,
        scratch_shapes=scratch,
        compiler_params=pltpu.CompilerParams(
            dimension_semantics=("arbitrary", "arbitrary"), vmem_limit_bytes=VMEM_LIMIT_BYTES),
        name="prompt_mixer",
    )(x, *consts, c, sn, sink, bias, *tail)


def _sample_mixer(x, cache_k, cache_v, state_conv, state_ssm, p):
    nb, tdec, _ = x.shape
    bt = SAMPLE_BT
    rows = bt * tdec
    c, sn = _rope_tables(np.tile(PAST_LEN + np.arange(tdec), bt))
    sink = jnp.repeat(p["sinks_gk"].reshape(-1), tdec).reshape(rows, 1)
    consts = [p[n] for n in _WEIGHT_ORDER]
    tail = [p[n] for n in _TAIL_ORDER]
    x2 = x.reshape(nb * tdec, D_MODEL)
    ck = cache_k.reshape(nb, WINDOW, KV_WIDTH)
    cv = cache_v.reshape(nb, WINDOW, KV_WIDTH)
    ssm = state_ssm.reshape(nb, SSM_WIDTH, D_STATE)
    pb = SAMPLE_PB
    steps = bt // pb
    t_of_row = (np.arange(rows) % tdec)[:, None]
    col = np.arange(rows)[None, :]
    vis_c = col >= t_of_row
    bias = jnp.asarray(np.stack(
        [np.where(np.concatenate([vis_c, (col // tdec == b) & (col % tdec <= t_of_row)], axis=1), 0.0, NEG)
         for b in range(bt)]).astype(np.float32))
    tmap = lambda i, j: (i, 0, 0)
    pmap = lambda i, j: (i * steps + j, 0, 0)
    in_specs = ([pl.BlockSpec((rows, D_MODEL), lambda i, j: (i, 0)),
                 pl.BlockSpec((pb, WINDOW, KV_WIDTH), pmap),
                 pl.BlockSpec((pb, WINDOW, KV_WIDTH), pmap),
                 pl.BlockSpec((bt, CONV_W - 1, CONV_DIM), tmap),
                 pl.BlockSpec((pb, SSM_WIDTH, D_STATE), pmap)]
                + [_const_spec(a.shape) for a in consts]
                + [_const_spec(c.shape)] * 2
                + [_const_spec(sink.shape), _const_spec(bias.shape)]
                + [_const_spec(a.shape) for a in tail])
    out_shape = (
        jax.ShapeDtypeStruct((nb * tdec, D_MODEL), F32),
        jax.ShapeDtypeStruct((nb, WINDOW, KV_WIDTH), F32),
        jax.ShapeDtypeStruct((nb, WINDOW, KV_WIDTH), F32),
        jax.ShapeDtypeStruct((nb, CONV_W - 1, CONV_DIM), F32),
        jax.ShapeDtypeStruct((nb, SSM_WIDTH, D_STATE), F32),
    )
    out_specs = (
        pl.BlockSpec((rows, D_MODEL), lambda i, j: (i, 0)),
        pl.BlockSpec((pb, WINDOW, KV_WIDTH), pmap),
        pl.BlockSpec((pb, WINDOW, KV_WIDTH), pmap),
        pl.BlockSpec((bt, CONV_W - 1, CONV_DIM), tmap),
        pl.BlockSpec((pb, SSM_WIDTH, D_STATE), pmap),
    )
    scratch = [
        pltpu.VMEM((rows, D_MODEL), BF16),
        pltpu.VMEM((rows, ATT_WIDTH), F32),
        pltpu.VMEM((rows, KV_WIDTH), F32),
        pltpu.VMEM((rows, KV_WIDTH), F32),
        pltpu.VMEM((rows, KV_WIDTH), BF16),
        pltpu.VMEM((rows, KV_WIDTH), BF16),
        pltpu.VMEM((bt, 2 * SUBLANES, CONV_DIM), F32),
        pltpu.VMEM((rows, CONV_DIM), F32),
        pltpu.VMEM((rows, SSM_GROUPS * D_STATE), BF16),
        pltpu.VMEM((rows, LANES), F32),
        pltpu.VMEM((rows, SSM_WIDTH), F32),
        pltpu.VMEM((SSM_WIDTH, rows), BF16),
        pltpu.VMEM((rows, SSM_WIDTH), F32),
        pltpu.VMEM((rows, SSM_WIDTH), F32),
        pltpu.VMEM((rows, MIX_WIDTH), F32),
        pltpu.VMEM((D_MODEL, ATT_WIDTH), BF16),
    ]
    return pl.pallas_call(
        _sample_kernel,
        grid=(nb // bt, steps),
        in_specs=in_specs,
        out_specs=out_specs,
        out_shape=out_shape,
        scratch_shapes=scratch,
        compiler_params=pltpu.CompilerParams(
            dimension_semantics=("arbitrary", "arbitrary"), vmem_limit_bytes=VMEM_LIMIT_BYTES),
        name="sample_mixer",
    )(x2, ck, cv, state_conv, ssm, *consts, c, sn, sink, bias, *tail)


def _ffn(xa, xb, gpre, wg, wu, wd, gpost):
    tf = FFN_TILE
    steps_a, steps_b = xa.shape[0] // tf, xb.shape[0] // tf
    consts = [gpre, wg, wu, wd, gpost]
    amap = lambda i: (jnp.minimum(i, steps_a - 1), 0)
    bmap = lambda i: (jnp.maximum(i - steps_a, 0), 0)
    return pl.pallas_call(
        functools.partial(_ffn_kernel, steps_a=steps_a),
        grid=(steps_a + steps_b,),
        in_specs=([pl.BlockSpec((tf, D_MODEL), amap), pl.BlockSpec((tf, D_MODEL), bmap)]
                  + [_const_spec(a.shape) for a in consts]),
        out_specs=(pl.BlockSpec((tf, D_MODEL), amap), pl.BlockSpec((tf, D_MODEL), bmap)),
        out_shape=(jax.ShapeDtypeStruct(xa.shape, F32), jax.ShapeDtypeStruct(xb.shape, F32)),
        scratch_shapes=[pltpu.VMEM((tf, D_MODEL), BF16)],
        compiler_params=pltpu.CompilerParams(
            dimension_semantics=("arbitrary",), vmem_limit_bytes=VMEM_LIMIT_BYTES),
        name="ffn",
    )(xa, xb, *consts)


def kernel(x_prompt, x_sample, cache_k_win, cache_v_win, state_conv, state_ssm, g_pre_mix, w_in, attn_sinks, conv_w, conv_b, dt_bias, a_log, d_skip, g_ssm_out, w_out, g_post_mix, g_pre_ffn, w_gate, w_up, w_down, g_post_ffn):
    depth = w_in.shape[0]
    bp, lp, _ = x_prompt.shape
    nb, ts, _ = x_sample.shape
    hp, hs = x_prompt, x_sample
    outs = [[] for _ in range(8)]
    for l in range(depth):
        p = _layer_params(g_pre_mix[l], w_in[l], attn_sinks[l], conv_w[l], conv_b[l], dt_bias[l], a_log[l],
                          d_skip[l], g_ssm_out[l], w_out[l], g_post_mix[l])
        ffn_w = (g_pre_ffn[l].reshape(1, D_MODEL), w_gate[l].astype(BF16), w_up[l].astype(BF16),
                 w_down[l].astype(BF16), g_post_ffn[l].reshape(1, D_MODEL))
        x1p, kp, vp, cp, sp = _prompt_mixer(hp, p)
        x1s, ksm, vsm, csm, ssm = _sample_mixer(hs, cache_k_win[l], cache_v_win[l], state_conv[l], state_ssm[l], p)
        hp, hs = _ffn(x1p.reshape(bp * lp, D_MODEL), x1s, *ffn_w)
        hp = hp.reshape(bp, lp, D_MODEL)
        hs = hs.reshape(nb, ts, D_MODEL)
        kv_shape = (WINDOW, ATT_KV_HEADS, HEAD_DIM)
        ssm_shape = (SSM_HEADS, SSM_HEAD_DIM, D_STATE)
        for lst, val in zip(outs, (kp.reshape((bp,) + kv_shape), vp.reshape((bp,) + kv_shape), cp,
                                   sp.reshape((bp,) + ssm_shape),
                                   ksm.reshape((nb,) + kv_shape), vsm.reshape((nb,) + kv_shape), csm,
                                   ssm.reshape((nb,) + ssm_shape))):
            lst.append(val)
    return (hp, hs) + tuple(jnp.stack(o) for o in outs)
```

```python
import functools
import math

import numpy as np
import jax
import jax.numpy as jnp
from jax import lax
from jax.experimental import pallas as pl
from jax.experimental.pallas import tpu as pltpu

F32 = jnp.float32
BF16 = jnp.bfloat16

D_MODEL = 1024
ATT_HEADS = 16
ATT_KV_HEADS = 4
ATT_GROUP = ATT_HEADS // ATT_KV_HEADS
HEAD_DIM = 64
ATT_WIDTH = ATT_HEADS * HEAD_DIM
KV_WIDTH = ATT_KV_HEADS * HEAD_DIM
WINDOW = 128
ROT_DIM = HEAD_DIM // 4
ROPE_THETA = 500000.0
SSM_HEADS = 16
SSM_HEAD_DIM = 64
SSM_WIDTH = SSM_HEADS * SSM_HEAD_DIM
SSM_GROUPS = 2
SSM_HPG = SSM_HEADS // SSM_GROUPS
SSM_GROUP_W = SSM_WIDTH // SSM_GROUPS
D_STATE = 128
CONV_W = 4
CONV_DIM = SSM_WIDTH + 2 * SSM_GROUPS * D_STATE
MIX_WIDTH = ATT_WIDTH + SSM_WIDTH
EPS = 1e-6
PAST_LEN = 8192

LANES = 128
SUBLANES = 8
VMEM_LIMIT_BYTES = 60 * 1024 * 1024

CHUNK = 128
NEG = -1e30
LOG2E = math.log2(math.e)
Q_SCALE = HEAD_DIM ** -0.5 * LOG2E
SEQ_TILE = 512
SAMPLE_BT = 16
SAMPLE_PB = 4
FFN_TILE = 512
PROJ_ROWS = 256
OUT_CB = 512
ROW_BLOCK = 32
FF_CHUNK = 256


def _nn(a, b):
    return jnp.dot(a, b, preferred_element_type=F32)


def _nt(a, b):
    return lax.dot_general(a, b, (((1,), (1,)), ((), ())), preferred_element_type=F32)


def _split_bf16(x, n):
    parts = []
    r = x
    for i in range(n):
        p = r.astype(BF16)
        parts.append(p)
        if i + 1 < n:
            r = r - p.astype(F32)
    return parts


def _expand_heads(x, expand2_ref):
    hi, mid = _split_bf16(x, 2)
    return _nn(jnp.concatenate([hi, mid], axis=1), expand2_ref[...])


def _cumsum_cols(m01, x):
    w = x.shape[1]
    r = _nn(m01, jnp.concatenate(_split_bf16(x, 3), axis=1))
    return r[:, :w] + r[:, w:2 * w] + r[:, 2 * w:]


def _heads_to_rows(x):
    return x.T[:SSM_HEADS]


def _rms(x, g):
    ms = jnp.mean(x * x, axis=-1, keepdims=True)
    return x * lax.rsqrt(ms + EPS) * g


def _rms_rows(src, g_ref, dst_ref, res_ref=None):
    g = g_ref[...]
    for r0 in range(0, dst_ref.shape[0], ROW_BLOCK):
        rs = slice(r0, r0 + ROW_BLOCK)
        y = _rms(src[rs, :], g)
        if res_ref is not None:
            y = res_ref[rs, :] + y
        dst_ref[rs, :] = y.astype(dst_ref.dtype)


def _silu(x):
    h = 0.5 * x
    return h + h * jnp.tanh(h)


R_K = ATT_WIDTH
R_V = R_K + KV_WIDTH
R_Z = R_V + KV_WIDTH
R_XBC = R_Z + SSM_WIDTH
R_END = R_XBC + CONV_DIM

def _conv4(x, w, bias):
    assert CONV_W == 4
    u = pltpu.roll(x, 2, 0)
    return bias + (x * w[3] + u * w[1]) + pltpu.roll(x * w[2] + u * w[0], 1, 0)


CONV_CB = 256
CONV_RB = 64


def _conv_silu_cols(xpad_ref, row0, ra, rb, convw_ref, convb_ref, out_ref, c0):
    cs = slice(c0, c0 + CONV_CB)
    w = [convw_ref[i:i + 1, cs] for i in range(CONV_W)]
    bias = convb_ref[:, cs]
    for r0 in range(ra, rb, CONV_RB):
        xh = xpad_ref[row0 + r0 - SUBLANES:row0 + r0 + CONV_RB, cs]
        out_ref[r0:r0 + CONV_RB, cs] = _silu(_conv4(xh, w, bias)[SUBLANES:])


def _softplus(x):
    return jnp.maximum(x, 0.0) + jnp.log1p(jnp.exp(-jnp.abs(x)))


def _rope(x, c, s):
    lane = lax.broadcasted_iota(jnp.int32, (1, LANES), 1)
    first = (lane % HEAD_DIM) < ROT_DIM // 2
    outs = []
    for j in range(x.shape[1] // LANES):
        xb = x[:, j * LANES:(j + 1) * LANES]
        partner = jnp.where(first, pltpu.roll(xb, LANES - ROT_DIM // 2, 1), pltpu.roll(xb, ROT_DIM // 2, 1))
        outs.append(xb * c + partner * s)
    return outs[0] if len(outs) == 1 else jnp.concatenate(outs, axis=1)


def _iota(shape, dim):
    return lax.broadcasted_iota(jnp.int32, shape, dim)


def _head_blocks(c_out):
    out = []
    for half in range(2):
        g, kvh = divmod(2 * c_out + half, ATT_KV_HEADS)
        b_in = kvh * ATT_GROUP + g
        out.append((b_in // 2, b_in % 2))
    return out


def _permute_q_weight(wr_ref, wq_s):
    low = _iota((1, LANES), 1) < HEAD_DIM
    for c_out in range(ATT_WIDTH // LANES):
        halves = []
        for half, (c_in, src_half) in enumerate(_head_blocks(c_out)):
            col = wr_ref[:, c_in * LANES:(c_in + 1) * LANES]
            halves.append(col if src_half == half else pltpu.roll(col, HEAD_DIM, 1))
        wq_s[:, c_out * LANES:(c_out + 1) * LANES] = jnp.where(low, halves[0], halves[1])


def _project(hn, wq_ref, wr_ref, ropec, ropes):
    q = _rope(_nn(hn, wq_ref[...]), ropec, ropes) * (HEAD_DIM ** -0.5)
    kv = _nn(hn, wr_ref[:, R_K:R_Z])
    k = _rope(kv[:, :KV_WIDTH], ropec, ropes)
    v = kv[:, KV_WIDTH:]
    return q, k, v


def _dt_cols(hn, wdt_ref, dtb_row_ref):
    return _softplus(_nn(hn, wdt_ref[...]) + dtb_row_ref[...])


def _a_row(alog_row_ref):
    lane = _iota((1, LANES), 1)
    return jnp.where(lane < SSM_HEADS, -jnp.exp(alog_row_ref[...]), 0.0)


def _log2_decay(acol, arow, dtr):
    return acol * LOG2E, (arow - jnp.log(dtr)) * LOG2E


SSD_QUAD = 4
SSD_NQUAD = SSM_HEADS // SSD_QUAD


def _ssd_cb(b_all, c_all):
    return [_nt(c_all[:, g * D_STATE:(g + 1) * D_STATE].astype(BF16),
                b_all[:, g * D_STATE:(g + 1) * D_STATE].astype(BF16)) for g in range(SSM_GROUPS)]


def _ssd_quad(qi, xs_bf, c_all, cbs, acol2, arow2, mask_bool, hT_bf=None):
    lane4 = _iota((1, SSD_QUAD * SSM_HEAD_DIM), 1) // SSM_HEAD_DIM
    zero = jnp.zeros((), BF16)
    e0 = SSD_QUAD * qi
    g = e0 // SSM_HPG
    cf = c_all[:, g * D_STATE:(g + 1) * D_STATE]
    lanes = slice(e0 * SSM_HEAD_DIM, (e0 + SSD_QUAD) * SSM_HEAD_DIM)
    xq = xs_bf[:, lanes]
    lhs, rhs = [], []
    for i in range(SSD_QUAD):
        e = e0 + i
        a_t = jnp.broadcast_to(acol2[:, e:e + 1], (CHUNK, CHUNK))
        w = cbs[g] * jnp.exp2(jnp.where(mask_bool, a_t - arow2[e:e + 1, :], NEG))
        lhs.append(w.astype(BF16))
        rhs.append(jnp.where(lane4 == i, xq, zero))
        if hT_bf is not None:
            lhs.append((cf * jnp.exp2(a_t)).astype(BF16))
            rhs.append(jnp.where(lane4 == i, hT_bf[:, lanes], zero))
    return _nn(jnp.concatenate(lhs, axis=1), jnp.concatenate(rhs, axis=0))


def _ssd_block(xs_bf, b_all, c_all, acol2, arow2, mask_bool):
    cbs = _ssd_cb(b_all, c_all)
    return jnp.concatenate([_ssd_quad(qi, xs_bf, c_all, cbs, acol2, arow2, mask_bool)
                            for qi in range(SSD_NQUAD)], axis=1)


def _gate_rows(rs, y_refs, xc_s, z, dskip_ref, gssm_ref, mix_s):
    y = y_refs[0][rs, :]
    for extra in y_refs[1:]:
        y = y + extra[rs, :]
    gated = (y + dskip_ref[...] * xc_s[rs, 0:SSM_WIDTH]) * _silu(z[rs, :])
    for g in range(SSM_GROUPS):
        gs = slice(g * SSM_GROUP_W, (g + 1) * SSM_GROUP_W)
        gg = gated[:, gs]
        ms = jnp.mean(gg * gg, axis=-1, keepdims=True)
        o = gg * lax.rsqrt(ms + EPS) * gssm_ref[:, gs]
        mix_s[rs, ATT_WIDTH + g * SSM_GROUP_W:ATT_WIDTH + (g + 1) * SSM_GROUP_W] = o.astype(mix_s.dtype)


def _gate_and_out(y_refs, xc_s, z, x_ref, dskip_ref, gssm_ref, wout_ref, gpost_ref, mix_s, out_ref):
    for r0 in range(0, out_ref.shape[0], ROW_BLOCK):
        _gate_rows(slice(r0, r0 + ROW_BLOCK), y_refs, xc_s, z, dskip_ref, gssm_ref, mix_s)
    mo = _nn(mix_s[...].astype(BF16), wout_ref[...])
    _rms_rows(mo, gpost_ref, out_ref, res_ref=x_ref)


def _prompt_kernel(x_ref, gpre_ref, wr_ref, wdt_ref,
                   ropec_ref, ropes_ref, sink_ref, biasT_ref,
                   convw_ref, convb_ref, dtb_row_ref, alog_row_ref,
                   dskip_ref, gssm_ref, wout_ref, gpost_ref, expand2_ref,
                   x1_ref, nk_ref, nv_ref, nconv_ref, nssm_ref,
                   hn_s, q_s, kbuf, vT_s, xbc_s, xc_s, dtc_s, y_s, mix_s, hT_s, wq_s, z_s, mo_s):
    tm = x_ref.shape[0]
    nchunk = tm // CHUNK
    s = pl.program_id(1)
    last = pl.num_programs(1) - 1

    @pl.when(jnp.logical_and(pl.program_id(0) == 0, s == 0))
    def _():
        _permute_q_weight(wr_ref, wq_s)

    @pl.when(s == 0)
    def _():
        kbuf[0:CHUNK, :] = jnp.zeros((CHUNK, KV_WIDTH), BF16)
        vT_s[0] = jnp.zeros((KV_WIDTH, CHUNK), BF16)
        xbc_s[0:SUBLANES, :] = jnp.zeros((SUBLANES, CONV_DIM), F32)
        hT_s[...] = jnp.zeros_like(hT_s)

    _rms_rows(x_ref, gpre_ref, hn_s)

    def projection_pieces(ra, rb):
        rows = slice(ra, rb)

        def rope(x):
            return _rope(x, ropec_ref[rows, :], ropes_ref[rows, :])

        def proj_xbc(c0):
            xbc_s[SUBLANES + ra:SUBLANES + rb, c0:c0 + CONV_CB] = _nn(
                hn_s[rows, :], wr_ref[:, R_XBC + c0:R_XBC + c0 + CONV_CB])

        def conv(c0):
            _conv_silu_cols(xbc_s, SUBLANES, ra, rb, convw_ref, convb_ref, xc_s, c0)

        def proj_q(c0):
            q_s[rows, c0:c0 + KV_WIDTH] = (
                rope(_nn(hn_s[rows, :], wq_s[:, c0:c0 + KV_WIDTH])) * Q_SCALE).astype(BF16)

        def proj_k():
            kbuf[CHUNK + ra:CHUNK + rb, :] = rope(_nn(hn_s[rows, :], wr_ref[:, R_K:R_V])).astype(BF16)

        def proj_v_dt():
            v = _nn(hn_s[rows, :], wr_ref[:, R_V:R_Z])
            for j in range(ra // CHUNK, rb // CHUNK):
                vT_s[1 + j] = v[j * CHUNK - ra:(j + 1) * CHUNK - ra, :].T.astype(BF16)
            dtc_s[rows, :] = _dt_cols(hn_s[rows, :], wdt_ref, dtb_row_ref)

        others = [functools.partial(proj_q, c0) for c0 in range(0, ATT_WIDTH, KV_WIDTH)] + [proj_k, proj_v_dt]
        conv_cols = list(range(0, CONV_DIM, CONV_CB))
        pieces = [functools.partial(proj_xbc, conv_cols[0])]
        for n, c0 in enumerate(conv_cols):
            if n + 1 < len(conv_cols):
                pieces.append(functools.partial(proj_xbc, conv_cols[n + 1]))
            pieces.append(functools.partial(conv, c0))
            if n < len(others):
                pieces.append(others[n])
        return pieces + others[len(conv_cols):]

    a_row = _a_row(alog_row_ref)

    r2 = _iota((CHUNK, CHUNK), 0)
    c2 = _iota((CHUNK, CHUNK), 1)
    tril = c2 <= r2
    tril_bf = tril.astype(BF16)
    lane_kv = _iota((1, KV_WIDTH), 1) // HEAD_DIM
    ones_rows = jnp.ones((2 * SUBLANES, 2 * CHUNK), BF16)

    def chunk_body(c, extras):
        r0 = c * CHUNK
        per = -(-len(extras) // ATT_KV_HEADS)
        first = jnp.logical_and(s == 0, c == 0).astype(jnp.int32)
        bias = biasT_ref[first]
        qcat = jnp.concatenate([q_s[pl.ds(r0, CHUNK), g * KV_WIDTH:(g + 1) * KV_WIDTH]
                                for g in range(ATT_GROUP)], axis=0)
        kwin = kbuf[pl.ds(r0, 2 * CHUNK), :]
        kstack = jnp.concatenate([jnp.where(lane_kv == kvh, kwin, jnp.zeros((), BF16))
                                  for kvh in range(ATT_KV_HEADS)], axis=0)
        sT = _nt(kstack, qcat)
        vT_win = jnp.concatenate([vT_s[c], vT_s[c + 1]], axis=1)

        dtc_c = dtc_s[pl.ds(r0, CHUNK), :]
        dtr_c = _heads_to_rows(dtc_c)
        acol = _cumsum_cols(tril_bf, dtc_c * a_row)
        arow = _heads_to_rows(acol)
        a_end = acol[CHUNK - 1:CHUNK, :]
        tailc = jnp.exp(a_end - acol) * dtc_c
        ex = _expand_heads(
            jnp.concatenate([tailc, jnp.broadcast_to(jnp.exp(a_end), (SUBLANES, LANES))], axis=0), expand2_ref)
        tlx = ex[:CHUNK]
        dec_row = ex[CHUNK:CHUNK + 1]
        xs = xc_s[pl.ds(r0, CHUNK), 0:SSM_WIDTH]
        b_all = xc_s[pl.ds(r0, CHUNK), SSM_WIDTH:SSM_WIDTH + SSM_GROUPS * D_STATE]
        c_all = xc_s[pl.ds(r0, CHUNK), SSM_WIDTH + SSM_GROUPS * D_STATE:CONV_DIM]
        hT = hT_s[...]
        acol2, arow2 = _log2_decay(acol, arow, dtr_c)
        xs_bf = xs.astype(BF16)
        hT_bf = hT.astype(BF16)
        cbs = _ssd_cb(b_all, c_all)

        o_rows = []
        for i in range(ATT_KV_HEADS):
            blk = sT[i * 2 * CHUNK:(i + 1) * 2 * CHUNK] + bias
            sink = sink_ref[i] * LOG2E
            m = jnp.maximum(jnp.max(blk, axis=0, keepdims=True), sink)
            p = jnp.exp2(blk - m).astype(BF16)
            lhs = jnp.concatenate([vT_win[i * HEAD_DIM:(i + 1) * HEAD_DIM], ones_rows], axis=0)
            oT = _nn(lhs, p)
            den = oT[HEAD_DIM:HEAD_DIM + 1] + jnp.exp2(sink - m)
            o_rows.append(oT[:HEAD_DIM] * (1.0 / den))
            for qi in range(i * SSD_NQUAD // ATT_KV_HEADS, (i + 1) * SSD_NQUAD // ATT_KV_HEADS):
                lanes = slice(qi * SSD_QUAD * SSM_HEAD_DIM, (qi + 1) * SSD_QUAD * SSM_HEAD_DIM)
                y_s[pl.ds(r0, CHUNK), lanes] = _ssd_quad(qi, xs_bf, c_all, cbs, acol2, arow2, tril, hT_bf)
            for t in extras[i * per:(i + 1) * per]:
                t()
        for c_out in range(ATT_WIDTH // LANES):
            kvh, g0 = divmod(2 * c_out, ATT_GROUP)
            two = jnp.concatenate([o_rows[kvh][:, g * CHUNK:(g + 1) * CHUNK] for g in (g0, g0 + 1)], axis=0)
            mix_s[pl.ds(r0, CHUNK), c_out * LANES:(c_out + 1) * LANES] = two.T.astype(BF16)

        xtl = (xs * tlx).astype(BF16)
        for g in range(SSM_GROUPS):
            sl = slice(g * SSM_GROUP_W, (g + 1) * SSM_GROUP_W)
            bt = b_all[:, g * D_STATE:(g + 1) * D_STATE].T.astype(BF16)
            hT_s[:, sl] = hT[:, sl] * dec_row[:, sl] + _nn(bt, xtl[:, sl])

    def output_pieces(ra, rb):
        rows = slice(ra, rb)
        g_post = gpost_ref[...]

        def proj_z(c0):
            z_s[rows, c0:c0 + OUT_CB] = _nn(hn_s[rows, :], wr_ref[:, R_Z + c0:R_Z + c0 + OUT_CB])

        def gate(r0):
            _gate_rows(slice(r0, r0 + ROW_BLOCK), (y_s,), xc_s, z_s, dskip_ref, gssm_ref, mix_s)

        def proj_out(c0):
            mo_s[rows, c0:c0 + OUT_CB] = _nn(mix_s[rows, :], wout_ref[:, c0:c0 + OUT_CB])

        def post(r0):
            rs = slice(r0, r0 + ROW_BLOCK)
            x1_ref[rs, :] = x_ref[rs, :] + _rms(mo_s[rs, :], g_post)

        return ([functools.partial(proj_z, c0) for c0 in range(0, SSM_WIDTH, OUT_CB)]
                + [functools.partial(gate, r0) for r0 in range(ra, rb, ROW_BLOCK)]
                + [functools.partial(proj_out, c0) for c0 in range(0, D_MODEL, OUT_CB)]
                + [functools.partial(post, r0) for r0 in range(ra, rb, ROW_BLOCK)])

    groups = list(range(0, tm, PROJ_ROWS))
    cpg = PROJ_ROWS // CHUNK
    for t in projection_pieces(0, PROJ_ROWS):
        t()
    for gi, ra in enumerate(groups):
        side = projection_pieces(ra + PROJ_ROWS, ra + 2 * PROJ_ROWS) if gi + 1 < len(groups) else []
        if gi > 0:
            side = side + output_pieces(ra - PROJ_ROWS, ra)
        share = -(-len(side) // cpg)
        for k in range(cpg):
            chunk_body(ra // CHUNK + k, side[k * share:(k + 1) * share])

    kbuf[0:CHUNK, :] = kbuf[tm:tm + CHUNK, :]
    vT_s[0] = vT_s[nchunk]
    xbc_s[0:SUBLANES, :] = xbc_s[tm:tm + SUBLANES, :]

    for t in output_pieces(tm - PROJ_ROWS, tm):
        t()

    @pl.when(s == last)
    def _():
        hn_w = hn_s[tm - WINDOW:, :]
        nk_ref[...] = _rope(_nn(hn_w, wr_ref[:, R_K:R_V]), ropec_ref[tm - WINDOW:, :], ropes_ref[tm - WINDOW:, :])
        nv_ref[...] = _nn(hn_w, wr_ref[:, R_V:R_Z])
        nconv_ref[...] = xbc_s[SUBLANES - (CONV_W - 1):SUBLANES, :]
        nssm_ref[...] = hT_s[...].T


def _sample_kernel(x_ref, ck_ref, cv_ref, sconv_ref, sssm_ref,
                   gpre_ref, wr_ref, wdt_ref,
                   ropec_ref, ropes_ref, sink_ref, bias_ref,
                   convw_ref, convb_ref, dtb_row_ref, alog_row_ref,
                   dskip_ref, gssm_ref, wout_ref, gpost_ref, expand2_ref,
                   x1_ref, nk_ref, nv_ref, nconv_ref, nssm_ref,
                   hn_s, q_s, kn_s, vn_s, knb_s, vnb_s, xpad_s, xc_s, bb_s, ea_s, eax_s, xT_s, y_s, yoff_s, mix_s, wq_s):
    bt_n = sconv_ref.shape[0]
    pb_n = ck_ref.shape[0]
    m_rows = x_ref.shape[0]
    tdec = m_rows // bt_n
    j = pl.program_id(1)

    @pl.when(jnp.logical_and(pl.program_id(0) == 0, j == 0))
    def _():
        _permute_q_weight(wr_ref, wq_s)

    @pl.when(j == 0)
    def _():
        _rms_rows(x_ref, gpre_ref, hn_s)
        hn = hn_s[...]
        q, k, v = _project(hn, wq_s, wr_ref, ropec_ref[...], ropes_ref[...])
        q_s[...] = q
        kn_s[...] = k
        vn_s[...] = v
        knb_s[...] = k.astype(BF16)
        vnb_s[...] = v.astype(BF16)

        xbc = _nn(hn, wr_ref[:, R_XBC:R_END])
        xpad_s[:, 0:SUBLANES - 3, :] = jnp.zeros((bt_n, SUBLANES - 3, CONV_DIM), F32)
        xpad_s[:, SUBLANES - 3:SUBLANES, :] = sconv_ref[...]
        xpad_s[:, SUBLANES:2 * SUBLANES, :] = xbc.reshape(bt_n, tdec, CONV_DIM)
        nconv_ref[...] = xpad_s[:, 2 * SUBLANES - 3:2 * SUBLANES, :]
        for c0 in range(0, CONV_DIM, CONV_CB):
            cs = slice(c0, c0 + CONV_CB)
            xh = xpad_s[:, :, cs].reshape(bt_n * 2 * SUBLANES, CONV_CB)
            acc = _conv4(xh, [convw_ref[i:i + 1, cs] for i in range(CONV_W)], convb_ref[:, cs])
            xc_s[:, cs] = _silu(acc.reshape(bt_n, 2 * SUBLANES, CONV_CB)[:, SUBLANES:, :].reshape(m_rows, CONV_CB))
        xs = xc_s[:, 0:SSM_WIDTH]
        b_all = xc_s[:, SSM_WIDTH:SSM_WIDTH + SSM_GROUPS * D_STATE]
        c_all = xc_s[:, SSM_WIDTH + SSM_GROUPS * D_STATE:CONV_DIM]
        bb_s[...] = b_all.astype(BF16)

        dtc = _dt_cols(hn, wdt_ref, dtb_row_ref)
        dtr = _heads_to_rows(dtc)
        a_row = _a_row(alog_row_ref)

        r2 = _iota((m_rows, m_rows), 0)
        c2 = _iota((m_rows, m_rows), 1)
        same = (r2 // tdec) == (c2 // tdec)
        causal = jnp.logical_and(same, c2 <= r2)
        causal_bf = causal.astype(BF16)
        same_bf = same.astype(BF16)

        dac = dtc * a_row
        acol = _cumsum_cols(causal_bf, dac)
        alast = _cumsum_cols(same_bf, dac)
        arow = _heads_to_rows(acol)
        tailc = jnp.exp(alast - acol) * dtc
        ex = _expand_heads(jnp.concatenate([jnp.exp(acol), tailc], axis=0), expand2_ref)
        ea_s[...] = jnp.exp(alast)
        eax_s[...] = ex[:m_rows]
        acol2, arow2 = _log2_decay(acol, arow, dtr)
        y_s[...] = _ssd_block(xs.astype(BF16), b_all, c_all, acol2, arow2, causal)
        xtl = xs * ex[m_rows:]
        for jj in range(SSM_WIDTH // LANES):
            xT_s[jj * LANES:(jj + 1) * LANES, :] = xtl[:, jj * LANES:(jj + 1) * LANES].T.astype(BF16)

    lane_kv = _iota((1, KV_WIDTH), 1) // HEAD_DIM
    row_b = _iota((m_rows, 1), 0) // tdec
    low_half = _iota((1, LANES), 1) < HEAD_DIM
    sink = sink_ref[...]

    pbs = range(pb_n)
    bs = [j * pb_n + pb for pb in pbs]
    rs = [pl.multiple_of(b * tdec, tdec) for b in bs]

    qbd, kc, vc, sc = [], [], [], []
    for pb in pbs:
        q8 = q_s[pl.ds(rs[pb], tdec), :]
        qbd.append(jnp.concatenate(
            [jnp.where(lane_kv == kvh, q8[:, g * KV_WIDTH:(g + 1) * KV_WIDTH], 0.0)
             for g in range(ATT_GROUP) for kvh in range(ATT_KV_HEADS)], axis=0).astype(BF16))
        kc.append(ck_ref[pb])
        vc.append(cv_ref[pb])
    for pb in pbs:
        keys = jnp.concatenate([kc[pb].astype(BF16), knb_s[...]], axis=0)
        sc.append(_nt(qbd[pb], keys) + bias_ref[bs[pb]])
    p, inv = [], []
    for pb in pbs:
        m = jnp.maximum(jnp.max(sc[pb], axis=1, keepdims=True), sink)
        e = jnp.exp(sc[pb] - m)
        inv.append(1.0 / (jnp.sum(e, axis=1, keepdims=True) + jnp.exp(sink - m)))
        p.append(e.astype(BF16))
    for pb in pbs:
        o = _nn(p[pb], jnp.concatenate([vc[pb].astype(BF16), vnb_s[...]], axis=0)) * inv[pb]
        for c_out in range(ATT_WIDTH // LANES):
            kvh, g0 = divmod(2 * c_out, ATT_GROUP)
            halves = []
            for half in range(2):
                i0 = ((g0 + half) * ATT_KV_HEADS + kvh) * tdec
                piece = o[i0:i0 + tdec, (kvh // 2) * LANES:(kvh // 2 + 1) * LANES]
                halves.append(piece if kvh % 2 == half else pltpu.roll(piece, HEAD_DIM, 1))
            mix_s[pl.ds(rs[pb], tdec), c_out * LANES:(c_out + 1) * LANES] = jnp.where(low_half, halves[0], halves[1])
        nk_ref[pb, 0:WINDOW - tdec, :] = kc[pb][tdec:, :]
        nk_ref[pb, WINDOW - tdec:WINDOW, :] = kn_s[pl.ds(rs[pb], tdec), :]
        nv_ref[pb, 0:WINDOW - tdec, :] = vc[pb][tdec:, :]
        nv_ref[pb, WINDOW - tdec:WINDOW, :] = vn_s[pl.ds(rs[pb], tdec), :]

    for g in range(SSM_GROUPS):
        sl = slice(g * SSM_GROUP_W, (g + 1) * SSM_GROUP_W)
        c0 = SSM_WIDTH + (SSM_GROUPS + g) * D_STATE
        hg, upd = [], []
        for pb in pbs:
            hg.append(sssm_ref[pb, sl, :])
            bm = jnp.where(row_b == bs[pb], bb_s[:, g * D_STATE:(g + 1) * D_STATE], jnp.zeros((), BF16))
            upd.append(_nn(xT_s[sl, :], bm))
        for pb in pbs:
            cc = xc_s[pl.ds(rs[pb], tdec), c0:c0 + D_STATE].astype(BF16)
            yoff_s[pl.ds(rs[pb], tdec), sl] = (_nt(cc, hg[pb].astype(BF16))
                                               * eax_s[pl.ds(rs[pb], tdec), sl])
        for pb in pbs:
            ea_b = ea_s[pl.ds(rs[pb], 1), :]
            dec = jnp.concatenate(
                [jnp.broadcast_to(ea_b[:, g * SSM_HPG + e:g * SSM_HPG + e + 1], (SSM_HEAD_DIM, D_STATE))
                 for e in range(SSM_HPG)], axis=0)
            nssm_ref[pb, sl, :] = hg[pb] * dec + upd[pb]

    @pl.when(j == pl.num_programs(1) - 1)
    def _():
        z = _nn(hn_s[...], wr_ref[:, R_Z:R_XBC])
        _gate_and_out((y_s, yoff_s), xc_s, z, x_ref, dskip_ref, gssm_ref, wout_ref, gpost_ref, mix_s, x1_ref)


def _ffn_kernel(xa_ref, xb_ref, gpre_ref, wg_ref, wu_ref, wd_ref, gpost_ref, oa_ref, ob_ref, f_s, *, steps_a):
    def tile(x_ref, o_ref):
        _rms_rows(x_ref, gpre_ref, f_s)
        f = f_s[...]
        d_ff = wg_ref.shape[1]
        acc = jnp.zeros(x_ref.shape, F32)
        for j in range(d_ff // FF_CHUNK):
            sl = slice(j * FF_CHUNK, (j + 1) * FF_CHUNK)
            gate = _nn(f, wg_ref[:, sl])
            up = _nn(f, wu_ref[:, sl])
            acc = acc + _nn((_silu(gate) * up).astype(BF16), wd_ref[sl, :])
        _rms_rows(acc, gpost_ref, o_ref, res_ref=x_ref)

    i = pl.program_id(0)

    @pl.when(i < steps_a)
    def _():
        tile(xa_ref, oa_ref)

    @pl.when(i >= steps_a)
    def _():
        tile(xb_ref, ob_ref)


def _const_spec(shape):
    nd = len(shape)
    return pl.BlockSpec(shape, lambda *_: (0,) * nd, pipeline_mode=pl.Buffered(1))


def _rope_tables(pos):
    half = ROT_DIM // 2
    inv = ROPE_THETA ** (-np.arange(half, dtype=np.float64) * 2.0 / ROT_DIM)
    ang = pos.astype(np.float64)[:, None] * inv[None, :]
    cos = np.cos(ang).astype(np.float32)
    sin = np.sin(ang).astype(np.float32)
    n = pos.shape[0]
    pad = HEAD_DIM - ROT_DIM
    c = np.concatenate([cos, cos, np.ones((n, pad), np.float32)], axis=1)
    s = np.concatenate([-sin, sin, np.zeros((n, pad), np.float32)], axis=1)
    rep = LANES // HEAD_DIM
    return tuple(jnp.asarray(np.tile(t, (1, rep))) for t in (c, s))


def _layer_params(g_pre_mix, w_in, attn_sinks, conv_w, conv_b, dt_bias, a_log, d_skip, g_ssm_out, w_out, g_post_mix):
    wr = w_in.astype(BF16)
    wdt = jnp.pad(w_in[:, R_END:], ((0, 0), (0, LANES - SSM_HEADS))).astype(BF16)
    wout = w_out.astype(BF16)
    pad16 = ((0, 0), (0, LANES - SSM_HEADS))
    expand = (np.arange(LANES)[:, None] == (np.arange(SSM_WIDTH)[None, :] // SSM_HEAD_DIM)).astype(np.float32)
    return dict(
        gpre=g_pre_mix.reshape(1, D_MODEL), wr=wr, wdt=wdt,
        convw=conv_w, convb=conv_b.reshape(1, CONV_DIM),
        dtb_row=jnp.pad(dt_bias.reshape(1, SSM_HEADS), pad16),
        alog_row=jnp.pad(a_log.reshape(1, SSM_HEADS), pad16),
        dskip=jnp.repeat(d_skip, SSM_HEAD_DIM).reshape(1, SSM_WIDTH), gssm=g_ssm_out.reshape(1, SSM_WIDTH),
        wout=wout, gpost=g_post_mix.reshape(1, D_MODEL), expand2=jnp.asarray(np.concatenate([expand, expand], axis=0), BF16),
        sinks_gk=attn_sinks.reshape(ATT_KV_HEADS, ATT_GROUP).T,
    )


_WEIGHT_ORDER = ("gpre", "wr", "wdt")
_TAIL_ORDER = ("convw", "convb", "dtb_row", "alog_row",
               "dskip", "gssm", "wout", "gpost", "expand2")


def _prompt_mixer(x, p):
    bsz, seq, _ = x.shape
    tm = SEQ_TILE
    nchunk = tm // CHUNK
    c, sn = _rope_tables(np.arange(seq))
    sink = jnp.repeat(p["sinks_gk"].T, CHUNK, axis=1)[:, None, :]
    jj = np.arange(2 * CHUNK)[:, None]
    tt = np.arange(CHUNK)[None, :]
    vis = (jj >= tt) & (jj <= tt + WINDOW)
    bias = np.stack([np.where(vis, 0.0, NEG), np.where(vis & (jj >= CHUNK), 0.0, NEG)]).astype(np.float32)
    bias = jnp.asarray(np.tile(bias, (1, 1, ATT_GROUP)))
    consts = [p[n] for n in _WEIGHT_ORDER]
    tail = [p[n] for n in _TAIL_ORDER]
    in_specs = ([pl.BlockSpec((None, tm, D_MODEL), lambda b, s: (b, s, 0))]
                + [_const_spec(a.shape) for a in consts]
                + [pl.BlockSpec((tm, LANES), lambda b, s: (s, 0))] * 2
                + [_const_spec(sink.shape), _const_spec(bias.shape)]
                + [_const_spec(a.shape) for a in tail])
    out_shape = (
        jax.ShapeDtypeStruct((bsz, seq, D_MODEL), F32),
        jax.ShapeDtypeStruct((bsz, WINDOW, KV_WIDTH), F32),
        jax.ShapeDtypeStruct((bsz, WINDOW, KV_WIDTH), F32),
        jax.ShapeDtypeStruct((bsz, CONV_W - 1, CONV_DIM), F32),
        jax.ShapeDtypeStruct((bsz, SSM_WIDTH, D_STATE), F32),
    )
    out_specs = (
        pl.BlockSpec((None, tm, D_MODEL), lambda b, s: (b, s, 0)),
        pl.BlockSpec((None, WINDOW, KV_WIDTH), lambda b, s: (b, 0, 0)),
        pl.BlockSpec((None, WINDOW, KV_WIDTH), lambda b, s: (b, 0, 0)),
        pl.BlockSpec((None, CONV_W - 1, CONV_DIM), lambda b, s: (b, 0, 0)),
        pl.BlockSpec((None, SSM_WIDTH, D_STATE), lambda b, s: (b, 0, 0)),
    )
    scratch = [
        pltpu.VMEM((tm, D_MODEL), BF16),
        pltpu.VMEM((tm, ATT_WIDTH), BF16),
        pltpu.VMEM((CHUNK + tm, KV_WIDTH), BF16),
        pltpu.VMEM((nchunk + 1, KV_WIDTH, CHUNK), BF16),
        pltpu.VMEM((SUBLANES + tm, CONV_DIM), F32),
        pltpu.VMEM((tm, CONV_DIM), F32),
        pltpu.VMEM((tm, LANES), F32),
        pltpu.VMEM((tm, SSM_WIDTH), F32),
        pltpu.VMEM((tm, MIX_WIDTH), BF16),
        pltpu.VMEM((D_STATE, SSM_WIDTH), F32),
        pltpu.VMEM((D_MODEL, ATT_WIDTH), BF16),
        pltpu.VMEM((tm, SSM_WIDTH), F32),
        pltpu.VMEM((tm, D_MODEL), F32),
    ]
    return pl.pallas_call(
        _prompt_kernel,
        grid=(bsz, seq // tm),
        in_specs=in_specs,
        out_specs=out_specs,
        out_shape=out_shape,
        scratch_shapes=scratch,
        compiler_params=pltpu.CompilerParams(
            dimension_semantics=("arbitrary", "arbitrary"), vmem_limit_bytes=VMEM_LIMIT_BYTES),
        name="prompt_mixer",
    )(x, *consts, c, sn, sink, bias, *tail)


def _sample_mixer(x, cache_k, cache_v, state_conv, state_ssm, p):
    nb, tdec, _ = x.shape
    bt = SAMPLE_BT
    rows = bt * tdec
    c, sn = _rope_tables(np.tile(PAST_LEN + np.arange(tdec), bt))
    sink = jnp.repeat(p["sinks_gk"].reshape(-1), tdec).reshape(rows, 1)
    consts = [p[n] for n in _WEIGHT_ORDER]
    tail = [p[n] for n in _TAIL_ORDER]
    x2 = x.reshape(nb * tdec, D_MODEL)
    ck = cache_k.reshape(nb, WINDOW, KV_WIDTH)
    cv = cache_v.reshape(nb, WINDOW, KV_WIDTH)
    ssm = state_ssm.reshape(nb, SSM_WIDTH, D_STATE)
    pb = SAMPLE_PB
    steps = bt // pb
    t_of_row = (np.arange(rows) % tdec)[:, None]
    col = np.arange(rows)[None, :]
    vis_c = col >= t_of_row
    bias = jnp.asarray(np.stack(
        [np.where(np.concatenate([vis_c, (col // tdec == b) & (col % tdec <= t_of_row)], axis=1), 0.0, NEG)
         for b in range(bt)]).astype(np.float32))
    tmap = lambda i, j: (i, 0, 0)
    pmap = lambda i, j: (i * steps + j, 0, 0)
    in_specs = ([pl.BlockSpec((rows, D_MODEL), lambda i, j: (i, 0)),
                 pl.BlockSpec((pb, WINDOW, KV_WIDTH), pmap),
                 pl.BlockSpec((pb, WINDOW, KV_WIDTH), pmap),
                 pl.BlockSpec((bt, CONV_W - 1, CONV_DIM), tmap),
                 pl.BlockSpec((pb, SSM_WIDTH, D_STATE), pmap)]
                + [_const_spec(a.shape) for a in consts]
                + [_const_spec(c.shape)] * 2
                + [_const_spec(sink.shape), _const_spec(bias.shape)]
                + [_const_spec(a.shape) for a in tail])
    out_shape = (
        jax.ShapeDtypeStruct((nb * tdec, D_MODEL), F32),
        jax.ShapeDtypeStruct((nb, WINDOW, KV_WIDTH), F32),
        jax.ShapeDtypeStruct((nb, WINDOW, KV_WIDTH), F32),
        jax.ShapeDtypeStruct((nb, CONV_W - 1, CONV_DIM), F32),
        jax.ShapeDtypeStruct((nb, SSM_WIDTH, D_STATE), F32),
    )
    out_specs = (
        pl.BlockSpec((rows, D_MODEL), lambda i, j: (i, 0)),
        pl.BlockSpec((pb, WINDOW, KV_WIDTH), pmap),
        pl.BlockSpec((pb, WINDOW, KV_WIDTH), pmap),
        pl.BlockSpec((bt, CONV_W - 1, CONV_DIM), tmap),
        pl.BlockSpec((pb, SSM_WIDTH, D_STATE), pmap),
    )
    scratch = [
        pltpu.VMEM((rows, D_MODEL), BF16),
        pltpu.VMEM((rows, ATT_WIDTH), F32),
        pltpu.VMEM((rows, KV_WIDTH), F32),
        pltpu.VMEM((rows, KV_WIDTH), F32),
        pltpu.VMEM((rows, KV_WIDTH), BF16),
        pltpu.VMEM((rows, KV_WIDTH), BF16),
        pltpu.VMEM((bt, 2 * SUBLANES, CONV_DIM), F32),
        pltpu.VMEM((rows, CONV_DIM), F32),
        pltpu.VMEM((rows, SSM_GROUPS * D_STATE), BF16),
        pltpu.VMEM((rows, LANES), F32),
        pltpu.VMEM((rows, SSM_WIDTH), F32),
        pltpu.VMEM((SSM_WIDTH, rows), BF16),
        pltpu.VMEM((rows, SSM_WIDTH), F32),
        pltpu.VMEM((rows, SSM_WIDTH), F32),
        pltpu.VMEM((rows, MIX_WIDTH), F32),
        pltpu.VMEM((D_MODEL, ATT_WIDTH), BF16),
    ]
    return pl.pallas_call(
        _sample_kernel,
        grid=(nb // bt, steps),
        in_specs=in_specs,
        out_specs=out_specs,
        out_shape=out_shape,
        scratch_shapes=scratch,
        compiler_params=pltpu.CompilerParams(
            dimension_semantics=("arbitrary", "arbitrary"), vmem_limit_bytes=VMEM_LIMIT_BYTES),
        name="sample_mixer",
    )(x2, ck, cv, state_conv, ssm, *consts, c, sn, sink, bias, *tail)


def _ffn(xa, xb, gpre, wg, wu, wd, gpost):
    tf = FFN_TILE
    steps_a, steps_b = xa.shape[0] // tf, xb.shape[0] // tf
    consts = [gpre, wg, wu, wd, gpost]
    amap = lambda i: (jnp.minimum(i, steps_a - 1), 0)
    bmap = lambda i: (jnp.maximum(i - steps_a, 0), 0)
    return pl.pallas_call(
        functools.partial(_ffn_kernel, steps_a=steps_a),
        grid=(steps_a + steps_b,),
        in_specs=([pl.BlockSpec((tf, D_MODEL), amap), pl.BlockSpec((tf, D_MODEL), bmap)]
                  + [_const_spec(a.shape) for a in consts]),
        out_specs=(pl.BlockSpec((tf, D_MODEL), amap), pl.BlockSpec((tf, D_MODEL), bmap)),
        out_shape=(jax.ShapeDtypeStruct(xa.shape, F32), jax.ShapeDtypeStruct(xb.shape, F32)),
        scratch_shapes=[pltpu.VMEM((tf, D_MODEL), BF16)],
        compiler_params=pltpu.CompilerParams(
            dimension_semantics=("arbitrary",), vmem_limit_bytes=VMEM_LIMIT_BYTES),
        name="ffn",
    )(xa, xb, *consts)


def kernel(x_prompt, x_sample, cache_k_win, cache_v_win, state_conv, state_ssm, g_pre_mix, w_in, attn_sinks, conv_w, conv_b, dt_bias, a_log, d_skip, g_ssm_out, w_out, g_post_mix, g_pre_ffn, w_gate, w_up, w_down, g_post_ffn):
    depth = w_in.shape[0]
    bp, lp, _ = x_prompt.shape
    nb, ts, _ = x_sample.shape
    hp, hs = x_prompt, x_sample
    outs = [[] for _ in range(8)]
    for l in range(depth):
        p = _layer_params(g_pre_mix[l], w_in[l], attn_sinks[l], conv_w[l], conv_b[l], dt_bias[l], a_log[l],
                          d_skip[l], g_ssm_out[l], w_out[l], g_post_mix[l])
        ffn_w = (g_pre_ffn[l].reshape(1, D_MODEL), w_gate[l].astype(BF16), w_up[l].astype(BF16),
                 w_down[l].astype(BF16), g_post_ffn[l].reshape(1, D_MODEL))
        x1p, kp, vp, cp, sp = _prompt_mixer(hp, p)
        x1s, ksm, vsm, csm, ssm = _sample_mixer(hs, cache_k_win[l], cache_v_win[l], state_conv[l], state_ssm[l], p)
        hp, hs = _ffn(x1p.reshape(bp * lp, D_MODEL), x1s, *ffn_w)
        hp = hp.reshape(bp, lp, D_MODEL)
        hs = hs.reshape(nb, ts, D_MODEL)
        kv_shape = (WINDOW, ATT_KV_HEADS, HEAD_DIM)
        ssm_shape = (SSM_HEADS, SSM_HEAD_DIM, D_STATE)
        for lst, val in zip(outs, (kp.reshape((bp,) + kv_shape), vp.reshape((bp,) + kv_shape), cp,
                                   sp.reshape((bp,) + ssm_shape),
                                   ksm.reshape((nb,) + kv_shape), vsm.reshape((nb,) + kv_shape), csm,
                                   ssm.reshape((nb,) + ssm_shape))):
            lst.append(val)
    return (hp, hs) + tuple(jnp.stack(o) for o in outs)
```

```python
import functools
import math

import numpy as np
import jax
import jax.numpy as jnp
from jax import lax
from jax.experimental import pallas as pl
from jax.experimental.pallas import tpu as pltpu

F32 = jnp.float32
BF16 = jnp.bfloat16

D_MODEL = 1024
ATT_HEADS = 16
ATT_KV_HEADS = 4
ATT_GROUP = ATT_HEADS // ATT_KV_HEADS
HEAD_DIM = 64
ATT_WIDTH = ATT_HEADS * HEAD_DIM
KV_WIDTH = ATT_KV_HEADS * HEAD_DIM
WINDOW = 128
ROT_DIM = HEAD_DIM // 4
ROPE_THETA = 500000.0
SSM_HEADS = 16
SSM_HEAD_DIM = 64
SSM_WIDTH = SSM_HEADS * SSM_HEAD_DIM
SSM_GROUPS = 2
SSM_HPG = SSM_HEADS // SSM_GROUPS
SSM_GROUP_W = SSM_WIDTH // SSM_GROUPS
D_STATE = 128
CONV_W = 4
CONV_DIM = SSM_WIDTH + 2 * SSM_GROUPS * D_STATE
MIX_WIDTH = ATT_WIDTH + SSM_WIDTH
EPS = 1e-6
PAST_LEN = 8192

LANES = 128
SUBLANES = 8
VMEM_LIMIT_BYTES = 60 * 1024 * 1024

CHUNK = 128
NEG = -1e30
LOG2E = math.log2(math.e)
Q_SCALE = HEAD_DIM ** -0.5 * LOG2E
SEQ_TILE = 512
SAMPLE_BT = 16
SAMPLE_PB = 4
FFN_TILE = 512
PROJ_ROWS = 256
OUT_CB = 512
ROW_BLOCK = 32
FF_CHUNK = 256


def _nn(a, b):
    return jnp.dot(a, b, preferred_element_type=F32)


def _nt(a, b):
    return lax.dot_general(a, b, (((1,), (1,)), ((), ())), preferred_element_type=F32)


def _split_bf16(x, n):
    parts = []
    r = x
    for i in range(n):
        p = r.astype(BF16)
        parts.append(p)
        if i + 1 < n:
            r = r - p.astype(F32)
    return parts


def _expand_heads(x, expand2_ref):
    hi, mid = _split_bf16(x, 2)
    return _nn(jnp.concatenate([hi, mid], axis=1), expand2_ref[...])


def _cumsum_cols(m01, x):
    w = x.shape[1]
    r = _nn(m01, jnp.concatenate(_split_bf16(x, 3), axis=1))
    return r[:, :w] + r[:, w:2 * w] + r[:, 2 * w:]


def _heads_to_rows(x):
    return x.T[:SSM_HEADS]


def _rms(x, g):
    ms = jnp.mean(x * x, axis=-1, keepdims=True)
    return x * lax.rsqrt(ms + EPS) * g


def _rms_rows(src, g_ref, dst_ref, res_ref=None):
    g = g_ref[...]
    for r0 in range(0, dst_ref.shape[0], ROW_BLOCK):
        rs = slice(r0, r0 + ROW_BLOCK)
        y = _rms(src[rs, :], g)
        if res_ref is not None:
            y = res_ref[rs, :] + y
        dst_ref[rs, :] = y.astype(dst_ref.dtype)


def _silu(x):
    h = 0.5 * x
    return h + h * jnp.tanh(h)


R_K = ATT_WIDTH
R_V = R_K + KV_WIDTH
R_Z = R_V + KV_WIDTH
R_XBC = R_Z + SSM_WIDTH
R_END = R_XBC + CONV_DIM

def _conv4(x, w, bias):
    assert CONV_W == 4
    u = pltpu.roll(x, 2, 0)
    return bias + (x * w[3] + u * w[1]) + pltpu.roll(x * w[2] + u * w[0], 1, 0)


CONV_CB = 256
CONV_RB = 64


def _conv_silu_cols(xpad_ref, row0, ra, rb, convw_ref, convb_ref, out_ref, c0):
    cs = slice(c0, c0 + CONV_CB)
    w = [convw_ref[i:i + 1, cs] for i in range(CONV_W)]
    bias = convb_ref[:, cs]
    for r0 in range(ra, rb, CONV_RB):
        xh = xpad_ref[row0 + r0 - SUBLANES:row0 + r0 + CONV_RB, cs]
        out_ref[r0:r0 + CONV_RB, cs] = _silu(_conv4(xh, w, bias)[SUBLANES:])


def _softplus(x):
    return jnp.maximum(x, 0.0) + jnp.log1p(jnp.exp(-jnp.abs(x)))


def _rope(x, c, s):
    lane = lax.broadcasted_iota(jnp.int32, (1, LANES), 1)
    first = (lane % HEAD_DIM) < ROT_DIM // 2
    outs = []
    for j in range(x.shape[1] // LANES):
        xb = x[:, j * LANES:(j + 1) * LANES]
        partner = jnp.where(first, pltpu.roll(xb, LANES - ROT_DIM // 2, 1), pltpu.roll(xb, ROT_DIM // 2, 1))
        outs.append(xb * c + partner * s)
    return outs[0] if len(outs) == 1 else jnp.concatenate(outs, axis=1)


def _iota(shape, dim):
    return lax.broadcasted_iota(jnp.int32, shape, dim)


def _head_blocks(c_out):
    out = []
    for half in range(2):
        g, kvh = divmod(2 * c_out + half, ATT_KV_HEADS)
        b_in = kvh * ATT_GROUP + g
        out.append((b_in // 2, b_in % 2))
    return out


def _permute_q_weight(wr_ref, wq_s):
    low = _iota((1, LANES), 1) < HEAD_DIM
    for c_out in range(ATT_WIDTH // LANES):
        halves = []
        for half, (c_in, src_half) in enumerate(_head_blocks(c_out)):
            col = wr_ref[:, c_in * LANES:(c_in + 1) * LANES]
            halves.append(col if src_half == half else pltpu.roll(col, HEAD_DIM, 1))
        wq_s[:, c_out * LANES:(c_out + 1) * LANES] = jnp.where(low, halves[0], halves[1])


def _project(hn, wq_ref, wr_ref, ropec, ropes):
    q = _rope(_nn(hn, wq_ref[...]), ropec, ropes) * (HEAD_DIM ** -0.5)
    kv = _nn(hn, wr_ref[:, R_K:R_Z])
    k = _rope(kv[:, :KV_WIDTH], ropec, ropes)
    v = kv[:, KV_WIDTH:]
    return q, k, v


def _dt_cols(hn, wdt_ref, dtb_row_ref):
    return _softplus(_nn(hn, wdt_ref[...]) + dtb_row_ref[...])


def _a_row(alog_row_ref):
    lane = _iota((1, LANES), 1)
    return jnp.where(lane < SSM_HEADS, -jnp.exp(alog_row_ref[...]), 0.0)


def _log2_decay(acol):
    acol2 = acol * LOG2E
    return acol2, _heads_to_rows(acol2)


SSD_QUAD = 4
SSD_NQUAD = SSM_HEADS // SSD_QUAD


def _ssd_cb(b_all, c_all):
    return [_nt(c_all[:, g * D_STATE:(g + 1) * D_STATE].astype(BF16),
                b_all[:, g * D_STATE:(g + 1) * D_STATE].astype(BF16)) for g in range(SSM_GROUPS)]


def _ssd_quad(qi, xs_bf, c_all, cbs, acol2, arow2, dtr, mask_bool, hT_bf=None):
    lane4 = _iota((1, SSD_QUAD * SSM_HEAD_DIM), 1) // SSM_HEAD_DIM
    zero = jnp.zeros((), BF16)
    e0 = SSD_QUAD * qi
    g = e0 // SSM_HPG
    cf = c_all[:, g * D_STATE:(g + 1) * D_STATE]
    lanes = slice(e0 * SSM_HEAD_DIM, (e0 + SSD_QUAD) * SSM_HEAD_DIM)
    xq = xs_bf[:, lanes]
    lhs, rhs = [], []
    for i in range(SSD_QUAD):
        e = e0 + i
        a_t = jnp.broadcast_to(acol2[:, e:e + 1], (CHUNK, CHUNK))
        w = cbs[g] * jnp.exp2(jnp.where(mask_bool, a_t - arow2[e:e + 1, :], NEG)) * dtr[e:e + 1, :]
        lhs.append(w.astype(BF16))
        rhs.append(jnp.where(lane4 == i, xq, zero))
        if hT_bf is not None:
            lhs.append((cf * jnp.exp2(a_t)).astype(BF16))
            rhs.append(jnp.where(lane4 == i, hT_bf[:, lanes], zero))
    return _nn(jnp.concatenate(lhs, axis=1), jnp.concatenate(rhs, axis=0))


def _ssd_block(xs_bf, b_all, c_all, acol2, arow2, dtr, mask_bool):
    cbs = _ssd_cb(b_all, c_all)
    return jnp.concatenate([_ssd_quad(qi, xs_bf, c_all, cbs, acol2, arow2, dtr, mask_bool)
                            for qi in range(SSD_NQUAD)], axis=1)


def _gate_rows(rs, y_refs, xc_s, z, dskip_ref, gssm_ref, mix_s):
    y = y_refs[0][rs, :]
    for extra in y_refs[1:]:
        y = y + extra[rs, :]
    gated = (y + dskip_ref[...] * xc_s[rs, 0:SSM_WIDTH]) * _silu(z[rs, :])
    for g in range(SSM_GROUPS):
        gs = slice(g * SSM_GROUP_W, (g + 1) * SSM_GROUP_W)
        gg = gated[:, gs]
        ms = jnp.mean(gg * gg, axis=-1, keepdims=True)
        o = gg * lax.rsqrt(ms + EPS) * gssm_ref[:, gs]
        mix_s[rs, ATT_WIDTH + g * SSM_GROUP_W:ATT_WIDTH + (g + 1) * SSM_GROUP_W] = o.astype(mix_s.dtype)


def _gate_and_out(y_refs, xc_s, z, x_ref, dskip_ref, gssm_ref, wout_ref, gpost_ref, mix_s, out_ref):
    for r0 in range(0, out_ref.shape[0], ROW_BLOCK):
        _gate_rows(slice(r0, r0 + ROW_BLOCK), y_refs, xc_s, z, dskip_ref, gssm_ref, mix_s)
    mo = _nn(mix_s[...].astype(BF16), wout_ref[...])
    _rms_rows(mo, gpost_ref, out_ref, res_ref=x_ref)


def _prompt_kernel(x_ref, gpre_ref, wr_ref, wdt_ref,
                   ropec_ref, ropes_ref, sink_ref, biasT_ref,
                   convw_ref, convb_ref, dtb_row_ref, alog_row_ref,
                   dskip_ref, gssm_ref, wout_ref, gpost_ref, expand2_ref,
                   x1_ref, nk_ref, nv_ref, nconv_ref, nssm_ref,
                   hn_s, q_s, kbuf, vT_s, xbc_s, xc_s, dtc_s, y_s, mix_s, hT_s, wq_s, z_s, mo_s):
    tm = x_ref.shape[0]
    nchunk = tm // CHUNK
    s = pl.program_id(1)
    last = pl.num_programs(1) - 1

    @pl.when(jnp.logical_and(pl.program_id(0) == 0, s == 0))
    def _():
        _permute_q_weight(wr_ref, wq_s)

    @pl.when(s == 0)
    def _():
        kbuf[0:CHUNK, :] = jnp.zeros((CHUNK, KV_WIDTH), BF16)
        vT_s[0] = jnp.zeros((KV_WIDTH, CHUNK), BF16)
        xbc_s[0:SUBLANES, :] = jnp.zeros((SUBLANES, CONV_DIM), F32)
        hT_s[...] = jnp.zeros_like(hT_s)

    _rms_rows(x_ref, gpre_ref, hn_s)

    def projection_pieces(ra, rb):
        rows = slice(ra, rb)

        def rope(x):
            return _rope(x, ropec_ref[rows, :], ropes_ref[rows, :])

        def proj_xbc(c0):
            xbc_s[SUBLANES + ra:SUBLANES + rb, c0:c0 + CONV_CB] = _nn(
                hn_s[rows, :], wr_ref[:, R_XBC + c0:R_XBC + c0 + CONV_CB])

        def conv(c0):
            _conv_silu_cols(xbc_s, SUBLANES, ra, rb, convw_ref, convb_ref, xc_s, c0)

        def proj_q(c0):
            q_s[rows, c0:c0 + KV_WIDTH] = (
                rope(_nn(hn_s[rows, :], wq_s[:, c0:c0 + KV_WIDTH])) * Q_SCALE).astype(BF16)

        def proj_k():
            kbuf[CHUNK + ra:CHUNK + rb, :] = rope(_nn(hn_s[rows, :], wr_ref[:, R_K:R_V])).astype(BF16)

        def proj_v_dt():
            v = _nn(hn_s[rows, :], wr_ref[:, R_V:R_Z])
            for j in range(ra // CHUNK, rb // CHUNK):
                vT_s[1 + j] = v[j * CHUNK - ra:(j + 1) * CHUNK - ra, :].T.astype(BF16)
            dtc_s[rows, :] = _dt_cols(hn_s[rows, :], wdt_ref, dtb_row_ref)

        others = [functools.partial(proj_q, c0) for c0 in range(0, ATT_WIDTH, KV_WIDTH)] + [proj_k, proj_v_dt]
        conv_cols = list(range(0, CONV_DIM, CONV_CB))
        pieces = [functools.partial(proj_xbc, conv_cols[0])]
        for n, c0 in enumerate(conv_cols):
            if n + 1 < len(conv_cols):
                pieces.append(functools.partial(proj_xbc, conv_cols[n + 1]))
            pieces.append(functools.partial(conv, c0))
            if n < len(others):
                pieces.append(others[n])
        return pieces + others[len(conv_cols):]

    a_row = _a_row(alog_row_ref)

    r2 = _iota((CHUNK, CHUNK), 0)
    c2 = _iota((CHUNK, CHUNK), 1)
    tril = c2 <= r2
    tril_bf = tril.astype(BF16)
    lane_kv = _iota((1, KV_WIDTH), 1) // HEAD_DIM
    ones_rows = jnp.ones((2 * SUBLANES, 2 * CHUNK), BF16)

    def chunk_body(c, extras):
        r0 = c * CHUNK
        per = -(-len(extras) // ATT_KV_HEADS)
        first = jnp.logical_and(s == 0, c == 0).astype(jnp.int32)
        bias = biasT_ref[first]
        qcat = jnp.concatenate([q_s[pl.ds(r0, CHUNK), g * KV_WIDTH:(g + 1) * KV_WIDTH]
                                for g in range(ATT_GROUP)], axis=0)
        kwin = kbuf[pl.ds(r0, 2 * CHUNK), :]
        kstack = jnp.concatenate([jnp.where(lane_kv == kvh, kwin, jnp.zeros((), BF16))
                                  for kvh in range(ATT_KV_HEADS)], axis=0)
        sT = _nt(kstack, qcat)
        vT_win = jnp.concatenate([vT_s[c], vT_s[c + 1]], axis=1)

        dtc_c = dtc_s[pl.ds(r0, CHUNK), :]
        acol = _cumsum_cols(tril_bf, dtc_c * a_row)
        a_end = acol[CHUNK - 1:CHUNK, :]
        tailc = jnp.exp(a_end - acol) * dtc_c
        ex = _expand_heads(
            jnp.concatenate([tailc, jnp.broadcast_to(jnp.exp(a_end), (SUBLANES, LANES))], axis=0), expand2_ref)
        tlx = ex[:CHUNK]
        dec_row = ex[CHUNK:CHUNK + 1]
        xs = xc_s[pl.ds(r0, CHUNK), 0:SSM_WIDTH]
        b_all = xc_s[pl.ds(r0, CHUNK), SSM_WIDTH:SSM_WIDTH + SSM_GROUPS * D_STATE]
        c_all = xc_s[pl.ds(r0, CHUNK), SSM_WIDTH + SSM_GROUPS * D_STATE:CONV_DIM]
        hT = hT_s[...]
        acol2, arow2 = _log2_decay(acol)
        dtr_c = _heads_to_rows(dtc_c)
        xs_bf = xs.astype(BF16)
        hT_bf = hT.astype(BF16)
        cbs = _ssd_cb(b_all, c_all)

        o_rows = []
        for i in range(ATT_KV_HEADS):
            blk = sT[i * 2 * CHUNK:(i + 1) * 2 * CHUNK] + bias
            sink = sink_ref[i] * LOG2E
            m = jnp.maximum(jnp.max(blk, axis=0, keepdims=True), sink)
            p = jnp.exp2(blk - m).astype(BF16)
            lhs = jnp.concatenate([vT_win[i * HEAD_DIM:(i + 1) * HEAD_DIM], ones_rows], axis=0)
            oT = _nn(lhs, p)
            den = oT[HEAD_DIM:HEAD_DIM + 1] + jnp.exp2(sink - m)
            o_rows.append(oT[:HEAD_DIM] * (1.0 / den))
            for qi in range(i * SSD_NQUAD // ATT_KV_HEADS, (i + 1) * SSD_NQUAD // ATT_KV_HEADS):
                lanes = slice(qi * SSD_QUAD * SSM_HEAD_DIM, (qi + 1) * SSD_QUAD * SSM_HEAD_DIM)
                y_s[pl.ds(r0, CHUNK), lanes] = _ssd_quad(qi, xs_bf, c_all, cbs, acol2, arow2, dtr_c, tril, hT_bf)
            for t in extras[i * per:(i + 1) * per]:
                t()
        for c_out in range(ATT_WIDTH // LANES):
            kvh, g0 = divmod(2 * c_out, ATT_GROUP)
            two = jnp.concatenate([o_rows[kvh][:, g * CHUNK:(g + 1) * CHUNK] for g in (g0, g0 + 1)], axis=0)
            mix_s[pl.ds(r0, CHUNK), c_out * LANES:(c_out + 1) * LANES] = two.T.astype(BF16)

        xtl = (xs * tlx).astype(BF16)
        for g in range(SSM_GROUPS):
            sl = slice(g * SSM_GROUP_W, (g + 1) * SSM_GROUP_W)
            bt = b_all[:, g * D_STATE:(g + 1) * D_STATE].T.astype(BF16)
            hT_s[:, sl] = hT[:, sl] * dec_row[:, sl] + _nn(bt, xtl[:, sl])

    def output_pieces(ra, rb):
        rows = slice(ra, rb)
        g_post = gpost_ref[...]

        def proj_z(c0):
            z_s[rows, c0:c0 + OUT_CB] = _nn(hn_s[rows, :], wr_ref[:, R_Z + c0:R_Z + c0 + OUT_CB])

        def gate(r0):
            _gate_rows(slice(r0, r0 + ROW_BLOCK), (y_s,), xc_s, z_s, dskip_ref, gssm_ref, mix_s)

        def proj_out(c0):
            mo_s[rows, c0:c0 + OUT_CB] = _nn(mix_s[rows, :], wout_ref[:, c0:c0 + OUT_CB])

        def post(r0):
            rs = slice(r0, r0 + ROW_BLOCK)
            x1_ref[rs, :] = x_ref[rs, :] + _rms(mo_s[rs, :], g_post)

        return ([functools.partial(proj_z, c0) for c0 in range(0, SSM_WIDTH, OUT_CB)]
                + [functools.partial(gate, r0) for r0 in range(ra, rb, ROW_BLOCK)]
                + [functools.partial(proj_out, c0) for c0 in range(0, D_MODEL, OUT_CB)]
                + [functools.partial(post, r0) for r0 in range(ra, rb, ROW_BLOCK)])

    groups = list(range(0, tm, PROJ_ROWS))
    cpg = PROJ_ROWS // CHUNK
    for t in projection_pieces(0, PROJ_ROWS):
        t()
    for gi, ra in enumerate(groups):
        side = projection_pieces(ra + PROJ_ROWS, ra + 2 * PROJ_ROWS) if gi + 1 < len(groups) else []
        if gi > 0:
            side = side + output_pieces(ra - PROJ_ROWS, ra)
        share = -(-len(side) // cpg)
        for k in range(cpg):
            chunk_body(ra // CHUNK + k, side[k * share:(k + 1) * share])

    kbuf[0:CHUNK, :] = kbuf[tm:tm + CHUNK, :]
    vT_s[0] = vT_s[nchunk]
    xbc_s[0:SUBLANES, :] = xbc_s[tm:tm + SUBLANES, :]

    for t in output_pieces(tm - PROJ_ROWS, tm):
        t()

    @pl.when(s == last)
    def _():
        hn_w = hn_s[tm - WINDOW:, :]
        nk_ref[...] = _rope(_nn(hn_w, wr_ref[:, R_K:R_V]), ropec_ref[tm - WINDOW:, :], ropes_ref[tm - WINDOW:, :])
        nv_ref[...] = _nn(hn_w, wr_ref[:, R_V:R_Z])
        nconv_ref[...] = xbc_s[SUBLANES - (CONV_W - 1):SUBLANES, :]
        nssm_ref[...] = hT_s[...].T


def _sample_kernel(x_ref, ck_ref, cv_ref, sconv_ref, sssm_ref,
                   gpre_ref, wr_ref, wdt_ref,
                   ropec_ref, ropes_ref, sink_ref, bias_ref,
                   convw_ref, convb_ref, dtb_row_ref, alog_row_ref,
                   dskip_ref, gssm_ref, wout_ref, gpost_ref, expand2_ref,
                   x1_ref, nk_ref, nv_ref, nconv_ref, nssm_ref,
                   hn_s, q_s, kn_s, vn_s, knb_s, vnb_s, xpad_s, xc_s, bb_s, ea_s, eax_s, xT_s, y_s, yoff_s, mix_s, wq_s):
    bt_n = sconv_ref.shape[0]
    pb_n = ck_ref.shape[0]
    m_rows = x_ref.shape[0]
    tdec = m_rows // bt_n
    j = pl.program_id(1)

    @pl.when(jnp.logical_and(pl.program_id(0) == 0, j == 0))
    def _():
        _permute_q_weight(wr_ref, wq_s)

    @pl.when(j == 0)
    def _():
        _rms_rows(x_ref, gpre_ref, hn_s)
        hn = hn_s[...]
        q, k, v = _project(hn, wq_s, wr_ref, ropec_ref[...], ropes_ref[...])
        q_s[...] = q
        kn_s[...] = k
        vn_s[...] = v
        knb_s[...] = k.astype(BF16)
        vnb_s[...] = v.astype(BF16)

        xbc = _nn(hn, wr_ref[:, R_XBC:R_END])
        xpad_s[:, 0:SUBLANES - 3, :] = jnp.zeros((bt_n, SUBLANES - 3, CONV_DIM), F32)
        xpad_s[:, SUBLANES - 3:SUBLANES, :] = sconv_ref[...]
        xpad_s[:, SUBLANES:2 * SUBLANES, :] = xbc.reshape(bt_n, tdec, CONV_DIM)
        nconv_ref[...] = xpad_s[:, 2 * SUBLANES - 3:2 * SUBLANES, :]
        for c0 in range(0, CONV_DIM, CONV_CB):
            cs = slice(c0, c0 + CONV_CB)
            xh = xpad_s[:, :, cs].reshape(bt_n * 2 * SUBLANES, CONV_CB)
            acc = _conv4(xh, [convw_ref[i:i + 1, cs] for i in range(CONV_W)], convb_ref[:, cs])
            xc_s[:, cs] = _silu(acc.reshape(bt_n, 2 * SUBLANES, CONV_CB)[:, SUBLANES:, :].reshape(m_rows, CONV_CB))
        xs = xc_s[:, 0:SSM_WIDTH]
        b_all = xc_s[:, SSM_WIDTH:SSM_WIDTH + SSM_GROUPS * D_STATE]
        c_all = xc_s[:, SSM_WIDTH + SSM_GROUPS * D_STATE:CONV_DIM]
        bb_s[...] = b_all.astype(BF16)

        dtc = _dt_cols(hn, wdt_ref, dtb_row_ref)
        a_row = _a_row(alog_row_ref)

        r2 = _iota((m_rows, m_rows), 0)
        c2 = _iota((m_rows, m_rows), 1)
        same = (r2 // tdec) == (c2 // tdec)
        causal = jnp.logical_and(same, c2 <= r2)
        causal_bf = causal.astype(BF16)
        same_bf = same.astype(BF16)

        dac = dtc * a_row
        acol = _cumsum_cols(causal_bf, dac)
        alast = _cumsum_cols(same_bf, dac)
        tailc = jnp.exp(alast - acol) * dtc
        ex = _expand_heads(jnp.concatenate([jnp.exp(acol), tailc], axis=0), expand2_ref)
        ea_s[...] = jnp.exp(alast)
        eax_s[...] = ex[:m_rows]
        acol2, arow2 = _log2_decay(acol)
        y_s[...] = _ssd_block(xs.astype(BF16), b_all, c_all, acol2, arow2, _heads_to_rows(dtc), causal)
        xtl = xs * ex[m_rows:]
        for jj in range(SSM_WIDTH // LANES):
            xT_s[jj * LANES:(jj + 1) * LANES, :] = xtl[:, jj * LANES:(jj + 1) * LANES].T.astype(BF16)

    lane_kv = _iota((1, KV_WIDTH), 1) // HEAD_DIM
    row_b = _iota((m_rows, 1), 0) // tdec
    low_half = _iota((1, LANES), 1) < HEAD_DIM
    sink = sink_ref[...]

    pbs = range(pb_n)
    bs = [j * pb_n + pb for pb in pbs]
    rs = [pl.multiple_of(b * tdec, tdec) for b in bs]

    qbd, kc, vc, sc = [], [], [], []
    for pb in pbs:
        q8 = q_s[pl.ds(rs[pb], tdec), :]
        qbd.append(jnp.concatenate(
            [jnp.where(lane_kv == kvh, q8[:, g * KV_WIDTH:(g + 1) * KV_WIDTH], 0.0)
             for g in range(ATT_GROUP) for kvh in range(ATT_KV_HEADS)], axis=0).astype(BF16))
        kc.append(ck_ref[pb])
        vc.append(cv_ref[pb])
    for pb in pbs:
        keys = jnp.concatenate([kc[pb].astype(BF16), knb_s[...]], axis=0)
        sc.append(_nt(qbd[pb], keys) + bias_ref[bs[pb]])
    p, inv = [], []
    for pb in pbs:
        m = jnp.maximum(jnp.max(sc[pb], axis=1, keepdims=True), sink)
        e = jnp.exp(sc[pb] - m)
        inv.append(1.0 / (jnp.sum(e, axis=1, keepdims=True) + jnp.exp(sink - m)))
        p.append(e.astype(BF16))
    for pb in pbs:
        o = _nn(p[pb], jnp.concatenate([vc[pb].astype(BF16), vnb_s[...]], axis=0)) * inv[pb]
        for c_out in range(ATT_WIDTH // LANES):
            kvh, g0 = divmod(2 * c_out, ATT_GROUP)
            halves = []
            for half in range(2):
                i0 = ((g0 + half) * ATT_KV_HEADS + kvh) * tdec
                piece = o[i0:i0 + tdec, (kvh // 2) * LANES:(kvh // 2 + 1) * LANES]
                halves.append(piece if kvh % 2 == half else pltpu.roll(piece, HEAD_DIM, 1))
            mix_s[pl.ds(rs[pb], tdec), c_out * LANES:(c_out + 1) * LANES] = jnp.where(low_half, halves[0], halves[1])
        nk_ref[pb, 0:WINDOW - tdec, :] = kc[pb][tdec:, :]
        nk_ref[pb, WINDOW - tdec:WINDOW, :] = kn_s[pl.ds(rs[pb], tdec), :]
        nv_ref[pb, 0:WINDOW - tdec, :] = vc[pb][tdec:, :]
        nv_ref[pb, WINDOW - tdec:WINDOW, :] = vn_s[pl.ds(rs[pb], tdec), :]

    for g in range(SSM_GROUPS):
        sl = slice(g * SSM_GROUP_W, (g + 1) * SSM_GROUP_W)
        c0 = SSM_WIDTH + (SSM_GROUPS + g) * D_STATE
        hg, upd = [], []
        for pb in pbs:
            hg.append(sssm_ref[pb, sl, :])
            bm = jnp.where(row_b == bs[pb], bb_s[:, g * D_STATE:(g + 1) * D_STATE], jnp.zeros((), BF16))
            upd.append(_nn(xT_s[sl, :], bm))
        for pb in pbs:
            cc = xc_s[pl.ds(rs[pb], tdec), c0:c0 + D_STATE].astype(BF16)
            yoff_s[pl.ds(rs[pb], tdec), sl] = (_nt(cc, hg[pb].astype(BF16))
                                               * eax_s[pl.ds(rs[pb], tdec), sl])
        for pb in pbs:
            ea_b = ea_s[pl.ds(rs[pb], 1), :]
            dec = jnp.concatenate(
                [jnp.broadcast_to(ea_b[:, g * SSM_HPG + e:g * SSM_HPG + e + 1], (SSM_HEAD_DIM, D_STATE))
                 for e in range(SSM_HPG)], axis=0)
            nssm_ref[pb, sl, :] = hg[pb] * dec + upd[pb]

    @pl.when(j == pl.num_programs(1) - 1)
    def _():
        z = _nn(hn_s[...], wr_ref[:, R_Z:R_XBC])
        _gate_and_out((y_s, yoff_s), xc_s, z, x_ref, dskip_ref, gssm_ref, wout_ref, gpost_ref, mix_s, x1_ref)


def _ffn_kernel(xa_ref, xb_ref, gpre_ref, wg_ref, wu_ref, wd_ref, gpost_ref, oa_ref, ob_ref, f_s, *, steps_a):
    def tile(x_ref, o_ref):
        _rms_rows(x_ref, gpre_ref, f_s)
        f = f_s[...]
        d_ff = wg_ref.shape[1]
        acc = jnp.zeros(x_ref.shape, F32)
        for j in range(d_ff // FF_CHUNK):
            sl = slice(j * FF_CHUNK, (j + 1) * FF_CHUNK)
            gate = _nn(f, wg_ref[:, sl])
            up = _nn(f, wu_ref[:, sl])
            acc = acc + _nn((_silu(gate) * up).astype(BF16), wd_ref[sl, :])
        _rms_rows(acc, gpost_ref, o_ref, res_ref=x_ref)

    i = pl.program_id(0)

    @pl.when(i < steps_a)
    def _():
        tile(xa_ref, oa_ref)

    @pl.when(i >= steps_a)
    def _():
        tile(xb_ref, ob_ref)


def _const_spec(shape):
    nd = len(shape)
    return pl.BlockSpec(shape, lambda *_: (0,) * nd, pipeline_mode=pl.Buffered(1))


def _rope_tables(pos):
    half = ROT_DIM // 2
    inv = ROPE_THETA ** (-np.arange(half, dtype=np.float64) * 2.0 / ROT_DIM)
    ang = pos.astype(np.float64)[:, None] * inv[None, :]
    cos = np.cos(ang).astype(np.float32)
    sin = np.sin(ang).astype(np.float32)
    n = pos.shape[0]
    pad = HEAD_DIM - ROT_DIM
    c = np.concatenate([cos, cos, np.ones((n, pad), np.float32)], axis=1)
    s = np.concatenate([-sin, sin, np.zeros((n, pad), np.float32)], axis=1)
    rep = LANES // HEAD_DIM
    return tuple(jnp.asarray(np.tile(t, (1, rep))) for t in (c, s))


def _layer_params(g_pre_mix, w_in, attn_sinks, conv_w, conv_b, dt_bias, a_log, d_skip, g_ssm_out, w_out, g_post_mix):
    wr = w_in.astype(BF16)
    wdt = jnp.pad(w_in[:, R_END:], ((0, 0), (0, LANES - SSM_HEADS))).astype(BF16)
    wout = w_out.astype(BF16)
    pad16 = ((0, 0), (0, LANES - SSM_HEADS))
    expand = (np.arange(LANES)[:, None] == (np.arange(SSM_WIDTH)[None, :] // SSM_HEAD_DIM)).astype(np.float32)
    return dict(
        gpre=g_pre_mix.reshape(1, D_MODEL), wr=wr, wdt=wdt,
        convw=conv_w, convb=conv_b.reshape(1, CONV_DIM),
        dtb_row=jnp.pad(dt_bias.reshape(1, SSM_HEADS), pad16),
        alog_row=jnp.pad(a_log.reshape(1, SSM_HEADS), pad16),
        dskip=jnp.repeat(d_skip, SSM_HEAD_DIM).reshape(1, SSM_WIDTH), gssm=g_ssm_out.reshape(1, SSM_WIDTH),
        wout=wout, gpost=g_post_mix.reshape(1, D_MODEL), expand2=jnp.asarray(np.concatenate([expand, expand], axis=0), BF16),
        sinks_gk=attn_sinks.reshape(ATT_KV_HEADS, ATT_GROUP).T,
    )


_WEIGHT_ORDER = ("gpre", "wr", "wdt")
_TAIL_ORDER = ("convw", "convb", "dtb_row", "alog_row",
               "dskip", "gssm", "wout", "gpost", "expand2")


def _prompt_mixer(x, p):
    bsz, seq, _ = x.shape
    tm = SEQ_TILE
    nchunk = tm // CHUNK
    c, sn = _rope_tables(np.arange(seq))
    sink = jnp.repeat(p["sinks_gk"].T, CHUNK, axis=1)[:, None, :]
    jj = np.arange(2 * CHUNK)[:, None]
    tt = np.arange(CHUNK)[None, :]
    vis = (jj >= tt) & (jj <= tt + WINDOW)
    bias = np.stack([np.where(vis, 0.0, NEG), np.where(vis & (jj >= CHUNK), 0.0, NEG)]).astype(np.float32)
    bias = jnp.asarray(np.tile(bias, (1, 1, ATT_GROUP)))
    consts = [p[n] for n in _WEIGHT_ORDER]
    tail = [p[n] for n in _TAIL_ORDER]
    in_specs = ([pl.BlockSpec((None, tm, D_MODEL), lambda b, s: (b, s, 0))]
                + [_const_spec(a.shape) for a in consts]
                + [pl.BlockSpec((tm, LANES), lambda b, s: (s, 0))] * 2
                + [_const_spec(sink.shape), _const_spec(bias.shape)]
                + [_const_spec(a.shape) for a in tail])
    out_shape = (
        jax.ShapeDtypeStruct((bsz, seq, D_MODEL), F32),
        jax.ShapeDtypeStruct((bsz, WINDOW, KV_WIDTH), F32),
        jax.ShapeDtypeStruct((bsz, WINDOW, KV_WIDTH), F32),
        jax.ShapeDtypeStruct((bsz, CONV_W - 1, CONV_DIM), F32),
        jax.ShapeDtypeStruct((bsz, SSM_WIDTH, D_STATE), F32),
    )
    out_specs = (
        pl.BlockSpec((None, tm, D_MODEL), lambda b, s: (b, s, 0)),
        pl.BlockSpec((None, WINDOW, KV_WIDTH), lambda b, s: (b, 0, 0)),
        pl.BlockSpec((None, WINDOW, KV_WIDTH), lambda b, s: (b, 0, 0)),
        pl.BlockSpec((None, CONV_W - 1, CONV_DIM), lambda b, s: (b, 0, 0)),
        pl.BlockSpec((None, SSM_WIDTH, D_STATE), lambda b, s: (b, 0, 0)),
    )
    scratch = [
        pltpu.VMEM((tm, D_MODEL), BF16),
        pltpu.VMEM((tm, ATT_WIDTH), BF16),
        pltpu.VMEM((CHUNK + tm, KV_WIDTH), BF16),
        pltpu.VMEM((nchunk + 1, KV_WIDTH, CHUNK), BF16),
        pltpu.VMEM((SUBLANES + tm, CONV_DIM), F32),
        pltpu.VMEM((tm, CONV_DIM), F32),
        pltpu.VMEM((tm, LANES), F32),
        pltpu.VMEM((tm, SSM_WIDTH), F32),
        pltpu.VMEM((tm, MIX_WIDTH), BF16),
        pltpu.VMEM((D_STATE, SSM_WIDTH), F32),
        pltpu.VMEM((D_MODEL, ATT_WIDTH), BF16),
        pltpu.VMEM((tm, SSM_WIDTH), F32),
        pltpu.VMEM((tm, D_MODEL), F32),
    ]
    return pl.pallas_call(
        _prompt_kernel,
        grid=(bsz, seq // tm),
        in_specs=in_specs,
        out_specs=out_specs,
        out_shape=out_shape,
        scratch_shapes=scratch,
        compiler_params=pltpu.CompilerParams(
            dimension_semantics=("arbitrary", "arbitrary"), vmem_limit_bytes=VMEM_LIMIT_BYTES),
        name="prompt_mixer",
    )(x, *consts, c, sn, sink, bias, *tail)


def _sample_mixer(x, cache_k, cache_v, state_conv, state_ssm, p):
    nb, tdec, _ = x.shape
    bt = SAMPLE_BT
    rows = bt * tdec
    c, sn = _rope_tables(np.tile(PAST_LEN + np.arange(tdec), bt))
    sink = jnp.repeat(p["sinks_gk"].reshape(-1), tdec).reshape(rows, 1)
    consts = [p[n] for n in _WEIGHT_ORDER]
    tail = [p[n] for n in _TAIL_ORDER]
    x2 = x.reshape(nb * tdec, D_MODEL)
    ck = cache_k.reshape(nb, WINDOW, KV_WIDTH)
    cv = cache_v.reshape(nb, WINDOW, KV_WIDTH)
    ssm = state_ssm.reshape(nb, SSM_WIDTH, D_STATE)
    pb = SAMPLE_PB
    steps = bt // pb
    t_of_row = (np.arange(rows) % tdec)[:, None]
    col = np.arange(rows)[None, :]
    vis_c = col >= t_of_row
    bias = jnp.asarray(np.stack(
        [np.where(np.concatenate([vis_c, (col // tdec == b) & (col % tdec <= t_of_row)], axis=1), 0.0, NEG)
         for b in range(bt)]).astype(np.float32))
    tmap = lambda i, j: (i, 0, 0)
    pmap = lambda i, j: (i * steps + j, 0, 0)
    in_specs = ([pl.BlockSpec((rows, D_MODEL), lambda i, j: (i, 0)),
                 pl.BlockSpec((pb, WINDOW, KV_WIDTH), pmap),
                 pl.BlockSpec((pb, WINDOW, KV_WIDTH), pmap),
                 pl.BlockSpec((bt, CONV_W - 1, CONV_DIM), tmap),
                 pl.BlockSpec((pb, SSM_WIDTH, D_STATE), pmap)]
                + [_const_spec(a.shape) for a in consts]
                + [_const_spec(c.shape)] * 2
                + [_const_spec(sink.shape), _const_spec(bias.shape)]
                + [_const_spec(a.shape) for a in tail])
    out_shape = (
        jax.ShapeDtypeStruct((nb * tdec, D_MODEL), F32),
        jax.ShapeDtypeStruct((nb, WINDOW, KV_WIDTH), F32),
        jax.ShapeDtypeStruct((nb, WINDOW, KV_WIDTH), F32),
        jax.ShapeDtypeStruct((nb, CONV_W - 1, CONV_DIM), F32),
        jax.ShapeDtypeStruct((nb, SSM_WIDTH, D_STATE), F32),
    )
    out_specs = (
        pl.BlockSpec((rows, D_MODEL), lambda i, j: (i, 0)),
        pl.BlockSpec((pb, WINDOW, KV_WIDTH), pmap),
        pl.BlockSpec((pb, WINDOW, KV_WIDTH), pmap),
        pl.BlockSpec((bt, CONV_W - 1, CONV_DIM), tmap),
        pl.BlockSpec((pb, SSM_WIDTH, D_STATE), pmap),
    )
    scratch = [
        pltpu.VMEM((rows, D_MODEL), BF16),
        pltpu.VMEM((rows, ATT_WIDTH), F32),
        pltpu.VMEM((rows, KV_WIDTH), F32),
        pltpu.VMEM((rows, KV_WIDTH), F32),
        pltpu.VMEM((rows, KV_WIDTH), BF16),
        pltpu.VMEM((rows, KV_WIDTH), BF16),
        pltpu.VMEM((bt, 2 * SUBLANES, CONV_DIM), F32),
        pltpu.VMEM((rows, CONV_DIM), F32),
        pltpu.VMEM((rows, SSM_GROUPS * D_STATE), BF16),
        pltpu.VMEM((rows, LANES), F32),
        pltpu.VMEM((rows, SSM_WIDTH), F32),
        pltpu.VMEM((SSM_WIDTH, rows), BF16),
        pltpu.VMEM((rows, SSM_WIDTH), F32),
        pltpu.VMEM((rows, SSM_WIDTH), F32),
        pltpu.VMEM((rows, MIX_WIDTH), F32),
        pltpu.VMEM((D_MODEL, ATT_WIDTH), BF16),
    ]
    return pl.pallas_call(
        _sample_kernel,
        grid=(nb // bt, steps),
        in_specs=in_specs,
        out_specs=out_specs,
        out_shape=out_shape,
        scratch_shapes=scratch,
        compiler_params=pltpu.CompilerParams(
            dimension_semantics=("arbitrary", "arbitrary"), vmem_limit_bytes=VMEM_LIMIT_BYTES),
        name="sample_mixer",
    )(x2, ck, cv, state_conv, ssm, *consts, c, sn, sink, bias, *tail)


def _ffn(xa, xb, gpre, wg, wu, wd, gpost):
    tf = FFN_TILE
    steps_a, steps_b = xa.shape[0] // tf, xb.shape[0] // tf
    consts = [gpre, wg, wu, wd, gpost]
    amap = lambda i: (jnp.minimum(i, steps_a - 1), 0)
    bmap = lambda i: (jnp.maximum(i - steps_a, 0), 0)
    return pl.pallas_call(
        functools.partial(_ffn_kernel, steps_a=steps_a),
        grid=(steps_a + steps_b,),
        in_specs=([pl.BlockSpec((tf, D_MODEL), amap), pl.BlockSpec((tf, D_MODEL), bmap)]
                  + [_const_spec(a.shape) for a in consts]),
        out_specs=(pl.BlockSpec((tf, D_MODEL), amap), pl.BlockSpec((tf, D_MODEL), bmap)),
        out_shape=(jax.ShapeDtypeStruct(xa.shape, F32), jax.ShapeDtypeStruct(xb.shape, F32)),
        scratch_shapes=[pltpu.VMEM((tf, D_MODEL), BF16)],
        compiler_params=pltpu.CompilerParams(
            dimension_semantics=("arbitrary",), vmem_limit_bytes=VMEM_LIMIT_BYTES),
        name="ffn",
    )(xa, xb, *consts)


def kernel(x_prompt, x_sample, cache_k_win, cache_v_win, state_conv, state_ssm, g_pre_mix, w_in, attn_sinks, conv_w, conv_b, dt_bias, a_log, d_skip, g_ssm_out, w_out, g_post_mix, g_pre_ffn, w_gate, w_up, w_down, g_post_ffn):
    depth = w_in.shape[0]
    bp, lp, _ = x_prompt.shape
    nb, ts, _ = x_sample.shape
    hp, hs = x_prompt, x_sample
    outs = [[] for _ in range(8)]
    for l in range(depth):
        p = _layer_params(g_pre_mix[l], w_in[l], attn_sinks[l], conv_w[l], conv_b[l], dt_bias[l], a_log[l],
                          d_skip[l], g_ssm_out[l], w_out[l], g_post_mix[l])
        ffn_w = (g_pre_ffn[l].reshape(1, D_MODEL), w_gate[l].astype(BF16), w_up[l].astype(BF16),
                 w_down[l].astype(BF16), g_post_ffn[l].reshape(1, D_MODEL))
        x1p, kp, vp, cp, sp = _prompt_mixer(hp, p)
        x1s, ksm, vsm, csm, ssm = _sample_mixer(hs, cache_k_win[l], cache_v_win[l], state_conv[l], state_ssm[l], p)
        hp, hs = _ffn(x1p.reshape(bp * lp, D_MODEL), x1s, *ffn_w)
        hp = hp.reshape(bp, lp, D_MODEL)
        hs = hs.reshape(nb, ts, D_MODEL)
        kv_shape = (WINDOW, ATT_KV_HEADS, HEAD_DIM)
        ssm_shape = (SSM_HEADS, SSM_HEAD_DIM, D_STATE)
        for lst, val in zip(outs, (kp.reshape((bp,) + kv_shape), vp.reshape((bp,) + kv_shape), cp,
                                   sp.reshape((bp,) + ssm_shape),
                                   ksm.reshape((nb,) + kv_shape), vsm.reshape((nb,) + kv_shape), csm,
                                   ssm.reshape((nb,) + ssm_shape))):
            lst.append(val)
    return (hp, hs) + tuple(jnp.stack(o) for o in outs)
```

```python
import functools
import math

import numpy as np
import jax
import jax.numpy as jnp
from jax import lax
from jax.experimental import pallas as pl
from jax.experimental.pallas import tpu as pltpu

F32 = jnp.float32
BF16 = jnp.bfloat16

D_MODEL = 1024
ATT_HEADS = 16
ATT_KV_HEADS = 4
ATT_GROUP = ATT_HEADS // ATT_KV_HEADS
HEAD_DIM = 64
ATT_WIDTH = ATT_HEADS * HEAD_DIM
KV_WIDTH = ATT_KV_HEADS * HEAD_DIM
WINDOW = 128
ROT_DIM = HEAD_DIM // 4
ROPE_THETA = 500000.0
SSM_HEADS = 16
SSM_HEAD_DIM = 64
SSM_WIDTH = SSM_HEADS * SSM_HEAD_DIM
SSM_GROUPS = 2
SSM_HPG = SSM_HEADS // SSM_GROUPS
SSM_GROUP_W = SSM_WIDTH // SSM_GROUPS
D_STATE = 128
CONV_W = 4
CONV_DIM = SSM_WIDTH + 2 * SSM_GROUPS * D_STATE
MIX_WIDTH = ATT_WIDTH + SSM_WIDTH
EPS = 1e-6
PAST_LEN = 8192

LANES = 128
SUBLANES = 8
VMEM_LIMIT_BYTES = 60 * 1024 * 1024

CHUNK = 128
NEG = -1e30
LOG2E = math.log2(math.e)
Q_SCALE = HEAD_DIM ** -0.5 * LOG2E
SEQ_TILE = 512
SAMPLE_BT = 16
SAMPLE_PB = 4
FFN_TILE = 512
PROJ_ROWS = 256
OUT_CB = 512
ROW_BLOCK = 32
FF_CHUNK = 256


def _nn(a, b):
    return jnp.dot(a, b, preferred_element_type=F32)


def _nt(a, b):
    return lax.dot_general(a, b, (((1,), (1,)), ((), ())), preferred_element_type=F32)


def _split_bf16(x, n):
    parts = []
    r = x
    for i in range(n):
        p = r.astype(BF16)
        parts.append(p)
        if i + 1 < n:
            r = r - p.astype(F32)
    return parts


def _expand_heads(x, expand2_ref):
    hi, mid = _split_bf16(x, 2)
    return _nn(jnp.concatenate([hi, mid], axis=1), expand2_ref[...])


def _cumsum_cols(m01, x):
    w = x.shape[1]
    r = _nn(m01, jnp.concatenate(_split_bf16(x, 3), axis=1))
    return r[:, :w] + r[:, w:2 * w] + r[:, 2 * w:]


def _heads_to_rows(x):
    return x.T[:SSM_HEADS]


def _rms(x, g):
    ms = jnp.mean(x * x, axis=-1, keepdims=True)
    return x * lax.rsqrt(ms + EPS) * g


def _rms_rows(src, g_ref, dst_ref, res_ref=None):
    g = g_ref[...]
    for r0 in range(0, dst_ref.shape[0], ROW_BLOCK):
        rs = slice(r0, r0 + ROW_BLOCK)
        y = _rms(src[rs, :], g)
        if res_ref is not None:
            y = res_ref[rs, :] + y
        dst_ref[rs, :] = y.astype(dst_ref.dtype)


def _silu(x):
    h = 0.5 * x
    return h + h * jnp.tanh(h)


R_K = ATT_WIDTH
R_V = R_K + KV_WIDTH
R_Z = R_V + KV_WIDTH
R_XBC = R_Z + SSM_WIDTH
R_END = R_XBC + CONV_DIM

def _conv4(x, w, bias):
    assert CONV_W == 4
    u = pltpu.roll(x, 2, 0)
    return bias + (x * w[3] + u * w[1]) + pltpu.roll(x * w[2] + u * w[0], 1, 0)


CONV_CB = 256
CONV_RB = 64


def _conv_silu_cols(xpad_ref, row0, ra, rb, convw_ref, convb_ref, out_ref, c0):
    cs = slice(c0, c0 + CONV_CB)
    w = [convw_ref[i:i + 1, cs] for i in range(CONV_W)]
    bias = convb_ref[:, cs]
    for r0 in range(ra, rb, CONV_RB):
        xh = xpad_ref[row0 + r0 - SUBLANES:row0 + r0 + CONV_RB, cs]
        out_ref[r0:r0 + CONV_RB, cs] = _silu(_conv4(xh, w, bias)[SUBLANES:])


def _softplus(x):
    return jnp.maximum(x, 0.0) + jnp.log1p(jnp.exp(-jnp.abs(x)))


def _rope(x, c, s):
    lane = lax.broadcasted_iota(jnp.int32, (1, LANES), 1)
    first = (lane % HEAD_DIM) < ROT_DIM // 2
    outs = []
    for j in range(x.shape[1] // LANES):
        xb = x[:, j * LANES:(j + 1) * LANES]
        partner = jnp.where(first, pltpu.roll(xb, LANES - ROT_DIM // 2, 1), pltpu.roll(xb, ROT_DIM // 2, 1))
        outs.append(xb * c + partner * s)
    return outs[0] if len(outs) == 1 else jnp.concatenate(outs, axis=1)


def _iota(shape, dim):
    return lax.broadcasted_iota(jnp.int32, shape, dim)


def _head_blocks(c_out):
    out = []
    for half in range(2):
        g, kvh = divmod(2 * c_out + half, ATT_KV_HEADS)
        b_in = kvh * ATT_GROUP + g
        out.append((b_in // 2, b_in % 2))
    return out


def _permute_q_weight(wr_ref, wq_s):
    low = _iota((1, LANES), 1) < HEAD_DIM
    for c_out in range(ATT_WIDTH // LANES):
        halves = []
        for half, (c_in, src_half) in enumerate(_head_blocks(c_out)):
            col = wr_ref[:, c_in * LANES:(c_in + 1) * LANES]
            halves.append(col if src_half == half else pltpu.roll(col, HEAD_DIM, 1))
        wq_s[:, c_out * LANES:(c_out + 1) * LANES] = jnp.where(low, halves[0], halves[1])


def _project(hn, wq_ref, wr_ref, ropec, ropes):
    q = _rope(_nn(hn, wq_ref[...]), ropec, ropes) * (HEAD_DIM ** -0.5)
    kv = _nn(hn, wr_ref[:, R_K:R_Z])
    k = _rope(kv[:, :KV_WIDTH], ropec, ropes)
    v = kv[:, KV_WIDTH:]
    return q, k, v


def _dt_cols(hn, wdt_ref, dtb_row_ref):
    return _softplus(_nn(hn, wdt_ref[...]) + dtb_row_ref[...])


def _a_row(alog_row_ref):
    lane = _iota((1, LANES), 1)
    return jnp.where(lane < SSM_HEADS, -jnp.exp(alog_row_ref[...]), 0.0)


def _log2_decay(acol):
    acol2 = acol * LOG2E
    return acol2, _heads_to_rows(acol2)


SSD_QUAD = 4
SSD_NQUAD = SSM_HEADS // SSD_QUAD


def _ssd_cb(b_all, c_all):
    return [_nt(c_all[:, g * D_STATE:(g + 1) * D_STATE].astype(BF16),
                b_all[:, g * D_STATE:(g + 1) * D_STATE].astype(BF16)) for g in range(SSM_GROUPS)]


def _ssd_quad(qi, xs_bf, c_all, cbs, acol2, arow2, dtr, mask_bool, hT_bf=None):
    lane4 = _iota((1, SSD_QUAD * SSM_HEAD_DIM), 1) // SSM_HEAD_DIM
    zero = jnp.zeros((), BF16)
    e0 = SSD_QUAD * qi
    g = e0 // SSM_HPG
    cf = c_all[:, g * D_STATE:(g + 1) * D_STATE]
    lanes = slice(e0 * SSM_HEAD_DIM, (e0 + SSD_QUAD) * SSM_HEAD_DIM)
    xq = xs_bf[:, lanes]
    lhs, rhs = [], []
    for i in range(SSD_QUAD):
        e = e0 + i
        a_t = jnp.broadcast_to(acol2[:, e:e + 1], (CHUNK, CHUNK))
        w = cbs[g] * jnp.exp2(jnp.where(mask_bool, a_t - arow2[e:e + 1, :], NEG)) * dtr[e:e + 1, :]
        lhs.append(w.astype(BF16))
        rhs.append(jnp.where(lane4 == i, xq, zero))
        if hT_bf is not None:
            lhs.append((cf * jnp.exp2(a_t)).astype(BF16))
            rhs.append(jnp.where(lane4 == i, hT_bf[:, lanes], zero))
    return _nn(jnp.concatenate(lhs, axis=1), jnp.concatenate(rhs, axis=0))


def _ssd_block(xs_bf, b_all, c_all, acol2, arow2, dtr, mask_bool):
    cbs = _ssd_cb(b_all, c_all)
    return jnp.concatenate([_ssd_quad(qi, xs_bf, c_all, cbs, acol2, arow2, dtr, mask_bool)
                            for qi in range(SSD_NQUAD)], axis=1)


def _gate_rows(rs, y_refs, xc_s, z, dskip_ref, gssm_ref, mix_s):
    y = y_refs[0][rs, :]
    for extra in y_refs[1:]:
        y = y + extra[rs, :]
    gated = (y + dskip_ref[...] * xc_s[rs, 0:SSM_WIDTH]) * _silu(z[rs, :])
    for g in range(SSM_GROUPS):
        gs = slice(g * SSM_GROUP_W, (g + 1) * SSM_GROUP_W)
        gg = gated[:, gs]
        ms = jnp.mean(gg * gg, axis=-1, keepdims=True)
        o = gg * lax.rsqrt(ms + EPS) * gssm_ref[:, gs]
        mix_s[rs, ATT_WIDTH + g * SSM_GROUP_W:ATT_WIDTH + (g + 1) * SSM_GROUP_W] = o.astype(mix_s.dtype)


def _gate_and_out(y_refs, xc_s, z, x_ref, dskip_ref, gssm_ref, wout_ref, gpost_ref, mix_s, out_ref):
    for r0 in range(0, out_ref.shape[0], ROW_BLOCK):
        _gate_rows(slice(r0, r0 + ROW_BLOCK), y_refs, xc_s, z, dskip_ref, gssm_ref, mix_s)
    mo = _nn(mix_s[...].astype(BF16), wout_ref[...])
    _rms_rows(mo, gpost_ref, out_ref, res_ref=x_ref)


def _prompt_kernel(x_ref, xn_ref, gpre_ref, wr_ref, wdt_ref,
                   ropec_ref, ropes_ref, sink_ref, biasT_ref,
                   convw_ref, convb_ref, dtb_row_ref, alog_row_ref,
                   dskip_ref, gssm_ref, wout_ref, gpost_ref, expand2_ref,
                   x1_ref, nk_ref, nv_ref, nconv_ref, nssm_ref,
                   hn_s, q_s, kbuf, vT_s, xbc_s, xc_s, dtc_s, y_s, mix_s, hT_s, wq_s, z_s, mo_s):
    tm = x_ref.shape[0]
    nchunk = tm // CHUNK
    s = pl.program_id(1)
    last = pl.num_programs(1) - 1

    def norm_pieces(src_ref, ra, rb):
        def block(r0):
            hn_s[ra + r0:ra + r0 + ROW_BLOCK, :] = _rms(src_ref[r0:r0 + ROW_BLOCK, :], gpre_ref[...]).astype(BF16)
        return [functools.partial(block, r0) for r0 in range(0, rb - ra, ROW_BLOCK)]

    def projection_pieces(ra, rb, rope_row0):
        rows = slice(ra, rb)

        def rope(x):
            start = rope_row0 + ra
            if not isinstance(start, int):
                start = pl.multiple_of(start, PROJ_ROWS)
            pos = pl.ds(start, rb - ra)
            return _rope(x, ropec_ref[pos, :], ropes_ref[pos, :])

        def proj_xbc(c0):
            xbc_s[SUBLANES + ra:SUBLANES + rb, c0:c0 + CONV_CB] = _nn(
                hn_s[rows, :], wr_ref[:, R_XBC + c0:R_XBC + c0 + CONV_CB])

        def conv(c0):
            _conv_silu_cols(xbc_s, SUBLANES, ra, rb, convw_ref, convb_ref, xc_s, c0)

        def proj_q(c0):
            q_s[rows, c0:c0 + KV_WIDTH] = (
                rope(_nn(hn_s[rows, :], wq_s[:, c0:c0 + KV_WIDTH])) * Q_SCALE).astype(BF16)

        def proj_k():
            kbuf[CHUNK + ra:CHUNK + rb, :] = rope(_nn(hn_s[rows, :], wr_ref[:, R_K:R_V])).astype(BF16)

        def proj_v_dt():
            v = _nn(hn_s[rows, :], wr_ref[:, R_V:R_Z])
            for j in range(ra // CHUNK, rb // CHUNK):
                vT_s[1 + j] = v[j * CHUNK - ra:(j + 1) * CHUNK - ra, :].T.astype(BF16)
            dtc_s[rows, :] = _dt_cols(hn_s[rows, :], wdt_ref, dtb_row_ref)

        others = [functools.partial(proj_q, c0) for c0 in range(0, ATT_WIDTH, KV_WIDTH)] + [proj_k, proj_v_dt]
        conv_cols = list(range(0, CONV_DIM, CONV_CB))
        pieces = [functools.partial(proj_xbc, conv_cols[0])]
        for n, c0 in enumerate(conv_cols):
            if n + 1 < len(conv_cols):
                pieces.append(functools.partial(proj_xbc, conv_cols[n + 1]))
            pieces.append(functools.partial(conv, c0))
            if n < len(others):
                pieces.append(others[n])
        return pieces + others[len(conv_cols):]

    a_row = _a_row(alog_row_ref)

    r2 = _iota((CHUNK, CHUNK), 0)
    c2 = _iota((CHUNK, CHUNK), 1)
    tril = c2 <= r2
    tril_bf = tril.astype(BF16)
    lane_kv = _iota((1, KV_WIDTH), 1) // HEAD_DIM
    ones_rows = jnp.ones((2 * SUBLANES, 2 * CHUNK), BF16)

    def chunk_body(c, extras):
        r0 = c * CHUNK
        per = -(-len(extras) // ATT_KV_HEADS)
        first = jnp.logical_and(s == 0, c == 0).astype(jnp.int32)
        bias = biasT_ref[first]
        qcat = jnp.concatenate([q_s[pl.ds(r0, CHUNK), g * KV_WIDTH:(g + 1) * KV_WIDTH]
                                for g in range(ATT_GROUP)], axis=0)
        kwin = kbuf[pl.ds(r0, 2 * CHUNK), :]
        kstack = jnp.concatenate([jnp.where(lane_kv == kvh, kwin, jnp.zeros((), BF16))
                                  for kvh in range(ATT_KV_HEADS)], axis=0)
        sT = _nt(kstack, qcat)
        vT_win = jnp.concatenate([vT_s[c], vT_s[c + 1]], axis=1)

        dtc_c = dtc_s[pl.ds(r0, CHUNK), :]
        acol = _cumsum_cols(tril_bf, dtc_c * a_row)
        a_end = acol[CHUNK - 1:CHUNK, :]
        tailc = jnp.exp(a_end - acol) * dtc_c
        ex = _expand_heads(
            jnp.concatenate([tailc, jnp.broadcast_to(jnp.exp(a_end), (SUBLANES, LANES))], axis=0), expand2_ref)
        tlx = ex[:CHUNK]
        dec_row = ex[CHUNK:CHUNK + 1]
        xs = xc_s[pl.ds(r0, CHUNK), 0:SSM_WIDTH]
        b_all = xc_s[pl.ds(r0, CHUNK), SSM_WIDTH:SSM_WIDTH + SSM_GROUPS * D_STATE]
        c_all = xc_s[pl.ds(r0, CHUNK), SSM_WIDTH + SSM_GROUPS * D_STATE:CONV_DIM]
        hT = hT_s[...]
        acol2, arow2 = _log2_decay(acol)
        dtr_c = _heads_to_rows(dtc_c)
        xs_bf = xs.astype(BF16)
        hT_bf = hT.astype(BF16)
        cbs = _ssd_cb(b_all, c_all)

        o_rows = []
        for i in range(ATT_KV_HEADS):
            blk = sT[i * 2 * CHUNK:(i + 1) * 2 * CHUNK] + bias
            sink = sink_ref[i] * LOG2E
            m = jnp.maximum(jnp.max(blk, axis=0, keepdims=True), sink)
            p = jnp.exp2(blk - m).astype(BF16)
            lhs = jnp.concatenate([vT_win[i * HEAD_DIM:(i + 1) * HEAD_DIM], ones_rows], axis=0)
            oT = _nn(lhs, p)
            den = oT[HEAD_DIM:HEAD_DIM + 1] + jnp.exp2(sink - m)
            o_rows.append(oT[:HEAD_DIM] * (1.0 / den))
            for qi in range(i * SSD_NQUAD // ATT_KV_HEADS, (i + 1) * SSD_NQUAD // ATT_KV_HEADS):
                lanes = slice(qi * SSD_QUAD * SSM_HEAD_DIM, (qi + 1) * SSD_QUAD * SSM_HEAD_DIM)
                y_s[pl.ds(r0, CHUNK), lanes] = _ssd_quad(qi, xs_bf, c_all, cbs, acol2, arow2, dtr_c, tril, hT_bf)
            for t in extras[i * per:(i + 1) * per]:
                t()
        for c_out in range(ATT_WIDTH // LANES):
            kvh, g0 = divmod(2 * c_out, ATT_GROUP)
            two = jnp.concatenate([o_rows[kvh][:, g * CHUNK:(g + 1) * CHUNK] for g in (g0, g0 + 1)], axis=0)
            mix_s[pl.ds(r0, CHUNK), c_out * LANES:(c_out + 1) * LANES] = two.T.astype(BF16)

        xtl = (xs * tlx).astype(BF16)
        for g in range(SSM_GROUPS):
            sl = slice(g * SSM_GROUP_W, (g + 1) * SSM_GROUP_W)
            bt = b_all[:, g * D_STATE:(g + 1) * D_STATE].T.astype(BF16)
            hT_s[:, sl] = hT[:, sl] * dec_row[:, sl] + _nn(bt, xtl[:, sl])

    def output_pieces(ra, rb):
        rows = slice(ra, rb)
        g_post = gpost_ref[...]

        def proj_z(c0):
            z_s[rows, c0:c0 + OUT_CB] = _nn(hn_s[rows, :], wr_ref[:, R_Z + c0:R_Z + c0 + OUT_CB])

        def gate(r0):
            _gate_rows(slice(r0, r0 + ROW_BLOCK), (y_s,), xc_s, z_s, dskip_ref, gssm_ref, mix_s)

        def proj_out(c0):
            mo_s[rows, c0:c0 + OUT_CB] = _nn(mix_s[rows, :], wout_ref[:, c0:c0 + OUT_CB])

        def post(r0):
            rs = slice(r0, r0 + ROW_BLOCK)
            x1_ref[rs, :] = x_ref[rs, :] + _rms(mo_s[rs, :], g_post)

        return ([functools.partial(proj_z, c0) for c0 in range(0, SSM_WIDTH, OUT_CB)]
                + [functools.partial(gate, r0) for r0 in range(ra, rb, ROW_BLOCK)]
                + [functools.partial(proj_out, c0) for c0 in range(0, D_MODEL, OUT_CB)]
                + [functools.partial(post, r0) for r0 in range(ra, rb, ROW_BLOCK)])

    groups = list(range(0, tm, PROJ_ROWS))
    cpg = PROJ_ROWS // CHUNK
    seq0 = s * tm

    @pl.when(jnp.logical_and(pl.program_id(0) == 0, s == 0))
    def _():
        _permute_q_weight(wr_ref, wq_s)
        kbuf[0:CHUNK, :] = jnp.zeros((CHUNK, KV_WIDTH), BF16)
        vT_s[0] = jnp.zeros((KV_WIDTH, CHUNK), BF16)
        xbc_s[0:SUBLANES, :] = jnp.zeros((SUBLANES, CONV_DIM), F32)
        for t in norm_pieces(x_ref, 0, PROJ_ROWS) + projection_pieces(0, PROJ_ROWS, 0):
            t()

    @pl.when(s == 0)
    def _():
        hT_s[...] = jnp.zeros_like(hT_s)

    for t in norm_pieces(x_ref.at[pl.ds(PROJ_ROWS, tm - PROJ_ROWS)], PROJ_ROWS, tm):
        t()
    for gi, ra in enumerate(groups):
        side = projection_pieces(ra + PROJ_ROWS, ra + 2 * PROJ_ROWS, seq0) if gi + 1 < len(groups) else []
        if gi > 0:
            side = side + output_pieces(ra - PROJ_ROWS, ra)
        share = -(-len(side) // cpg)
        for k in range(cpg):
            chunk_body(ra // CHUNK + k, side[k * share:(k + 1) * share])

    kbuf[0:CHUNK, :] = kbuf[tm:tm + CHUNK, :]
    vT_s[0] = vT_s[nchunk]
    xbc_s[0:SUBLANES, :] = jnp.where(s == last, 0.0, xbc_s[tm:tm + SUBLANES, :])

    next_seq0 = jnp.where(s == last, 0, seq0 + tm)
    tail = output_pieces(tm - PROJ_ROWS, tm)
    head = norm_pieces(xn_ref, 0, PROJ_ROWS) + projection_pieces(0, PROJ_ROWS, next_seq0)
    per = -(-len(head) // len(tail))
    for n, t in enumerate(tail):
        t()
        for u in head[n * per:(n + 1) * per]:
            u()

    @pl.when(s == last)
    def _():
        hn_w = hn_s[tm - WINDOW:, :]
        pos = pl.ds(pl.multiple_of(seq0 + tm - WINDOW, WINDOW), WINDOW)
        nk_ref[...] = _rope(_nn(hn_w, wr_ref[:, R_K:R_V]), ropec_ref[pos, :], ropes_ref[pos, :])
        nv_ref[...] = _nn(hn_w, wr_ref[:, R_V:R_Z])
        nconv_ref[...] = xbc_s[SUBLANES + tm - (CONV_W - 1):SUBLANES + tm, :]
        nssm_ref[...] = hT_s[...].T


def _sample_kernel(x_ref, ck_ref, cv_ref, sconv_ref, sssm_ref,
                   gpre_ref, wr_ref, wdt_ref,
                   ropec_ref, ropes_ref, sink_ref, bias_ref,
                   convw_ref, convb_ref, dtb_row_ref, alog_row_ref,
                   dskip_ref, gssm_ref, wout_ref, gpost_ref, expand2_ref,
                   x1_ref, nk_ref, nv_ref, nconv_ref, nssm_ref,
                   hn_s, q_s, kn_s, vn_s, knb_s, vnb_s, xpad_s, xc_s, bb_s, ea_s, eax_s, xT_s, y_s, yoff_s, mix_s, wq_s):
    bt_n = sconv_ref.shape[0]
    pb_n = ck_ref.shape[0]
    m_rows = x_ref.shape[0]
    tdec = m_rows // bt_n
    j = pl.program_id(1)

    @pl.when(jnp.logical_and(pl.program_id(0) == 0, j == 0))
    def _():
        _permute_q_weight(wr_ref, wq_s)

    @pl.when(j == 0)
    def _():
        _rms_rows(x_ref, gpre_ref, hn_s)
        hn = hn_s[...]
        q, k, v = _project(hn, wq_s, wr_ref, ropec_ref[...], ropes_ref[...])
        q_s[...] = q
        kn_s[...] = k
        vn_s[...] = v
        knb_s[...] = k.astype(BF16)
        vnb_s[...] = v.astype(BF16)

        xbc = _nn(hn, wr_ref[:, R_XBC:R_END])
        xpad_s[:, 0:SUBLANES - 3, :] = jnp.zeros((bt_n, SUBLANES - 3, CONV_DIM), F32)
        xpad_s[:, SUBLANES - 3:SUBLANES, :] = sconv_ref[...]
        xpad_s[:, SUBLANES:2 * SUBLANES, :] = xbc.reshape(bt_n, tdec, CONV_DIM)
        nconv_ref[...] = xpad_s[:, 2 * SUBLANES - 3:2 * SUBLANES, :]
        for c0 in range(0, CONV_DIM, CONV_CB):
            cs = slice(c0, c0 + CONV_CB)
            xh = xpad_s[:, :, cs].reshape(bt_n * 2 * SUBLANES, CONV_CB)
            acc = _conv4(xh, [convw_ref[i:i + 1, cs] for i in range(CONV_W)], convb_ref[:, cs])
            xc_s[:, cs] = _silu(acc.reshape(bt_n, 2 * SUBLANES, CONV_CB)[:, SUBLANES:, :].reshape(m_rows, CONV_CB))
        xs = xc_s[:, 0:SSM_WIDTH]
        b_all = xc_s[:, SSM_WIDTH:SSM_WIDTH + SSM_GROUPS * D_STATE]
        c_all = xc_s[:, SSM_WIDTH + SSM_GROUPS * D_STATE:CONV_DIM]
        bb_s[...] = b_all.astype(BF16)

        dtc = _dt_cols(hn, wdt_ref, dtb_row_ref)
        a_row = _a_row(alog_row_ref)

        r2 = _iota((m_rows, m_rows), 0)
        c2 = _iota((m_rows, m_rows), 1)
        same = (r2 // tdec) == (c2 // tdec)
        causal = jnp.logical_and(same, c2 <= r2)
        causal_bf = causal.astype(BF16)
        same_bf = same.astype(BF16)

        dac = dtc * a_row
        acol = _cumsum_cols(causal_bf, dac)
        alast = _cumsum_cols(same_bf, dac)
        tailc = jnp.exp(alast - acol) * dtc
        ex = _expand_heads(jnp.concatenate([jnp.exp(acol), tailc], axis=0), expand2_ref)
        ea_s[...] = jnp.exp(alast)
        eax_s[...] = ex[:m_rows]
        acol2, arow2 = _log2_decay(acol)
        y_s[...] = _ssd_block(xs.astype(BF16), b_all, c_all, acol2, arow2, _heads_to_rows(dtc), causal)
        xtl = xs * ex[m_rows:]
        for jj in range(SSM_WIDTH // LANES):
            xT_s[jj * LANES:(jj + 1) * LANES, :] = xtl[:, jj * LANES:(jj + 1) * LANES].T.astype(BF16)

    lane_kv = _iota((1, KV_WIDTH), 1) // HEAD_DIM
    row_b = _iota((m_rows, 1), 0) // tdec
    low_half = _iota((1, LANES), 1) < HEAD_DIM
    sink = sink_ref[...]

    pbs = range(pb_n)
    bs = [j * pb_n + pb for pb in pbs]
    rs = [pl.multiple_of(b * tdec, tdec) for b in bs]

    qbd, kc, vc, sc = [], [], [], []
    for pb in pbs:
        q8 = q_s[pl.ds(rs[pb], tdec), :]
        qbd.append(jnp.concatenate(
            [jnp.where(lane_kv == kvh, q8[:, g * KV_WIDTH:(g + 1) * KV_WIDTH], 0.0)
             for g in range(ATT_GROUP) for kvh in range(ATT_KV_HEADS)], axis=0).astype(BF16))
        kc.append(ck_ref[pb])
        vc.append(cv_ref[pb])
    for pb in pbs:
        keys = jnp.concatenate([kc[pb].astype(BF16), knb_s[...]], axis=0)
        sc.append(_nt(qbd[pb], keys) + bias_ref[bs[pb]])
    p, inv = [], []
    for pb in pbs:
        m = jnp.maximum(jnp.max(sc[pb], axis=1, keepdims=True), sink)
        e = jnp.exp(sc[pb] - m)
        inv.append(1.0 / (jnp.sum(e, axis=1, keepdims=True) + jnp.exp(sink - m)))
        p.append(e.astype(BF16))
    for pb in pbs:
        o = _nn(p[pb], jnp.concatenate([vc[pb].astype(BF16), vnb_s[...]], axis=0)) * inv[pb]
        for c_out in range(ATT_WIDTH // LANES):
            kvh, g0 = divmod(2 * c_out, ATT_GROUP)
            halves = []
            for half in range(2):
                i0 = ((g0 + half) * ATT_KV_HEADS + kvh) * tdec
                piece = o[i0:i0 + tdec, (kvh // 2) * LANES:(kvh // 2 + 1) * LANES]
                halves.append(piece if kvh % 2 == half else pltpu.roll(piece, HEAD_DIM, 1))
            mix_s[pl.ds(rs[pb], tdec), c_out * LANES:(c_out + 1) * LANES] = jnp.where(low_half, halves[0], halves[1])
        nk_ref[pb, 0:WINDOW - tdec, :] = kc[pb][tdec:, :]
        nk_ref[pb, WINDOW - tdec:WINDOW, :] = kn_s[pl.ds(rs[pb], tdec), :]
        nv_ref[pb, 0:WINDOW - tdec, :] = vc[pb][tdec:, :]
        nv_ref[pb, WINDOW - tdec:WINDOW, :] = vn_s[pl.ds(rs[pb], tdec), :]

    for g in range(SSM_GROUPS):
        sl = slice(g * SSM_GROUP_W, (g + 1) * SSM_GROUP_W)
        c0 = SSM_WIDTH + (SSM_GROUPS + g) * D_STATE
        hg, upd = [], []
        for pb in pbs:
            hg.append(sssm_ref[pb, sl, :])
            bm = jnp.where(row_b == bs[pb], bb_s[:, g * D_STATE:(g + 1) * D_STATE], jnp.zeros((), BF16))
            upd.append(_nn(xT_s[sl, :], bm))
        for pb in pbs:
            cc = xc_s[pl.ds(rs[pb], tdec), c0:c0 + D_STATE].astype(BF16)
            yoff_s[pl.ds(rs[pb], tdec), sl] = (_nt(cc, hg[pb].astype(BF16))
                                               * eax_s[pl.ds(rs[pb], tdec), sl])
        for pb in pbs:
            ea_b = ea_s[pl.ds(rs[pb], 1), :]
            dec = jnp.concatenate(
                [jnp.broadcast_to(ea_b[:, g * SSM_HPG + e:g * SSM_HPG + e + 1], (SSM_HEAD_DIM, D_STATE))
                 for e in range(SSM_HPG)], axis=0)
            nssm_ref[pb, sl, :] = hg[pb] * dec + upd[pb]

    @pl.when(j == pl.num_programs(1) - 1)
    def _():
        z = _nn(hn_s[...], wr_ref[:, R_Z:R_XBC])
        _gate_and_out((y_s, yoff_s), xc_s, z, x_ref, dskip_ref, gssm_ref, wout_ref, gpost_ref, mix_s, x1_ref)


def _ffn_kernel(xa_ref, xb_ref, gpre_ref, wg_ref, wu_ref, wd_ref, gpost_ref, oa_ref, ob_ref, f_s, *, steps_a):
    def tile(x_ref, o_ref):
        _rms_rows(x_ref, gpre_ref, f_s)
        f = f_s[...]
        d_ff = wg_ref.shape[1]
        acc = jnp.zeros(x_ref.shape, F32)
        for j in range(d_ff // FF_CHUNK):
            sl = slice(j * FF_CHUNK, (j + 1) * FF_CHUNK)
            gate = _nn(f, wg_ref[:, sl])
            up = _nn(f, wu_ref[:, sl])
            acc = acc + _nn((_silu(gate) * up).astype(BF16), wd_ref[sl, :])
        _rms_rows(acc, gpost_ref, o_ref, res_ref=x_ref)

    i = pl.program_id(0)

    @pl.when(i < steps_a)
    def _():
        tile(xa_ref, oa_ref)

    @pl.when(i >= steps_a)
    def _():
        tile(xb_ref, ob_ref)


def _const_spec(shape):
    nd = len(shape)
    return pl.BlockSpec(shape, lambda *_: (0,) * nd, pipeline_mode=pl.Buffered(1))


def _rope_tables(pos):
    half = ROT_DIM // 2
    inv = ROPE_THETA ** (-np.arange(half, dtype=np.float64) * 2.0 / ROT_DIM)
    ang = pos.astype(np.float64)[:, None] * inv[None, :]
    cos = np.cos(ang).astype(np.float32)
    sin = np.sin(ang).astype(np.float32)
    n = pos.shape[0]
    pad = HEAD_DIM - ROT_DIM
    c = np.concatenate([cos, cos, np.ones((n, pad), np.float32)], axis=1)
    s = np.concatenate([-sin, sin, np.zeros((n, pad), np.float32)], axis=1)
    rep = LANES // HEAD_DIM
    return tuple(jnp.asarray(np.tile(t, (1, rep))) for t in (c, s))


def _layer_params(g_pre_mix, w_in, attn_sinks, conv_w, conv_b, dt_bias, a_log, d_skip, g_ssm_out, w_out, g_post_mix):
    wr = w_in.astype(BF16)
    wdt = jnp.pad(w_in[:, R_END:], ((0, 0), (0, LANES - SSM_HEADS))).astype(BF16)
    wout = w_out.astype(BF16)
    pad16 = ((0, 0), (0, LANES - SSM_HEADS))
    expand = (np.arange(LANES)[:, None] == (np.arange(SSM_WIDTH)[None, :] // SSM_HEAD_DIM)).astype(np.float32)
    return dict(
        gpre=g_pre_mix.reshape(1, D_MODEL), wr=wr, wdt=wdt,
        convw=conv_w, convb=conv_b.reshape(1, CONV_DIM),
        dtb_row=jnp.pad(dt_bias.reshape(1, SSM_HEADS), pad16),
        alog_row=jnp.pad(a_log.reshape(1, SSM_HEADS), pad16),
        dskip=jnp.repeat(d_skip, SSM_HEAD_DIM).reshape(1, SSM_WIDTH), gssm=g_ssm_out.reshape(1, SSM_WIDTH),
        wout=wout, gpost=g_post_mix.reshape(1, D_MODEL), expand2=jnp.asarray(np.concatenate([expand, expand], axis=0), BF16),
        sinks_gk=attn_sinks.reshape(ATT_KV_HEADS, ATT_GROUP).T,
    )


_WEIGHT_ORDER = ("gpre", "wr", "wdt")
_TAIL_ORDER = ("convw", "convb", "dtb_row", "alog_row",
               "dskip", "gssm", "wout", "gpost", "expand2")


def _prompt_mixer(x, p):
    bsz, seq, _ = x.shape
    tm = SEQ_TILE
    nchunk = tm // CHUNK
    c, sn = _rope_tables(np.arange(seq))
    sink = jnp.repeat(p["sinks_gk"].T, CHUNK, axis=1)[:, None, :]
    jj = np.arange(2 * CHUNK)[:, None]
    tt = np.arange(CHUNK)[None, :]
    vis = (jj >= tt) & (jj <= tt + WINDOW)
    bias = np.stack([np.where(vis, 0.0, NEG), np.where(vis & (jj >= CHUNK), 0.0, NEG)]).astype(np.float32)
    bias = jnp.asarray(np.tile(bias, (1, 1, ATT_GROUP)))
    consts = [p[n] for n in _WEIGHT_ORDER]
    tail = [p[n] for n in _TAIL_ORDER]
    tiles = seq // tm
    gpt = tm // PROJ_ROWS

    def next_group(b, s):
        n = jnp.minimum(b * tiles + s + 1, bsz * tiles - 1)
        return (n // tiles, (n % tiles) * gpt, 0)

    in_specs = ([pl.BlockSpec((None, tm, D_MODEL), lambda b, s: (b, s, 0)),
                 pl.BlockSpec((None, PROJ_ROWS, D_MODEL), next_group)]
                + [_const_spec(a.shape) for a in consts]
                + [_const_spec(c.shape)] * 2
                + [_const_spec(sink.shape), _const_spec(bias.shape)]
                + [_const_spec(a.shape) for a in tail])
    out_shape = (
        jax.ShapeDtypeStruct((bsz, seq, D_MODEL), F32),
        jax.ShapeDtypeStruct((bsz, WINDOW, KV_WIDTH), F32),
        jax.ShapeDtypeStruct((bsz, WINDOW, KV_WIDTH), F32),
        jax.ShapeDtypeStruct((bsz, CONV_W - 1, CONV_DIM), F32),
        jax.ShapeDtypeStruct((bsz, SSM_WIDTH, D_STATE), F32),
    )
    out_specs = (
        pl.BlockSpec((None, tm, D_MODEL), lambda b, s: (b, s, 0)),
        pl.BlockSpec((None, WINDOW, KV_WIDTH), lambda b, s: (b, 0, 0)),
        pl.BlockSpec((None, WINDOW, KV_WIDTH), lambda b, s: (b, 0, 0)),
        pl.BlockSpec((None, CONV_W - 1, CONV_DIM), lambda b, s: (b, 0, 0)),
        pl.BlockSpec((None, SSM_WIDTH, D_STATE), lambda b, s: (b, 0, 0)),
    )
    scratch = [
        pltpu.VMEM((tm, D_MODEL), BF16),
        pltpu.VMEM((tm, ATT_WIDTH), BF16),
        pltpu.VMEM((CHUNK + tm, KV_WIDTH), BF16),
        pltpu.VMEM((nchunk + 1, KV_WIDTH, CHUNK), BF16),
        pltpu.VMEM((SUBLANES + tm, CONV_DIM), F32),
        pltpu.VMEM((tm, CONV_DIM), F32),
        pltpu.VMEM((tm, LANES), F32),
        pltpu.VMEM((tm, SSM_WIDTH), F32),
        pltpu.VMEM((tm, MIX_WIDTH), BF16),
        pltpu.VMEM((D_STATE, SSM_WIDTH), F32),
        pltpu.VMEM((D_MODEL, ATT_WIDTH), BF16),
        pltpu.VMEM((tm, SSM_WIDTH), F32),
        pltpu.VMEM((tm, D_MODEL), F32),
    ]
    return pl.pallas_call(
        _prompt_kernel,
        grid=(bsz, seq // tm),
        in_specs=in_specs,
        out_specs=out_specs,
        out_shape=out_shape,
        scratch_shapes=scratch,
        compiler_params=pltpu.CompilerParams(
            dimension_semantics=("arbitrary", "arbitrary"), vmem_limit_bytes=VMEM_LIMIT_BYTES),
        name="prompt_mixer",
    )(x, x, *consts, c, sn, sink, bias, *tail)


def _sample_mixer(x, cache_k, cache_v, state_conv, state_ssm, p):
    nb, tdec, _ = x.shape
    bt = SAMPLE_BT
    rows = bt * tdec
    c, sn = _rope_tables(np.tile(PAST_LEN + np.arange(tdec), bt))
    sink = jnp.repeat(p["sinks_gk"].reshape(-1), tdec).reshape(rows, 1)
    consts = [p[n] for n in _WEIGHT_ORDER]
    tail = [p[n] for n in _TAIL_ORDER]
    x2 = x.reshape(nb * tdec, D_MODEL)
    ck = cache_k.reshape(nb, WINDOW, KV_WIDTH)
    cv = cache_v.reshape(nb, WINDOW, KV_WIDTH)
    ssm = state_ssm.reshape(nb, SSM_WIDTH, D_STATE)
    pb = SAMPLE_PB
    steps = bt // pb
    t_of_row = (np.arange(rows) % tdec)[:, None]
    col = np.arange(rows)[None, :]
    vis_c = col >= t_of_row
    bias = jnp.asarray(np.stack(
        [np.where(np.concatenate([vis_c, (col // tdec == b) & (col % tdec <= t_of_row)], axis=1), 0.0, NEG)
         for b in range(bt)]).astype(np.float32))
    tmap = lambda i, j: (i, 0, 0)
    pmap = lambda i, j: (i * steps + j, 0, 0)
    in_specs = ([pl.BlockSpec((rows, D_MODEL), lambda i, j: (i, 0)),
                 pl.BlockSpec((pb, WINDOW, KV_WIDTH), pmap),
                 pl.BlockSpec((pb, WINDOW, KV_WIDTH), pmap),
                 pl.BlockSpec((bt, CONV_W - 1, CONV_DIM), tmap),
                 pl.BlockSpec((pb, SSM_WIDTH, D_STATE), pmap)]
                + [_const_spec(a.shape) for a in consts]
                + [_const_spec(c.shape)] * 2
                + [_const_spec(sink.shape), _const_spec(bias.shape)]
                + [_const_spec(a.shape) for a in tail])
    out_shape = (
        jax.ShapeDtypeStruct((nb * tdec, D_MODEL), F32),
        jax.ShapeDtypeStruct((nb, WINDOW, KV_WIDTH), F32),
        jax.ShapeDtypeStruct((nb, WINDOW, KV_WIDTH), F32),
        jax.ShapeDtypeStruct((nb, CONV_W - 1, CONV_DIM), F32),
        jax.ShapeDtypeStruct((nb, SSM_WIDTH, D_STATE), F32),
    )
    out_specs = (
        pl.BlockSpec((rows, D_MODEL), lambda i, j: (i, 0)),
        pl.BlockSpec((pb, WINDOW, KV_WIDTH), pmap),
        pl.BlockSpec((pb, WINDOW, KV_WIDTH), pmap),
        pl.BlockSpec((bt, CONV_W - 1, CONV_DIM), tmap),
        pl.BlockSpec((pb, SSM_WIDTH, D_STATE), pmap),
    )
    scratch = [
        pltpu.VMEM((rows, D_MODEL), BF16),
        pltpu.VMEM((rows, ATT_WIDTH), F32),
        pltpu.VMEM((rows, KV_WIDTH), F32),
        pltpu.VMEM((rows, KV_WIDTH), F32),
        pltpu.VMEM((rows, KV_WIDTH), BF16),
        pltpu.VMEM((rows, KV_WIDTH), BF16),
        pltpu.VMEM((bt, 2 * SUBLANES, CONV_DIM), F32),
        pltpu.VMEM((rows, CONV_DIM), F32),
        pltpu.VMEM((rows, SSM_GROUPS * D_STATE), BF16),
        pltpu.VMEM((rows, LANES), F32),
        pltpu.VMEM((rows, SSM_WIDTH), F32),
        pltpu.VMEM((SSM_WIDTH, rows), BF16),
        pltpu.VMEM((rows, SSM_WIDTH), F32),
        pltpu.VMEM((rows, SSM_WIDTH), F32),
        pltpu.VMEM((rows, MIX_WIDTH), F32),
        pltpu.VMEM((D_MODEL, ATT_WIDTH), BF16),
    ]
    return pl.pallas_call(
        _sample_kernel,
        grid=(nb // bt, steps),
        in_specs=in_specs,
        out_specs=out_specs,
        out_shape=out_shape,
        scratch_shapes=scratch,
        compiler_params=pltpu.CompilerParams(
            dimension_semantics=("arbitrary", "arbitrary"), vmem_limit_bytes=VMEM_LIMIT_BYTES),
        name="sample_mixer",
    )(x2, ck, cv, state_conv, ssm, *consts, c, sn, sink, bias, *tail)


def _ffn(xa, xb, gpre, wg, wu, wd, gpost):
    tf = FFN_TILE
    steps_a, steps_b = xa.shape[0] // tf, xb.shape[0] // tf
    consts = [gpre, wg, wu, wd, gpost]
    amap = lambda i: (jnp.minimum(i, steps_a - 1), 0)
    bmap = lambda i: (jnp.maximum(i - steps_a, 0), 0)
    return pl.pallas_call(
        functools.partial(_ffn_kernel, steps_a=steps_a),
        grid=(steps_a + steps_b,),
        in_specs=([pl.BlockSpec((tf, D_MODEL), amap), pl.BlockSpec((tf, D_MODEL), bmap)]
                  + [_const_spec(a.shape) for a in consts]),
        out_specs=(pl.BlockSpec((tf, D_MODEL), amap), pl.BlockSpec((tf, D_MODEL), bmap)),
        out_shape=(jax.ShapeDtypeStruct(xa.shape, F32), jax.ShapeDtypeStruct(xb.shape, F32)),
        scratch_shapes=[pltpu.VMEM((tf, D_MODEL), BF16)],
        compiler_params=pltpu.CompilerParams(
            dimension_semantics=("arbitrary",), vmem_limit_bytes=VMEM_LIMIT_BYTES),
        name="ffn",
    )(xa, xb, *consts)


def kernel(x_prompt, x_sample, cache_k_win, cache_v_win, state_conv, state_ssm, g_pre_mix, w_in, attn_sinks, conv_w, conv_b, dt_bias, a_log, d_skip, g_ssm_out, w_out, g_post_mix, g_pre_ffn, w_gate, w_up, w_down, g_post_ffn):
    depth = w_in.shape[0]
    bp, lp, _ = x_prompt.shape
    nb, ts, _ = x_sample.shape
    hp, hs = x_prompt, x_sample
    outs = [[] for _ in range(8)]
    for l in range(depth):
        p = _layer_params(g_pre_mix[l], w_in[l], attn_sinks[l], conv_w[l], conv_b[l], dt_bias[l], a_log[l],
                          d_skip[l], g_ssm_out[l], w_out[l], g_post_mix[l])
        ffn_w = (g_pre_ffn[l].reshape(1, D_MODEL), w_gate[l].astype(BF16), w_up[l].astype(BF16),
                 w_down[l].astype(BF16), g_post_ffn[l].reshape(1, D_MODEL))
        x1p, kp, vp, cp, sp = _prompt_mixer(hp, p)
        x1s, ksm, vsm, csm, ssm = _sample_mixer(hs, cache_k_win[l], cache_v_win[l], state_conv[l], state_ssm[l], p)
        hp, hs = _ffn(x1p.reshape(bp * lp, D_MODEL), x1s, *ffn_w)
        hp = hp.reshape(bp, lp, D_MODEL)
        hs = hs.reshape(nb, ts, D_MODEL)
        kv_shape = (WINDOW, ATT_KV_HEADS, HEAD_DIM)
        ssm_shape = (SSM_HEADS, SSM_HEAD_DIM, D_STATE)
        for lst, val in zip(outs, (kp.reshape((bp,) + kv_shape), vp.reshape((bp,) + kv_shape), cp,
                                   sp.reshape((bp,) + ssm_shape),
                                   ksm.reshape((nb,) + kv_shape), vsm.reshape((nb,) + kv_shape), csm,
                                   ssm.reshape((nb,) + ssm_shape))):
            lst.append(val)
    return (hp, hs) + tuple(jnp.stack(o) for o in outs)
```

```python
import functools
import math

import numpy as np
import jax
import jax.numpy as jnp
from jax import lax
from jax.experimental import pallas as pl
from jax.experimental.pallas import tpu as pltpu

F32 = jnp.float32
BF16 = jnp.bfloat16

D_MODEL = 1024
ATT_HEADS = 16
ATT_KV_HEADS = 4
ATT_GROUP = ATT_HEADS // ATT_KV_HEADS
HEAD_DIM = 64
ATT_WIDTH = ATT_HEADS * HEAD_DIM
KV_WIDTH = ATT_KV_HEADS * HEAD_DIM
WINDOW = 128
ROT_DIM = HEAD_DIM // 4
ROPE_THETA = 500000.0
SSM_HEADS = 16
SSM_HEAD_DIM = 64
SSM_WIDTH = SSM_HEADS * SSM_HEAD_DIM
SSM_GROUPS = 2
SSM_HPG = SSM_HEADS // SSM_GROUPS
SSM_GROUP_W = SSM_WIDTH // SSM_GROUPS
D_STATE = 128
CONV_W = 4
CONV_DIM = SSM_WIDTH + 2 * SSM_GROUPS * D_STATE
MIX_WIDTH = ATT_WIDTH + SSM_WIDTH
EPS = 1e-6
PAST_LEN = 8192

LANES = 128
SUBLANES = 8
VMEM_LIMIT_BYTES = 60 * 1024 * 1024

CHUNK = 128
NEG = -1e30
LOG2E = math.log2(math.e)
Q_SCALE = HEAD_DIM ** -0.5 * LOG2E
SEQ_TILE = 512
SAMPLE_BT = 16
SAMPLE_PB = 4
FFN_TILE = 512
PROJ_ROWS = 256
OUT_CB = 512
ROW_BLOCK = 32
FF_CHUNK = 256


def _nn(a, b):
    return jnp.dot(a, b, preferred_element_type=F32)


def _nt(a, b):
    return lax.dot_general(a, b, (((1,), (1,)), ((), ())), preferred_element_type=F32)


def _split_bf16(x, n):
    parts = []
    r = x
    for i in range(n):
        p = r.astype(BF16)
        parts.append(p)
        if i + 1 < n:
            r = r - p.astype(F32)
    return parts


def _expand_heads(x, expand2_ref):
    hi, mid = _split_bf16(x, 2)
    return _nn(jnp.concatenate([hi, mid], axis=1), expand2_ref[...])


def _cumsum_cols(m01, x):
    w = x.shape[1]
    r = _nn(m01, jnp.concatenate(_split_bf16(x, 3), axis=1))
    return r[:, :w] + r[:, w:2 * w] + r[:, 2 * w:]


def _heads_to_rows(x):
    return x.T[:SSM_HEADS]


def _rms(x, g):
    ms = jnp.mean(x * x, axis=-1, keepdims=True)
    return x * lax.rsqrt(ms + EPS) * g


def _rms_rows(src, g_ref, dst_ref, res_ref=None):
    g = g_ref[...]
    for r0 in range(0, dst_ref.shape[0], ROW_BLOCK):
        rs = slice(r0, r0 + ROW_BLOCK)
        y = _rms(src[rs, :], g)
        if res_ref is not None:
            y = res_ref[rs, :] + y
        dst_ref[rs, :] = y.astype(dst_ref.dtype)


def _silu(x):
    h = 0.5 * x
    return h + h * jnp.tanh(h)


R_K = ATT_WIDTH
R_V = R_K + KV_WIDTH
R_Z = R_V + KV_WIDTH
R_XBC = R_Z + SSM_WIDTH
R_END = R_XBC + CONV_DIM

def _conv4(x, w, bias):
    assert CONV_W == 4
    u = pltpu.roll(x, 2, 0)
    return bias + (x * w[3] + u * w[1]) + pltpu.roll(x * w[2] + u * w[0], 1, 0)


CONV_CB = 256
CONV_RB = 64


def _conv_silu_cols(xpad_ref, row0, ra, rb, convw_ref, convb_ref, out_ref, c0):
    cs = slice(c0, c0 + CONV_CB)
    w = [convw_ref[i:i + 1, cs] for i in range(CONV_W)]
    bias = convb_ref[:, cs]
    for r0 in range(ra, rb, CONV_RB):
        xh = xpad_ref[row0 + r0 - SUBLANES:row0 + r0 + CONV_RB, cs]
        out_ref[r0:r0 + CONV_RB, cs] = _silu(_conv4(xh, w, bias)[SUBLANES:])


def _softplus(x):
    return jnp.maximum(x, 0.0) + jnp.log1p(jnp.exp(-jnp.abs(x)))


def _rope(x, c, s):
    lane = lax.broadcasted_iota(jnp.int32, (1, LANES), 1)
    first = (lane % HEAD_DIM) < ROT_DIM // 2
    outs = []
    for j in range(x.shape[1] // LANES):
        xb = x[:, j * LANES:(j + 1) * LANES]
        partner = jnp.where(first, pltpu.roll(xb, LANES - ROT_DIM // 2, 1), pltpu.roll(xb, ROT_DIM // 2, 1))
        outs.append(xb * c + partner * s)
    return outs[0] if len(outs) == 1 else jnp.concatenate(outs, axis=1)


def _iota(shape, dim):
    return lax.broadcasted_iota(jnp.int32, shape, dim)


def _head_blocks(c_out):
    out = []
    for half in range(2):
        g, kvh = divmod(2 * c_out + half, ATT_KV_HEADS)
        b_in = kvh * ATT_GROUP + g
        out.append((b_in // 2, b_in % 2))
    return out


def _permute_q_weight(wr_ref, wq_s):
    low = _iota((1, LANES), 1) < HEAD_DIM
    for c_out in range(ATT_WIDTH // LANES):
        halves = []
        for half, (c_in, src_half) in enumerate(_head_blocks(c_out)):
            col = wr_ref[:, c_in * LANES:(c_in + 1) * LANES]
            halves.append(col if src_half == half else pltpu.roll(col, HEAD_DIM, 1))
        wq_s[:, c_out * LANES:(c_out + 1) * LANES] = jnp.where(low, halves[0], halves[1])


def _project(hn, wq_ref, wr_ref, ropec, ropes):
    q = _rope(_nn(hn, wq_ref[...]), ropec, ropes) * (HEAD_DIM ** -0.5)
    kv = _nn(hn, wr_ref[:, R_K:R_Z])
    k = _rope(kv[:, :KV_WIDTH], ropec, ropes)
    v = kv[:, KV_WIDTH:]
    return q, k, v


def _dt_cols(hn, wdt_ref, dtb_row_ref):
    return _softplus(_nn(hn, wdt_ref[...]) + dtb_row_ref[...])


def _a_row(alog_row_ref):
    lane = _iota((1, LANES), 1)
    return jnp.where(lane < SSM_HEADS, -jnp.exp(alog_row_ref[...]), 0.0)


def _log2_decay(acol):
    acol2 = acol * LOG2E
    return acol2, _heads_to_rows(acol2)


SSD_QUAD = 4
SSD_NQUAD = SSM_HEADS // SSD_QUAD


def _ssd_cb(b_all, c_all):
    return [_nt(c_all[:, g * D_STATE:(g + 1) * D_STATE].astype(BF16),
                b_all[:, g * D_STATE:(g + 1) * D_STATE].astype(BF16)) for g in range(SSM_GROUPS)]


def _ssd_quad(qi, xs_bf, c_all, cbs, acol2, arow2, dtr, mask_bool, hT_bf=None):
    lane4 = _iota((1, SSD_QUAD * SSM_HEAD_DIM), 1) // SSM_HEAD_DIM
    zero = jnp.zeros((), BF16)
    e0 = SSD_QUAD * qi
    g = e0 // SSM_HPG
    cf = c_all[:, g * D_STATE:(g + 1) * D_STATE]
    lanes = slice(e0 * SSM_HEAD_DIM, (e0 + SSD_QUAD) * SSM_HEAD_DIM)
    xq = xs_bf[:, lanes]
    lhs, rhs = [], []
    for i in range(SSD_QUAD):
        e = e0 + i
        a_t = jnp.broadcast_to(acol2[:, e:e + 1], (CHUNK, CHUNK))
        w = cbs[g] * jnp.exp2(jnp.where(mask_bool, a_t - arow2[e:e + 1, :], NEG)) * dtr[e:e + 1, :]
        lhs.append(w.astype(BF16))
        rhs.append(jnp.where(lane4 == i, xq, zero))
        if hT_bf is not None:
            lhs.append((cf * jnp.exp2(a_t)).astype(BF16))
            rhs.append(jnp.where(lane4 == i, hT_bf[:, lanes], zero))
    return _nn(jnp.concatenate(lhs, axis=1), jnp.concatenate(rhs, axis=0))


def _ssd_block(xs_bf, b_all, c_all, acol2, arow2, dtr, mask_bool):
    cbs = _ssd_cb(b_all, c_all)
    return jnp.concatenate([_ssd_quad(qi, xs_bf, c_all, cbs, acol2, arow2, dtr, mask_bool)
                            for qi in range(SSD_NQUAD)], axis=1)


def _gate_rows(rs, y_refs, xc_s, z, dskip_ref, gssm_ref, mix_s):
    y = y_refs[0][rs, :]
    for extra in y_refs[1:]:
        y = y + extra[rs, :]
    gated = (y + dskip_ref[...] * xc_s[rs, 0:SSM_WIDTH]) * _silu(z[rs, :])
    for g in range(SSM_GROUPS):
        gs = slice(g * SSM_GROUP_W, (g + 1) * SSM_GROUP_W)
        gg = gated[:, gs]
        ms = jnp.mean(gg * gg, axis=-1, keepdims=True)
        o = gg * lax.rsqrt(ms + EPS) * gssm_ref[:, gs]
        mix_s[rs, ATT_WIDTH + g * SSM_GROUP_W:ATT_WIDTH + (g + 1) * SSM_GROUP_W] = o.astype(mix_s.dtype)


def _gate_and_out(y_refs, xc_s, z, x_ref, dskip_ref, gssm_ref, wout_ref, gpost_ref, mix_s, out_ref):
    for r0 in range(0, out_ref.shape[0], ROW_BLOCK):
        _gate_rows(slice(r0, r0 + ROW_BLOCK), y_refs, xc_s, z, dskip_ref, gssm_ref, mix_s)
    mo = _nn(mix_s[...].astype(BF16), wout_ref[...])
    _rms_rows(mo, gpost_ref, out_ref, res_ref=x_ref)


def _prompt_kernel(x_ref, gpre_ref, wr_ref, wdt_ref,
                   ropec_ref, ropes_ref, sink_ref, biasT_ref,
                   convw_ref, convb_ref, dtb_row_ref, alog_row_ref,
                   dskip_ref, gssm_ref, wout_ref, gpost_ref, expand2_ref,
                   x1_ref, nk_ref, nv_ref, nconv_ref, nssm_ref,
                   hn_s, q_s, kbuf, vT_s, xbc_s, xc_s, dtc_s, y_s, mix_s, hT_s, wq_s, z_s, mo_s):
    tm = x_ref.shape[0]
    nchunk = tm // CHUNK
    s = pl.program_id(1)
    last = pl.num_programs(1) - 1

    @pl.when(jnp.logical_and(pl.program_id(0) == 0, s == 0))
    def _():
        _permute_q_weight(wr_ref, wq_s)

    @pl.when(s == 0)
    def _():
        kbuf[0:CHUNK, :] = jnp.zeros((CHUNK, KV_WIDTH), BF16)
        vT_s[0] = jnp.zeros((KV_WIDTH, CHUNK), BF16)
        xbc_s[0:SUBLANES, :] = jnp.zeros((SUBLANES, CONV_DIM), F32)
        hT_s[...] = jnp.zeros_like(hT_s)

    _rms_rows(x_ref, gpre_ref, hn_s)

    def projection_pieces(ra, rb):
        rows = slice(ra, rb)

        def rope(x):
            return _rope(x, ropec_ref[rows, :], ropes_ref[rows, :])

        def proj_xbc(c0):
            xbc_s[SUBLANES + ra:SUBLANES + rb, c0:c0 + CONV_CB] = _nn(
                hn_s[rows, :], wr_ref[:, R_XBC + c0:R_XBC + c0 + CONV_CB])

        def conv(c0):
            _conv_silu_cols(xbc_s, SUBLANES, ra, rb, convw_ref, convb_ref, xc_s, c0)

        def proj_q(c0):
            q_s[rows, c0:c0 + KV_WIDTH] = (
                rope(_nn(hn_s[rows, :], wq_s[:, c0:c0 + KV_WIDTH])) * Q_SCALE).astype(BF16)

        def proj_k():
            kbuf[CHUNK + ra:CHUNK + rb, :] = rope(_nn(hn_s[rows, :], wr_ref[:, R_K:R_V])).astype(BF16)

        def proj_v_dt():
            v = _nn(hn_s[rows, :], wr_ref[:, R_V:R_Z])
            for j in range(ra // CHUNK, rb // CHUNK):
                vT_s[1 + j] = v[j * CHUNK - ra:(j + 1) * CHUNK - ra, :].T.astype(BF16)
            dtc_s[rows, :] = _dt_cols(hn_s[rows, :], wdt_ref, dtb_row_ref)

        others = [functools.partial(proj_q, c0) for c0 in range(0, ATT_WIDTH, KV_WIDTH)] + [proj_k, proj_v_dt]
        conv_cols = list(range(0, CONV_DIM, CONV_CB))
        pieces = [functools.partial(proj_xbc, conv_cols[0])]
        for n, c0 in enumerate(conv_cols):
            if n + 1 < len(conv_cols):
                pieces.append(functools.partial(proj_xbc, conv_cols[n + 1]))
            pieces.append(functools.partial(conv, c0))
            if n < len(others):
                pieces.append(others[n])
        return pieces + others[len(conv_cols):]

    a_row = _a_row(alog_row_ref)

    r2 = _iota((CHUNK, CHUNK), 0)
    c2 = _iota((CHUNK, CHUNK), 1)
    tril = c2 <= r2
    tril_bf = tril.astype(BF16)
    lane_kv = _iota((1, KV_WIDTH), 1) // HEAD_DIM
    ones_rows = jnp.ones((2 * SUBLANES, 2 * CHUNK), BF16)

    def chunk_body(c, extras):
        r0 = c * CHUNK
        per = -(-len(extras) // ATT_KV_HEADS)
        first = jnp.logical_and(s == 0, c == 0).astype(jnp.int32)
        bias = biasT_ref[first]
        qcat = jnp.concatenate([q_s[pl.ds(r0, CHUNK), g * KV_WIDTH:(g + 1) * KV_WIDTH]
                                for g in range(ATT_GROUP)], axis=0)
        kwin = kbuf[pl.ds(r0, 2 * CHUNK), :]
        kstack = jnp.concatenate([jnp.where(lane_kv == kvh, kwin, jnp.zeros((), BF16))
                                  for kvh in range(ATT_KV_HEADS)], axis=0)
        sT = _nt(kstack, qcat)
        vT_win = jnp.concatenate([vT_s[c], vT_s[c + 1]], axis=1)

        dtc_c = dtc_s[pl.ds(r0, CHUNK), :]
        acol = _cumsum_cols(tril_bf, dtc_c * a_row)
        a_end = acol[CHUNK - 1:CHUNK, :]
        tailc = jnp.exp(a_end - acol) * dtc_c
        ex = _expand_heads(
            jnp.concatenate([tailc, jnp.broadcast_to(jnp.exp(a_end), (SUBLANES, LANES))], axis=0), expand2_ref)
        tlx = ex[:CHUNK]
        dec_row = ex[CHUNK:CHUNK + 1]
        xs = xc_s[pl.ds(r0, CHUNK), 0:SSM_WIDTH]
        b_all = xc_s[pl.ds(r0, CHUNK), SSM_WIDTH:SSM_WIDTH + SSM_GROUPS * D_STATE]
        c_all = xc_s[pl.ds(r0, CHUNK), SSM_WIDTH + SSM_GROUPS * D_STATE:CONV_DIM]
        hT = hT_s[...]
        acol2, arow2 = _log2_decay(acol)
        dtr_c = _heads_to_rows(dtc_c)
        xs_bf = xs.astype(BF16)
        hT_bf = hT.astype(BF16)
        cbs = _ssd_cb(b_all, c_all)

        o_rows = []
        for i in range(ATT_KV_HEADS):
            blk = sT[i * 2 * CHUNK:(i + 1) * 2 * CHUNK] + bias
            sink = sink_ref[i] * LOG2E
            m = jnp.maximum(jnp.max(blk, axis=0, keepdims=True), sink)
            p = jnp.exp2(blk - m).astype(BF16)
            lhs = jnp.concatenate([vT_win[i * HEAD_DIM:(i + 1) * HEAD_DIM], ones_rows], axis=0)
            oT = _nn(lhs, p)
            den = oT[HEAD_DIM:HEAD_DIM + 1] + jnp.exp2(sink - m)
            o_rows.append(oT[:HEAD_DIM] * (1.0 / den))
            for qi in range(i * SSD_NQUAD // ATT_KV_HEADS, (i + 1) * SSD_NQUAD // ATT_KV_HEADS):
                lanes = slice(qi * SSD_QUAD * SSM_HEAD_DIM, (qi + 1) * SSD_QUAD * SSM_HEAD_DIM)
                y_s[pl.ds(r0, CHUNK), lanes] = _ssd_quad(qi, xs_bf, c_all, cbs, acol2, arow2, dtr_c, tril, hT_bf)
            for t in extras[i * per:(i + 1) * per]:
                t()
        for c_out in range(ATT_WIDTH // LANES):
            kvh, g0 = divmod(2 * c_out, ATT_GROUP)
            two = jnp.concatenate([o_rows[kvh][:, g * CHUNK:(g + 1) * CHUNK] for g in (g0, g0 + 1)], axis=0)
            mix_s[pl.ds(r0, CHUNK), c_out * LANES:(c_out + 1) * LANES] = two.T.astype(BF16)

        xtl = (xs * tlx).astype(BF16)
        for g in range(SSM_GROUPS):
            sl = slice(g * SSM_GROUP_W, (g + 1) * SSM_GROUP_W)
            bt = b_all[:, g * D_STATE:(g + 1) * D_STATE].T.astype(BF16)
            hT_s[:, sl] = hT[:, sl] * dec_row[:, sl] + _nn(bt, xtl[:, sl])

    def output_pieces(ra, rb):
        rows = slice(ra, rb)
        g_post = gpost_ref[...]

        def proj_z(c0):
            z_s[rows, c0:c0 + OUT_CB] = _nn(hn_s[rows, :], wr_ref[:, R_Z + c0:R_Z + c0 + OUT_CB])

        def gate(r0):
            _gate_rows(slice(r0, r0 + ROW_BLOCK), (y_s,), xc_s, z_s, dskip_ref, gssm_ref, mix_s)

        def proj_out(c0):
            mo_s[rows, c0:c0 + OUT_CB] = _nn(mix_s[rows, :], wout_ref[:, c0:c0 + OUT_CB])

        def post(r0):
            rs = slice(r0, r0 + ROW_BLOCK)
            x1_ref[rs, :] = x_ref[rs, :] + _rms(mo_s[rs, :], g_post)

        return ([functools.partial(proj_z, c0) for c0 in range(0, SSM_WIDTH, OUT_CB)]
                + [functools.partial(gate, r0) for r0 in range(ra, rb, ROW_BLOCK)]
                + [functools.partial(proj_out, c0) for c0 in range(0, D_MODEL, OUT_CB)]
                + [functools.partial(post, r0) for r0 in range(ra, rb, ROW_BLOCK)])

    groups = list(range(0, tm, PROJ_ROWS))
    cpg = PROJ_ROWS // CHUNK
    for t in projection_pieces(0, PROJ_ROWS):
        t()
    for gi, ra in enumerate(groups):
        side = projection_pieces(ra + PROJ_ROWS, ra + 2 * PROJ_ROWS) if gi + 1 < len(groups) else []
        if gi > 0:
            side = side + output_pieces(ra - PROJ_ROWS, ra)
        share = -(-len(side) // cpg)
        for k in range(cpg):
            chunk_body(ra // CHUNK + k, side[k * share:(k + 1) * share])

    kbuf[0:CHUNK, :] = kbuf[tm:tm + CHUNK, :]
    vT_s[0] = vT_s[nchunk]
    xbc_s[0:SUBLANES, :] = xbc_s[tm:tm + SUBLANES, :]

    for t in output_pieces(tm - PROJ_ROWS, tm):
        t()

    @pl.when(s == last)
    def _():
        hn_w = hn_s[tm - WINDOW:, :]
        nk_ref[...] = _rope(_nn(hn_w, wr_ref[:, R_K:R_V]), ropec_ref[tm - WINDOW:, :], ropes_ref[tm - WINDOW:, :])
        nv_ref[...] = _nn(hn_w, wr_ref[:, R_V:R_Z])
        nconv_ref[...] = xbc_s[SUBLANES - (CONV_W - 1):SUBLANES, :]
        nssm_ref[...] = hT_s[...].T


def _sample_kernel(x_ref, ck_ref, cv_ref, sconv_ref, sssm_ref,
                   gpre_ref, wr_ref, wdt_ref,
                   ropec_ref, ropes_ref, sink_ref, bias_ref,
                   convw_ref, convb_ref, dtb_row_ref, alog_row_ref,
                   dskip_ref, gssm_ref, wout_ref, gpost_ref, expand2_ref,
                   x1_ref, nk_ref, nv_ref, nconv_ref, nssm_ref,
                   hn_s, q_s, knr_s, vnr_s, knb_s, vnb_s, xpad_s, xc_s, bb_s, ea_s, eax_s, xT_s, y_s, yoff_s, mix_s, wq_s):
    bt_n = sconv_ref.shape[0]
    pb_n = ck_ref.shape[0]
    m_rows = x_ref.shape[0]
    tdec = m_rows // bt_n
    j = pl.program_id(1)

    @pl.when(jnp.logical_and(pl.program_id(0) == 0, j == 0))
    def _():
        _permute_q_weight(wr_ref, wq_s)

    @pl.when(j == 0)
    def _():
        _rms_rows(x_ref, gpre_ref, hn_s)
        hn = hn_s[...]
        q, k, v = _project(hn, wq_s, wr_ref, ropec_ref[...], ropes_ref[...])
        q_s[...] = q
        vnb_s[...] = v.astype(BF16)
        k_t, v_t = k.T, v.T
        knb_s[...] = k_t.astype(BF16)
        for b in range(bt_n):
            shift = WINDOW - (b + 1) * tdec
            knr_s[b] = pltpu.roll(k_t, shift, 1) if shift else k_t
            vnr_s[b] = pltpu.roll(v_t, shift, 1) if shift else v_t

        xbc = _nn(hn, wr_ref[:, R_XBC:R_END])
        xpad_s[:, 0:SUBLANES - 3, :] = jnp.zeros((bt_n, SUBLANES - 3, CONV_DIM), F32)
        xpad_s[:, SUBLANES - 3:SUBLANES, :] = sconv_ref[...]
        xpad_s[:, SUBLANES:2 * SUBLANES, :] = xbc.reshape(bt_n, tdec, CONV_DIM)
        nconv_ref[...] = xpad_s[:, 2 * SUBLANES - 3:2 * SUBLANES, :]
        for c0 in range(0, CONV_DIM, CONV_CB):
            cs = slice(c0, c0 + CONV_CB)
            xh = xpad_s[:, :, cs].reshape(bt_n * 2 * SUBLANES, CONV_CB)
            acc = _conv4(xh, [convw_ref[i:i + 1, cs] for i in range(CONV_W)], convb_ref[:, cs])
            xc_s[:, cs] = _silu(acc.reshape(bt_n, 2 * SUBLANES, CONV_CB)[:, SUBLANES:, :].reshape(m_rows, CONV_CB))
        xs = xc_s[:, 0:SSM_WIDTH]
        b_all = xc_s[:, SSM_WIDTH:SSM_WIDTH + SSM_GROUPS * D_STATE]
        c_all = xc_s[:, SSM_WIDTH + SSM_GROUPS * D_STATE:CONV_DIM]
        bb_s[...] = b_all.astype(BF16)

        dtc = _dt_cols(hn, wdt_ref, dtb_row_ref)
        a_row = _a_row(alog_row_ref)

        r2 = _iota((m_rows, m_rows), 0)
        c2 = _iota((m_rows, m_rows), 1)
        same = (r2 // tdec) == (c2 // tdec)
        causal = jnp.logical_and(same, c2 <= r2)
        causal_bf = causal.astype(BF16)
        same_bf = same.astype(BF16)

        dac = dtc * a_row
        acol = _cumsum_cols(causal_bf, dac)
        alast = _cumsum_cols(same_bf, dac)
        tailc = jnp.exp(alast - acol) * dtc
        ex = _expand_heads(jnp.concatenate([jnp.exp(acol), tailc], axis=0), expand2_ref)
        ea_s[...] = jnp.exp(alast)
        eax_s[...] = ex[:m_rows]
        acol2, arow2 = _log2_decay(acol)
        y_s[...] = _ssd_block(xs.astype(BF16), b_all, c_all, acol2, arow2, _heads_to_rows(dtc), causal)
        xtl = xs * ex[m_rows:]
        for jj in range(SSM_WIDTH // LANES):
            xT_s[jj * LANES:(jj + 1) * LANES, :] = xtl[:, jj * LANES:(jj + 1) * LANES].T.astype(BF16)

    lane_kv = _iota((1, KV_WIDTH), 1) // HEAD_DIM
    row_b = _iota((m_rows, 1), 0) // tdec
    low_half = _iota((1, LANES), 1) < HEAD_DIM
    sink = sink_ref[...]
    new_lane = _iota((1, WINDOW), 1) >= WINDOW - tdec

    pbs = range(pb_n)
    bs = [j * pb_n + pb for pb in pbs]
    rs = [pl.multiple_of(b * tdec, tdec) for b in bs]

    qbd, kc, vc, sc = [], [], [], []
    for pb in pbs:
        q8 = q_s[pl.ds(rs[pb], tdec), :]
        qbd.append(jnp.concatenate(
            [jnp.where(lane_kv == kvh, q8[:, g * KV_WIDTH:(g + 1) * KV_WIDTH], 0.0)
             for g in range(ATT_GROUP) for kvh in range(ATT_KV_HEADS)], axis=0).astype(BF16))
        kc.append(ck_ref[pb])
        vc.append(cv_ref[pb])
    for pb in pbs:
        keys_t = jnp.concatenate([kc[pb].astype(BF16), knb_s[...]], axis=1)
        sc.append(_nn(qbd[pb], keys_t) + bias_ref[bs[pb]])
    p, inv = [], []
    for pb in pbs:
        m = jnp.maximum(jnp.max(sc[pb], axis=1, keepdims=True), sink)
        e = jnp.exp(sc[pb] - m)
        inv.append(1.0 / (jnp.sum(e, axis=1, keepdims=True) + jnp.exp(sink - m)))
        p.append(e.astype(BF16))
    for pb in pbs:
        o = (_nt(p[pb][:, :WINDOW], vc[pb].astype(BF16)) + _nn(p[pb][:, WINDOW:], vnb_s[...])) * inv[pb]
        for c_out in range(ATT_WIDTH // LANES):
            kvh, g0 = divmod(2 * c_out, ATT_GROUP)
            halves = []
            for half in range(2):
                i0 = ((g0 + half) * ATT_KV_HEADS + kvh) * tdec
                piece = o[i0:i0 + tdec, (kvh // 2) * LANES:(kvh // 2 + 1) * LANES]
                halves.append(piece if kvh % 2 == half else pltpu.roll(piece, HEAD_DIM, 1))
            mix_s[pl.ds(rs[pb], tdec), c_out * LANES:(c_out + 1) * LANES] = jnp.where(low_half, halves[0], halves[1])
        nk_ref[pb] = jnp.where(new_lane, knr_s[bs[pb]], pltpu.roll(kc[pb], WINDOW - tdec, 1))
        nv_ref[pb] = jnp.where(new_lane, vnr_s[bs[pb]], pltpu.roll(vc[pb], WINDOW - tdec, 1))

    for g in range(SSM_GROUPS):
        sl = slice(g * SSM_GROUP_W, (g + 1) * SSM_GROUP_W)
        c0 = SSM_WIDTH + (SSM_GROUPS + g) * D_STATE
        hg, upd = [], []
        for pb in pbs:
            hg.append(sssm_ref[pb, sl, :])
            bm = jnp.where(row_b == bs[pb], bb_s[:, g * D_STATE:(g + 1) * D_STATE], jnp.zeros((), BF16))
            upd.append(_nn(xT_s[sl, :], bm))
        for pb in pbs:
            cc = xc_s[pl.ds(rs[pb], tdec), c0:c0 + D_STATE].astype(BF16)
            yoff_s[pl.ds(rs[pb], tdec), sl] = (_nt(cc, hg[pb].astype(BF16))
                                               * eax_s[pl.ds(rs[pb], tdec), sl])
        for pb in pbs:
            ea_b = ea_s[pl.ds(rs[pb], 1), :]
            dec = jnp.concatenate(
                [jnp.broadcast_to(ea_b[:, g * SSM_HPG + e:g * SSM_HPG + e + 1], (SSM_HEAD_DIM, D_STATE))
                 for e in range(SSM_HPG)], axis=0)
            nssm_ref[pb, sl, :] = hg[pb] * dec + upd[pb]

    @pl.when(j == pl.num_programs(1) - 1)
    def _():
        z = _nn(hn_s[...], wr_ref[:, R_Z:R_XBC])
        _gate_and_out((y_s, yoff_s), xc_s, z, x_ref, dskip_ref, gssm_ref, wout_ref, gpost_ref, mix_s, x1_ref)


def _ffn_kernel(xa_ref, xb_ref, gpre_ref, wg_ref, wu_ref, wd_ref, gpost_ref, oa_ref, ob_ref, f_s, *, steps_a):
    def tile(x_ref, o_ref):
        _rms_rows(x_ref, gpre_ref, f_s)
        f = f_s[...]
        d_ff = wg_ref.shape[1]
        acc = jnp.zeros(x_ref.shape, F32)
        for j in range(d_ff // FF_CHUNK):
            sl = slice(j * FF_CHUNK, (j + 1) * FF_CHUNK)
            gate = _nn(f, wg_ref[:, sl])
            up = _nn(f, wu_ref[:, sl])
            acc = acc + _nn((_silu(gate) * up).astype(BF16), wd_ref[sl, :])
        _rms_rows(acc, gpost_ref, o_ref, res_ref=x_ref)

    i = pl.program_id(0)

    @pl.when(i < steps_a)
    def _():
        tile(xa_ref, oa_ref)

    @pl.when(i >= steps_a)
    def _():
        tile(xb_ref, ob_ref)


def _const_spec(shape):
    nd = len(shape)
    return pl.BlockSpec(shape, lambda *_: (0,) * nd, pipeline_mode=pl.Buffered(1))


def _rope_tables(pos):
    half = ROT_DIM // 2
    inv = ROPE_THETA ** (-np.arange(half, dtype=np.float64) * 2.0 / ROT_DIM)
    ang = pos.astype(np.float64)[:, None] * inv[None, :]
    cos = np.cos(ang).astype(np.float32)
    sin = np.sin(ang).astype(np.float32)
    n = pos.shape[0]
    pad = HEAD_DIM - ROT_DIM
    c = np.concatenate([cos, cos, np.ones((n, pad), np.float32)], axis=1)
    s = np.concatenate([-sin, sin, np.zeros((n, pad), np.float32)], axis=1)
    rep = LANES // HEAD_DIM
    return tuple(jnp.asarray(np.tile(t, (1, rep))) for t in (c, s))


def _layer_params(g_pre_mix, w_in, attn_sinks, conv_w, conv_b, dt_bias, a_log, d_skip, g_ssm_out, w_out, g_post_mix):
    wr = w_in.astype(BF16)
    wdt = jnp.pad(w_in[:, R_END:], ((0, 0), (0, LANES - SSM_HEADS))).astype(BF16)
    wout = w_out.astype(BF16)
    pad16 = ((0, 0), (0, LANES - SSM_HEADS))
    expand = (np.arange(LANES)[:, None] == (np.arange(SSM_WIDTH)[None, :] // SSM_HEAD_DIM)).astype(np.float32)
    return dict(
        gpre=g_pre_mix.reshape(1, D_MODEL), wr=wr, wdt=wdt,
        convw=conv_w, convb=conv_b.reshape(1, CONV_DIM),
        dtb_row=jnp.pad(dt_bias.reshape(1, SSM_HEADS), pad16),
        alog_row=jnp.pad(a_log.reshape(1, SSM_HEADS), pad16),
        dskip=jnp.repeat(d_skip, SSM_HEAD_DIM).reshape(1, SSM_WIDTH), gssm=g_ssm_out.reshape(1, SSM_WIDTH),
        wout=wout, gpost=g_post_mix.reshape(1, D_MODEL), expand2=jnp.asarray(np.concatenate([expand, expand], axis=0), BF16),
        sinks_gk=attn_sinks.reshape(ATT_KV_HEADS, ATT_GROUP).T,
    )


_WEIGHT_ORDER = ("gpre", "wr", "wdt")
_TAIL_ORDER = ("convw", "convb", "dtb_row", "alog_row",
               "dskip", "gssm", "wout", "gpost", "expand2")


def _prompt_mixer(x, p):
    bsz, seq, _ = x.shape
    tm = SEQ_TILE
    nchunk = tm // CHUNK
    c, sn = _rope_tables(np.arange(seq))
    sink = jnp.repeat(p["sinks_gk"].T, CHUNK, axis=1)[:, None, :]
    jj = np.arange(2 * CHUNK)[:, None]
    tt = np.arange(CHUNK)[None, :]
    vis = (jj >= tt) & (jj <= tt + WINDOW)
    bias = np.stack([np.where(vis, 0.0, NEG), np.where(vis & (jj >= CHUNK), 0.0, NEG)]).astype(np.float32)
    bias = jnp.asarray(np.tile(bias, (1, 1, ATT_GROUP)))
    consts = [p[n] for n in _WEIGHT_ORDER]
    tail = [p[n] for n in _TAIL_ORDER]
    in_specs = ([pl.BlockSpec((None, tm, D_MODEL), lambda b, s: (b, s, 0))]
                + [_const_spec(a.shape) for a in consts]
                + [pl.BlockSpec((tm, LANES), lambda b, s: (s, 0))] * 2
                + [_const_spec(sink.shape), _const_spec(bias.shape)]
                + [_const_spec(a.shape) for a in tail])
    out_shape = (
        jax.ShapeDtypeStruct((bsz, seq, D_MODEL), F32),
        jax.ShapeDtypeStruct((bsz, WINDOW, KV_WIDTH), F32),
        jax.ShapeDtypeStruct((bsz, WINDOW, KV_WIDTH), F32),
        jax.ShapeDtypeStruct((bsz, CONV_W - 1, CONV_DIM), F32),
        jax.ShapeDtypeStruct((bsz, SSM_WIDTH, D_STATE), F32),
    )
    out_specs = (
        pl.BlockSpec((None, tm, D_MODEL), lambda b, s: (b, s, 0)),
        pl.BlockSpec((None, WINDOW, KV_WIDTH), lambda b, s: (b, 0, 0)),
        pl.BlockSpec((None, WINDOW, KV_WIDTH), lambda b, s: (b, 0, 0)),
        pl.BlockSpec((None, CONV_W - 1, CONV_DIM), lambda b, s: (b, 0, 0)),
        pl.BlockSpec((None, SSM_WIDTH, D_STATE), lambda b, s: (b, 0, 0)),
    )
    scratch = [
        pltpu.VMEM((tm, D_MODEL), BF16),
        pltpu.VMEM((tm, ATT_WIDTH), BF16),
        pltpu.VMEM((CHUNK + tm, KV_WIDTH), BF16),
        pltpu.VMEM((nchunk + 1, KV_WIDTH, CHUNK), BF16),
        pltpu.VMEM((SUBLANES + tm, CONV_DIM), F32),
        pltpu.VMEM((tm, CONV_DIM), F32),
        pltpu.VMEM((tm, LANES), F32),
        pltpu.VMEM((tm, SSM_WIDTH), F32),
        pltpu.VMEM((tm, MIX_WIDTH), BF16),
        pltpu.VMEM((D_STATE, SSM_WIDTH), F32),
        pltpu.VMEM((D_MODEL, ATT_WIDTH), BF16),
        pltpu.VMEM((tm, SSM_WIDTH), F32),
        pltpu.VMEM((tm, D_MODEL), F32),
    ]
    return pl.pallas_call(
        _prompt_kernel,
        grid=(bsz, seq // tm),
        in_specs=in_specs,
        out_specs=out_specs,
        out_shape=out_shape,
        scratch_shapes=scratch,
        compiler_params=pltpu.CompilerParams(
            dimension_semantics=("arbitrary", "arbitrary"), vmem_limit_bytes=VMEM_LIMIT_BYTES),
        name="prompt_mixer",
    )(x, *consts, c, sn, sink, bias, *tail)


def _sample_mixer(x, cache_k, cache_v, state_conv, state_ssm, p):
    nb, tdec, _ = x.shape
    bt = SAMPLE_BT
    rows = bt * tdec
    c, sn = _rope_tables(np.tile(PAST_LEN + np.arange(tdec), bt))
    sink = jnp.repeat(p["sinks_gk"].reshape(-1), tdec).reshape(rows, 1)
    consts = [p[n] for n in _WEIGHT_ORDER]
    tail = [p[n] for n in _TAIL_ORDER]
    x2 = x.reshape(nb * tdec, D_MODEL)
    assert rows == WINDOW
    ck = jnp.transpose(cache_k, (0, 2, 3, 1)).reshape(nb, KV_WIDTH, WINDOW)
    cv = jnp.transpose(cache_v, (0, 2, 3, 1)).reshape(nb, KV_WIDTH, WINDOW)
    ssm = state_ssm.reshape(nb, SSM_WIDTH, D_STATE)
    pb = SAMPLE_PB
    steps = bt // pb
    t_of_row = (np.arange(rows) % tdec)[:, None]
    col = np.arange(rows)[None, :]
    vis_c = col >= t_of_row
    bias = jnp.asarray(np.stack(
        [np.where(np.concatenate([vis_c, (col // tdec == b) & (col % tdec <= t_of_row)], axis=1), 0.0, NEG)
         for b in range(bt)]).astype(np.float32))
    tmap = lambda i, j: (i, 0, 0)
    pmap = lambda i, j: (i * steps + j, 0, 0)
    in_specs = ([pl.BlockSpec((rows, D_MODEL), lambda i, j: (i, 0)),
                 pl.BlockSpec((pb, KV_WIDTH, WINDOW), pmap),
                 pl.BlockSpec((pb, KV_WIDTH, WINDOW), pmap),
                 pl.BlockSpec((bt, CONV_W - 1, CONV_DIM), tmap),
                 pl.BlockSpec((pb, SSM_WIDTH, D_STATE), pmap)]
                + [_const_spec(a.shape) for a in consts]
                + [_const_spec(c.shape)] * 2
                + [_const_spec(sink.shape), _const_spec(bias.shape)]
                + [_const_spec(a.shape) for a in tail])
    out_shape = (
        jax.ShapeDtypeStruct((nb * tdec, D_MODEL), F32),
        jax.ShapeDtypeStruct((nb, KV_WIDTH, WINDOW), F32),
        jax.ShapeDtypeStruct((nb, KV_WIDTH, WINDOW), F32),
        jax.ShapeDtypeStruct((nb, CONV_W - 1, CONV_DIM), F32),
        jax.ShapeDtypeStruct((nb, SSM_WIDTH, D_STATE), F32),
    )
    out_specs = (
        pl.BlockSpec((rows, D_MODEL), lambda i, j: (i, 0)),
        pl.BlockSpec((pb, KV_WIDTH, WINDOW), pmap),
        pl.BlockSpec((pb, KV_WIDTH, WINDOW), pmap),
        pl.BlockSpec((bt, CONV_W - 1, CONV_DIM), tmap),
        pl.BlockSpec((pb, SSM_WIDTH, D_STATE), pmap),
    )
    scratch = [
        pltpu.VMEM((rows, D_MODEL), BF16),
        pltpu.VMEM((rows, ATT_WIDTH), F32),
        pltpu.VMEM((bt, KV_WIDTH, WINDOW), F32),
        pltpu.VMEM((bt, KV_WIDTH, WINDOW), F32),
        pltpu.VMEM((KV_WIDTH, rows), BF16),
        pltpu.VMEM((rows, KV_WIDTH), BF16),
        pltpu.VMEM((bt, 2 * SUBLANES, CONV_DIM), F32),
        pltpu.VMEM((rows, CONV_DIM), F32),
        pltpu.VMEM((rows, SSM_GROUPS * D_STATE), BF16),
        pltpu.VMEM((rows, LANES), F32),
        pltpu.VMEM((rows, SSM_WIDTH), F32),
        pltpu.VMEM((SSM_WIDTH, rows), BF16),
        pltpu.VMEM((rows, SSM_WIDTH), F32),
        pltpu.VMEM((rows, SSM_WIDTH), F32),
        pltpu.VMEM((rows, MIX_WIDTH), F32),
        pltpu.VMEM((D_MODEL, ATT_WIDTH), BF16),
    ]
    return pl.pallas_call(
        _sample_kernel,
        grid=(nb // bt, steps),
        in_specs=in_specs,
        out_specs=out_specs,
        out_shape=out_shape,
        scratch_shapes=scratch,
        compiler_params=pltpu.CompilerParams(
            dimension_semantics=("arbitrary", "arbitrary"), vmem_limit_bytes=VMEM_LIMIT_BYTES),
        name="sample_mixer",
    )(x2, ck, cv, state_conv, ssm, *consts, c, sn, sink, bias, *tail)


def _window_major(kv_t):
    return jnp.transpose(kv_t.reshape(kv_t.shape[0], ATT_KV_HEADS, HEAD_DIM, WINDOW), (0, 3, 1, 2))


def _ffn(xa, xb, gpre, wg, wu, wd, gpost):
    tf = FFN_TILE
    steps_a, steps_b = xa.shape[0] // tf, xb.shape[0] // tf
    consts = [gpre, wg, wu, wd, gpost]
    amap = lambda i: (jnp.minimum(i, steps_a - 1), 0)
    bmap = lambda i: (jnp.maximum(i - steps_a, 0), 0)
    return pl.pallas_call(
        functools.partial(_ffn_kernel, steps_a=steps_a),
        grid=(steps_a + steps_b,),
        in_specs=([pl.BlockSpec((tf, D_MODEL), amap), pl.BlockSpec((tf, D_MODEL), bmap)]
                  + [_const_spec(a.shape) for a in consts]),
        out_specs=(pl.BlockSpec((tf, D_MODEL), amap), pl.BlockSpec((tf, D_MODEL), bmap)),
        out_shape=(jax.ShapeDtypeStruct(xa.shape, F32), jax.ShapeDtypeStruct(xb.shape, F32)),
        scratch_shapes=[pltpu.VMEM((tf, D_MODEL), BF16)],
        compiler_params=pltpu.CompilerParams(
            dimension_semantics=("arbitrary",), vmem_limit_bytes=VMEM_LIMIT_BYTES),
        name="ffn",
    )(xa, xb, *consts)


def kernel(x_prompt, x_sample, cache_k_win, cache_v_win, state_conv, state_ssm, g_pre_mix, w_in, attn_sinks, conv_w, conv_b, dt_bias, a_log, d_skip, g_ssm_out, w_out, g_post_mix, g_pre_ffn, w_gate, w_up, w_down, g_post_ffn):
    depth = w_in.shape[0]
    bp, lp, _ = x_prompt.shape
    nb, ts, _ = x_sample.shape
    hp, hs = x_prompt, x_sample
    outs = [[] for _ in range(8)]
    for l in range(depth):
        p = _layer_params(g_pre_mix[l], w_in[l], attn_sinks[l], conv_w[l], conv_b[l], dt_bias[l], a_log[l],
                          d_skip[l], g_ssm_out[l], w_out[l], g_post_mix[l])
        ffn_w = (g_pre_ffn[l].reshape(1, D_MODEL), w_gate[l].astype(BF16), w_up[l].astype(BF16),
                 w_down[l].astype(BF16), g_post_ffn[l].reshape(1, D_MODEL))
        x1p, kp, vp, cp, sp = _prompt_mixer(hp, p)
        x1s, ksm, vsm, csm, ssm = _sample_mixer(hs, cache_k_win[l], cache_v_win[l], state_conv[l], state_ssm[l], p)
        hp, hs = _ffn(x1p.reshape(bp * lp, D_MODEL), x1s, *ffn_w)
        hp = hp.reshape(bp, lp, D_MODEL)
        hs = hs.reshape(nb, ts, D_MODEL)
        kv_shape = (WINDOW, ATT_KV_HEADS, HEAD_DIM)
        ssm_shape = (SSM_HEADS, SSM_HEAD_DIM, D_STATE)
        for lst, val in zip(outs, (kp.reshape((bp,) + kv_shape), vp.reshape((bp,) + kv_shape), cp,
                                   sp.reshape((bp,) + ssm_shape),
                                   _window_major(ksm), _window_major(vsm), csm,
                                   ssm.reshape((nb,) + ssm_shape))):
            lst.append(val)
    return (hp, hs) + tuple(jnp.stack(o) for o in outs)
```

```python
import functools
import math

import numpy as np
import jax
import jax.numpy as jnp
from jax import lax
from jax.experimental import pallas as pl
from jax.experimental.pallas import tpu as pltpu

F32 = jnp.float32
BF16 = jnp.bfloat16

D_MODEL = 1024
ATT_HEADS = 16
ATT_KV_HEADS = 4
ATT_GROUP = ATT_HEADS // ATT_KV_HEADS
HEAD_DIM = 64
ATT_WIDTH = ATT_HEADS * HEAD_DIM
KV_WIDTH = ATT_KV_HEADS * HEAD_DIM
WINDOW = 128
ROT_DIM = HEAD_DIM // 4
ROPE_THETA = 500000.0
SSM_HEADS = 16
SSM_HEAD_DIM = 64
SSM_WIDTH = SSM_HEADS * SSM_HEAD_DIM
SSM_GROUPS = 2
SSM_HPG = SSM_HEADS // SSM_GROUPS
SSM_GROUP_W = SSM_WIDTH // SSM_GROUPS
D_STATE = 128
CONV_W = 4
CONV_DIM = SSM_WIDTH + 2 * SSM_GROUPS * D_STATE
MIX_WIDTH = ATT_WIDTH + SSM_WIDTH
EPS = 1e-6
PAST_LEN = 8192

LANES = 128
SUBLANES = 8
VMEM_LIMIT_BYTES = 60 * 1024 * 1024

CHUNK = 128
NEG = -1e30
LOG2E = math.log2(math.e)
Q_SCALE = HEAD_DIM ** -0.5 * LOG2E
SEQ_TILE = 512
SAMPLE_BT = 16
SAMPLE_PB = 4
FFN_TILE = 512
PROJ_ROWS = 256
OUT_CB = 512
ROW_BLOCK = 32
FF_CHUNK = 256


def _nn(a, b):
    return jnp.dot(a, b, preferred_element_type=F32)


def _nt(a, b):
    return lax.dot_general(a, b, (((1,), (1,)), ((), ())), preferred_element_type=F32)


def _split_bf16(x, n):
    parts = []
    r = x
    for i in range(n):
        p = r.astype(BF16)
        parts.append(p)
        if i + 1 < n:
            r = r - p.astype(F32)
    return parts


def _expand_heads(x, expand2_ref):
    hi, mid = _split_bf16(x, 2)
    return _nn(jnp.concatenate([hi, mid], axis=1), expand2_ref[...])


def _cumsum_cols(m01, x):
    w = x.shape[1]
    r = _nn(m01, jnp.concatenate(_split_bf16(x, 3), axis=1))
    return r[:, :w] + r[:, w:2 * w] + r[:, 2 * w:]


def _heads_to_rows(x):
    return x.T[:SSM_HEADS]


def _rms(x, g):
    ms = jnp.mean(x * x, axis=-1, keepdims=True)
    return x * lax.rsqrt(ms + EPS) * g


def _rms_rows(src, g_ref, dst_ref, res_ref=None):
    g = g_ref[...]
    for r0 in range(0, dst_ref.shape[0], ROW_BLOCK):
        rs = slice(r0, r0 + ROW_BLOCK)
        y = _rms(src[rs, :], g)
        if res_ref is not None:
            y = res_ref[rs, :] + y
        dst_ref[rs, :] = y.astype(dst_ref.dtype)


def _silu(x):
    h = 0.5 * x
    return h + h * jnp.tanh(h)


R_K = ATT_WIDTH
R_V = R_K + KV_WIDTH
R_Z = R_V + KV_WIDTH
R_XBC = R_Z + SSM_WIDTH
R_END = R_XBC + CONV_DIM

def _conv4(x, w, bias):
    assert CONV_W == 4
    u = pltpu.roll(x, 2, 0)
    return bias + (x * w[3] + u * w[1]) + pltpu.roll(x * w[2] + u * w[0], 1, 0)


CONV_CB = 256
CONV_RB = 64


def _conv_silu_cols(xpad_ref, row0, ra, rb, convw_ref, convb_ref, out_ref, c0):
    cs = slice(c0, c0 + CONV_CB)
    w = [convw_ref[i:i + 1, cs] for i in range(CONV_W)]
    bias = convb_ref[:, cs]
    for r0 in range(ra, rb, CONV_RB):
        xh = xpad_ref[row0 + r0 - SUBLANES:row0 + r0 + CONV_RB, cs]
        out_ref[r0:r0 + CONV_RB, cs] = _silu(_conv4(xh, w, bias)[SUBLANES:])


def _softplus(x):
    return jnp.maximum(x, 0.0) + jnp.log1p(jnp.exp(-jnp.abs(x)))


def _rope(x, c, s):
    lane = lax.broadcasted_iota(jnp.int32, (1, LANES), 1)
    first = (lane % HEAD_DIM) < ROT_DIM // 2
    outs = []
    for j in range(x.shape[1] // LANES):
        xb = x[:, j * LANES:(j + 1) * LANES]
        partner = jnp.where(first, pltpu.roll(xb, LANES - ROT_DIM // 2, 1), pltpu.roll(xb, ROT_DIM // 2, 1))
        outs.append(xb * c + partner * s)
    return outs[0] if len(outs) == 1 else jnp.concatenate(outs, axis=1)


def _iota(shape, dim):
    return lax.broadcasted_iota(jnp.int32, shape, dim)


def _head_blocks(c_out):
    out = []
    for half in range(2):
        g, kvh = divmod(2 * c_out + half, ATT_KV_HEADS)
        b_in = kvh * ATT_GROUP + g
        out.append((b_in // 2, b_in % 2))
    return out


def _permute_q_weight(wr_ref, wq_s):
    low = _iota((1, LANES), 1) < HEAD_DIM
    for c_out in range(ATT_WIDTH // LANES):
        halves = []
        for half, (c_in, src_half) in enumerate(_head_blocks(c_out)):
            col = wr_ref[:, c_in * LANES:(c_in + 1) * LANES]
            halves.append(col if src_half == half else pltpu.roll(col, HEAD_DIM, 1))
        wq_s[:, c_out * LANES:(c_out + 1) * LANES] = jnp.where(low, halves[0], halves[1])


def _project(hn, wq_ref, wr_ref, ropec, ropes):
    q = _rope(_nn(hn, wq_ref[...]), ropec, ropes) * (HEAD_DIM ** -0.5)
    kv = _nn(hn, wr_ref[:, R_K:R_Z])
    k = _rope(kv[:, :KV_WIDTH], ropec, ropes)
    v = kv[:, KV_WIDTH:]
    return q, k, v


def _dt_cols(hn, wdt_ref, dtb_row_ref):
    return _softplus(_nn(hn, wdt_ref[...]) + dtb_row_ref[...])


def _a_row(alog_row_ref):
    lane = _iota((1, LANES), 1)
    return jnp.where(lane < SSM_HEADS, -jnp.exp(alog_row_ref[...]), 0.0)


def _log2_decay(acol):
    acol2 = acol * LOG2E
    return acol2, _heads_to_rows(acol2)


SSD_QUAD = 4
SSD_NQUAD = SSM_HEADS // SSD_QUAD


def _ssd_cb(b_all, c_all):
    return [_nt(c_all[:, g * D_STATE:(g + 1) * D_STATE].astype(BF16),
                b_all[:, g * D_STATE:(g + 1) * D_STATE].astype(BF16)) for g in range(SSM_GROUPS)]


def _ssd_quad(qi, xs_bf, c_all, cbs, acol2, arow2, dtr, mask_bool, hT_bf=None):
    lane4 = _iota((1, SSD_QUAD * SSM_HEAD_DIM), 1) // SSM_HEAD_DIM
    zero = jnp.zeros((), BF16)
    e0 = SSD_QUAD * qi
    g = e0 // SSM_HPG
    cf = c_all[:, g * D_STATE:(g + 1) * D_STATE]
    lanes = slice(e0 * SSM_HEAD_DIM, (e0 + SSD_QUAD) * SSM_HEAD_DIM)
    xq = xs_bf[:, lanes]
    lhs, rhs = [], []
    for i in range(SSD_QUAD):
        e = e0 + i
        a_t = jnp.broadcast_to(acol2[:, e:e + 1], (CHUNK, CHUNK))
        w = cbs[g] * jnp.exp2(jnp.where(mask_bool, a_t - arow2[e:e + 1, :], NEG)) * dtr[e:e + 1, :]
        lhs.append(w.astype(BF16))
        rhs.append(jnp.where(lane4 == i, xq, zero))
        if hT_bf is not None:
            lhs.append((cf * jnp.exp2(a_t)).astype(BF16))
            rhs.append(jnp.where(lane4 == i, hT_bf[:, lanes], zero))
    return _nn(jnp.concatenate(lhs, axis=1), jnp.concatenate(rhs, axis=0))


def _ssd_block(xs_bf, b_all, c_all, acol2, arow2, dtr, mask_bool):
    cbs = _ssd_cb(b_all, c_all)
    return jnp.concatenate([_ssd_quad(qi, xs_bf, c_all, cbs, acol2, arow2, dtr, mask_bool)
                            for qi in range(SSD_NQUAD)], axis=1)


def _gate_rows(rs, y_refs, xc_s, z, dskip_ref, gssm_ref, mix_s):
    y = y_refs[0][rs, :]
    for extra in y_refs[1:]:
        y = y + extra[rs, :]
    gated = (y + dskip_ref[...] * xc_s[rs, 0:SSM_WIDTH]) * _silu(z[rs, :])
    for g in range(SSM_GROUPS):
        gs = slice(g * SSM_GROUP_W, (g + 1) * SSM_GROUP_W)
        gg = gated[:, gs]
        ms = jnp.mean(gg * gg, axis=-1, keepdims=True)
        o = gg * lax.rsqrt(ms + EPS) * gssm_ref[:, gs]
        mix_s[rs, ATT_WIDTH + g * SSM_GROUP_W:ATT_WIDTH + (g + 1) * SSM_GROUP_W] = o.astype(mix_s.dtype)


def _gate_and_out(y_refs, xc_s, z, x_ref, dskip_ref, gssm_ref, wout_ref, gpost_ref, mix_s, out_ref):
    for r0 in range(0, out_ref.shape[0], ROW_BLOCK):
        _gate_rows(slice(r0, r0 + ROW_BLOCK), y_refs, xc_s, z, dskip_ref, gssm_ref, mix_s)
    mo = _nn(mix_s[...].astype(BF16), wout_ref[...])
    _rms_rows(mo, gpost_ref, out_ref, res_ref=x_ref)


def _prompt_kernel(x_ref, gpre_ref, wr_ref, wdt_ref,
                   ropec_ref, ropes_ref, sink_ref, biasT_ref,
                   convw_ref, convb_ref, dtb_row_ref, alog_row_ref,
                   dskip_ref, gssm_ref, wout_ref, gpost_ref, expand2_ref,
                   x1_ref, nk_ref, nv_ref, nconv_ref, nssm_ref,
                   hn_s, q_s, kbuf, vT_s, xbc_s, xc_s, dtc_s, y_s, mix_s, hT_s, wq_s, z_s, mo_s):
    tm = x_ref.shape[0]
    nchunk = tm // CHUNK
    s = pl.program_id(1)
    last = pl.num_programs(1) - 1

    @pl.when(jnp.logical_and(pl.program_id(0) == 0, s == 0))
    def _():
        _permute_q_weight(wr_ref, wq_s)

    @pl.when(s == 0)
    def _():
        kbuf[0:CHUNK, :] = jnp.zeros((CHUNK, KV_WIDTH), BF16)
        vT_s[0] = jnp.zeros((KV_WIDTH, CHUNK), BF16)
        xbc_s[0:SUBLANES, :] = jnp.zeros((SUBLANES, CONV_DIM), F32)
        hT_s[...] = jnp.zeros_like(hT_s)

    _rms_rows(x_ref, gpre_ref, hn_s)

    def projection_pieces(ra, rb):
        rows = slice(ra, rb)

        def rope(x):
            return _rope(x, ropec_ref[rows, :], ropes_ref[rows, :])

        def proj_xbc(c0):
            xbc_s[SUBLANES + ra:SUBLANES + rb, c0:c0 + CONV_CB] = _nn(
                hn_s[rows, :], wr_ref[:, R_XBC + c0:R_XBC + c0 + CONV_CB])

        def conv(c0):
            _conv_silu_cols(xbc_s, SUBLANES, ra, rb, convw_ref, convb_ref, xc_s, c0)

        def proj_q(c0):
            q_s[rows, c0:c0 + KV_WIDTH] = (
                rope(_nn(hn_s[rows, :], wq_s[:, c0:c0 + KV_WIDTH])) * Q_SCALE).astype(BF16)

        def proj_k():
            kbuf[CHUNK + ra:CHUNK + rb, :] = rope(_nn(hn_s[rows, :], wr_ref[:, R_K:R_V])).astype(BF16)

        def proj_v_dt():
            v = _nn(hn_s[rows, :], wr_ref[:, R_V:R_Z])
            for j in range(ra // CHUNK, rb // CHUNK):
                vT_s[1 + j] = v[j * CHUNK - ra:(j + 1) * CHUNK - ra, :].T.astype(BF16)
            dtc_s[rows, :] = _dt_cols(hn_s[rows, :], wdt_ref, dtb_row_ref)

        others = [functools.partial(proj_q, c0) for c0 in range(0, ATT_WIDTH, KV_WIDTH)] + [proj_k, proj_v_dt]
        conv_cols = list(range(0, CONV_DIM, CONV_CB))
        pieces = [functools.partial(proj_xbc, conv_cols[0])]
        for n, c0 in enumerate(conv_cols):
            if n + 1 < len(conv_cols):
                pieces.append(functools.partial(proj_xbc, conv_cols[n + 1]))
            pieces.append(functools.partial(conv, c0))
            if n < len(others):
                pieces.append(others[n])
        return pieces + others[len(conv_cols):]

    a_row = _a_row(alog_row_ref)

    r2 = _iota((CHUNK, CHUNK), 0)
    c2 = _iota((CHUNK, CHUNK), 1)
    tril = c2 <= r2
    tril_bf = tril.astype(BF16)
    lane_kv = _iota((1, KV_WIDTH), 1) // HEAD_DIM
    ones_rows = jnp.ones((2 * SUBLANES, 2 * CHUNK), BF16)

    def chunk_body(c, extras):
        r0 = c * CHUNK
        per = -(-len(extras) // ATT_KV_HEADS)
        first = jnp.logical_and(s == 0, c == 0).astype(jnp.int32)
        bias = biasT_ref[first]
        qcat = jnp.concatenate([q_s[pl.ds(r0, CHUNK), g * KV_WIDTH:(g + 1) * KV_WIDTH]
                                for g in range(ATT_GROUP)], axis=0)
        kwin = kbuf[pl.ds(r0, 2 * CHUNK), :]
        kstack = jnp.concatenate([jnp.where(lane_kv == kvh, kwin, jnp.zeros((), BF16))
                                  for kvh in range(ATT_KV_HEADS)], axis=0)
        sT = _nt(kstack, qcat)
        vT_win = jnp.concatenate([vT_s[c], vT_s[c + 1]], axis=1)

        dtc_c = dtc_s[pl.ds(r0, CHUNK), :]
        acol = _cumsum_cols(tril_bf, dtc_c * a_row)
        a_end = acol[CHUNK - 1:CHUNK, :]
        tailc = jnp.exp(a_end - acol) * dtc_c
        ex = _expand_heads(
            jnp.concatenate([tailc, jnp.broadcast_to(jnp.exp(a_end), (SUBLANES, LANES))], axis=0), expand2_ref)
        tlx = ex[:CHUNK]
        dec_row = ex[CHUNK:CHUNK + 1]
        xs = xc_s[pl.ds(r0, CHUNK), 0:SSM_WIDTH]
        b_all = xc_s[pl.ds(r0, CHUNK), SSM_WIDTH:SSM_WIDTH + SSM_GROUPS * D_STATE]
        c_all = xc_s[pl.ds(r0, CHUNK), SSM_WIDTH + SSM_GROUPS * D_STATE:CONV_DIM]
        hT = hT_s[...]
        acol2, arow2 = _log2_decay(acol)
        dtr_c = _heads_to_rows(dtc_c)
        xs_bf = xs.astype(BF16)
        hT_bf = hT.astype(BF16)
        cbs = _ssd_cb(b_all, c_all)

        o_rows = []
        for i in range(ATT_KV_HEADS):
            blk = sT[i * 2 * CHUNK:(i + 1) * 2 * CHUNK] + bias
            sink = sink_ref[i] * LOG2E
            m = jnp.maximum(jnp.max(blk, axis=0, keepdims=True), sink)
            p = jnp.exp2(blk - m).astype(BF16)
            lhs = jnp.concatenate([vT_win[i * HEAD_DIM:(i + 1) * HEAD_DIM], ones_rows], axis=0)
            oT = _nn(lhs, p)
            den = oT[HEAD_DIM:HEAD_DIM + 1] + jnp.exp2(sink - m)
            o_rows.append(oT[:HEAD_DIM] * (1.0 / den))
            for qi in range(i * SSD_NQUAD // ATT_KV_HEADS, (i + 1) * SSD_NQUAD // ATT_KV_HEADS):
                lanes = slice(qi * SSD_QUAD * SSM_HEAD_DIM, (qi + 1) * SSD_QUAD * SSM_HEAD_DIM)
                y_s[pl.ds(r0, CHUNK), lanes] = _ssd_quad(qi, xs_bf, c_all, cbs, acol2, arow2, dtr_c, tril, hT_bf)
            for t in extras[i * per:(i + 1) * per]:
                t()
        for c_out in range(ATT_WIDTH // LANES):
            kvh, g0 = divmod(2 * c_out, ATT_GROUP)
            two = jnp.concatenate([o_rows[kvh][:, g * CHUNK:(g + 1) * CHUNK] for g in (g0, g0 + 1)], axis=0)
            mix_s[pl.ds(r0, CHUNK), c_out * LANES:(c_out + 1) * LANES] = two.T.astype(BF16)

        xtl = (xs * tlx).astype(BF16)
        for g in range(SSM_GROUPS):
            sl = slice(g * SSM_GROUP_W, (g + 1) * SSM_GROUP_W)
            bt = b_all[:, g * D_STATE:(g + 1) * D_STATE].T.astype(BF16)
            hT_s[:, sl] = hT[:, sl] * dec_row[:, sl] + _nn(bt, xtl[:, sl])

    def output_pieces(ra, rb):
        rows = slice(ra, rb)
        g_post = gpost_ref[...]

        def proj_z(c0):
            z_s[rows, c0:c0 + OUT_CB] = _nn(hn_s[rows, :], wr_ref[:, R_Z + c0:R_Z + c0 + OUT_CB])

        def gate(r0):
            _gate_rows(slice(r0, r0 + ROW_BLOCK), (y_s,), xc_s, z_s, dskip_ref, gssm_ref, mix_s)

        def proj_out(c0):
            mo_s[rows, c0:c0 + OUT_CB] = _nn(mix_s[rows, :], wout_ref[:, c0:c0 + OUT_CB])

        def post(r0):
            rs = slice(r0, r0 + ROW_BLOCK)
            x1_ref[rs, :] = x_ref[rs, :] + _rms(mo_s[rs, :], g_post)

        return ([functools.partial(proj_z, c0) for c0 in range(0, SSM_WIDTH, OUT_CB)]
                + [functools.partial(gate, r0) for r0 in range(ra, rb, ROW_BLOCK)]
                + [functools.partial(proj_out, c0) for c0 in range(0, D_MODEL, OUT_CB)]
                + [functools.partial(post, r0) for r0 in range(ra, rb, ROW_BLOCK)])

    groups = list(range(0, tm, PROJ_ROWS))
    cpg = PROJ_ROWS // CHUNK
    for t in projection_pieces(0, PROJ_ROWS):
        t()
    for gi, ra in enumerate(groups):
        side = projection_pieces(ra + PROJ_ROWS, ra + 2 * PROJ_ROWS) if gi + 1 < len(groups) else []
        if gi > 0:
            side = side + output_pieces(ra - PROJ_ROWS, ra)
        share = -(-len(side) // cpg)
        for k in range(cpg):
            chunk_body(ra // CHUNK + k, side[k * share:(k + 1) * share])

    kbuf[0:CHUNK, :] = kbuf[tm:tm + CHUNK, :]
    vT_s[0] = vT_s[nchunk]
    xbc_s[0:SUBLANES, :] = xbc_s[tm:tm + SUBLANES, :]

    for t in output_pieces(tm - PROJ_ROWS, tm):
        t()

    @pl.when(s == last)
    def _():
        hn_w = hn_s[tm - WINDOW:, :]
        nk_ref[...] = _rope(_nn(hn_w, wr_ref[:, R_K:R_V]), ropec_ref[tm - WINDOW:, :], ropes_ref[tm - WINDOW:, :]).T
        nv_ref[...] = _nn(hn_w, wr_ref[:, R_V:R_Z]).T
        nconv_ref[...] = xbc_s[SUBLANES - (CONV_W - 1):SUBLANES, :]
        nssm_ref[...] = hT_s[...].T


def _sample_kernel(x_ref, ck_ref, cv_ref, sconv_ref, sssm_ref,
                   gpre_ref, wr_ref, wdt_ref,
                   ropec_ref, ropes_ref, sink_ref, bias_ref,
                   convw_ref, convb_ref, dtb_row_ref, alog_row_ref,
                   dskip_ref, gssm_ref, wout_ref, gpost_ref, expand2_ref,
                   x1_ref, nk_ref, nv_ref, nconv_ref, nssm_ref,
                   hn_s, q_s, knr_s, vnr_s, knb_s, vnb_s, xpad_s, xc_s, bb_s, ea_s, eax_s, xT_s, y_s, yoff_s, mix_s, wq_s):
    bt_n = sconv_ref.shape[1]
    pb_n = ck_ref.shape[0]
    m_rows = x_ref.shape[0]
    tdec = m_rows // bt_n
    j = pl.program_id(1)

    @pl.when(jnp.logical_and(pl.program_id(0) == 0, j == 0))
    def _():
        _permute_q_weight(wr_ref, wq_s)

    @pl.when(j == 0)
    def _():
        _rms_rows(x_ref, gpre_ref, hn_s)
        hn = hn_s[...]
        q, k, v = _project(hn, wq_s, wr_ref, ropec_ref[...], ropes_ref[...])
        q_s[...] = q
        vnb_s[...] = v.astype(BF16)
        k_t, v_t = k.T, v.T
        knb_s[...] = k_t.astype(BF16)
        for b in range(bt_n):
            shift = WINDOW - (b + 1) * tdec
            knr_s[b] = pltpu.roll(k_t, shift, 1) if shift else k_t
            vnr_s[b] = pltpu.roll(v_t, shift, 1) if shift else v_t

        xbc = _nn(hn, wr_ref[:, R_XBC:R_END])
        xpad_s[:, 0:SUBLANES - 3, :] = jnp.zeros((bt_n, SUBLANES - 3, CONV_DIM), F32)
        for r in range(CONV_W - 1):
            xpad_s[:, SUBLANES - 3 + r, :] = sconv_ref[r]
        xpad_s[:, SUBLANES:2 * SUBLANES, :] = xbc.reshape(bt_n, tdec, CONV_DIM)
        for r in range(CONV_W - 1):
            nconv_ref[r] = xpad_s[:, 2 * SUBLANES - 3 + r, :]
        for c0 in range(0, CONV_DIM, CONV_CB):
            cs = slice(c0, c0 + CONV_CB)
            xh = xpad_s[:, :, cs].reshape(bt_n * 2 * SUBLANES, CONV_CB)
            acc = _conv4(xh, [convw_ref[i:i + 1, cs] for i in range(CONV_W)], convb_ref[:, cs])
            xc_s[:, cs] = _silu(acc.reshape(bt_n, 2 * SUBLANES, CONV_CB)[:, SUBLANES:, :].reshape(m_rows, CONV_CB))
        xs = xc_s[:, 0:SSM_WIDTH]
        b_all = xc_s[:, SSM_WIDTH:SSM_WIDTH + SSM_GROUPS * D_STATE]
        c_all = xc_s[:, SSM_WIDTH + SSM_GROUPS * D_STATE:CONV_DIM]
        bb_s[...] = b_all.astype(BF16)

        dtc = _dt_cols(hn, wdt_ref, dtb_row_ref)
        a_row = _a_row(alog_row_ref)

        r2 = _iota((m_rows, m_rows), 0)
        c2 = _iota((m_rows, m_rows), 1)
        same = (r2 // tdec) == (c2 // tdec)
        causal = jnp.logical_and(same, c2 <= r2)
        causal_bf = causal.astype(BF16)
        same_bf = same.astype(BF16)

        dac = dtc * a_row
        acol = _cumsum_cols(causal_bf, dac)
        alast = _cumsum_cols(same_bf, dac)
        tailc = jnp.exp(alast - acol) * dtc
        ex = _expand_heads(jnp.concatenate([jnp.exp(acol), tailc], axis=0), expand2_ref)
        ea_s[...] = jnp.exp(alast)
        eax_s[...] = ex[:m_rows]
        acol2, arow2 = _log2_decay(acol)
        y_s[...] = _ssd_block(xs.astype(BF16), b_all, c_all, acol2, arow2, _heads_to_rows(dtc), causal)
        xtl = xs * ex[m_rows:]
        for jj in range(SSM_WIDTH // LANES):
            xT_s[jj * LANES:(jj + 1) * LANES, :] = xtl[:, jj * LANES:(jj + 1) * LANES].T.astype(BF16)

    lane_kv = _iota((1, KV_WIDTH), 1) // HEAD_DIM
    row_b = _iota((m_rows, 1), 0) // tdec
    low_half = _iota((1, LANES), 1) < HEAD_DIM
    sink = sink_ref[...]
    new_lane = _iota((1, WINDOW), 1) >= WINDOW - tdec

    pbs = range(pb_n)
    bs = [j * pb_n + pb for pb in pbs]
    rs = [pl.multiple_of(b * tdec, tdec) for b in bs]

    qbd, kc, vc, sc = [], [], [], []
    for pb in pbs:
        q8 = q_s[pl.ds(rs[pb], tdec), :]
        qbd.append(jnp.concatenate(
            [jnp.where(lane_kv == kvh, q8[:, g * KV_WIDTH:(g + 1) * KV_WIDTH], 0.0)
             for g in range(ATT_GROUP) for kvh in range(ATT_KV_HEADS)], axis=0).astype(BF16))
        kc.append(ck_ref[pb])
        vc.append(cv_ref[pb])
    for pb in pbs:
        keys_t = jnp.concatenate([kc[pb].astype(BF16), knb_s[...]], axis=1)
        sc.append(_nn(qbd[pb], keys_t) + bias_ref[bs[pb]])
    p, inv = [], []
    for pb in pbs:
        m = jnp.maximum(jnp.max(sc[pb], axis=1, keepdims=True), sink)
        e = jnp.exp(sc[pb] - m)
        inv.append(1.0 / (jnp.sum(e, axis=1, keepdims=True) + jnp.exp(sink - m)))
        p.append(e.astype(BF16))
    for pb in pbs:
        o = (_nt(p[pb][:, :WINDOW], vc[pb].astype(BF16)) + _nn(p[pb][:, WINDOW:], vnb_s[...])) * inv[pb]
        for c_out in range(ATT_WIDTH // LANES):
            kvh, g0 = divmod(2 * c_out, ATT_GROUP)
            halves = []
            for half in range(2):
                i0 = ((g0 + half) * ATT_KV_HEADS + kvh) * tdec
                piece = o[i0:i0 + tdec, (kvh // 2) * LANES:(kvh // 2 + 1) * LANES]
                halves.append(piece if kvh % 2 == half else pltpu.roll(piece, HEAD_DIM, 1))
            mix_s[pl.ds(rs[pb], tdec), c_out * LANES:(c_out + 1) * LANES] = jnp.where(low_half, halves[0], halves[1])
        nk_ref[pb] = jnp.where(new_lane, knr_s[bs[pb]], pltpu.roll(kc[pb], WINDOW - tdec, 1))
        nv_ref[pb] = jnp.where(new_lane, vnr_s[bs[pb]], pltpu.roll(vc[pb], WINDOW - tdec, 1))

    for g in range(SSM_GROUPS):
        sl = slice(g * SSM_GROUP_W, (g + 1) * SSM_GROUP_W)
        c0 = SSM_WIDTH + (SSM_GROUPS + g) * D_STATE
        hg, upd = [], []
        for pb in pbs:
            hg.append(sssm_ref[pb, sl, :])
            bm = jnp.where(row_b == bs[pb], bb_s[:, g * D_STATE:(g + 1) * D_STATE], jnp.zeros((), BF16))
            upd.append(_nn(xT_s[sl, :], bm))
        for pb in pbs:
            cc = xc_s[pl.ds(rs[pb], tdec), c0:c0 + D_STATE].astype(BF16)
            yoff_s[pl.ds(rs[pb], tdec), sl] = (_nt(cc, hg[pb].astype(BF16))
                                               * eax_s[pl.ds(rs[pb], tdec), sl])
        for pb in pbs:
            ea_b = ea_s[pl.ds(rs[pb], 1), :]
            dec = jnp.concatenate(
                [jnp.broadcast_to(ea_b[:, g * SSM_HPG + e:g * SSM_HPG + e + 1], (SSM_HEAD_DIM, D_STATE))
                 for e in range(SSM_HPG)], axis=0)
            nssm_ref[pb, sl, :] = hg[pb] * dec + upd[pb]

    @pl.when(j == pl.num_programs(1) - 1)
    def _():
        z = _nn(hn_s[...], wr_ref[:, R_Z:R_XBC])
        _gate_and_out((y_s, yoff_s), xc_s, z, x_ref, dskip_ref, gssm_ref, wout_ref, gpost_ref, mix_s, x1_ref)


def _ffn_kernel(xa_ref, xb_ref, gpre_ref, wg_ref, wu_ref, wd_ref, gpost_ref, oa_ref, ob_ref, f_s, *, steps_a):
    def tile(x_ref, o_ref):
        _rms_rows(x_ref, gpre_ref, f_s)
        f = f_s[...]
        d_ff = wg_ref.shape[1]
        acc = jnp.zeros(x_ref.shape, F32)
        for j in range(d_ff // FF_CHUNK):
            sl = slice(j * FF_CHUNK, (j + 1) * FF_CHUNK)
            gate = _nn(f, wg_ref[:, sl])
            up = _nn(f, wu_ref[:, sl])
            acc = acc + _nn((_silu(gate) * up).astype(BF16), wd_ref[sl, :])
        _rms_rows(acc, gpost_ref, o_ref, res_ref=x_ref)

    i = pl.program_id(0)

    @pl.when(i < steps_a)
    def _():
        tile(xa_ref, oa_ref)

    @pl.when(i >= steps_a)
    def _():
        tile(xb_ref, ob_ref)


def _const_spec(shape):
    nd = len(shape)
    return pl.BlockSpec(shape, lambda *_: (0,) * nd, pipeline_mode=pl.Buffered(1))


def _rope_tables(pos):
    half = ROT_DIM // 2
    inv = ROPE_THETA ** (-np.arange(half, dtype=np.float64) * 2.0 / ROT_DIM)
    ang = pos.astype(np.float64)[:, None] * inv[None, :]
    cos = np.cos(ang).astype(np.float32)
    sin = np.sin(ang).astype(np.float32)
    n = pos.shape[0]
    pad = HEAD_DIM - ROT_DIM
    c = np.concatenate([cos, cos, np.ones((n, pad), np.float32)], axis=1)
    s = np.concatenate([-sin, sin, np.zeros((n, pad), np.float32)], axis=1)
    rep = LANES // HEAD_DIM
    return tuple(jnp.asarray(np.tile(t, (1, rep))) for t in (c, s))


def _layer_params(g_pre_mix, w_in, attn_sinks, conv_w, conv_b, dt_bias, a_log, d_skip, g_ssm_out, w_out, g_post_mix):
    wr = w_in.astype(BF16)
    wdt = jnp.pad(w_in[:, R_END:], ((0, 0), (0, LANES - SSM_HEADS))).astype(BF16)
    wout = w_out.astype(BF16)
    pad16 = ((0, 0), (0, LANES - SSM_HEADS))
    expand = (np.arange(LANES)[:, None] == (np.arange(SSM_WIDTH)[None, :] // SSM_HEAD_DIM)).astype(np.float32)
    return dict(
        gpre=g_pre_mix.reshape(1, D_MODEL), wr=wr, wdt=wdt,
        convw=conv_w, convb=conv_b.reshape(1, CONV_DIM),
        dtb_row=jnp.pad(dt_bias.reshape(1, SSM_HEADS), pad16),
        alog_row=jnp.pad(a_log.reshape(1, SSM_HEADS), pad16),
        dskip=jnp.repeat(d_skip, SSM_HEAD_DIM).reshape(1, SSM_WIDTH), gssm=g_ssm_out.reshape(1, SSM_WIDTH),
        wout=wout, gpost=g_post_mix.reshape(1, D_MODEL), expand2=jnp.asarray(np.concatenate([expand, expand], axis=0), BF16),
        sinks_gk=attn_sinks.reshape(ATT_KV_HEADS, ATT_GROUP).T,
    )


_WEIGHT_ORDER = ("gpre", "wr", "wdt")
_TAIL_ORDER = ("convw", "convb", "dtb_row", "alog_row",
               "dskip", "gssm", "wout", "gpost", "expand2")


def _prompt_mixer(x, p):
    bsz, seq, _ = x.shape
    tm = SEQ_TILE
    nchunk = tm // CHUNK
    c, sn = _rope_tables(np.arange(seq))
    sink = jnp.repeat(p["sinks_gk"].T, CHUNK, axis=1)[:, None, :]
    jj = np.arange(2 * CHUNK)[:, None]
    tt = np.arange(CHUNK)[None, :]
    vis = (jj >= tt) & (jj <= tt + WINDOW)
    bias = np.stack([np.where(vis, 0.0, NEG), np.where(vis & (jj >= CHUNK), 0.0, NEG)]).astype(np.float32)
    bias = jnp.asarray(np.tile(bias, (1, 1, ATT_GROUP)))
    consts = [p[n] for n in _WEIGHT_ORDER]
    tail = [p[n] for n in _TAIL_ORDER]
    in_specs = ([pl.BlockSpec((None, tm, D_MODEL), lambda b, s: (b, s, 0))]
                + [_const_spec(a.shape) for a in consts]
                + [pl.BlockSpec((tm, LANES), lambda b, s: (s, 0))] * 2
                + [_const_spec(sink.shape), _const_spec(bias.shape)]
                + [_const_spec(a.shape) for a in tail])
    out_shape = (
        jax.ShapeDtypeStruct((bsz, seq, D_MODEL), F32),
        jax.ShapeDtypeStruct((bsz, KV_WIDTH, WINDOW), F32),
        jax.ShapeDtypeStruct((bsz, KV_WIDTH, WINDOW), F32),
        jax.ShapeDtypeStruct((bsz, CONV_W - 1, CONV_DIM), F32),
        jax.ShapeDtypeStruct((bsz, SSM_WIDTH, D_STATE), F32),
    )
    out_specs = (
        pl.BlockSpec((None, tm, D_MODEL), lambda b, s: (b, s, 0)),
        pl.BlockSpec((None, KV_WIDTH, WINDOW), lambda b, s: (b, 0, 0)),
        pl.BlockSpec((None, KV_WIDTH, WINDOW), lambda b, s: (b, 0, 0)),
        pl.BlockSpec((None, CONV_W - 1, CONV_DIM), lambda b, s: (b, 0, 0)),
        pl.BlockSpec((None, SSM_WIDTH, D_STATE), lambda b, s: (b, 0, 0)),
    )
    scratch = [
        pltpu.VMEM((tm, D_MODEL), BF16),
        pltpu.VMEM((tm, ATT_WIDTH), BF16),
        pltpu.VMEM((CHUNK + tm, KV_WIDTH), BF16),
        pltpu.VMEM((nchunk + 1, KV_WIDTH, CHUNK), BF16),
        pltpu.VMEM((SUBLANES + tm, CONV_DIM), F32),
        pltpu.VMEM((tm, CONV_DIM), F32),
        pltpu.VMEM((tm, LANES), F32),
        pltpu.VMEM((tm, SSM_WIDTH), F32),
        pltpu.VMEM((tm, MIX_WIDTH), BF16),
        pltpu.VMEM((D_STATE, SSM_WIDTH), F32),
        pltpu.VMEM((D_MODEL, ATT_WIDTH), BF16),
        pltpu.VMEM((tm, SSM_WIDTH), F32),
        pltpu.VMEM((tm, D_MODEL), F32),
    ]
    return pl.pallas_call(
        _prompt_kernel,
        grid=(bsz, seq // tm),
        in_specs=in_specs,
        out_specs=out_specs,
        out_shape=out_shape,
        scratch_shapes=scratch,
        compiler_params=pltpu.CompilerParams(
            dimension_semantics=("arbitrary", "arbitrary"), vmem_limit_bytes=VMEM_LIMIT_BYTES),
        name="prompt_mixer",
    )(x, *consts, c, sn, sink, bias, *tail)


def _sample_mixer(x, cache_k, cache_v, state_conv, state_ssm, p):
    nb, tdec, _ = x.shape
    bt = SAMPLE_BT
    rows = bt * tdec
    c, sn = _rope_tables(np.tile(PAST_LEN + np.arange(tdec), bt))
    sink = jnp.repeat(p["sinks_gk"].reshape(-1), tdec).reshape(rows, 1)
    consts = [p[n] for n in _WEIGHT_ORDER]
    tail = [p[n] for n in _TAIL_ORDER]
    x2 = x.reshape(nb * tdec, D_MODEL)
    assert rows == WINDOW
    ck = jnp.transpose(cache_k, (0, 2, 3, 1)).reshape(nb, KV_WIDTH, WINDOW)
    cv = jnp.transpose(cache_v, (0, 2, 3, 1)).reshape(nb, KV_WIDTH, WINDOW)
    ssm = state_ssm.reshape(nb, SSM_WIDTH, D_STATE)
    pb = SAMPLE_PB
    steps = bt // pb
    t_of_row = (np.arange(rows) % tdec)[:, None]
    col = np.arange(rows)[None, :]
    vis_c = col >= t_of_row
    bias = jnp.asarray(np.stack(
        [np.where(np.concatenate([vis_c, (col // tdec == b) & (col % tdec <= t_of_row)], axis=1), 0.0, NEG)
         for b in range(bt)]).astype(np.float32))
    tmap = lambda i, j: (0, i, 0)
    pmap = lambda i, j: (i * steps + j, 0, 0)
    in_specs = ([pl.BlockSpec((rows, D_MODEL), lambda i, j: (i, 0)),
                 pl.BlockSpec((pb, KV_WIDTH, WINDOW), pmap),
                 pl.BlockSpec((pb, KV_WIDTH, WINDOW), pmap),
                 pl.BlockSpec((CONV_W - 1, bt, CONV_DIM), tmap),
                 pl.BlockSpec((pb, SSM_WIDTH, D_STATE), pmap)]
                + [_const_spec(a.shape) for a in consts]
                + [_const_spec(c.shape)] * 2
                + [_const_spec(sink.shape), _const_spec(bias.shape)]
                + [_const_spec(a.shape) for a in tail])
    out_shape = (
        jax.ShapeDtypeStruct((nb * tdec, D_MODEL), F32),
        jax.ShapeDtypeStruct((nb, KV_WIDTH, WINDOW), F32),
        jax.ShapeDtypeStruct((nb, KV_WIDTH, WINDOW), F32),
        jax.ShapeDtypeStruct((CONV_W - 1, nb, CONV_DIM), F32),
        jax.ShapeDtypeStruct((nb, SSM_WIDTH, D_STATE), F32),
    )
    out_specs = (
        pl.BlockSpec((rows, D_MODEL), lambda i, j: (i, 0)),
        pl.BlockSpec((pb, KV_WIDTH, WINDOW), pmap),
        pl.BlockSpec((pb, KV_WIDTH, WINDOW), pmap),
        pl.BlockSpec((CONV_W - 1, bt, CONV_DIM), tmap),
        pl.BlockSpec((pb, SSM_WIDTH, D_STATE), pmap),
    )
    scratch = [
        pltpu.VMEM((rows, D_MODEL), BF16),
        pltpu.VMEM((rows, ATT_WIDTH), F32),
        pltpu.VMEM((bt, KV_WIDTH, WINDOW), F32),
        pltpu.VMEM((bt, KV_WIDTH, WINDOW), F32),
        pltpu.VMEM((KV_WIDTH, rows), BF16),
        pltpu.VMEM((rows, KV_WIDTH), BF16),
        pltpu.VMEM((bt, 2 * SUBLANES, CONV_DIM), F32),
        pltpu.VMEM((rows, CONV_DIM), F32),
        pltpu.VMEM((rows, SSM_GROUPS * D_STATE), BF16),
        pltpu.VMEM((rows, LANES), F32),
        pltpu.VMEM((rows, SSM_WIDTH), F32),
        pltpu.VMEM((SSM_WIDTH, rows), BF16),
        pltpu.VMEM((rows, SSM_WIDTH), F32),
        pltpu.VMEM((rows, SSM_WIDTH), F32),
        pltpu.VMEM((rows, MIX_WIDTH), F32),
        pltpu.VMEM((D_MODEL, ATT_WIDTH), BF16),
    ]
    return pl.pallas_call(
        _sample_kernel,
        grid=(nb // bt, steps),
        in_specs=in_specs,
        out_specs=out_specs,
        out_shape=out_shape,
        scratch_shapes=scratch,
        compiler_params=pltpu.CompilerParams(
            dimension_semantics=("arbitrary", "arbitrary"), vmem_limit_bytes=VMEM_LIMIT_BYTES),
        name="sample_mixer",
    )(x2, ck, cv, jnp.transpose(state_conv, (1, 0, 2)), ssm, *consts, c, sn, sink, bias, *tail)


def _window_major(kv_t):
    return jnp.transpose(kv_t.reshape(kv_t.shape[0], ATT_KV_HEADS, HEAD_DIM, WINDOW), (0, 3, 1, 2))


def _ffn(xa, xb, gpre, wg, wu, wd, gpost):
    tf = FFN_TILE
    steps_a, steps_b = xa.shape[0] // tf, xb.shape[0] // tf
    consts = [gpre, wg, wu, wd, gpost]
    amap = lambda i: (jnp.minimum(i, steps_a - 1), 0)
    bmap = lambda i: (jnp.maximum(i - steps_a, 0), 0)
    return pl.pallas_call(
        functools.partial(_ffn_kernel, steps_a=steps_a),
        grid=(steps_a + steps_b,),
        in_specs=([pl.BlockSpec((tf, D_MODEL), amap), pl.BlockSpec((tf, D_MODEL), bmap)]
                  + [_const_spec(a.shape) for a in consts]),
        out_specs=(pl.BlockSpec((tf, D_MODEL), amap), pl.BlockSpec((tf, D_MODEL), bmap)),
        out_shape=(jax.ShapeDtypeStruct(xa.shape, F32), jax.ShapeDtypeStruct(xb.shape, F32)),
        scratch_shapes=[pltpu.VMEM((tf, D_MODEL), BF16)],
        compiler_params=pltpu.CompilerParams(
            dimension_semantics=("arbitrary",), vmem_limit_bytes=VMEM_LIMIT_BYTES),
        name="ffn",
    )(xa, xb, *consts)


def kernel(x_prompt, x_sample, cache_k_win, cache_v_win, state_conv, state_ssm, g_pre_mix, w_in, attn_sinks, conv_w, conv_b, dt_bias, a_log, d_skip, g_ssm_out, w_out, g_post_mix, g_pre_ffn, w_gate, w_up, w_down, g_post_ffn):
    depth = w_in.shape[0]
    bp, lp, _ = x_prompt.shape
    nb, ts, _ = x_sample.shape
    hp, hs = x_prompt, x_sample
    outs = [[] for _ in range(8)]
    for l in range(depth):
        p = _layer_params(g_pre_mix[l], w_in[l], attn_sinks[l], conv_w[l], conv_b[l], dt_bias[l], a_log[l],
                          d_skip[l], g_ssm_out[l], w_out[l], g_post_mix[l])
        ffn_w = (g_pre_ffn[l].reshape(1, D_MODEL), w_gate[l].astype(BF16), w_up[l].astype(BF16),
                 w_down[l].astype(BF16), g_post_ffn[l].reshape(1, D_MODEL))
        x1p, kp, vp, cp, sp = _prompt_mixer(hp, p)
        x1s, ksm, vsm, csm, ssm = _sample_mixer(hs, cache_k_win[l], cache_v_win[l], state_conv[l], state_ssm[l], p)
        hp, hs = _ffn(x1p.reshape(bp * lp, D_MODEL), x1s, *ffn_w)
        hp = hp.reshape(bp, lp, D_MODEL)
        hs = hs.reshape(nb, ts, D_MODEL)
        ssm_shape = (SSM_HEADS, SSM_HEAD_DIM, D_STATE)
        for lst, val in zip(outs, (_window_major(kp), _window_major(vp), cp,
                                   sp.reshape((bp,) + ssm_shape),
                                   _window_major(ksm), _window_major(vsm), jnp.transpose(csm, (1, 0, 2)),
                                   ssm.reshape((nb,) + ssm_shape))):
            lst.append(val)
    return (hp, hs) + tuple(jnp.stack(o) for o in outs)
```

```python
import functools
import math

import numpy as np
import jax
import jax.numpy as jnp
from jax import lax
from jax.experimental import pallas as pl
from jax.experimental.pallas import tpu as pltpu

F32 = jnp.float32
BF16 = jnp.bfloat16

D_MODEL = 1024
ATT_HEADS = 16
ATT_KV_HEADS = 4
ATT_GROUP = ATT_HEADS // ATT_KV_HEADS
HEAD_DIM = 64
ATT_WIDTH = ATT_HEADS * HEAD_DIM
KV_WIDTH = ATT_KV_HEADS * HEAD_DIM
WINDOW = 128
ROT_DIM = HEAD_DIM // 4
ROPE_THETA = 500000.0
SSM_HEADS = 16
SSM_HEAD_DIM = 64
SSM_WIDTH = SSM_HEADS * SSM_HEAD_DIM
SSM_GROUPS = 2
SSM_HPG = SSM_HEADS // SSM_GROUPS
SSM_GROUP_W = SSM_WIDTH // SSM_GROUPS
D_STATE = 128
CONV_W = 4
CONV_DIM = SSM_WIDTH + 2 * SSM_GROUPS * D_STATE
MIX_WIDTH = ATT_WIDTH + SSM_WIDTH
EPS = 1e-6
PAST_LEN = 8192

LANES = 128
SUBLANES = 8
VMEM_LIMIT_BYTES = 60 * 1024 * 1024

CHUNK = 128
NEG = -1e30
LOG2E = math.log2(math.e)
Q_SCALE = HEAD_DIM ** -0.5 * LOG2E
SEQ_TILE = 512
SAMPLE_BT = 16
SAMPLE_PB = 4
FFN_TILE = 512
PROJ_ROWS = 256
OUT_CB = 512
ROW_BLOCK = 32
FF_CHUNK = 256


def _nn(a, b):
    return jnp.dot(a, b, preferred_element_type=F32)


def _nt(a, b):
    return lax.dot_general(a, b, (((1,), (1,)), ((), ())), preferred_element_type=F32)


def _split_bf16(x, n):
    parts = []
    r = x
    for i in range(n):
        p = r.astype(BF16)
        parts.append(p)
        if i + 1 < n:
            r = r - p.astype(F32)
    return parts


def _expand_heads(x, expand2_ref):
    hi, mid = _split_bf16(x, 2)
    return _nn(jnp.concatenate([hi, mid], axis=1), expand2_ref[...])


def _cumsum_cols(m01, x):
    w = x.shape[1]
    r = _nn(m01, jnp.concatenate(_split_bf16(x, 3), axis=1))
    return r[:, :w] + r[:, w:2 * w] + r[:, 2 * w:]


def _heads_to_rows(x):
    return x.T[:SSM_HEADS]


def _rms(x, g):
    ms = jnp.mean(x * x, axis=-1, keepdims=True)
    return x * lax.rsqrt(ms + EPS) * g


def _rms_rows(src, g_ref, dst_ref, res_ref=None):
    g = g_ref[...]
    for r0 in range(0, dst_ref.shape[0], ROW_BLOCK):
        rs = slice(r0, r0 + ROW_BLOCK)
        y = _rms(src[rs, :], g)
        if res_ref is not None:
            y = res_ref[rs, :] + y
        dst_ref[rs, :] = y.astype(dst_ref.dtype)


def _silu(x):
    h = 0.5 * x
    return h + h * jnp.tanh(h)


R_K = ATT_WIDTH
R_V = R_K + KV_WIDTH
R_Z = R_V + KV_WIDTH
R_XBC = R_Z + SSM_WIDTH
R_END = R_XBC + CONV_DIM

def _conv4(x, w, bias):
    assert CONV_W == 4
    u = pltpu.roll(x, 2, 0)
    return bias + (x * w[3] + u * w[1]) + pltpu.roll(x * w[2] + u * w[0], 1, 0)


CONV_CB = 256
CONV_RB = 64


def _conv_silu_cols(xpad_ref, row0, ra, rb, convw_ref, convb_ref, out_ref, c0):
    cs = slice(c0, c0 + CONV_CB)
    w = [convw_ref[i:i + 1, cs] for i in range(CONV_W)]
    bias = convb_ref[:, cs]
    for r0 in range(ra, rb, CONV_RB):
        xh = xpad_ref[row0 + r0 - SUBLANES:row0 + r0 + CONV_RB, cs]
        out_ref[r0:r0 + CONV_RB, cs] = _silu(_conv4(xh, w, bias)[SUBLANES:])


def _softplus(x):
    return jnp.maximum(x, 0.0) + jnp.log1p(jnp.exp(-jnp.abs(x)))


def _rope(x, c, s):
    lane = lax.broadcasted_iota(jnp.int32, (1, LANES), 1)
    first = (lane % HEAD_DIM) < ROT_DIM // 2
    outs = []
    for j in range(x.shape[1] // LANES):
        xb = x[:, j * LANES:(j + 1) * LANES]
        partner = jnp.where(first, pltpu.roll(xb, LANES - ROT_DIM // 2, 1), pltpu.roll(xb, ROT_DIM // 2, 1))
        outs.append(xb * c + partner * s)
    return outs[0] if len(outs) == 1 else jnp.concatenate(outs, axis=1)


def _iota(shape, dim):
    return lax.broadcasted_iota(jnp.int32, shape, dim)


def _head_blocks(c_out):
    out = []
    for half in range(2):
        g, kvh = divmod(2 * c_out + half, ATT_KV_HEADS)
        b_in = kvh * ATT_GROUP + g
        out.append((b_in // 2, b_in % 2))
    return out


def _permute_q_weight(wr_ref, wq_s):
    low = _iota((1, LANES), 1) < HEAD_DIM
    for c_out in range(ATT_WIDTH // LANES):
        halves = []
        for half, (c_in, src_half) in enumerate(_head_blocks(c_out)):
            col = wr_ref[:, c_in * LANES:(c_in + 1) * LANES]
            halves.append(col if src_half == half else pltpu.roll(col, HEAD_DIM, 1))
        wq_s[:, c_out * LANES:(c_out + 1) * LANES] = jnp.where(low, halves[0], halves[1])


def _project(hn, wq_ref, wr_ref, ropec, ropes):
    q = _rope(_nn(hn, wq_ref[...]), ropec, ropes) * (HEAD_DIM ** -0.5)
    kv = _nn(hn, wr_ref[:, R_K:R_Z])
    k = _rope(kv[:, :KV_WIDTH], ropec, ropes)
    v = kv[:, KV_WIDTH:]
    return q, k, v


def _dt_cols(hn, wdt_ref, dtb_row_ref):
    return _softplus(_nn(hn, wdt_ref[...]) + dtb_row_ref[...])


def _a_row(alog_row_ref):
    lane = _iota((1, LANES), 1)
    return jnp.where(lane < SSM_HEADS, -jnp.exp(alog_row_ref[...]), 0.0)


def _log2_decay(acol):
    acol2 = acol * LOG2E
    return acol2, _heads_to_rows(acol2)


SSD_QUAD = 4
SSD_NQUAD = SSM_HEADS // SSD_QUAD


def _ssd_cb(b_all, c_all):
    return [_nt(c_all[:, g * D_STATE:(g + 1) * D_STATE].astype(BF16),
                b_all[:, g * D_STATE:(g + 1) * D_STATE].astype(BF16)) for g in range(SSM_GROUPS)]


def _ssd_quad(qi, xs_bf, c_all, cbs, acol2, arow2, dtr, mask_bool, hT_bf=None):
    lane4 = _iota((1, SSD_QUAD * SSM_HEAD_DIM), 1) // SSM_HEAD_DIM
    zero = jnp.zeros((), BF16)
    e0 = SSD_QUAD * qi
    g = e0 // SSM_HPG
    cf = c_all[:, g * D_STATE:(g + 1) * D_STATE]
    lanes = slice(e0 * SSM_HEAD_DIM, (e0 + SSD_QUAD) * SSM_HEAD_DIM)
    xq = xs_bf[:, lanes]
    lhs, rhs = [], []
    for i in range(SSD_QUAD):
        e = e0 + i
        a_t = jnp.broadcast_to(acol2[:, e:e + 1], (CHUNK, CHUNK))
        w = cbs[g] * jnp.exp2(jnp.where(mask_bool, a_t - arow2[e:e + 1, :], NEG)) * dtr[e:e + 1, :]
        lhs.append(w.astype(BF16))
        rhs.append(jnp.where(lane4 == i, xq, zero))
        if hT_bf is not None:
            lhs.append((cf * jnp.exp2(a_t)).astype(BF16))
            rhs.append(jnp.where(lane4 == i, hT_bf[:, lanes], zero))
    return _nn(jnp.concatenate(lhs, axis=1), jnp.concatenate(rhs, axis=0))


def _ssd_block(xs_bf, b_all, c_all, acol2, arow2, dtr, mask_bool):
    cbs = _ssd_cb(b_all, c_all)
    return jnp.concatenate([_ssd_quad(qi, xs_bf, c_all, cbs, acol2, arow2, dtr, mask_bool)
                            for qi in range(SSD_NQUAD)], axis=1)


def _gate_rows(rs, y_refs, xc_s, z, dskip_ref, gssm_ref, mix_s):
    y = y_refs[0][rs, :]
    for extra in y_refs[1:]:
        y = y + extra[rs, :]
    gated = (y + dskip_ref[...] * xc_s[rs, 0:SSM_WIDTH]) * _silu(z[rs, :])
    for g in range(SSM_GROUPS):
        gs = slice(g * SSM_GROUP_W, (g + 1) * SSM_GROUP_W)
        gg = gated[:, gs]
        ms = jnp.mean(gg * gg, axis=-1, keepdims=True)
        o = gg * lax.rsqrt(ms + EPS) * gssm_ref[:, gs]
        mix_s[rs, ATT_WIDTH + g * SSM_GROUP_W:ATT_WIDTH + (g + 1) * SSM_GROUP_W] = o.astype(mix_s.dtype)


def _gate_and_out(y_refs, xc_s, z, x_ref, dskip_ref, gssm_ref, wout_ref, gpost_ref, mix_s, out_ref):
    for r0 in range(0, out_ref.shape[0], ROW_BLOCK):
        _gate_rows(slice(r0, r0 + ROW_BLOCK), y_refs, xc_s, z, dskip_ref, gssm_ref, mix_s)
    mo = _nn(mix_s[...].astype(BF16), wout_ref[...])
    _rms_rows(mo, gpost_ref, out_ref, res_ref=x_ref)


def _prompt_kernel(x_ref, gpre_ref, wr_ref, wdt_ref,
                   ropec_ref, ropes_ref, sink_ref, biasT_ref,
                   convw_ref, convb_ref, dtb_row_ref, alog_row_ref,
                   dskip_ref, gssm_ref, wout_ref, gpost_ref, expand2_ref,
                   x1_ref, nk_ref, nv_ref, nconv_ref, nssm_ref,
                   hn_s, q_s, kbuf, vT_s, xbc_s, xc_s, dtc_s, y_s, mix_s, hT_s, wq_s, z_s, mo_s):
    tm = x_ref.shape[0]
    nchunk = tm // CHUNK
    s = pl.program_id(1)
    last = pl.num_programs(1) - 1

    @pl.when(jnp.logical_and(pl.program_id(0) == 0, s == 0))
    def _():
        _permute_q_weight(wr_ref, wq_s)

    @pl.when(s == 0)
    def _():
        kbuf[0:CHUNK, :] = jnp.zeros((CHUNK, KV_WIDTH), BF16)
        vT_s[0] = jnp.zeros((KV_WIDTH, CHUNK), BF16)
        xbc_s[0:SUBLANES, :] = jnp.zeros((SUBLANES, CONV_DIM), F32)
        hT_s[...] = jnp.zeros_like(hT_s)

    _rms_rows(x_ref, gpre_ref, hn_s)

    def projection_pieces(ra, rb):
        rows = slice(ra, rb)

        def rope(x):
            return _rope(x, ropec_ref[rows, :], ropes_ref[rows, :])

        def proj_xbc(c0):
            xbc_s[SUBLANES + ra:SUBLANES + rb, c0:c0 + CONV_CB] = _nn(
                hn_s[rows, :], wr_ref[:, R_XBC + c0:R_XBC + c0 + CONV_CB])

        def conv(c0):
            _conv_silu_cols(xbc_s, SUBLANES, ra, rb, convw_ref, convb_ref, xc_s, c0)

        def proj_q(c0):
            q_s[rows, c0:c0 + KV_WIDTH] = (
                rope(_nn(hn_s[rows, :], wq_s[:, c0:c0 + KV_WIDTH])) * Q_SCALE).astype(BF16)

        def proj_k():
            kbuf[CHUNK + ra:CHUNK + rb, :] = rope(_nn(hn_s[rows, :], wr_ref[:, R_K:R_V])).astype(BF16)

        def proj_v_dt():
            v = _nn(hn_s[rows, :], wr_ref[:, R_V:R_Z])
            for j in range(ra // CHUNK, rb // CHUNK):
                vT_s[1 + j] = v[j * CHUNK - ra:(j + 1) * CHUNK - ra, :].T.astype(BF16)
            dtc_s[rows, :] = _dt_cols(hn_s[rows, :], wdt_ref, dtb_row_ref)

        others = [functools.partial(proj_q, c0) for c0 in range(0, ATT_WIDTH, KV_WIDTH)] + [proj_k, proj_v_dt]
        conv_cols = list(range(0, CONV_DIM, CONV_CB))
        pieces = [functools.partial(proj_xbc, conv_cols[0])]
        for n, c0 in enumerate(conv_cols):
            if n + 1 < len(conv_cols):
                pieces.append(functools.partial(proj_xbc, conv_cols[n + 1]))
            pieces.append(functools.partial(conv, c0))
            if n < len(others):
                pieces.append(others[n])
        return pieces + others[len(conv_cols):]

    a_row = _a_row(alog_row_ref)

    r2 = _iota((CHUNK, CHUNK), 0)
    c2 = _iota((CHUNK, CHUNK), 1)
    tril = c2 <= r2
    tril_bf = tril.astype(BF16)
    lane_kv = _iota((1, KV_WIDTH), 1) // HEAD_DIM
    ones_rows = jnp.ones((2 * SUBLANES, 2 * CHUNK), BF16)

    def chunk_body(c, extras):
        r0 = c * CHUNK
        per = -(-len(extras) // ATT_KV_HEADS)
        first = jnp.logical_and(s == 0, c == 0).astype(jnp.int32)
        bias = biasT_ref[first]
        qcat = jnp.concatenate([q_s[pl.ds(r0, CHUNK), g * KV_WIDTH:(g + 1) * KV_WIDTH]
                                for g in range(ATT_GROUP)], axis=0)
        kwin = kbuf[pl.ds(r0, 2 * CHUNK), :]
        kstack = jnp.concatenate([jnp.where(lane_kv == kvh, kwin, jnp.zeros((), BF16))
                                  for kvh in range(ATT_KV_HEADS)], axis=0)
        sT = _nt(kstack, qcat)
        vT_win = jnp.concatenate([vT_s[c], vT_s[c + 1]], axis=1)

        dtc_c = dtc_s[pl.ds(r0, CHUNK), :]
        acol = _cumsum_cols(tril_bf, dtc_c * a_row)
        a_end = acol[CHUNK - 1:CHUNK, :]
        tailc = jnp.exp(a_end - acol) * dtc_c
        ex = _expand_heads(
            jnp.concatenate([tailc, jnp.broadcast_to(jnp.exp(a_end), (SUBLANES, LANES))], axis=0), expand2_ref)
        tlx = ex[:CHUNK]
        dec_row = ex[CHUNK:CHUNK + 1]
        xs = xc_s[pl.ds(r0, CHUNK), 0:SSM_WIDTH]
        b_all = xc_s[pl.ds(r0, CHUNK), SSM_WIDTH:SSM_WIDTH + SSM_GROUPS * D_STATE]
        c_all = xc_s[pl.ds(r0, CHUNK), SSM_WIDTH + SSM_GROUPS * D_STATE:CONV_DIM]
        hT = hT_s[...]
        acol2, arow2 = _log2_decay(acol)
        dtr_c = _heads_to_rows(dtc_c)
        xs_bf = xs.astype(BF16)
        hT_bf = hT.astype(BF16)
        cbs = _ssd_cb(b_all, c_all)

        o_rows = []
        for i in range(ATT_KV_HEADS):
            blk = sT[i * 2 * CHUNK:(i + 1) * 2 * CHUNK] + bias
            sink = sink_ref[i] * LOG2E
            m = jnp.maximum(jnp.max(blk, axis=0, keepdims=True), sink)
            p = jnp.exp2(blk - m).astype(BF16)
            lhs = jnp.concatenate([vT_win[i * HEAD_DIM:(i + 1) * HEAD_DIM], ones_rows], axis=0)
            oT = _nn(lhs, p)
            den = oT[HEAD_DIM:HEAD_DIM + 1] + jnp.exp2(sink - m)
            o_rows.append(oT[:HEAD_DIM] * (1.0 / den))
            for qi in range(i * SSD_NQUAD // ATT_KV_HEADS, (i + 1) * SSD_NQUAD // ATT_KV_HEADS):
                lanes = slice(qi * SSD_QUAD * SSM_HEAD_DIM, (qi + 1) * SSD_QUAD * SSM_HEAD_DIM)
                y_s[pl.ds(r0, CHUNK), lanes] = _ssd_quad(qi, xs_bf, c_all, cbs, acol2, arow2, dtr_c, tril, hT_bf)
            for t in extras[i * per:(i + 1) * per]:
                t()
        for c_out in range(ATT_WIDTH // LANES):
            kvh, g0 = divmod(2 * c_out, ATT_GROUP)
            two = jnp.concatenate([o_rows[kvh][:, g * CHUNK:(g + 1) * CHUNK] for g in (g0, g0 + 1)], axis=0)
            mix_s[pl.ds(r0, CHUNK), c_out * LANES:(c_out + 1) * LANES] = two.T.astype(BF16)

        xtl = (xs * tlx).astype(BF16)
        for g in range(SSM_GROUPS):
            sl = slice(g * SSM_GROUP_W, (g + 1) * SSM_GROUP_W)
            bt = b_all[:, g * D_STATE:(g + 1) * D_STATE].T.astype(BF16)
            hT_s[:, sl] = hT[:, sl] * dec_row[:, sl] + _nn(bt, xtl[:, sl])

    def output_pieces(ra, rb):
        rows = slice(ra, rb)
        g_post = gpost_ref[...]

        def proj_z(c0):
            z_s[rows, c0:c0 + OUT_CB] = _nn(hn_s[rows, :], wr_ref[:, R_Z + c0:R_Z + c0 + OUT_CB])

        def gate(r0):
            _gate_rows(slice(r0, r0 + ROW_BLOCK), (y_s,), xc_s, z_s, dskip_ref, gssm_ref, mix_s)

        def proj_out(c0):
            mo_s[rows, c0:c0 + OUT_CB] = _nn(mix_s[rows, :], wout_ref[:, c0:c0 + OUT_CB])

        def post(r0):
            rs = slice(r0, r0 + ROW_BLOCK)
            x1_ref[rs, :] = x_ref[rs, :] + _rms(mo_s[rs, :], g_post)

        return ([functools.partial(proj_z, c0) for c0 in range(0, SSM_WIDTH, OUT_CB)]
                + [functools.partial(gate, r0) for r0 in range(ra, rb, ROW_BLOCK)]
                + [functools.partial(proj_out, c0) for c0 in range(0, D_MODEL, OUT_CB)]
                + [functools.partial(post, r0) for r0 in range(ra, rb, ROW_BLOCK)])

    groups = list(range(0, tm, PROJ_ROWS))
    cpg = PROJ_ROWS // CHUNK
    for t in projection_pieces(0, PROJ_ROWS):
        t()
    for gi, ra in enumerate(groups):
        side = projection_pieces(ra + PROJ_ROWS, ra + 2 * PROJ_ROWS) if gi + 1 < len(groups) else []
        if gi > 0:
            side = side + output_pieces(ra - PROJ_ROWS, ra)
        share = -(-len(side) // cpg)
        for k in range(cpg):
            chunk_body(ra // CHUNK + k, side[k * share:(k + 1) * share])

    kbuf[0:CHUNK, :] = kbuf[tm:tm + CHUNK, :]
    vT_s[0] = vT_s[nchunk]
    xbc_s[0:SUBLANES, :] = xbc_s[tm:tm + SUBLANES, :]

    for t in output_pieces(tm - PROJ_ROWS, tm):
        t()

    @pl.when(s == last)
    def _():
        hn_w = hn_s[tm - WINDOW:, :]
        nk_ref[...] = _rope(_nn(hn_w, wr_ref[:, R_K:R_V]), ropec_ref[tm - WINDOW:, :], ropes_ref[tm - WINDOW:, :]).T
        nv_ref[...] = _nn(hn_w, wr_ref[:, R_V:R_Z]).T
        nconv_ref[...] = xbc_s[SUBLANES - (CONV_W - 1):SUBLANES, :]
        nssm_ref[...] = hT_s[...].T


def _sample_kernel(x_ref, ck_ref, cv_ref, sconv_ref, sssm_ref,
                   gpre_ref, wr_ref, wdt_ref,
                   ropec_ref, ropes_ref, sink_ref, bias_ref,
                   convw_ref, convb_ref, dtb_row_ref, alog_row_ref,
                   dskip_ref, gssm_ref, wout_ref, gpost_ref, expand2_ref,
                   x1_ref, nk_ref, nv_ref, nconv_ref, nssm_ref,
                   hn_s, q_s, knr_s, vnr_s, knb_s, vnb_s, xpad_s, xc_s, bb_s, ea_s, eax_s, xT_s, y_s, yoff_s, mix_s, wq_s):
    bt_n = sconv_ref.shape[1]
    pb_n = ck_ref.shape[0]
    m_rows = x_ref.shape[0]
    tdec = m_rows // bt_n
    j = pl.program_id(1)

    @pl.when(jnp.logical_and(pl.program_id(0) == 0, j == 0))
    def _():
        _permute_q_weight(wr_ref, wq_s)

    @pl.when(j == 0)
    def _():
        _rms_rows(x_ref, gpre_ref, hn_s)
        hn = hn_s[...]
        q, k, v = _project(hn, wq_s, wr_ref, ropec_ref[...], ropes_ref[...])
        q_s[...] = q
        vnb_s[...] = v.astype(BF16)
        k_t, v_t = k.T, v.T
        knb_s[...] = k_t.astype(BF16)
        for b in range(bt_n):
            shift = WINDOW - (b + 1) * tdec
            knr_s[b] = pltpu.roll(k_t, shift, 1) if shift else k_t
            vnr_s[b] = pltpu.roll(v_t, shift, 1) if shift else v_t

        xbc = _nn(hn, wr_ref[:, R_XBC:R_END])
        xpad_s[:, 0:SUBLANES - 3, :] = jnp.zeros((bt_n, SUBLANES - 3, CONV_DIM), F32)
        for r in range(CONV_W - 1):
            xpad_s[:, SUBLANES - 3 + r, :] = sconv_ref[r]
        xpad_s[:, SUBLANES:2 * SUBLANES, :] = xbc.reshape(bt_n, tdec, CONV_DIM)
        for r in range(CONV_W - 1):
            nconv_ref[r] = xpad_s[:, 2 * SUBLANES - 3 + r, :]
        for c0 in range(0, CONV_DIM, CONV_CB):
            cs = slice(c0, c0 + CONV_CB)
            xh = xpad_s[:, :, cs].reshape(bt_n * 2 * SUBLANES, CONV_CB)
            acc = _conv4(xh, [convw_ref[i:i + 1, cs] for i in range(CONV_W)], convb_ref[:, cs])
            xc_s[:, cs] = _silu(acc.reshape(bt_n, 2 * SUBLANES, CONV_CB)[:, SUBLANES:, :].reshape(m_rows, CONV_CB))
        xs = xc_s[:, 0:SSM_WIDTH]
        b_all = xc_s[:, SSM_WIDTH:SSM_WIDTH + SSM_GROUPS * D_STATE]
        c_all = xc_s[:, SSM_WIDTH + SSM_GROUPS * D_STATE:CONV_DIM]
        bb_s[...] = b_all.astype(BF16)

        dtc = _dt_cols(hn, wdt_ref, dtb_row_ref)
        a_row = _a_row(alog_row_ref)

        r2 = _iota((m_rows, m_rows), 0)
        c2 = _iota((m_rows, m_rows), 1)
        same = (r2 // tdec) == (c2 // tdec)
        causal = jnp.logical_and(same, c2 <= r2)
        causal_bf = causal.astype(BF16)
        same_bf = same.astype(BF16)

        dac = dtc * a_row
        acol = _cumsum_cols(causal_bf, dac)
        alast = _cumsum_cols(same_bf, dac)
        tailc = jnp.exp(alast - acol) * dtc
        ex = _expand_heads(jnp.concatenate([jnp.exp(acol), tailc], axis=0), expand2_ref)
        ea_s[...] = jnp.exp(alast)
        eax_s[...] = ex[:m_rows]
        acol2, arow2 = _log2_decay(acol)
        y_s[...] = _ssd_block(xs.astype(BF16), b_all, c_all, acol2, arow2, _heads_to_rows(dtc), causal)
        xtl = xs * ex[m_rows:]
        for jj in range(SSM_WIDTH // LANES):
            xT_s[jj * LANES:(jj + 1) * LANES, :] = xtl[:, jj * LANES:(jj + 1) * LANES].T.astype(BF16)

    lane_kv = _iota((1, KV_WIDTH), 1) // HEAD_DIM
    row_b = _iota((m_rows, 1), 0) // tdec
    low_half = _iota((1, LANES), 1) < HEAD_DIM
    sink = sink_ref[...]
    new_lane = _iota((1, WINDOW), 1) >= WINDOW - tdec

    pbs = range(pb_n)
    bs = [j * pb_n + pb for pb in pbs]
    rs = [pl.multiple_of(b * tdec, tdec) for b in bs]

    qbd, kc, vc, sc = [], [], [], []
    for pb in pbs:
        q8 = q_s[pl.ds(rs[pb], tdec), :]
        qbd.append(jnp.concatenate(
            [jnp.where(lane_kv == kvh, q8[:, g * KV_WIDTH:(g + 1) * KV_WIDTH], 0.0)
             for g in range(ATT_GROUP) for kvh in range(ATT_KV_HEADS)], axis=0).astype(BF16))
        kc.append(ck_ref[pb])
        vc.append(cv_ref[pb])
    for pb in pbs:
        keys_t = jnp.concatenate([kc[pb].astype(BF16), knb_s[...]], axis=1)
        sc.append(_nn(qbd[pb], keys_t) + bias_ref[bs[pb]])
    p, inv = [], []
    for pb in pbs:
        m = jnp.maximum(jnp.max(sc[pb], axis=1, keepdims=True), sink)
        e = jnp.exp(sc[pb] - m)
        inv.append(1.0 / (jnp.sum(e, axis=1, keepdims=True) + jnp.exp(sink - m)))
        p.append(e.astype(BF16))
    for pb in pbs:
        o = (_nt(p[pb][:, :WINDOW], vc[pb].astype(BF16)) + _nn(p[pb][:, WINDOW:], vnb_s[...])) * inv[pb]
        for c_out in range(ATT_WIDTH // LANES):
            kvh, g0 = divmod(2 * c_out, ATT_GROUP)
            halves = []
            for half in range(2):
                i0 = ((g0 + half) * ATT_KV_HEADS + kvh) * tdec
                piece = o[i0:i0 + tdec, (kvh // 2) * LANES:(kvh // 2 + 1) * LANES]
                halves.append(piece if kvh % 2 == half else pltpu.roll(piece, HEAD_DIM, 1))
            mix_s[pl.ds(rs[pb], tdec), c_out * LANES:(c_out + 1) * LANES] = jnp.where(low_half, halves[0], halves[1])
        nk_ref[pb] = jnp.where(new_lane, knr_s[bs[pb]], pltpu.roll(kc[pb], WINDOW - tdec, 1))
        nv_ref[pb] = jnp.where(new_lane, vnr_s[bs[pb]], pltpu.roll(vc[pb], WINDOW - tdec, 1))

    for g in range(SSM_GROUPS):
        sl = slice(g * SSM_GROUP_W, (g + 1) * SSM_GROUP_W)
        c0 = SSM_WIDTH + (SSM_GROUPS + g) * D_STATE
        hg, upd = [], []
        for pb in pbs:
            hg.append(sssm_ref[pb, sl, :])
            bm = jnp.where(row_b == bs[pb], bb_s[:, g * D_STATE:(g + 1) * D_STATE], jnp.zeros((), BF16))
            upd.append(_nn(xT_s[sl, :], bm))
        for pb in pbs:
            cc = xc_s[pl.ds(rs[pb], tdec), c0:c0 + D_STATE].astype(BF16)
            yoff_s[pl.ds(rs[pb], tdec), sl] = (_nt(cc, hg[pb].astype(BF16))
                                               * eax_s[pl.ds(rs[pb], tdec), sl])
        for pb in pbs:
            ea_b = ea_s[pl.ds(rs[pb], 1), :]
            dec = jnp.concatenate(
                [jnp.broadcast_to(ea_b[:, g * SSM_HPG + e:g * SSM_HPG + e + 1], (SSM_HEAD_DIM, D_STATE))
                 for e in range(SSM_HPG)], axis=0)
            nssm_ref[pb, sl, :] = hg[pb] * dec + upd[pb]

    @pl.when(j == pl.num_programs(1) - 1)
    def _():
        z = _nn(hn_s[...], wr_ref[:, R_Z:R_XBC])
        _gate_and_out((y_s, yoff_s), xc_s, z, x_ref, dskip_ref, gssm_ref, wout_ref, gpost_ref, mix_s, x1_ref)


def _ffn_kernel(xa_ref, xb_ref, gpre_ref, wg_ref, wu_ref, wd_ref, gpost_ref, oa_ref, ob_ref, f_s, *, steps_a):
    def tile(x_ref, o_ref):
        _rms_rows(x_ref, gpre_ref, f_s)
        f = f_s[...]
        d_ff = wg_ref.shape[1]
        acc = jnp.zeros(x_ref.shape, F32)
        for j in range(d_ff // FF_CHUNK):
            sl = slice(j * FF_CHUNK, (j + 1) * FF_CHUNK)
            gate = _nn(f, wg_ref[:, sl])
            up = _nn(f, wu_ref[:, sl])
            acc = acc + _nn((_silu(gate) * up).astype(BF16), wd_ref[sl, :])
        _rms_rows(acc, gpost_ref, o_ref, res_ref=x_ref)

    i = pl.program_id(0)

    @pl.when(i < steps_a)
    def _():
        tile(xa_ref, oa_ref)

    @pl.when(i >= steps_a)
    def _():
        tile(xb_ref, ob_ref)


def _const_spec(shape):
    nd = len(shape)
    return pl.BlockSpec(shape, lambda *_: (0,) * nd, pipeline_mode=pl.Buffered(1))


def _rope_tables(pos):
    half = ROT_DIM // 2
    inv = ROPE_THETA ** (-np.arange(half, dtype=np.float64) * 2.0 / ROT_DIM)
    ang = pos.astype(np.float64)[:, None] * inv[None, :]
    cos = np.cos(ang).astype(np.float32)
    sin = np.sin(ang).astype(np.float32)
    n = pos.shape[0]
    pad = HEAD_DIM - ROT_DIM
    c = np.concatenate([cos, cos, np.ones((n, pad), np.float32)], axis=1)
    s = np.concatenate([-sin, sin, np.zeros((n, pad), np.float32)], axis=1)
    rep = LANES // HEAD_DIM
    return tuple(jnp.asarray(np.tile(t, (1, rep))) for t in (c, s))


def _spread(values, index):
    hit = np.arange(values.shape[0]).reshape((-1,) + (1,) * index.ndim) == index[None]
    return jnp.sum(jnp.where(hit, values.reshape((-1,) + (1,) * index.ndim), 0.0), axis=0)


def _layer_params(g_pre_mix, w_in, attn_sinks, conv_w, conv_b, dt_bias, a_log, d_skip, g_ssm_out, w_out, g_post_mix):
    wr = w_in.astype(BF16)
    wdt = jnp.pad(w_in[:, R_END:], ((0, 0), (0, LANES - SSM_HEADS))).astype(BF16)
    wout = w_out.astype(BF16)
    pad16 = ((0, 0), (0, LANES - SSM_HEADS))
    expand = (np.arange(LANES)[:, None] == (np.arange(SSM_WIDTH)[None, :] // SSM_HEAD_DIM)).astype(np.float32)
    return dict(
        gpre=g_pre_mix.reshape(1, D_MODEL), wr=wr, wdt=wdt,
        convw=conv_w, convb=conv_b.reshape(1, CONV_DIM),
        dtb_row=jnp.pad(dt_bias.reshape(1, SSM_HEADS), pad16),
        alog_row=jnp.pad(a_log.reshape(1, SSM_HEADS), pad16),
        dskip=jnp.repeat(d_skip, SSM_HEAD_DIM).reshape(1, SSM_WIDTH), gssm=g_ssm_out.reshape(1, SSM_WIDTH),
        wout=wout, gpost=g_post_mix.reshape(1, D_MODEL), expand2=jnp.asarray(np.concatenate([expand, expand], axis=0), BF16),
        sinks=attn_sinks.reshape(ATT_KV_HEADS * ATT_GROUP),
    )


_WEIGHT_ORDER = ("gpre", "wr", "wdt")
_TAIL_ORDER = ("convw", "convb", "dtb_row", "alog_row",
               "dskip", "gssm", "wout", "gpost", "expand2")


def _prompt_mixer(x, p):
    bsz, seq, _ = x.shape
    tm = SEQ_TILE
    nchunk = tm // CHUNK
    c, sn = _rope_tables(np.arange(seq))
    sink = _spread(p["sinks"], np.arange(ATT_KV_HEADS)[:, None, None] * ATT_GROUP
                   + np.arange(ATT_GROUP * CHUNK)[None, None, :] // CHUNK)
    jj = np.arange(2 * CHUNK)[:, None]
    tt = np.arange(CHUNK)[None, :]
    vis = (jj >= tt) & (jj <= tt + WINDOW)
    bias = np.stack([np.where(vis, 0.0, NEG), np.where(vis & (jj >= CHUNK), 0.0, NEG)]).astype(np.float32)
    bias = jnp.asarray(np.tile(bias, (1, 1, ATT_GROUP)))
    consts = [p[n] for n in _WEIGHT_ORDER]
    tail = [p[n] for n in _TAIL_ORDER]
    in_specs = ([pl.BlockSpec((None, tm, D_MODEL), lambda b, s: (b, s, 0))]
                + [_const_spec(a.shape) for a in consts]
                + [pl.BlockSpec((tm, LANES), lambda b, s: (s, 0))] * 2
                + [_const_spec(sink.shape), _const_spec(bias.shape)]
                + [_const_spec(a.shape) for a in tail])
    out_shape = (
        jax.ShapeDtypeStruct((bsz, seq, D_MODEL), F32),
        jax.ShapeDtypeStruct((bsz, KV_WIDTH, WINDOW), F32),
        jax.ShapeDtypeStruct((bsz, KV_WIDTH, WINDOW), F32),
        jax.ShapeDtypeStruct((bsz, CONV_W - 1, CONV_DIM), F32),
        jax.ShapeDtypeStruct((bsz, SSM_WIDTH, D_STATE), F32),
    )
    out_specs = (
        pl.BlockSpec((None, tm, D_MODEL), lambda b, s: (b, s, 0)),
        pl.BlockSpec((None, KV_WIDTH, WINDOW), lambda b, s: (b, 0, 0)),
        pl.BlockSpec((None, KV_WIDTH, WINDOW), lambda b, s: (b, 0, 0)),
        pl.BlockSpec((None, CONV_W - 1, CONV_DIM), lambda b, s: (b, 0, 0)),
        pl.BlockSpec((None, SSM_WIDTH, D_STATE), lambda b, s: (b, 0, 0)),
    )
    scratch = [
        pltpu.VMEM((tm, D_MODEL), BF16),
        pltpu.VMEM((tm, ATT_WIDTH), BF16),
        pltpu.VMEM((CHUNK + tm, KV_WIDTH), BF16),
        pltpu.VMEM((nchunk + 1, KV_WIDTH, CHUNK), BF16),
        pltpu.VMEM((SUBLANES + tm, CONV_DIM), F32),
        pltpu.VMEM((tm, CONV_DIM), F32),
        pltpu.VMEM((tm, LANES), F32),
        pltpu.VMEM((tm, SSM_WIDTH), F32),
        pltpu.VMEM((tm, MIX_WIDTH), BF16),
        pltpu.VMEM((D_STATE, SSM_WIDTH), F32),
        pltpu.VMEM((D_MODEL, ATT_WIDTH), BF16),
        pltpu.VMEM((tm, SSM_WIDTH), F32),
        pltpu.VMEM((tm, D_MODEL), F32),
    ]
    return pl.pallas_call(
        _prompt_kernel,
        grid=(bsz, seq // tm),
        in_specs=in_specs,
        out_specs=out_specs,
        out_shape=out_shape,
        scratch_shapes=scratch,
        compiler_params=pltpu.CompilerParams(
            dimension_semantics=("arbitrary", "arbitrary"), vmem_limit_bytes=VMEM_LIMIT_BYTES),
        name="prompt_mixer",
    )(x, *consts, c, sn, sink, bias, *tail)


def _sample_mixer(x, cache_k, cache_v, state_conv, state_ssm, p):
    nb, tdec, _ = x.shape
    bt = SAMPLE_BT
    rows = bt * tdec
    c, sn = _rope_tables(np.tile(PAST_LEN + np.arange(tdec), bt))
    r = np.arange(rows)[:, None]
    sink = _spread(p["sinks"], (r // tdec) % ATT_KV_HEADS * ATT_GROUP + r // (ATT_KV_HEADS * tdec))
    consts = [p[n] for n in _WEIGHT_ORDER]
    tail = [p[n] for n in _TAIL_ORDER]
    x2 = x.reshape(nb * tdec, D_MODEL)
    assert rows == WINDOW
    ck = jnp.transpose(cache_k, (0, 2, 3, 1)).reshape(nb, KV_WIDTH, WINDOW)
    cv = jnp.transpose(cache_v, (0, 2, 3, 1)).reshape(nb, KV_WIDTH, WINDOW)
    ssm = state_ssm.reshape(nb, SSM_WIDTH, D_STATE)
    pb = SAMPLE_PB
    steps = bt // pb
    t_of_row = (np.arange(rows) % tdec)[:, None]
    col = np.arange(rows)[None, :]
    vis_c = col >= t_of_row
    bias = jnp.asarray(np.stack(
        [np.where(np.concatenate([vis_c, (col // tdec == b) & (col % tdec <= t_of_row)], axis=1), 0.0, NEG)
         for b in range(bt)]).astype(np.float32))
    tmap = lambda i, j: (0, i, 0)
    pmap = lambda i, j: (i * steps + j, 0, 0)
    in_specs = ([pl.BlockSpec((rows, D_MODEL), lambda i, j: (i, 0)),
                 pl.BlockSpec((pb, KV_WIDTH, WINDOW), pmap),
                 pl.BlockSpec((pb, KV_WIDTH, WINDOW), pmap),
                 pl.BlockSpec((CONV_W - 1, bt, CONV_DIM), tmap),
                 pl.BlockSpec((pb, SSM_WIDTH, D_STATE), pmap)]
                + [_const_spec(a.shape) for a in consts]
                + [_const_spec(c.shape)] * 2
                + [_const_spec(sink.shape), _const_spec(bias.shape)]
                + [_const_spec(a.shape) for a in tail])
    out_shape = (
        jax.ShapeDtypeStruct((nb * tdec, D_MODEL), F32),
        jax.ShapeDtypeStruct((nb, KV_WIDTH, WINDOW), F32),
        jax.ShapeDtypeStruct((nb, KV_WIDTH, WINDOW), F32),
        jax.ShapeDtypeStruct((CONV_W - 1, nb, CONV_DIM), F32),
        jax.ShapeDtypeStruct((nb, SSM_WIDTH, D_STATE), F32),
    )
    out_specs = (
        pl.BlockSpec((rows, D_MODEL), lambda i, j: (i, 0)),
        pl.BlockSpec((pb, KV_WIDTH, WINDOW), pmap),
        pl.BlockSpec((pb, KV_WIDTH, WINDOW), pmap),
        pl.BlockSpec((CONV_W - 1, bt, CONV_DIM), tmap),
        pl.BlockSpec((pb, SSM_WIDTH, D_STATE), pmap),
    )
    scratch = [
        pltpu.VMEM((rows, D_MODEL), BF16),
        pltpu.VMEM((rows, ATT_WIDTH), F32),
        pltpu.VMEM((bt, KV_WIDTH, WINDOW), F32),
        pltpu.VMEM((bt, KV_WIDTH, WINDOW), F32),
        pltpu.VMEM((KV_WIDTH, rows), BF16),
        pltpu.VMEM((rows, KV_WIDTH), BF16),
        pltpu.VMEM((bt, 2 * SUBLANES, CONV_DIM), F32),
        pltpu.VMEM((rows, CONV_DIM), F32),
        pltpu.VMEM((rows, SSM_GROUPS * D_STATE), BF16),
        pltpu.VMEM((rows, LANES), F32),
        pltpu.VMEM((rows, SSM_WIDTH), F32),
        pltpu.VMEM((SSM_WIDTH, rows), BF16),
        pltpu.VMEM((rows, SSM_WIDTH), F32),
        pltpu.VMEM((rows, SSM_WIDTH), F32),
        pltpu.VMEM((rows, MIX_WIDTH), F32),
        pltpu.VMEM((D_MODEL, ATT_WIDTH), BF16),
    ]
    return pl.pallas_call(
        _sample_kernel,
        grid=(nb // bt, steps),
        in_specs=in_specs,
        out_specs=out_specs,
        out_shape=out_shape,
        scratch_shapes=scratch,
        compiler_params=pltpu.CompilerParams(
            dimension_semantics=("arbitrary", "arbitrary"), vmem_limit_bytes=VMEM_LIMIT_BYTES),
        name="sample_mixer",
    )(x2, ck, cv, jnp.transpose(state_conv, (1, 0, 2)), ssm, *consts, c, sn, sink, bias, *tail)


def _window_major(kv_t):
    return jnp.transpose(kv_t.reshape(kv_t.shape[0], ATT_KV_HEADS, HEAD_DIM, WINDOW), (0, 3, 1, 2))


def _ffn(xa, xb, gpre, wg, wu, wd, gpost):
    tf = FFN_TILE
    steps_a, steps_b = xa.shape[0] // tf, xb.shape[0] // tf
    consts = [gpre, wg, wu, wd, gpost]
    amap = lambda i: (jnp.minimum(i, steps_a - 1), 0)
    bmap = lambda i: (jnp.maximum(i - steps_a, 0), 0)
    return pl.pallas_call(
        functools.partial(_ffn_kernel, steps_a=steps_a),
        grid=(steps_a + steps_b,),
        in_specs=([pl.BlockSpec((tf, D_MODEL), amap), pl.BlockSpec((tf, D_MODEL), bmap)]
                  + [_const_spec(a.shape) for a in consts]),
        out_specs=(pl.BlockSpec((tf, D_MODEL), amap), pl.BlockSpec((tf, D_MODEL), bmap)),
        out_shape=(jax.ShapeDtypeStruct(xa.shape, F32), jax.ShapeDtypeStruct(xb.shape, F32)),
        scratch_shapes=[pltpu.VMEM((tf, D_MODEL), BF16)],
        compiler_params=pltpu.CompilerParams(
            dimension_semantics=("arbitrary",), vmem_limit_bytes=VMEM_LIMIT_BYTES),
        name="ffn",
    )(xa, xb, *consts)


def kernel(x_prompt, x_sample, cache_k_win, cache_v_win, state_conv, state_ssm, g_pre_mix, w_in, attn_sinks, conv_w, conv_b, dt_bias, a_log, d_skip, g_ssm_out, w_out, g_post_mix, g_pre_ffn, w_gate, w_up, w_down, g_post_ffn):
    depth = w_in.shape[0]
    bp, lp, _ = x_prompt.shape
    nb, ts, _ = x_sample.shape
    hp, hs = x_prompt, x_sample
    outs = [[] for _ in range(8)]
    for l in range(depth):
        p = _layer_params(g_pre_mix[l], w_in[l], attn_sinks[l], conv_w[l], conv_b[l], dt_bias[l], a_log[l],
                          d_skip[l], g_ssm_out[l], w_out[l], g_post_mix[l])
        ffn_w = (g_pre_ffn[l].reshape(1, D_MODEL), w_gate[l].astype(BF16), w_up[l].astype(BF16),
                 w_down[l].astype(BF16), g_post_ffn[l].reshape(1, D_MODEL))
        x1p, kp, vp, cp, sp = _prompt_mixer(hp, p)
        x1s, ksm, vsm, csm, ssm = _sample_mixer(hs, cache_k_win[l], cache_v_win[l], state_conv[l], state_ssm[l], p)
        hp, hs = _ffn(x1p.reshape(bp * lp, D_MODEL), x1s, *ffn_w)
        hp = hp.reshape(bp, lp, D_MODEL)
        hs = hs.reshape(nb, ts, D_MODEL)
        ssm_shape = (SSM_HEADS, SSM_HEAD_DIM, D_STATE)
        for lst, val in zip(outs, (_window_major(kp), _window_major(vp), cp,
                                   sp.reshape((bp,) + ssm_shape),
                                   _window_major(ksm), _window_major(vsm), jnp.transpose(csm, (1, 0, 2)),
                                   ssm.reshape((nb,) + ssm_shape))):
            lst.append(val)
    return (hp, hs) + tuple(jnp.stack(o) for o in outs)
```

```python
import functools
import math

import numpy as np
import jax
import jax.numpy as jnp
from jax import lax
from jax.experimental import pallas as pl
from jax.experimental.pallas import tpu as pltpu

F32 = jnp.float32
BF16 = jnp.bfloat16

D_MODEL = 1024
ATT_HEADS = 16
ATT_KV_HEADS = 4
ATT_GROUP = ATT_HEADS // ATT_KV_HEADS
HEAD_DIM = 64
ATT_WIDTH = ATT_HEADS * HEAD_DIM
KV_WIDTH = ATT_KV_HEADS * HEAD_DIM
WINDOW = 128
ROT_DIM = HEAD_DIM // 4
ROPE_THETA = 500000.0
SSM_HEADS = 16
SSM_HEAD_DIM = 64
SSM_WIDTH = SSM_HEADS * SSM_HEAD_DIM
SSM_GROUPS = 2
SSM_HPG = SSM_HEADS // SSM_GROUPS
SSM_GROUP_W = SSM_WIDTH // SSM_GROUPS
D_STATE = 128
CONV_W = 4
CONV_DIM = SSM_WIDTH + 2 * SSM_GROUPS * D_STATE
MIX_WIDTH = ATT_WIDTH + SSM_WIDTH
EPS = 1e-6
PAST_LEN = 8192

LANES = 128
SUBLANES = 8
VMEM_LIMIT_BYTES = 60 * 1024 * 1024

CHUNK = 128
NEG = -1e30
LOG2E = math.log2(math.e)
Q_SCALE = HEAD_DIM ** -0.5 * LOG2E
SEQ_TILE = 512
SAMPLE_BT = 16
SAMPLE_PB = 8
FFN_TILE = 512
PROJ_ROWS = 256
OUT_CB = 512
ROW_BLOCK = 32
FF_CHUNK = 256


def _nn(a, b):
    return jnp.dot(a, b, preferred_element_type=F32)


def _nt(a, b):
    return lax.dot_general(a, b, (((1,), (1,)), ((), ())), preferred_element_type=F32)


def _split_bf16(x, n):
    parts = []
    r = x
    for i in range(n):
        p = r.astype(BF16)
        parts.append(p)
        if i + 1 < n:
            r = r - p.astype(F32)
    return parts


def _expand_heads(x, expand2_ref):
    hi, mid = _split_bf16(x, 2)
    return _nn(jnp.concatenate([hi, mid], axis=1), expand2_ref[...])


def _cumsum_cols(m01, x):
    w = x.shape[1]
    r = _nn(m01, jnp.concatenate(_split_bf16(x, 3), axis=1))
    return r[:, :w] + r[:, w:2 * w] + r[:, 2 * w:]


def _heads_to_rows(x):
    return x.T[:SSM_HEADS]


def _rms(x, g):
    ms = jnp.mean(x * x, axis=-1, keepdims=True)
    return x * lax.rsqrt(ms + EPS) * g


def _rms_rows(src, g_ref, dst_ref, res_ref=None):
    g = g_ref[...]
    for r0 in range(0, dst_ref.shape[0], ROW_BLOCK):
        rs = slice(r0, r0 + ROW_BLOCK)
        y = _rms(src[rs, :], g)
        if res_ref is not None:
            y = res_ref[rs, :] + y
        dst_ref[rs, :] = y.astype(dst_ref.dtype)


def _silu(x):
    h = 0.5 * x
    return h + h * jnp.tanh(h)


R_K = ATT_WIDTH
R_V = R_K + KV_WIDTH
R_Z = R_V + KV_WIDTH
R_XBC = R_Z + SSM_WIDTH
R_END = R_XBC + CONV_DIM

def _conv4(x, w, bias):
    assert CONV_W == 4
    u = pltpu.roll(x, 2, 0)
    return bias + (x * w[3] + u * w[1]) + pltpu.roll(x * w[2] + u * w[0], 1, 0)


CONV_CB = 256
CONV_RB = 64


def _conv_silu_cols(xpad_ref, row0, ra, rb, convw_ref, convb_ref, out_ref, c0):
    cs = slice(c0, c0 + CONV_CB)
    w = [convw_ref[i:i + 1, cs] for i in range(CONV_W)]
    bias = convb_ref[:, cs]
    for r0 in range(ra, rb, CONV_RB):
        xh = xpad_ref[row0 + r0 - SUBLANES:row0 + r0 + CONV_RB, cs]
        out_ref[r0:r0 + CONV_RB, cs] = _silu(_conv4(xh, w, bias)[SUBLANES:])


def _softplus(x):
    return jnp.maximum(x, 0.0) + jnp.log1p(jnp.exp(-jnp.abs(x)))


def _rope(x, c, s):
    lane = lax.broadcasted_iota(jnp.int32, (1, LANES), 1)
    first = (lane % HEAD_DIM) < ROT_DIM // 2
    outs = []
    for j in range(x.shape[1] // LANES):
        xb = x[:, j * LANES:(j + 1) * LANES]
        partner = jnp.where(first, pltpu.roll(xb, LANES - ROT_DIM // 2, 1), pltpu.roll(xb, ROT_DIM // 2, 1))
        outs.append(xb * c + partner * s)
    return outs[0] if len(outs) == 1 else jnp.concatenate(outs, axis=1)


def _iota(shape, dim):
    return lax.broadcasted_iota(jnp.int32, shape, dim)


def _head_blocks(c_out):
    out = []
    for half in range(2):
        g, kvh = divmod(2 * c_out + half, ATT_KV_HEADS)
        b_in = kvh * ATT_GROUP + g
        out.append((b_in // 2, b_in % 2))
    return out


def _permute_q_weight(wr_ref, wq_s):
    low = _iota((1, LANES), 1) < HEAD_DIM
    for c_out in range(ATT_WIDTH // LANES):
        halves = []
        for half, (c_in, src_half) in enumerate(_head_blocks(c_out)):
            col = wr_ref[:, c_in * LANES:(c_in + 1) * LANES]
            halves.append(col if src_half == half else pltpu.roll(col, HEAD_DIM, 1))
        wq_s[:, c_out * LANES:(c_out + 1) * LANES] = jnp.where(low, halves[0], halves[1])


def _project(hn, wq_ref, wr_ref, ropec, ropes):
    q = _rope(_nn(hn, wq_ref[...]), ropec, ropes) * (HEAD_DIM ** -0.5)
    kv = _nn(hn, wr_ref[:, R_K:R_Z])
    k = _rope(kv[:, :KV_WIDTH], ropec, ropes)
    v = kv[:, KV_WIDTH:]
    return q, k, v


def _dt_cols(hn, wdt_ref, dtb_row_ref):
    return _softplus(_nn(hn, wdt_ref[...]) + dtb_row_ref[...])


def _a_row(alog_row_ref):
    lane = _iota((1, LANES), 1)
    return jnp.where(lane < SSM_HEADS, -jnp.exp(alog_row_ref[...]), 0.0)


def _log2_decay(acol):
    acol2 = acol * LOG2E
    return acol2, _heads_to_rows(acol2)


SSD_QUAD = 4
SSD_NQUAD = SSM_HEADS // SSD_QUAD


def _ssd_cb(b_all, c_all):
    return [_nt(c_all[:, g * D_STATE:(g + 1) * D_STATE].astype(BF16),
                b_all[:, g * D_STATE:(g + 1) * D_STATE].astype(BF16)) for g in range(SSM_GROUPS)]


def _ssd_quad(qi, xs_bf, c_all, cbs, acol2, arow2, dtr, mask_bool, hT_bf=None):
    lane4 = _iota((1, SSD_QUAD * SSM_HEAD_DIM), 1) // SSM_HEAD_DIM
    zero = jnp.zeros((), BF16)
    e0 = SSD_QUAD * qi
    g = e0 // SSM_HPG
    cf = c_all[:, g * D_STATE:(g + 1) * D_STATE]
    lanes = slice(e0 * SSM_HEAD_DIM, (e0 + SSD_QUAD) * SSM_HEAD_DIM)
    xq = xs_bf[:, lanes]
    lhs, rhs = [], []
    for i in range(SSD_QUAD):
        e = e0 + i
        a_t = jnp.broadcast_to(acol2[:, e:e + 1], (CHUNK, CHUNK))
        w = cbs[g] * jnp.exp2(jnp.where(mask_bool, a_t - arow2[e:e + 1, :], NEG)) * dtr[e:e + 1, :]
        lhs.append(w.astype(BF16))
        rhs.append(jnp.where(lane4 == i, xq, zero))
        if hT_bf is not None:
            lhs.append((cf * jnp.exp2(a_t)).astype(BF16))
            rhs.append(jnp.where(lane4 == i, hT_bf[:, lanes], zero))
    return _nn(jnp.concatenate(lhs, axis=1), jnp.concatenate(rhs, axis=0))


def _ssd_block(xs_bf, b_all, c_all, acol2, arow2, dtr, mask_bool):
    cbs = _ssd_cb(b_all, c_all)
    return jnp.concatenate([_ssd_quad(qi, xs_bf, c_all, cbs, acol2, arow2, dtr, mask_bool)
                            for qi in range(SSD_NQUAD)], axis=1)


def _gate_rows(rs, y_refs, xc_s, z, dskip_ref, gssm_ref, mix_s):
    y = y_refs[0][rs, :]
    for extra in y_refs[1:]:
        y = y + extra[rs, :]
    gated = (y + dskip_ref[...] * xc_s[rs, 0:SSM_WIDTH]) * _silu(z[rs, :])
    for g in range(SSM_GROUPS):
        gs = slice(g * SSM_GROUP_W, (g + 1) * SSM_GROUP_W)
        gg = gated[:, gs]
        ms = jnp.mean(gg * gg, axis=-1, keepdims=True)
        o = gg * lax.rsqrt(ms + EPS) * gssm_ref[:, gs]
        mix_s[rs, ATT_WIDTH + g * SSM_GROUP_W:ATT_WIDTH + (g + 1) * SSM_GROUP_W] = o.astype(mix_s.dtype)


def _gate_and_out(y_refs, xc_s, z, x_ref, dskip_ref, gssm_ref, wout_ref, gpost_ref, mix_s, out_ref):
    for r0 in range(0, out_ref.shape[0], ROW_BLOCK):
        _gate_rows(slice(r0, r0 + ROW_BLOCK), y_refs, xc_s, z, dskip_ref, gssm_ref, mix_s)
    mo = _nn(mix_s[...].astype(BF16), wout_ref[...])
    _rms_rows(mo, gpost_ref, out_ref, res_ref=x_ref)


def _prompt_kernel(x_ref, gpre_ref, wr_ref, wdt_ref,
                   ropec_ref, ropes_ref, sink_ref, biasT_ref,
                   convw_ref, convb_ref, dtb_row_ref, alog_row_ref,
                   dskip_ref, gssm_ref, wout_ref, gpost_ref, expand2_ref,
                   x1_ref, nk_ref, nv_ref, nconv_ref, nssm_ref,
                   hn_s, q_s, kbuf, vT_s, xbc_s, xc_s, dtc_s, y_s, mix_s, hT_s, wq_s, z_s, mo_s):
    tm = x_ref.shape[0]
    nchunk = tm // CHUNK
    s = pl.program_id(1)
    last = pl.num_programs(1) - 1

    @pl.when(jnp.logical_and(pl.program_id(0) == 0, s == 0))
    def _():
        _permute_q_weight(wr_ref, wq_s)

    @pl.when(s == 0)
    def _():
        kbuf[0:CHUNK, :] = jnp.zeros((CHUNK, KV_WIDTH), BF16)
        vT_s[0] = jnp.zeros((KV_WIDTH, CHUNK), BF16)
        xbc_s[0:SUBLANES, :] = jnp.zeros((SUBLANES, CONV_DIM), F32)
        hT_s[...] = jnp.zeros_like(hT_s)

    _rms_rows(x_ref, gpre_ref, hn_s)

    def projection_pieces(ra, rb):
        rows = slice(ra, rb)

        def rope(x):
            return _rope(x, ropec_ref[rows, :], ropes_ref[rows, :])

        def proj_xbc(c0):
            xbc_s[SUBLANES + ra:SUBLANES + rb, c0:c0 + CONV_CB] = _nn(
                hn_s[rows, :], wr_ref[:, R_XBC + c0:R_XBC + c0 + CONV_CB])

        def conv(c0):
            _conv_silu_cols(xbc_s, SUBLANES, ra, rb, convw_ref, convb_ref, xc_s, c0)

        def proj_q(c0):
            q_s[rows, c0:c0 + KV_WIDTH] = (
                rope(_nn(hn_s[rows, :], wq_s[:, c0:c0 + KV_WIDTH])) * Q_SCALE).astype(BF16)

        def proj_k():
            kbuf[CHUNK + ra:CHUNK + rb, :] = rope(_nn(hn_s[rows, :], wr_ref[:, R_K:R_V])).astype(BF16)

        def proj_v_dt():
            v = _nn(hn_s[rows, :], wr_ref[:, R_V:R_Z])
            for j in range(ra // CHUNK, rb // CHUNK):
                vT_s[1 + j] = v[j * CHUNK - ra:(j + 1) * CHUNK - ra, :].T.astype(BF16)
            dtc_s[rows, :] = _dt_cols(hn_s[rows, :], wdt_ref, dtb_row_ref)

        others = [functools.partial(proj_q, c0) for c0 in range(0, ATT_WIDTH, KV_WIDTH)] + [proj_k, proj_v_dt]
        conv_cols = list(range(0, CONV_DIM, CONV_CB))
        pieces = [functools.partial(proj_xbc, conv_cols[0])]
        for n, c0 in enumerate(conv_cols):
            if n + 1 < len(conv_cols):
                pieces.append(functools.partial(proj_xbc, conv_cols[n + 1]))
            pieces.append(functools.partial(conv, c0))
            if n < len(others):
                pieces.append(others[n])
        return pieces + others[len(conv_cols):]

    a_row = _a_row(alog_row_ref)

    r2 = _iota((CHUNK, CHUNK), 0)
    c2 = _iota((CHUNK, CHUNK), 1)
    tril = c2 <= r2
    tril_bf = tril.astype(BF16)
    lane_kv = _iota((1, KV_WIDTH), 1) // HEAD_DIM
    ones_rows = jnp.ones((2 * SUBLANES, 2 * CHUNK), BF16)

    def chunk_body(c, extras):
        r0 = c * CHUNK
        per = -(-len(extras) // ATT_KV_HEADS)
        first = jnp.logical_and(s == 0, c == 0).astype(jnp.int32)
        bias = biasT_ref[first]
        qcat = jnp.concatenate([q_s[pl.ds(r0, CHUNK), g * KV_WIDTH:(g + 1) * KV_WIDTH]
                                for g in range(ATT_GROUP)], axis=0)
        kwin = kbuf[pl.ds(r0, 2 * CHUNK), :]
        kstack = jnp.concatenate([jnp.where(lane_kv == kvh, kwin, jnp.zeros((), BF16))
                                  for kvh in range(ATT_KV_HEADS)], axis=0)
        sT = _nt(kstack, qcat)
        vT_win = jnp.concatenate([vT_s[c], vT_s[c + 1]], axis=1)

        dtc_c = dtc_s[pl.ds(r0, CHUNK), :]
        acol = _cumsum_cols(tril_bf, dtc_c * a_row)
        a_end = acol[CHUNK - 1:CHUNK, :]
        tailc = jnp.exp(a_end - acol) * dtc_c
        ex = _expand_heads(
            jnp.concatenate([tailc, jnp.broadcast_to(jnp.exp(a_end), (SUBLANES, LANES))], axis=0), expand2_ref)
        tlx = ex[:CHUNK]
        dec_row = ex[CHUNK:CHUNK + 1]
        xs = xc_s[pl.ds(r0, CHUNK), 0:SSM_WIDTH]
        b_all = xc_s[pl.ds(r0, CHUNK), SSM_WIDTH:SSM_WIDTH + SSM_GROUPS * D_STATE]
        c_all = xc_s[pl.ds(r0, CHUNK), SSM_WIDTH + SSM_GROUPS * D_STATE:CONV_DIM]
        hT = hT_s[...]
        acol2, arow2 = _log2_decay(acol)
        dtr_c = _heads_to_rows(dtc_c)
        xs_bf = xs.astype(BF16)
        hT_bf = hT.astype(BF16)
        cbs = _ssd_cb(b_all, c_all)

        o_rows = []
        for i in range(ATT_KV_HEADS):
            blk = sT[i * 2 * CHUNK:(i + 1) * 2 * CHUNK] + bias
            sink = sink_ref[i] * LOG2E
            m = jnp.maximum(jnp.max(blk, axis=0, keepdims=True), sink)
            p = jnp.exp2(blk - m).astype(BF16)
            lhs = jnp.concatenate([vT_win[i * HEAD_DIM:(i + 1) * HEAD_DIM], ones_rows], axis=0)
            oT = _nn(lhs, p)
            den = oT[HEAD_DIM:HEAD_DIM + 1] + jnp.exp2(sink - m)
            o_rows.append(oT[:HEAD_DIM] * (1.0 / den))
            for qi in range(i * SSD_NQUAD // ATT_KV_HEADS, (i + 1) * SSD_NQUAD // ATT_KV_HEADS):
                lanes = slice(qi * SSD_QUAD * SSM_HEAD_DIM, (qi + 1) * SSD_QUAD * SSM_HEAD_DIM)
                y_s[pl.ds(r0, CHUNK), lanes] = _ssd_quad(qi, xs_bf, c_all, cbs, acol2, arow2, dtr_c, tril, hT_bf)
            for t in extras[i * per:(i + 1) * per]:
                t()
        for c_out in range(ATT_WIDTH // LANES):
            kvh, g0 = divmod(2 * c_out, ATT_GROUP)
            two = jnp.concatenate([o_rows[kvh][:, g * CHUNK:(g + 1) * CHUNK] for g in (g0, g0 + 1)], axis=0)
            mix_s[pl.ds(r0, CHUNK), c_out * LANES:(c_out + 1) * LANES] = two.T.astype(BF16)

        xtl = (xs * tlx).astype(BF16)
        for g in range(SSM_GROUPS):
            sl = slice(g * SSM_GROUP_W, (g + 1) * SSM_GROUP_W)
            bt = b_all[:, g * D_STATE:(g + 1) * D_STATE].T.astype(BF16)
            hT_s[:, sl] = hT[:, sl] * dec_row[:, sl] + _nn(bt, xtl[:, sl])

    def output_pieces(ra, rb):
        rows = slice(ra, rb)
        g_post = gpost_ref[...]

        def proj_z(c0):
            z_s[rows, c0:c0 + OUT_CB] = _nn(hn_s[rows, :], wr_ref[:, R_Z + c0:R_Z + c0 + OUT_CB])

        def gate(r0):
            _gate_rows(slice(r0, r0 + ROW_BLOCK), (y_s,), xc_s, z_s, dskip_ref, gssm_ref, mix_s)

        def proj_out(c0):
            mo_s[rows, c0:c0 + OUT_CB] = _nn(mix_s[rows, :], wout_ref[:, c0:c0 + OUT_CB])

        def post(r0):
            rs = slice(r0, r0 + ROW_BLOCK)
            x1_ref[rs, :] = x_ref[rs, :] + _rms(mo_s[rs, :], g_post)

        return ([functools.partial(proj_z, c0) for c0 in range(0, SSM_WIDTH, OUT_CB)]
                + [functools.partial(gate, r0) for r0 in range(ra, rb, ROW_BLOCK)]
                + [functools.partial(proj_out, c0) for c0 in range(0, D_MODEL, OUT_CB)]
                + [functools.partial(post, r0) for r0 in range(ra, rb, ROW_BLOCK)])

    groups = list(range(0, tm, PROJ_ROWS))
    cpg = PROJ_ROWS // CHUNK
    for t in projection_pieces(0, PROJ_ROWS):
        t()
    for gi, ra in enumerate(groups):
        side = projection_pieces(ra + PROJ_ROWS, ra + 2 * PROJ_ROWS) if gi + 1 < len(groups) else []
        if gi > 0:
            side = side + output_pieces(ra - PROJ_ROWS, ra)
        share = -(-len(side) // cpg)
        for k in range(cpg):
            chunk_body(ra // CHUNK + k, side[k * share:(k + 1) * share])

    kbuf[0:CHUNK, :] = kbuf[tm:tm + CHUNK, :]
    vT_s[0] = vT_s[nchunk]
    xbc_s[0:SUBLANES, :] = xbc_s[tm:tm + SUBLANES, :]

    for t in output_pieces(tm - PROJ_ROWS, tm):
        t()

    @pl.when(s == last)
    def _():
        hn_w = hn_s[tm - WINDOW:, :]
        nk_ref[...] = _rope(_nn(hn_w, wr_ref[:, R_K:R_V]), ropec_ref[tm - WINDOW:, :], ropes_ref[tm - WINDOW:, :]).T
        nv_ref[...] = _nn(hn_w, wr_ref[:, R_V:R_Z]).T
        nconv_ref[...] = xbc_s[SUBLANES - (CONV_W - 1):SUBLANES, :]
        nssm_ref[...] = hT_s[...].T


def _sample_kernel(x_ref, ck_ref, cv_ref, sconv_ref, sssm_ref,
                   gpre_ref, wr_ref, wdt_ref,
                   ropec_ref, ropes_ref, sink_ref, bias_ref,
                   convw_ref, convb_ref, dtb_row_ref, alog_row_ref,
                   dskip_ref, gssm_ref, wout_ref, gpost_ref, expand2_ref,
                   x1_ref, nk_ref, nv_ref, nconv_ref, nssm_ref,
                   hn_s, q_s, knr_s, vnr_s, knb_s, vnb_s, xpad_s, xc_s, bb_s, ea_s, eax_s, xT_s, y_s, yoff_s, mix_s, wq_s):
    bt_n = sconv_ref.shape[1]
    pb_n = ck_ref.shape[0]
    m_rows = x_ref.shape[0]
    tdec = m_rows // bt_n
    j = pl.program_id(1)

    @pl.when(jnp.logical_and(pl.program_id(0) == 0, j == 0))
    def _():
        _permute_q_weight(wr_ref, wq_s)

    @pl.when(j == 0)
    def _():
        _rms_rows(x_ref, gpre_ref, hn_s)
        hn = hn_s[...]
        q, k, v = _project(hn, wq_s, wr_ref, ropec_ref[...], ropes_ref[...])
        q_s[...] = q
        vnb_s[...] = v.astype(BF16)
        k_t, v_t = k.T, v.T
        knb_s[...] = k_t.astype(BF16)
        knr_s[...] = k_t
        vnr_s[...] = v_t

        xbc = _nn(hn, wr_ref[:, R_XBC:R_END])
        xpad_s[:, 0:SUBLANES - 3, :] = jnp.zeros((bt_n, SUBLANES - 3, CONV_DIM), F32)
        for r in range(CONV_W - 1):
            xpad_s[:, SUBLANES - 3 + r, :] = sconv_ref[r]
        xpad_s[:, SUBLANES:2 * SUBLANES, :] = xbc.reshape(bt_n, tdec, CONV_DIM)
        for r in range(CONV_W - 1):
            nconv_ref[r] = xpad_s[:, 2 * SUBLANES - 3 + r, :]
        for c0 in range(0, CONV_DIM, CONV_CB):
            cs = slice(c0, c0 + CONV_CB)
            xh = xpad_s[:, :, cs].reshape(bt_n * 2 * SUBLANES, CONV_CB)
            acc = _conv4(xh, [convw_ref[i:i + 1, cs] for i in range(CONV_W)], convb_ref[:, cs])
            xc_s[:, cs] = _silu(acc.reshape(bt_n, 2 * SUBLANES, CONV_CB)[:, SUBLANES:, :].reshape(m_rows, CONV_CB))
        xs = xc_s[:, 0:SSM_WIDTH]
        b_all = xc_s[:, SSM_WIDTH:SSM_WIDTH + SSM_GROUPS * D_STATE]
        c_all = xc_s[:, SSM_WIDTH + SSM_GROUPS * D_STATE:CONV_DIM]
        bb_s[...] = b_all.astype(BF16)

        dtc = _dt_cols(hn, wdt_ref, dtb_row_ref)
        a_row = _a_row(alog_row_ref)

        r2 = _iota((m_rows, m_rows), 0)
        c2 = _iota((m_rows, m_rows), 1)
        same = (r2 // tdec) == (c2 // tdec)
        causal = jnp.logical_and(same, c2 <= r2)
        causal_bf = causal.astype(BF16)
        same_bf = same.astype(BF16)

        dac = dtc * a_row
        acol = _cumsum_cols(causal_bf, dac)
        alast = _cumsum_cols(same_bf, dac)
        tailc = jnp.exp(alast - acol) * dtc
        ex = _expand_heads(jnp.concatenate([jnp.exp(acol), tailc], axis=0), expand2_ref)
        ea_s[...] = jnp.exp(alast)
        eax_s[...] = ex[:m_rows]
        acol2, arow2 = _log2_decay(acol)
        y_s[...] = _ssd_block(xs.astype(BF16), b_all, c_all, acol2, arow2, _heads_to_rows(dtc), causal)
        xtl = xs * ex[m_rows:]
        for jj in range(SSM_WIDTH // LANES):
            xT_s[jj * LANES:(jj + 1) * LANES, :] = xtl[:, jj * LANES:(jj + 1) * LANES].T.astype(BF16)

    lane_kv = _iota((1, KV_WIDTH), 1) // HEAD_DIM
    row_b = _iota((m_rows, 1), 0) // tdec
    low_half = _iota((1, LANES), 1) < HEAD_DIM
    sink = sink_ref[...]
    new_lane = _iota((1, WINDOW), 1) >= WINDOW - tdec

    pbs = range(pb_n)
    bs = [j * pb_n + pb for pb in pbs]
    rs = [pl.multiple_of(b * tdec, tdec) for b in bs]

    qbd, kc, vc, sc = [], [], [], []
    for pb in pbs:
        q8 = q_s[pl.ds(rs[pb], tdec), :]
        qbd.append(jnp.concatenate(
            [jnp.where(lane_kv == kvh, q8[:, g * KV_WIDTH:(g + 1) * KV_WIDTH], 0.0)
             for g in range(ATT_GROUP) for kvh in range(ATT_KV_HEADS)], axis=0).astype(BF16))
        kc.append(ck_ref[pb])
        vc.append(cv_ref[pb])
    for pb in pbs:
        keys_t = jnp.concatenate([kc[pb].astype(BF16), knb_s[...]], axis=1)
        sc.append(_nn(qbd[pb], keys_t) + bias_ref[bs[pb]])
    p, inv = [], []
    for pb in pbs:
        m = jnp.maximum(jnp.max(sc[pb], axis=1, keepdims=True), sink)
        e = jnp.exp(sc[pb] - m)
        inv.append(1.0 / (jnp.sum(e, axis=1, keepdims=True) + jnp.exp(sink - m)))
        p.append(e.astype(BF16))
    for pb in pbs:
        o = (_nt(p[pb][:, :WINDOW], vc[pb].astype(BF16)) + _nn(p[pb][:, WINDOW:], vnb_s[...])) * inv[pb]
        for c_out in range(ATT_WIDTH // LANES):
            kvh, g0 = divmod(2 * c_out, ATT_GROUP)
            halves = []
            for half in range(2):
                i0 = ((g0 + half) * ATT_KV_HEADS + kvh) * tdec
                piece = o[i0:i0 + tdec, (kvh // 2) * LANES:(kvh // 2 + 1) * LANES]
                halves.append(piece if kvh % 2 == half else pltpu.roll(piece, HEAD_DIM, 1))
            mix_s[pl.ds(rs[pb], tdec), c_out * LANES:(c_out + 1) * LANES] = jnp.where(low_half, halves[0], halves[1])
        to_end = WINDOW - (bs[pb] + 1) * tdec
        nk_ref[pb] = jnp.where(new_lane, pltpu.roll(knr_s[...], to_end, 1), pltpu.roll(kc[pb], WINDOW - tdec, 1))
        nv_ref[pb] = jnp.where(new_lane, pltpu.roll(vnr_s[...], to_end, 1), pltpu.roll(vc[pb], WINDOW - tdec, 1))

    for g in range(SSM_GROUPS):
        sl = slice(g * SSM_GROUP_W, (g + 1) * SSM_GROUP_W)
        c0 = SSM_WIDTH + (SSM_GROUPS + g) * D_STATE
        hg, upd = [], []
        for pb in pbs:
            hg.append(sssm_ref[pb, sl, :])
            bm = jnp.where(row_b == bs[pb], bb_s[:, g * D_STATE:(g + 1) * D_STATE], jnp.zeros((), BF16))
            upd.append(_nn(xT_s[sl, :], bm))
        for pb in pbs:
            cc = xc_s[pl.ds(rs[pb], tdec), c0:c0 + D_STATE].astype(BF16)
            yoff_s[pl.ds(rs[pb], tdec), sl] = (_nt(cc, hg[pb].astype(BF16))
                                               * eax_s[pl.ds(rs[pb], tdec), sl])
        for pb in pbs:
            ea_b = ea_s[pl.ds(rs[pb], 1), :]
            dec = jnp.concatenate(
                [jnp.broadcast_to(ea_b[:, g * SSM_HPG + e:g * SSM_HPG + e + 1], (SSM_HEAD_DIM, D_STATE))
                 for e in range(SSM_HPG)], axis=0)
            nssm_ref[pb, sl, :] = hg[pb] * dec + upd[pb]

    @pl.when(j == pl.num_programs(1) - 1)
    def _():
        z = _nn(hn_s[...], wr_ref[:, R_Z:R_XBC])
        _gate_and_out((y_s, yoff_s), xc_s, z, x_ref, dskip_ref, gssm_ref, wout_ref, gpost_ref, mix_s, x1_ref)


def _ffn_kernel(xa_ref, xb_ref, gpre_ref, wg_ref, wu_ref, wd_ref, gpost_ref, oa_ref, ob_ref, f_s, *, steps_a):
    def tile(x_ref, o_ref):
        _rms_rows(x_ref, gpre_ref, f_s)
        f = f_s[...]
        d_ff = wg_ref.shape[1]
        acc = jnp.zeros(x_ref.shape, F32)
        for j in range(d_ff // FF_CHUNK):
            sl = slice(j * FF_CHUNK, (j + 1) * FF_CHUNK)
            gate = _nn(f, wg_ref[:, sl])
            up = _nn(f, wu_ref[:, sl])
            acc = acc + _nn((_silu(gate) * up).astype(BF16), wd_ref[sl, :])
        _rms_rows(acc, gpost_ref, o_ref, res_ref=x_ref)

    i = pl.program_id(0)

    @pl.when(i < steps_a)
    def _():
        tile(xa_ref, oa_ref)

    @pl.when(i >= steps_a)
    def _():
        tile(xb_ref, ob_ref)


def _const_spec(shape):
    nd = len(shape)
    return pl.BlockSpec(shape, lambda *_: (0,) * nd, pipeline_mode=pl.Buffered(1))


def _rope_tables(pos):
    half = ROT_DIM // 2
    inv = ROPE_THETA ** (-np.arange(half, dtype=np.float64) * 2.0 / ROT_DIM)
    ang = pos.astype(np.float64)[:, None] * inv[None, :]
    cos = np.cos(ang).astype(np.float32)
    sin = np.sin(ang).astype(np.float32)
    n = pos.shape[0]
    pad = HEAD_DIM - ROT_DIM
    c = np.concatenate([cos, cos, np.ones((n, pad), np.float32)], axis=1)
    s = np.concatenate([-sin, sin, np.zeros((n, pad), np.float32)], axis=1)
    rep = LANES // HEAD_DIM
    return tuple(jnp.asarray(np.tile(t, (1, rep))) for t in (c, s))


def _layer_params(g_pre_mix, w_in, attn_sinks, conv_w, conv_b, dt_bias, a_log, d_skip, g_ssm_out, w_out, g_post_mix):
    wr = w_in.astype(BF16)
    wdt = jnp.pad(w_in[:, R_END:], ((0, 0), (0, LANES - SSM_HEADS))).astype(BF16)
    wout = w_out.astype(BF16)
    pad16 = ((0, 0), (0, LANES - SSM_HEADS))
    expand = (np.arange(LANES)[:, None] == (np.arange(SSM_WIDTH)[None, :] // SSM_HEAD_DIM)).astype(np.float32)
    return dict(
        gpre=g_pre_mix.reshape(1, D_MODEL), wr=wr, wdt=wdt,
        convw=conv_w, convb=conv_b.reshape(1, CONV_DIM),
        dtb_row=jnp.pad(dt_bias.reshape(1, SSM_HEADS), pad16),
        alog_row=jnp.pad(a_log.reshape(1, SSM_HEADS), pad16),
        dskip=jnp.repeat(d_skip, SSM_HEAD_DIM).reshape(1, SSM_WIDTH), gssm=g_ssm_out.reshape(1, SSM_WIDTH),
        wout=wout, gpost=g_post_mix.reshape(1, D_MODEL), expand2=jnp.asarray(np.concatenate([expand, expand], axis=0), BF16),
        sinks_gk=attn_sinks.reshape(ATT_KV_HEADS, ATT_GROUP).T,
    )


_WEIGHT_ORDER = ("gpre", "wr", "wdt")
_TAIL_ORDER = ("convw", "convb", "dtb_row", "alog_row",
               "dskip", "gssm", "wout", "gpost", "expand2")


def _prompt_mixer(x, p):
    bsz, seq, _ = x.shape
    tm = SEQ_TILE
    nchunk = tm // CHUNK
    c, sn = _rope_tables(np.arange(seq))
    sink = jnp.repeat(p["sinks_gk"].T, CHUNK, axis=1)[:, None, :]
    jj = np.arange(2 * CHUNK)[:, None]
    tt = np.arange(CHUNK)[None, :]
    vis = (jj >= tt) & (jj <= tt + WINDOW)
    bias = np.stack([np.where(vis, 0.0, NEG), np.where(vis & (jj >= CHUNK), 0.0, NEG)]).astype(np.float32)
    bias = jnp.asarray(np.tile(bias, (1, 1, ATT_GROUP)))
    consts = [p[n] for n in _WEIGHT_ORDER]
    tail = [p[n] for n in _TAIL_ORDER]
    in_specs = ([pl.BlockSpec((None, tm, D_MODEL), lambda b, s: (b, s, 0))]
                + [_const_spec(a.shape) for a in consts]
                + [pl.BlockSpec((tm, LANES), lambda b, s: (s, 0))] * 2
                + [_const_spec(sink.shape), _const_spec(bias.shape)]
                + [_const_spec(a.shape) for a in tail])
    out_shape = (
        jax.ShapeDtypeStruct((bsz, seq, D_MODEL), F32),
        jax.ShapeDtypeStruct((bsz, KV_WIDTH, WINDOW), F32),
        jax.ShapeDtypeStruct((bsz, KV_WIDTH, WINDOW), F32),
        jax.ShapeDtypeStruct((bsz, CONV_W - 1, CONV_DIM), F32),
        jax.ShapeDtypeStruct((bsz, SSM_WIDTH, D_STATE), F32),
    )
    out_specs = (
        pl.BlockSpec((None, tm, D_MODEL), lambda b, s: (b, s, 0)),
        pl.BlockSpec((None, KV_WIDTH, WINDOW), lambda b, s: (b, 0, 0)),
        pl.BlockSpec((None, KV_WIDTH, WINDOW), lambda b, s: (b, 0, 0)),
        pl.BlockSpec((None, CONV_W - 1, CONV_DIM), lambda b, s: (b, 0, 0)),
        pl.BlockSpec((None, SSM_WIDTH, D_STATE), lambda b, s: (b, 0, 0)),
    )
    scratch = [
        pltpu.VMEM((tm, D_MODEL), BF16),
        pltpu.VMEM((tm, ATT_WIDTH), BF16),
        pltpu.VMEM((CHUNK + tm, KV_WIDTH), BF16),
        pltpu.VMEM((nchunk + 1, KV_WIDTH, CHUNK), BF16),
        pltpu.VMEM((SUBLANES + tm, CONV_DIM), F32),
        pltpu.VMEM((tm, CONV_DIM), F32),
        pltpu.VMEM((tm, LANES), F32),
        pltpu.VMEM((tm, SSM_WIDTH), F32),
        pltpu.VMEM((tm, MIX_WIDTH), BF16),
        pltpu.VMEM((D_STATE, SSM_WIDTH), F32),
        pltpu.VMEM((D_MODEL, ATT_WIDTH), BF16),
        pltpu.VMEM((tm, SSM_WIDTH), F32),
        pltpu.VMEM((tm, D_MODEL), F32),
    ]
    return pl.pallas_call(
        _prompt_kernel,
        grid=(bsz, seq // tm),
        in_specs=in_specs,
        out_specs=out_specs,
        out_shape=out_shape,
        scratch_shapes=scratch,
        compiler_params=pltpu.CompilerParams(
            dimension_semantics=("arbitrary", "arbitrary"), vmem_limit_bytes=VMEM_LIMIT_BYTES),
        name="prompt_mixer",
    )(x, *consts, c, sn, sink, bias, *tail)


def _sample_mixer(x, cache_k, cache_v, state_conv, state_ssm, p):
    nb, tdec, _ = x.shape
    bt = SAMPLE_BT
    rows = bt * tdec
    c, sn = _rope_tables(np.tile(PAST_LEN + np.arange(tdec), bt))
    sink = jnp.repeat(p["sinks_gk"].reshape(-1), tdec).reshape(rows, 1)
    consts = [p[n] for n in _WEIGHT_ORDER]
    tail = [p[n] for n in _TAIL_ORDER]
    x2 = x.reshape(nb * tdec, D_MODEL)
    assert rows == WINDOW
    ck = jnp.transpose(cache_k, (0, 2, 3, 1)).reshape(nb, KV_WIDTH, WINDOW)
    cv = jnp.transpose(cache_v, (0, 2, 3, 1)).reshape(nb, KV_WIDTH, WINDOW)
    ssm = state_ssm.reshape(nb, SSM_WIDTH, D_STATE)
    pb = SAMPLE_PB
    steps = bt // pb
    t_of_row = (np.arange(rows) % tdec)[:, None]
    col = np.arange(rows)[None, :]
    vis_c = col >= t_of_row
    bias = jnp.asarray(np.stack(
        [np.where(np.concatenate([vis_c, (col // tdec == b) & (col % tdec <= t_of_row)], axis=1), 0.0, NEG)
         for b in range(bt)]).astype(np.float32))
    tmap = lambda i, j: (0, i, 0)
    pmap = lambda i, j: (i * steps + j, 0, 0)
    in_specs = ([pl.BlockSpec((rows, D_MODEL), lambda i, j: (i, 0)),
                 pl.BlockSpec((pb, KV_WIDTH, WINDOW), pmap),
                 pl.BlockSpec((pb, KV_WIDTH, WINDOW), pmap),
                 pl.BlockSpec((CONV_W - 1, bt, CONV_DIM), tmap),
                 pl.BlockSpec((pb, SSM_WIDTH, D_STATE), pmap)]
                + [_const_spec(a.shape) for a in consts]
                + [_const_spec(c.shape)] * 2
                + [_const_spec(sink.shape), _const_spec(bias.shape)]
                + [_const_spec(a.shape) for a in tail])
    out_shape = (
        jax.ShapeDtypeStruct((nb * tdec, D_MODEL), F32),
        jax.ShapeDtypeStruct((nb, KV_WIDTH, WINDOW), F32),
        jax.ShapeDtypeStruct((nb, KV_WIDTH, WINDOW), F32),
        jax.ShapeDtypeStruct((CONV_W - 1, nb, CONV_DIM), F32),
        jax.ShapeDtypeStruct((nb, SSM_WIDTH, D_STATE), F32),
    )
    out_specs = (
        pl.BlockSpec((rows, D_MODEL), lambda i, j: (i, 0)),
        pl.BlockSpec((pb, KV_WIDTH, WINDOW), pmap),
        pl.BlockSpec((pb, KV_WIDTH, WINDOW), pmap),
        pl.BlockSpec((CONV_W - 1, bt, CONV_DIM), tmap),
        pl.BlockSpec((pb, SSM_WIDTH, D_STATE), pmap),
    )
    scratch = [
        pltpu.VMEM((rows, D_MODEL), BF16),
        pltpu.VMEM((rows, ATT_WIDTH), F32),
        pltpu.VMEM((KV_WIDTH, rows), F32),
        pltpu.VMEM((KV_WIDTH, rows), F32),
        pltpu.VMEM((KV_WIDTH, rows), BF16),
        pltpu.VMEM((rows, KV_WIDTH), BF16),
        pltpu.VMEM((bt, 2 * SUBLANES, CONV_DIM), F32),
        pltpu.VMEM((rows, CONV_DIM), F32),
        pltpu.VMEM((rows, SSM_GROUPS * D_STATE), BF16),
        pltpu.VMEM((rows, LANES), F32),
        pltpu.VMEM((rows, SSM_WIDTH), F32),
        pltpu.VMEM((SSM_WIDTH, rows), BF16),
        pltpu.VMEM((rows, SSM_WIDTH), F32),
        pltpu.VMEM((rows, SSM_WIDTH), F32),
        pltpu.VMEM((rows, MIX_WIDTH), F32),
        pltpu.VMEM((D_MODEL, ATT_WIDTH), BF16),
    ]
    return pl.pallas_call(
        _sample_kernel,
        grid=(nb // bt, steps),
        in_specs=in_specs,
        out_specs=out_specs,
        out_shape=out_shape,
        scratch_shapes=scratch,
        compiler_params=pltpu.CompilerParams(
            dimension_semantics=("arbitrary", "arbitrary"), vmem_limit_bytes=VMEM_LIMIT_BYTES),
        name="sample_mixer",
    )(x2, ck, cv, jnp.transpose(state_conv, (1, 0, 2)), ssm, *consts, c, sn, sink, bias, *tail)


def _window_major(kv_t):
    return jnp.transpose(kv_t.reshape(kv_t.shape[0], ATT_KV_HEADS, HEAD_DIM, WINDOW), (0, 3, 1, 2))


def _ffn(xa, xb, gpre, wg, wu, wd, gpost):
    tf = FFN_TILE
    steps_a, steps_b = xa.shape[0] // tf, xb.shape[0] // tf
    consts = [gpre, wg, wu, wd, gpost]
    amap = lambda i: (jnp.minimum(i, steps_a - 1), 0)
    bmap = lambda i: (jnp.maximum(i - steps_a, 0), 0)
    return pl.pallas_call(
        functools.partial(_ffn_kernel, steps_a=steps_a),
        grid=(steps_a + steps_b,),
        in_specs=([pl.BlockSpec((tf, D_MODEL), amap), pl.BlockSpec((tf, D_MODEL), bmap)]
                  + [_const_spec(a.shape) for a in consts]),
        out_specs=(pl.BlockSpec((tf, D_MODEL), amap), pl.BlockSpec((tf, D_MODEL), bmap)),
        out_shape=(jax.ShapeDtypeStruct(xa.shape, F32), jax.ShapeDtypeStruct(xb.shape, F32)),
        scratch_shapes=[pltpu.VMEM((tf, D_MODEL), BF16)],
        compiler_params=pltpu.CompilerParams(
            dimension_semantics=("arbitrary",), vmem_limit_bytes=VMEM_LIMIT_BYTES),
        name="ffn",
    )(xa, xb, *consts)


def kernel(x_prompt, x_sample, cache_k_win, cache_v_win, state_conv, state_ssm, g_pre_mix, w_in, attn_sinks, conv_w, conv_b, dt_bias, a_log, d_skip, g_ssm_out, w_out, g_post_mix, g_pre_ffn, w_gate, w_up, w_down, g_post_ffn):
    depth = w_in.shape[0]
    bp, lp, _ = x_prompt.shape
    nb, ts, _ = x_sample.shape
    hp, hs = x_prompt, x_sample
    outs = [[] for _ in range(8)]
    for l in range(depth):
        p = _layer_params(g_pre_mix[l], w_in[l], attn_sinks[l], conv_w[l], conv_b[l], dt_bias[l], a_log[l],
                          d_skip[l], g_ssm_out[l], w_out[l], g_post_mix[l])
        ffn_w = (g_pre_ffn[l].reshape(1, D_MODEL), w_gate[l].astype(BF16), w_up[l].astype(BF16),
                 w_down[l].astype(BF16), g_post_ffn[l].reshape(1, D_MODEL))
        x1p, kp, vp, cp, sp = _prompt_mixer(hp, p)
        x1s, ksm, vsm, csm, ssm = _sample_mixer(hs, cache_k_win[l], cache_v_win[l], state_conv[l], state_ssm[l], p)
        hp, hs = _ffn(x1p.reshape(bp * lp, D_MODEL), x1s, *ffn_w)
        hp = hp.reshape(bp, lp, D_MODEL)
        hs = hs.reshape(nb, ts, D_MODEL)
        ssm_shape = (SSM_HEADS, SSM_HEAD_DIM, D_STATE)
        for lst, val in zip(outs, (_window_major(kp), _window_major(vp), cp,
                                   sp.reshape((bp,) + ssm_shape),
                                   _window_major(ksm), _window_major(vsm), jnp.transpose(csm, (1, 0, 2)),
                                   ssm.reshape((nb,) + ssm_shape))):
            lst.append(val)
    return (hp, hs) + tuple(jnp.stack(o) for o in outs)
```
